```python
import math
import jax, jax.numpy as jnp
from jax import lax
import numpy as np

D_MODEL = 1024
BATCH = 16
SEQ = 2048
DEPTH = 4

D_MIX = 2 * D_MODEL
N_GROUPS = 4
GROUP_W = D_MIX // N_GROUPS
HEAD_DIM = 64
N_HEADS = GROUP_W // HEAD_DIM
EPS = 1e-6

GLA_DK = HEAD_DIM // 2
GLA_DV = HEAD_DIM
GLA_RANK = 16
GLA_NORMALIZER = 16.0
GLA_CHUNK = 64

Q_BLOCK = 128

MLA_Q_RANK = D_MODEL // 4
MLA_KV_RANK = D_MODEL // 8
MLA_NOPE = HEAD_DIM
MLA_ROPE = HEAD_DIM // 2
MLA_V = HEAD_DIM
ROPE_THETA = 10000.0

LRU_BLOCKS = N_HEADS
LRU_BW = GROUP_W // LRU_BLOCKS
CONV_W = 4
LRU_C = 8.0

IN_SIZES = (
    D_MIX,
    N_HEADS * GLA_DK, N_HEADS * GLA_DK,
    N_HEADS * GLA_DV, GLA_RANK,
    GROUP_W, GROUP_W, GROUP_W, N_HEADS,
    MLA_Q_RANK, MLA_KV_RANK, MLA_ROPE,
    GROUP_W,
)
D_IN = sum(IN_SIZES)

kernel_name = "hymba_style_gla_fox_mla_rglru"


def rms_norm(x, g):
    xf = x.astype(jnp.float32)
    y = xf * lax.rsqrt(jnp.mean(xf * xf, axis=-1, keepdims=True) + EPS)
    return (y * g.astype(jnp.float32)).astype(x.dtype)


def split_heads(z, d):
    return z.reshape(z.shape[0], z.shape[1], -1, d)


def gla_mixer(q, k, v, g_in, w_up, b_up, g_out):
    B_, T, _ = q.shape
    nc = T // GLA_CHUNK
    f32 = jnp.float32
    log_a = jax.nn.log_sigmoid((g_in.astype(f32) @ w_up.astype(f32)) + b_up.astype(f32)) / GLA_NORMALIZER

    def chunks(z, d):
        return z.astype(f32).reshape(B_, nc, GLA_CHUNK, N_HEADS, d).transpose(1, 0, 3, 2, 4)

    qc = chunks(q, GLA_DK) * (GLA_DK ** -0.5)
    kc, vc, gc = chunks(k, GLA_DK), chunks(v, GLA_DV), chunks(log_a, GLA_DK)
    causal = jnp.tril(jnp.ones((GLA_CHUNK, GLA_CHUNK), dtype=bool))[:, :, None]

    def step(S, inp):
        q_, k_, v_, g_ = inp
        G = jnp.cumsum(g_, axis=2)
        o_inter = jnp.einsum('bhtk,bhkv->bhtv', q_ * jnp.exp(G), S)
        diff = G[:, :, :, None, :] - G[:, :, None, :, :]
        decay = jnp.exp(jnp.where(causal, diff, -jnp.inf))
        A = jnp.sum(q_[:, :, :, None, :] * k_[:, :, None, :, :] * decay, axis=-1)
        o = o_inter + jnp.einsum('bhts,bhsv->bhtv', A, v_)
        G_last = G[:, :, -1:, :]
        S_new = jnp.exp(G_last[:, :, 0, :])[..., None] * S + jnp.einsum('bhsk,bhsv->bhkv', k_ * jnp.exp(G_last - G), v_)
        return S_new, o

    S0 = jnp.zeros((B_, N_HEADS, GLA_DK, GLA_DV), f32)
    _, o = lax.scan(step, S0, (qc, kc, vc, gc))
    o = o.transpose(1, 0, 3, 2, 4).reshape(B_, T, N_HEADS, GLA_DV)
    o = rms_norm(o, g_out)
    return o.reshape(B_, T, N_HEADS * GLA_DV).astype(q.dtype)


def causal_block_attention(q, k, v, log_decay_cum=None):
    T = q.shape[1]
    scale = q.shape[-1] ** -0.5
    if log_decay_cum is not None:
        cum = jnp.transpose(log_decay_cum, (0, 2, 1))
    outs = []
    for blk in range(T // Q_BLOCK):
        lo, hi = blk * Q_BLOCK, (blk + 1) * Q_BLOCK
        s = jnp.einsum('bqhd,bkhd->bhqk', q[:, lo:hi], k[:, :hi]) * scale
        if log_decay_cum is not None:
            s = s + (cum[:, :, lo:hi, None] - cum[:, :, None, :hi])
        mask = (lo + jnp.arange(Q_BLOCK))[:, None] >= jnp.arange(hi)[None, :]
        p = jax.nn.softmax(jnp.where(mask, s, -jnp.inf), axis=-1)
        outs.append(jnp.einsum('bhqk,bkhd->bqhd', p, v[:, :hi]))
    return jnp.concatenate(outs, axis=1)


def fox_mixer(q, k, v, f_logit, b_f, g_q, g_k):
    B_, T, _ = q.shape
    f32 = jnp.float32
    qh = rms_norm(split_heads(q, HEAD_DIM), g_q).astype(f32)
    kh = rms_norm(split_heads(k, HEAD_DIM), g_k).astype(f32)
    vh = split_heads(v, HEAD_DIM).astype(f32)
    log_f = jax.nn.log_sigmoid(f_logit.astype(f32) + b_f.astype(f32))
    cum = jnp.cumsum(log_f, axis=1)
    o = causal_block_attention(qh, kh, vh, cum)
    return o.reshape(B_, T, GROUP_W).astype(q.dtype)


def apply_rope(z, positions):
    half = MLA_ROPE // 2
    inv_freq = ROPE_THETA ** (-jnp.arange(half, dtype=jnp.float32) / half)
    ang = positions.astype(jnp.float32)[:, :, None, None] * inv_freq
    cos, sin = jnp.cos(ang), jnp.sin(ang)
    z1, z2 = z[..., :half], z[..., half:]
    return jnp.concatenate([z1 * cos - z2 * sin, z1 * sin + z2 * cos], axis=-1)


def mla_mixer(c_q, c_kv, k_rope, positions, g_cq, w_uq, g_ckv, w_ukv, g_q, g_k):
    B_, T, _ = c_q.shape
    f32 = jnp.float32
    q = split_heads(rms_norm(c_q, g_cq) @ w_uq, MLA_NOPE + MLA_ROPE)
    kv = split_heads(rms_norm(c_kv, g_ckv) @ w_ukv, MLA_NOPE + MLA_V)
    k_nope, v = kv[..., :MLA_NOPE], kv[..., MLA_NOPE:]
    k_r = jnp.broadcast_to(k_rope[:, :, None, :], (B_, T, N_HEADS, MLA_ROPE))
    k = jnp.concatenate([k_nope, k_r.astype(k_nope.dtype)], axis=-1)
    q = rms_norm(q, g_q).astype(f32)
    k = rms_norm(k, g_k).astype(f32)
    q = jnp.concatenate([q[..., :MLA_NOPE], apply_rope(q[..., MLA_NOPE:], positions)], axis=-1)
    k = jnp.concatenate([k[..., :MLA_NOPE], apply_rope(k[..., MLA_NOPE:], positions)], axis=-1)
    o = causal_block_attention(q, k, v.astype(f32))
    return o.reshape(B_, T, N_HEADS * MLA_V).astype(c_q.dtype)


def rglru_mixer(xb, conv_w, conv_b, w_r, b_r, w_i, b_i, lam):
    B_, T, C = xb.shape
    f32 = jnp.float32
    xc = lax.conv_general_dilated(
        xb.astype(f32), conv_w.astype(f32)[:, None, :], window_strides=(1,), padding=[(CONV_W - 1, 0)],
        dimension_numbers=('NWC', 'WIO', 'NWC'), feature_group_count=C) + conv_b.astype(f32)
    xg = xc.reshape(B_, T, LRU_BLOCKS, LRU_BW)
    r = jax.nn.sigmoid(jnp.einsum('btnc,ncd->btnd', xg, w_r.astype(f32)).reshape(B_, T, C) + b_r.astype(f32))
    i = jax.nn.sigmoid(jnp.einsum('btnc,ncd->btnd', xg, w_i.astype(f32)).reshape(B_, T, C) + b_i.astype(f32))
    log_a = -LRU_C * r * jax.nn.softplus(-lam.astype(f32))
    a = jnp.exp(log_a)
    bx = jnp.sqrt(-jnp.expm1(2.0 * log_a)) * (i * xc)

    def combine(e1, e2):
        a1, b1 = e1
        a2, b2 = e2
        return a1 * a2, a2 * b1 + b2

    _, h = lax.associative_scan(combine, (a, bx), axis=1)
    return h.astype(xb.dtype)


def hybrid_layer(x, positions, norm_g, w_in, w_out, gla_w_up, gla_b_up, gla_g_out, fox_b_f, fox_g_q, fox_g_k,
                 mla_g_cq, mla_w_uq, mla_g_ckv, mla_w_ukv, mla_g_q, mla_g_k,
                 lru_conv_w, lru_conv_b, lru_w_r, lru_b_r, lru_w_i, lru_b_i, lru_lam):
    h = rms_norm(x, norm_g)
    proj = h @ w_in
    points = np.cumsum(np.array(IN_SIZES))[:-1].tolist()
    (gate, a_q, a_k, a_v, a_g, b_q, b_k, b_v, b_f, c_q, c_kv, c_kr, d_x) = jnp.split(proj, points, axis=-1)
    ya = gla_mixer(a_q, a_k, a_v, a_g, gla_w_up, gla_b_up, gla_g_out)
    yb = fox_mixer(b_q, b_k, b_v, b_f, fox_b_f, fox_g_q, fox_g_k)
    yc = mla_mixer(c_q, c_kv, c_kr, positions, mla_g_cq, mla_w_uq, mla_g_ckv, mla_w_ukv, mla_g_q, mla_g_k)
    yd = rglru_mixer(d_x, lru_conv_w, lru_conv_b, lru_w_r, lru_b_r, lru_w_i, lru_b_i, lru_lam)
    y = jnp.concatenate([ya, yb, yc, yd], axis=-1).astype(proj.dtype) * jax.nn.silu(gate)
    return x + (y @ w_out).astype(x.dtype)


def setup_inputs(seed: int = 0) -> dict:
    key = jax.random.key(seed)
    ks = jax.random.split(key, 32)
    f32 = jnp.float32

    def nrm(k, shape, fan_in):
        return jax.random.normal(k, shape, f32) * fan_in ** -0.5

    def gain(k, shape):
        return 1.0 + 0.05 * jax.random.normal(k, shape, f32)

    def bias(k, shape):
        return 0.02 * jax.random.normal(k, shape, f32)

    L = DEPTH
    x = jax.random.normal(ks[0], (BATCH, SEQ, D_MODEL), f32)
    offset = jax.random.randint(ks[1], (BATCH, 1), 0, 1024, dtype=jnp.int32)
    positions = offset + jnp.arange(SEQ, dtype=jnp.int32)[None, :]
    u = jax.random.uniform(ks[23], (L, GROUP_W), f32, 0.9, 0.999)
    a0 = u ** (1.0 / LRU_C)
    lru_lam = jnp.log(a0) - jnp.log1p(-a0)
    return {
        "x": x,
        "positions": positions,
        "norm_g": gain(ks[2], (L, D_MODEL)),
        "w_in": nrm(ks[3], (L, D_MODEL, D_IN), D_MODEL),
        "w_out": nrm(ks[4], (L, D_MIX, D_MODEL), D_MIX),
        "gla_w_up": nrm(ks[5], (L, GLA_RANK, N_HEADS * GLA_DK), GLA_RANK),
        "gla_b_up": bias(ks[6], (L, N_HEADS * GLA_DK)),
        "gla_g_out": gain(ks[7], (L, GLA_DV)),
        "fox_b_f": jax.random.uniform(ks[8], (L, N_HEADS), f32, 1.0, 4.0),
        "fox_g_q": gain(ks[9], (L, HEAD_DIM)),
        "fox_g_k": gain(ks[10], (L, HEAD_DIM)),
        "mla_g_cq": gain(ks[11], (L, MLA_Q_RANK)),
        "mla_w_uq": nrm(ks[12], (L, MLA_Q_RANK, N_HEADS * (MLA_NOPE + MLA_ROPE)), MLA_Q_RANK),
        "mla_g_ckv": gain(ks[13], (L, MLA_KV_RANK)),
        "mla_w_ukv": nrm(ks[14], (L, MLA_KV_RANK, N_HEADS * (MLA_NOPE + MLA_V)), MLA_KV_RANK),
        "mla_g_q": gain(ks[15], (L, MLA_NOPE + MLA_ROPE)),
        "mla_g_k": gain(ks[16], (L, MLA_NOPE + MLA_ROPE)),
        "lru_conv_w": nrm(ks[17], (L, CONV_W, GROUP_W), CONV_W),
        "lru_conv_b": bias(ks[18], (L, GROUP_W)),
        "lru_w_r": nrm(ks[19], (L, LRU_BLOCKS, LRU_BW, LRU_BW), LRU_BW),
        "lru_b_r": bias(ks[20], (L, GROUP_W)),
        "lru_w_i": nrm(ks[21], (L, LRU_BLOCKS, LRU_BW, LRU_BW), LRU_BW),
        "lru_b_i": bias(ks[22], (L, GROUP_W)),
        "lru_lam": lru_lam,
    }


def reference(x, positions, norm_g, w_in, w_out, gla_w_up, gla_b_up, gla_g_out, fox_b_f, fox_g_q, fox_g_k,
              mla_g_cq, mla_w_uq, mla_g_ckv, mla_w_ukv, mla_g_q, mla_g_k,
              lru_conv_w, lru_conv_b, lru_w_r, lru_b_r, lru_w_i, lru_b_i, lru_lam):
    for l in range(DEPTH):
        x = hybrid_layer(x, positions, norm_g[l], w_in[l], w_out[l], gla_w_up[l], gla_b_up[l], gla_g_out[l],
                         fox_b_f[l], fox_g_q[l], fox_g_k[l], mla_g_cq[l], mla_w_uq[l], mla_g_ckv[l], mla_w_ukv[l],
                         mla_g_q[l], mla_g_k[l], lru_conv_w[l], lru_conv_b[l], lru_w_r[l], lru_b_r[l],
                         lru_w_i[l], lru_b_i[l], lru_lam[l])
    return x
```

```python
import functools
import math

import numpy as np
import jax
import jax.numpy as jnp
from jax import lax
from jax.experimental import pallas as pl
from jax.experimental.pallas import tpu as pltpu

F32 = jnp.float32
BF16 = jnp.bfloat16

D_MODEL = 1024
D_MIX = 2048
GROUP_W = 512
N_HEADS = 8
HEAD_DIM = 64
EPS = 1e-6
GLA_DK = 32
GLA_RANK = 16
GLA_NORMALIZER = 16.0
GLA_CHUNK = 64
MLA_Q_RANK = 256
MLA_KV_RANK = 128
MLA_NOPE = 64
MLA_ROPE = 32
ROPE_THETA = 10000.0
CONV_W = 4
LRU_C = 8.0
LOG2E = 1.4426950408889634
NEG_BIG = -1e30

LANES = 128
HEAD_PAD = 128

COL_GATE = 0
COL_AQ, COL_AK, COL_AV = 2048, 2304, 2560
COL_BQ, COL_BK, COL_BV = 3072, 3584, 4096
COL_CQ, COL_CKV, COL_CKR = 4608, 4864, 4992
COL_DX = 5120
D_PROJ = 5632
SM_AG, SM_BF = 0, 16


def _cparams(sem, vmem_mb=48):
    return pltpu.CompilerParams(dimension_semantics=sem, vmem_limit_bytes=vmem_mb * 1024 * 1024)


def _sigmoid(z):
    return 1.0 / (1.0 + jnp.exp(-z))


def _log_sigmoid(z):
    return jnp.minimum(z, 0.0) - jnp.log1p(jnp.exp(-jnp.abs(z)))


def _row_prefix_sum(v, seg):
    rows = v.shape[0]
    rid = lax.broadcasted_iota(jnp.int32, v.shape, 0) % seg
    s = 1
    while s < seg:
        v = v + jnp.where(rid >= s, pltpu.roll(v, s, 0), 0.0)
        s *= 2
    del rows
    return v


def _inproj_kernel(x_ref, g_ref, w_ref, ws_ref, o_ref, os_ref, h_ref):
    @pl.when(pl.program_id(1) == 0)
    def _():
        x = x_ref[...]
        ms = jnp.mean(x * x, axis=-1, keepdims=True)
        h = (x * lax.rsqrt(ms + EPS) * g_ref[...]).astype(BF16)
        h_ref[...] = h
        os_ref[...] = jnp.dot(h, ws_ref[...], preferred_element_type=F32)

    o_ref[...] = jnp.dot(h_ref[...], w_ref[...], preferred_element_type=F32).astype(BF16)


def _norm_inproj(x, g, w_all, ws_all, layer):
    n = x.shape[0]
    tm = min(1024, n)
    tn = 1408
    return pl.pallas_call(
        _inproj_kernel,
        grid=(n // tm, D_PROJ // tn),
        in_specs=[
            pl.BlockSpec((tm, D_MODEL), lambda i, j: (i, 0)),
            pl.BlockSpec((1, D_MODEL), lambda i, j: (0, 0)),
            pl.BlockSpec((None, D_MODEL, tn), lambda i, j: (layer, 0, j)),
            pl.BlockSpec((None, D_MODEL, LANES), lambda i, j: (layer, 0, 0)),
        ],
        out_specs=[
            pl.BlockSpec((tm, tn), lambda i, j: (i, j)),
            pl.BlockSpec((tm, LANES), lambda i, j: (i, 0)),
        ],
        out_shape=[jax.ShapeDtypeStruct((n, D_PROJ), BF16), jax.ShapeDtypeStruct((n, LANES), F32)],
        scratch_shapes=[pltpu.VMEM((tm, D_MODEL), BF16)],
        compiler_params=_cparams(("parallel", "arbitrary")),
        name="norm_inproj",
    )(x, g, w_all, ws_all)


def _outproj_kernel(ya_ref, yb_ref, yc_ref, yd_ref, gate_ref, w_ref, x_ref, o_ref, z_ref):
    for gi, y_ref in enumerate((ya_ref, yb_ref, yc_ref, yd_ref)):
        lo, hi = gi * GROUP_W, (gi + 1) * GROUP_W
        g = gate_ref[:, lo:hi].astype(F32)
        z_ref[:, lo:hi] = (y_ref[...].astype(F32) * (g * _sigmoid(g))).astype(BF16)
    o_ref[...] = x_ref[...] + jnp.dot(z_ref[...], w_ref[...], preferred_element_type=F32)


def _gate_outproj(ya, yb, yc, yd, proj, w_all, x, layer):
    n = x.shape[0]
    tm = min(512, n)
    yspec = pl.BlockSpec((tm, GROUP_W), lambda i: (i, 0))
    return pl.pallas_call(
        _outproj_kernel,
        grid=(n // tm,),
        in_specs=[
            yspec, yspec, yspec, yspec,
            pl.BlockSpec((tm, D_MIX), lambda i: (i, 0)),
            pl.BlockSpec((None, D_MIX, D_MODEL), lambda i: (layer, 0, 0)),
            pl.BlockSpec((tm, D_MODEL), lambda i: (i, 0)),
        ],
        out_specs=pl.BlockSpec((tm, D_MODEL), lambda i: (i, 0)),
        out_shape=jax.ShapeDtypeStruct((n, D_MODEL), F32),
        scratch_shapes=[pltpu.VMEM((tm, D_MIX), BF16)],
        compiler_params=_cparams(("parallel",)),
        name="gate_outproj",
    )(ya, yb, yc, yd, proj, w_all, x)


def _attn_kernel(q_ref, k_ref, v_ref, o_ref, *, tq):
    i = pl.program_id(2)
    row = lax.broadcasted_iota(jnp.int32, (tq, tq), 0)
    col = lax.broadcasted_iota(jnp.int32, (tq, tq), 1)
    causal = row >= col
    outs = []
    for h in range(2):
        q = q_ref[:, h * HEAD_PAD:(h + 1) * HEAD_PAD]

        def step(j, carry, masked, h=h, q=q):
            m, l, acc = carry
            off = pl.multiple_of(j * tq, tq)
            kb = k_ref[pl.ds(off, tq), h * HEAD_PAD:(h + 1) * HEAD_PAD]
            vb = v_ref[pl.ds(off, tq), :]
            s = lax.dot_general(q, kb, (((1,), (1,)), ((), ())), preferred_element_type=F32)
            if masked:
                s = jnp.where(causal, s, NEG_BIG)
            m_new = jnp.maximum(m, jnp.max(s, axis=-1, keepdims=True))
            alpha = jnp.exp2(m - m_new)
            p = jnp.exp2(s - m_new)
            l = alpha * l + jnp.sum(p, axis=-1, keepdims=True)
            acc = alpha * acc + jnp.dot(p.astype(BF16), vb, preferred_element_type=F32)
            return m_new, l, acc

        init = (jnp.full((tq, 1), NEG_BIG, F32), jnp.zeros((tq, 1), F32), jnp.zeros((tq, LANES), F32))
        carry = lax.fori_loop(0, i, functools.partial(step, masked=False), init)
        m, l, acc = step(i, carry, True)
        outs.append(acc / l)
    lane = lax.broadcasted_iota(jnp.int32, (tq, LANES), 1)
    o_ref[...] = jnp.where(lane < HEAD_DIM, outs[0], outs[1]).astype(BF16)


def _attention(q, k, v, v_col0, batch, seq):
    n = q.shape[0]
    tq = min(256, seq)
    nq = seq // tq
    vblk0 = v_col0 // LANES
    return pl.pallas_call(
        functools.partial(_attn_kernel, tq=tq),
        grid=(batch, N_HEADS // 2, nq),
        in_specs=[
            pl.BlockSpec((tq, 2 * HEAD_PAD), lambda b, p, i: (b * nq + i, p)),
            pl.BlockSpec((seq, 2 * HEAD_PAD), lambda b, p, i: (b, p)),
            pl.BlockSpec((seq, LANES), lambda b, p, i: (b, vblk0 + p)),
        ],
        out_specs=pl.BlockSpec((tq, LANES), lambda b, p, i: (b * nq + i, p)),
        out_shape=jax.ShapeDtypeStruct((n, GROUP_W), BF16),
        compiler_params=_cparams(("parallel", "parallel", "arbitrary")),
        name="causal_attention",
    )(q, k, v)


def _fox_prep_kernel(q_ref, k_ref, sm_ref, bf_ref, gq_ref, gk_ref, ones_ref, qo_ref, ko_ref, carry_ref, *, tt):
    @pl.when(pl.program_id(1) == 0)
    def _():
        carry_ref[...] = jnp.zeros_like(carry_ref)

    def headnorm(ref, g_ref):
        z = ref[...].astype(F32)
        ssq = jnp.dot((z * z).astype(BF16), ones_ref[...], preferred_element_type=F32)
        return z * lax.rsqrt(ssq * (1.0 / HEAD_DIM) + EPS) * g_ref[...]

    qn = headnorm(q_ref, gq_ref)
    kn = headnorm(k_ref, gk_ref)

    log_f = _log_sigmoid(sm_ref[...] + bf_ref[...])
    cum = _row_prefix_sum(log_f, tt) + carry_ref[0:1, :]
    carry_ref[0:1, :] = cum[tt - 1:tt, :]
    neg = cum * (-LOG2E)

    lane = lax.broadcasted_iota(jnp.int32, (tt, LANES), 1)
    for h in range(N_HEADS):
        pair, e = h // 2, h % 2
        in_head = (lane >= e * HEAD_DIM) & (lane < (e + 1) * HEAD_DIM)
        base = HEAD_DIM if e == 0 else 0
        cb = jnp.broadcast_to(neg[:, SM_BF + h:SM_BF + h + 1], (tt, LANES))
        hi = cb.astype(BF16).astype(F32)
        r1 = cb - hi
        mid = r1.astype(BF16).astype(F32)
        lo = r1 - mid
        aug_k = jnp.where(lane == base, hi, jnp.where(lane == base + 1, mid, jnp.where(lane == base + 2, lo, 0.0)))
        aug_q = jnp.where((lane >= base) & (lane < base + 3), 1.0, 0.0)
        kpair = kn[:, pair * LANES:(pair + 1) * LANES]
        qpair = qn[:, pair * LANES:(pair + 1) * LANES]
        ko_ref[:, h * HEAD_PAD:(h + 1) * HEAD_PAD] = jnp.where(in_head, kpair, aug_k).astype(BF16)
        qo_ref[:, h * HEAD_PAD:(h + 1) * HEAD_PAD] = jnp.where(in_head, qpair, aug_q).astype(BF16)


def _fox_prep(proj, small, bf_row, gq, gk, ones64, batch, seq):
    n = proj.shape[0]
    tt = min(512, seq)
    nt = seq // tt
    row = lambda shape: pl.BlockSpec(shape, lambda b, t: (0, 0))
    return pl.pallas_call(
        functools.partial(_fox_prep_kernel, tt=tt),
        grid=(batch, nt),
        in_specs=[
            pl.BlockSpec((tt, GROUP_W), lambda b, t: (b * nt + t, COL_BQ // GROUP_W)),
            pl.BlockSpec((tt, GROUP_W), lambda b, t: (b * nt + t, COL_BK // GROUP_W)),
            pl.BlockSpec((tt, LANES), lambda b, t: (b * nt + t, 0)),
            row((1, LANES)), row((1, GROUP_W)), row((1, GROUP_W)), row((GROUP_W, GROUP_W)),
        ],
        out_specs=[
            pl.BlockSpec((tt, N_HEADS * HEAD_PAD), lambda b, t: (b * nt + t, 0)),
            pl.BlockSpec((tt, N_HEADS * HEAD_PAD), lambda b, t: (b * nt + t, 0)),
        ],
        out_shape=[jax.ShapeDtypeStruct((n, N_HEADS * HEAD_PAD), BF16)] * 2,
        scratch_shapes=[pltpu.VMEM((8, LANES), F32)],
        compiler_params=_cparams(("parallel", "arbitrary")),
        name="fox_prep",
    )(proj, proj, small, bf_row, gq, gk, ones64)


def _rope_kernel(pos_ref, invf_ref, c_ref, s1_ref, s2_ref):
    ang = pos_ref[...].astype(F32) * invf_ref[...]
    lane = lax.broadcasted_iota(jnp.int32, ang.shape, 1)
    half = MLA_ROPE // 2
    c, s = jnp.cos(ang), jnp.sin(ang)
    c_ref[...] = jnp.where((lane >= MLA_NOPE) & (lane < MLA_NOPE + MLA_ROPE), c, 1.0)
    s1_ref[...] = jnp.where((lane >= MLA_NOPE) & (lane < MLA_NOPE + half), -s, 0.0)
    s2_ref[...] = jnp.where((lane >= MLA_NOPE + half) & (lane < MLA_NOPE + MLA_ROPE), s, 0.0)


def _rope_tables(pos_col, invf_row):
    n = pos_col.shape[0]
    tm = min(1024, n)
    out = pl.BlockSpec((tm, LANES), lambda i: (i, 0))
    return pl.pallas_call(
        _rope_kernel,
        grid=(n // tm,),
        in_specs=[pl.BlockSpec((tm, 1), lambda i: (i, 0)), pl.BlockSpec((1, LANES), lambda i: (0, 0))],
        out_specs=[out, out, out],
        out_shape=[jax.ShapeDtypeStruct((n, LANES), F32)] * 3,
        compiler_params=_cparams(("parallel",)),
        name="rope_tables",
    )(pos_col, invf_row)


def _mla_prep_kernel(cq_ref, ckv_ref, ckr_ref, c_ref, s1_ref, s2_ref, gcq_ref, wuq_ref, gckv_ref, wuk_ref, wuv_ref,
                     gq_ref, gk_ref, ones_ref, qo_ref, ko_ref, vo_ref):
    d_qk = float(MLA_NOPE + MLA_ROPE)
    cos, s1, s2 = c_ref[...], s1_ref[...], s2_ref[...]
    half = MLA_ROPE // 2

    def rms(z, g):
        return (z * lax.rsqrt(jnp.mean(z * z, axis=-1, keepdims=True) + EPS) * g).astype(BF16)

    def head_ssq(z):
        zz = (z * z).astype(BF16)
        parts = [jnp.dot(zz[:, c * 256:(c + 1) * 256], ones_ref[...], preferred_element_type=F32) for c in range(4)]
        return jnp.concatenate(parts, axis=1)

    def rope(zh):
        return zh * cos + pltpu.roll(zh, HEAD_PAD - half, 1) * s1 + pltpu.roll(zh, half, 1) * s2

    q = jnp.dot(rms(cq_ref[...].astype(F32), gcq_ref[...]), wuq_ref[...], preferred_element_type=F32)
    qn = q * lax.rsqrt(head_ssq(q) * (1.0 / d_qk) + EPS) * gq_ref[...]

    kvn = rms(ckv_ref[...].astype(F32), gckv_ref[...])
    kn = jnp.dot(kvn, wuk_ref[...], preferred_element_type=F32)
    vo_ref[...] = jnp.dot(kvn, wuv_ref[...], preferred_element_type=F32).astype(BF16)
    kr = ckr_ref[...].astype(F32)
    ss_r = jnp.sum(kr * kr, axis=-1, keepdims=True)
    kr_placed = pltpu.roll(kr, MLA_NOPE, 1)
    k_scale = lax.rsqrt((head_ssq(kn) + ss_r) * (1.0 / d_qk) + EPS) * gk_ref[...]

    for h in range(N_HEADS):
        sl = slice(h * HEAD_PAD, (h + 1) * HEAD_PAD)
        qo_ref[:, sl] = rope(qn[:, sl]).astype(BF16)
        ko_ref[:, sl] = rope((kn[:, sl] + kr_placed) * k_scale[:, sl]).astype(BF16)


def _mla_prep(proj, cos_t, s1_t, s2_t, gcq, wuq, gckv, wuk, wuv, gq, gk, ones128):
    n = proj.shape[0]
    tm = min(512, n)
    full = lambda shape: pl.BlockSpec(shape, lambda i: (0, 0))
    tab = pl.BlockSpec((tm, LANES), lambda i: (i, 0))
    wide = pl.BlockSpec((tm, N_HEADS * HEAD_PAD), lambda i: (i, 0))
    return pl.pallas_call(
        _mla_prep_kernel,
        grid=(n // tm,),
        in_specs=[
            pl.BlockSpec((tm, MLA_Q_RANK), lambda i: (i, COL_CQ // MLA_Q_RANK)),
            pl.BlockSpec((tm, MLA_KV_RANK), lambda i: (i, COL_CKV // LANES)),
            pl.BlockSpec((tm, LANES), lambda i: (i, COL_CKR // LANES)),
            tab, tab, tab,
            full((1, MLA_Q_RANK)), full((MLA_Q_RANK, N_HEADS * HEAD_PAD)),
            full((1, MLA_KV_RANK)), full((MLA_KV_RANK, N_HEADS * HEAD_PAD)), full((MLA_KV_RANK, GROUP_W)),
            full((1, N_HEADS * HEAD_PAD)), full((1, N_HEADS * HEAD_PAD)), full((256, 256)),
        ],
        out_specs=[wide, wide, pl.BlockSpec((tm, GROUP_W), lambda i: (i, 0))],
        out_shape=[jax.ShapeDtypeStruct((n, N_HEADS * HEAD_PAD), BF16)] * 2 + [jax.ShapeDtypeStruct((n, GROUP_W), BF16)],
        compiler_params=_cparams(("parallel",)),
        name="mla_prep",
    )(proj, proj, proj, cos_t, s1_t, s2_t, gcq, wuq, gckv, wuk, wuv, gq, gk, ones128)


def _gla_kernel(q_ref, k_ref, v_ref, sm_ref, wh_ref, wl_ref, b_ref, go_ref, ones_ref, o_ref, st_ref, *, tc):
    hk = N_HEADS * GLA_DK
    ck = GLA_CHUNK

    @pl.when(pl.program_id(1) == 0)
    def _():
        st_ref[...] = jnp.zeros_like(st_ref)

    sm = sm_ref[...]
    sm_hi = sm.astype(BF16)
    sm_lo = (sm - sm_hi.astype(F32)).astype(BF16)
    z = (jnp.dot(sm_hi, wh_ref[...], preferred_element_type=F32)
         + jnp.dot(sm_lo, wh_ref[...], preferred_element_type=F32)
         + jnp.dot(sm_hi, wl_ref[...], preferred_element_type=F32)) + b_ref[...]
    g_all = _row_prefix_sum(_log_sigmoid(z) * (1.0 / GLA_NORMALIZER), ck)

    lane_k = lax.broadcasted_iota(jnp.int32, (ck, hk), 1) // GLA_DK
    lane_v = lax.broadcasted_iota(jnp.int32, (ck, GROUP_W), 1) // HEAD_DIM
    a_row = lax.broadcasted_iota(jnp.int32, (N_HEADS * ck, ck), 0) % ck
    a_col = lax.broadcasted_iota(jnp.int32, (N_HEADS * ck, ck), 1)
    st_blockdiag = (lax.broadcasted_iota(jnp.int32, (GROUP_W, hk), 0) // HEAD_DIM
                    == lax.broadcasted_iota(jnp.int32, (GROUP_W, hk), 1) // GLA_DK)

    outs = []
    for c in range(tc // ck):
        rows = slice(c * ck, (c + 1) * ck)
        g = g_all[rows]
        q = q_ref[rows, :].astype(F32) * (GLA_DK ** -0.5)
        k = k_ref[rows, :].astype(F32)
        v = v_ref[rows, :]
        g_last = g[ck - 1:ck, :]
        qt = (q * jnp.exp(g)).astype(BF16)
        kt = (k * jnp.exp(-g)).astype(BF16)
        kh = (k * jnp.exp(g_last - g)).astype(BF16)

        q_stack = jnp.concatenate([jnp.where(lane_k == h, qt, jnp.zeros_like(qt)) for h in range(N_HEADS)], axis=0)
        a = lax.dot_general(q_stack, kt, (((1,), (1,)), ((), ())), preferred_element_type=F32)
        a = jnp.where(a_row >= a_col, a, 0.0).astype(BF16)
        r = jnp.dot(a, v, preferred_element_type=F32)
        o = lax.dot_general(qt, st_ref[...].astype(BF16), (((1,), (1,)), ((), ())), preferred_element_type=F32)
        for h in range(N_HEADS):
            o = o + jnp.where(lane_v == h, r[h * ck:(h + 1) * ck, :], 0.0)
        outs.append(o)

        u = lax.dot_general(v, kh, (((0,), (0,)), ((), ())), preferred_element_type=F32)
        st_ref[...] = st_ref[...] * jnp.exp(g_last) + jnp.where(st_blockdiag, u, 0.0)

    o = jnp.concatenate(outs, axis=0)
    ssq = jnp.dot((o * o).astype(BF16), ones_ref[...], preferred_element_type=F32)
    o_ref[...] = (o * lax.rsqrt(ssq * (1.0 / HEAD_DIM) + EPS) * go_ref[...]).astype(BF16)


def _gla(proj, small, w_hi, w_lo, b_up, g_out, ones64, batch, seq):
    n = proj.shape[0]
    tc = min(256, seq)
    nt = seq // tc
    hk = N_HEADS * GLA_DK
    full = lambda shape: pl.BlockSpec(shape, lambda b, t: (0, 0))
    return pl.pallas_call(
        functools.partial(_gla_kernel, tc=tc),
        grid=(batch, nt),
        in_specs=[
            pl.BlockSpec((tc, hk), lambda b, t: (b * nt + t, COL_AQ // hk)),
            pl.BlockSpec((tc, hk), lambda b, t: (b * nt + t, COL_AK // hk)),
            pl.BlockSpec((tc, GROUP_W), lambda b, t: (b * nt + t, COL_AV // GROUP_W)),
            pl.BlockSpec((tc, LANES), lambda b, t: (b * nt + t, 0)),
            full((LANES, hk)), full((LANES, hk)), full((1, hk)), full((1, GROUP_W)), full((GROUP_W, GROUP_W)),
        ],
        out_specs=pl.BlockSpec((tc, GROUP_W), lambda b, t: (b * nt + t, 0)),
        out_shape=jax.ShapeDtypeStruct((n, GROUP_W), BF16),
        scratch_shapes=[pltpu.VMEM((GROUP_W, hk), F32)],
        compiler_params=_cparams(("parallel", "arbitrary")),
        name="gla",
    )(proj, proj, proj, small, w_hi, w_lo, b_up, g_out, ones64)


def _lru_kernel(x_ref, cw_ref, cb_ref, wr_ref, br_ref, wi_ref, bi_ref, lam_ref, o_ref, xpad_ref, h_ref, *, tr):
    @pl.when(pl.program_id(1) == 0)
    def _():
        xpad_ref[0:8, :] = jnp.zeros((8, GROUP_W), F32)
        h_ref[...] = jnp.zeros_like(h_ref)

    xpad_ref[8:8 + tr, :] = x_ref[...].astype(F32)
    xc = cb_ref[...]
    for j in range(CONV_W):
        off = 8 - (CONV_W - 1) + j
        xc = xc + cw_ref[j:j + 1, :] * xpad_ref[off:off + tr, :]
    xpad_ref[0:8, :] = xpad_ref[tr:tr + 8, :]

    xb = xc.astype(BF16)
    r = _sigmoid(jnp.dot(xb, wr_ref[...], preferred_element_type=F32) + br_ref[...])
    i = _sigmoid(jnp.dot(xb, wi_ref[...], preferred_element_type=F32) + bi_ref[...])
    lam = lam_ref[...]
    softplus = jnp.maximum(-lam, 0.0) + jnp.log1p(jnp.exp(-jnp.abs(lam)))
    log_a = (-LRU_C) * r * softplus
    a = jnp.exp(log_a)
    bx = jnp.sqrt(1.0 - jnp.exp(2.0 * log_a)) * (i * xc)

    rid = lax.broadcasted_iota(jnp.int32, (tr, GROUP_W), 0)
    s = 1
    while s < tr:
        keep = rid >= s
        a_prev = jnp.where(keep, pltpu.roll(a, s, 0), 1.0)
        b_prev = jnp.where(keep, pltpu.roll(bx, s, 0), 0.0)
        bx = a * b_prev + bx
        a = a * a_prev
        s *= 2
    h = a * h_ref[0:1, :] + bx
    h_ref[0:1, :] = h[tr - 1:tr, :]
    o_ref[...] = h.astype(BF16)


def _lru(proj, conv_w, conv_b, wr, br, wi, bi, lam, batch, seq):
    n = proj.shape[0]
    tr = min(256, seq)
    nt = seq // tr
    full = lambda shape: pl.BlockSpec(shape, lambda b, t: (0, 0))
    return pl.pallas_call(
        functools.partial(_lru_kernel, tr=tr),
        grid=(batch, nt),
        in_specs=[
            pl.BlockSpec((tr, GROUP_W), lambda b, t: (b * nt + t, COL_DX // GROUP_W)),
            full((8, GROUP_W)), full((1, GROUP_W)),
            full((GROUP_W, GROUP_W)), full((1, GROUP_W)),
            full((GROUP_W, GROUP_W)), full((1, GROUP_W)), full((1, GROUP_W)),
        ],
        out_specs=pl.BlockSpec((tr, GROUP_W), lambda b, t: (b * nt + t, 0)),
        out_shape=jax.ShapeDtypeStruct((n, GROUP_W), BF16),
        scratch_shapes=[pltpu.VMEM((tr + 8, GROUP_W), F32), pltpu.VMEM((8, GROUP_W), F32)],
        compiler_params=_cparams(("parallel", "arbitrary")),
        name="rglru",
    )(proj, conv_w, conv_b, wr, br, wi, bi, lam)


def _pad_cols(w, width):
    return jnp.pad(w, [(0, 0)] * (w.ndim - 1) + [(0, width - w.shape[-1])])


def _block_diag_ones(n_blocks, size):
    return jnp.asarray(np.kron(np.eye(n_blocks, dtype=np.float32), np.ones((size, size), np.float32)), BF16)


def _pad_heads(w, d_real):
    lead = w.shape[:-1]
    w = w.reshape(lead + (N_HEADS, d_real))
    w = jnp.pad(w, [(0, 0)] * len(lead) + [(0, 0), (0, HEAD_PAD - d_real)])
    return w.reshape(lead + (N_HEADS * HEAD_PAD,))


def kernel(x, positions, norm_g, w_in, w_out, gla_w_up, gla_b_up, gla_g_out, fox_b_f, fox_g_q, fox_g_k, mla_g_cq, mla_w_uq, mla_g_ckv, mla_w_ukv, mla_g_q, mla_g_k, lru_conv_w, lru_conv_b, lru_w_r, lru_b_r, lru_w_i, lru_b_i, lru_lam):
    batch, seq, _ = x.shape
    depth = w_in.shape[0]
    n = batch * seq

    sizes = (D_MIX, 256, 256, 512, GLA_RANK, 512, 512, 512, N_HEADS, MLA_Q_RANK, MLA_KV_RANK, MLA_ROPE, GROUP_W)
    offs = np.concatenate([[0], np.cumsum(sizes)])
    seg = [w_in[:, :, offs[i]:offs[i + 1]] for i in range(len(sizes))]
    (w_gate, w_aq, w_ak, w_av, w_ag, w_bq, w_bk, w_bv, w_bf, w_cq, w_ckv, w_ckr, w_dx) = seg
    w_main = jnp.concatenate(
        [w_gate, w_aq, w_ak, w_av, w_bq, w_bk, w_bv, w_cq, w_ckv, _pad_cols(w_ckr, LANES), w_dx], axis=-1).astype(BF16)
    w_small = _pad_cols(jnp.concatenate([w_ag, w_bf], axis=-1), LANES).astype(BF16)
    w_out_b = w_out.astype(BF16)

    ones64 = _block_diag_ones(GROUP_W // HEAD_DIM, HEAD_DIM)
    ones128 = _block_diag_ones(2, HEAD_PAD)

    wup = jnp.pad(gla_w_up, ((0, 0), (0, LANES - GLA_RANK), (0, 0)))
    wup_hi = wup.astype(BF16)
    wup_lo = (wup - wup_hi.astype(F32)).astype(BF16)

    fox_scale = HEAD_DIM ** -0.5 * LOG2E
    mla_scale = (MLA_NOPE + MLA_ROPE) ** -0.5 * LOG2E
    fox_gq = jnp.tile(fox_g_q, (1, N_HEADS)) * fox_scale
    fox_gk = jnp.tile(fox_g_k, (1, N_HEADS))
    fox_bf = jnp.pad(fox_b_f, ((0, 0), (SM_BF, LANES - SM_BF - N_HEADS)))

    wuq = _pad_heads(mla_w_uq, MLA_NOPE + MLA_ROPE).astype(BF16)
    wukv = mla_w_ukv.reshape(depth, MLA_KV_RANK, N_HEADS, 2 * HEAD_DIM)
    wuk = _pad_heads(wukv[..., :MLA_NOPE].reshape(depth, MLA_KV_RANK, N_HEADS * MLA_NOPE), MLA_NOPE).astype(BF16)
    wuv = wukv[..., MLA_NOPE:].reshape(depth, MLA_KV_RANK, GROUP_W).astype(BF16)
    mla_gq = _pad_heads(jnp.tile(mla_g_q, (1, N_HEADS)), MLA_NOPE + MLA_ROPE) * mla_scale
    mla_gk = _pad_heads(jnp.tile(mla_g_k, (1, N_HEADS)), MLA_NOPE + MLA_ROPE)

    eye = jnp.eye(N_HEADS, dtype=F32)
    wr_bd = jnp.einsum("lncd,nm->lncmd", lru_w_r, eye).reshape(depth, GROUP_W, GROUP_W).astype(BF16)
    wi_bd = jnp.einsum("lncd,nm->lncmd", lru_w_i, eye).reshape(depth, GROUP_W, GROUP_W).astype(BF16)
    conv_w = jnp.pad(lru_conv_w, ((0, 0), (0, 8 - CONV_W), (0, 0)))

    half = MLA_ROPE // 2
    inv_freq = ROPE_THETA ** (-jnp.arange(half, dtype=F32) / half)
    invf_row = jnp.pad(jnp.concatenate([inv_freq, inv_freq]), (MLA_NOPE, LANES - MLA_NOPE - MLA_ROPE))[None, :]
    cos_t, s1_t, s2_t = _rope_tables(positions.reshape(n, 1), invf_row)

    xf = x.reshape(n, D_MODEL)
    for l in range(depth):
        proj, small = _norm_inproj(xf, norm_g[l][None, :], w_main, w_small, l)
        ya = _gla(proj, small, wup_hi[l], wup_lo[l], gla_b_up[l][None, :],
                  jnp.tile(gla_g_out[l], N_HEADS)[None, :], ones64, batch, seq)
        fq, fk = _fox_prep(proj, small, fox_bf[l][None, :], fox_gq[l][None, :], fox_gk[l][None, :], ones64, batch, seq)
        yb = _attention(fq, fk, proj, COL_BV, batch, seq)
        mq, mk, mv = _mla_prep(proj, cos_t, s1_t, s2_t, mla_g_cq[l][None, :], wuq[l], mla_g_ckv[l][None, :],
                               wuk[l], wuv[l], mla_gq[l][None, :], mla_gk[l][None, :], ones128)
        yc = _attention(mq, mk, mv, 0, batch, seq)
        yd = _lru(proj, conv_w[l], lru_conv_b[l][None, :], wr_bd[l], lru_b_r[l][None, :], wi_bd[l],
                  lru_b_i[l][None, :], lru_lam[l][None, :], batch, seq)
        xf = _gate_outproj(ya, yb, yc, yd, proj, w_out_b, xf, l)
    return xf.reshape(batch, seq, D_MODEL)
```

```python
import functools
import math

import numpy as np
import jax
import jax.numpy as jnp
from jax import lax
from jax.experimental import pallas as pl
from jax.experimental.pallas import tpu as pltpu

F32 = jnp.float32
BF16 = jnp.bfloat16

D_MODEL = 1024
D_MIX = 2048
GROUP_W = 512
N_HEADS = 8
HEAD_DIM = 64
EPS = 1e-6
GLA_DK = 32
GLA_RANK = 16
GLA_NORMALIZER = 16.0
GLA_CHUNK = 64
MLA_Q_RANK = 256
MLA_KV_RANK = 128
MLA_NOPE = 64
MLA_ROPE = 32
ROPE_THETA = 10000.0
CONV_W = 4
LRU_C = 8.0
LOG2E = 1.4426950408889634
NEG_BIG = -1e30

LANES = 128
HEAD_PAD = 128

COL_GATE = 0
COL_AQ, COL_AK, COL_AV = 2048, 2304, 2560
COL_BQ, COL_BK, COL_BV = 3072, 3584, 4096
COL_CQ, COL_CKV, COL_CKR = 4608, 4864, 4992
COL_DX = 5120
D_PROJ = 5632
SM_AG, SM_BF = 0, 16


def _cparams(sem, vmem_mb=48):
    return pltpu.CompilerParams(dimension_semantics=sem, vmem_limit_bytes=vmem_mb * 1024 * 1024)


def _sigmoid(z):
    return 1.0 / (1.0 + jnp.exp(-z))


def _log_sigmoid(z):
    return jnp.minimum(z, 0.0) - jnp.log1p(jnp.exp(-jnp.abs(z)))


def _row_prefix_sum(v, seg):
    rows = v.shape[0]
    rid = lax.broadcasted_iota(jnp.int32, v.shape, 0) % seg
    s = 1
    while s < seg:
        v = v + jnp.where(rid >= s, pltpu.roll(v, s, 0), 0.0)
        s *= 2
    del rows
    return v


def _inproj_kernel(x_ref, g_ref, w_ref, ws_ref, o_ref, os_ref, h_ref):
    @pl.when(pl.program_id(1) == 0)
    def _():
        x = x_ref[...]
        ms = jnp.mean(x * x, axis=-1, keepdims=True)
        h = (x * lax.rsqrt(ms + EPS) * g_ref[...]).astype(BF16)
        h_ref[...] = h
        os_ref[...] = jnp.dot(h, ws_ref[...], preferred_element_type=F32)

    o_ref[...] = jnp.dot(h_ref[...], w_ref[...], preferred_element_type=F32).astype(BF16)


def _norm_inproj(x, g, w_all, ws_all, layer):
    n = x.shape[0]
    tm = min(1024, n)
    tn = 1408
    return pl.pallas_call(
        _inproj_kernel,
        grid=(n // tm, D_PROJ // tn),
        in_specs=[
            pl.BlockSpec((tm, D_MODEL), lambda i, j: (i, 0)),
            pl.BlockSpec((1, D_MODEL), lambda i, j: (0, 0)),
            pl.BlockSpec((None, D_MODEL, tn), lambda i, j: (layer, 0, j)),
            pl.BlockSpec((None, D_MODEL, LANES), lambda i, j: (layer, 0, 0)),
        ],
        out_specs=[
            pl.BlockSpec((tm, tn), lambda i, j: (i, j)),
            pl.BlockSpec((tm, LANES), lambda i, j: (i, 0)),
        ],
        out_shape=[jax.ShapeDtypeStruct((n, D_PROJ), BF16), jax.ShapeDtypeStruct((n, LANES), F32)],
        scratch_shapes=[pltpu.VMEM((tm, D_MODEL), BF16)],
        compiler_params=_cparams(("parallel", "arbitrary")),
        name="norm_inproj",
    )(x, g, w_all, ws_all)


def _outproj_kernel(ya_ref, yb_ref, yc_ref, yd_ref, gate_ref, w_ref, x_ref, o_ref, z_ref):
    for gi, y_ref in enumerate((ya_ref, yb_ref, yc_ref, yd_ref)):
        lo, hi = gi * GROUP_W, (gi + 1) * GROUP_W
        g = gate_ref[:, lo:hi].astype(F32)
        z_ref[:, lo:hi] = (y_ref[...].astype(F32) * (g * _sigmoid(g))).astype(BF16)
    o_ref[...] = x_ref[...] + jnp.dot(z_ref[...], w_ref[...], preferred_element_type=F32)


def _gate_outproj(ya, yb, yc, yd, proj, w_all, x, layer):
    n = x.shape[0]
    tm = min(512, n)
    yspec = pl.BlockSpec((tm, GROUP_W), lambda i: (i, 0))
    return pl.pallas_call(
        _outproj_kernel,
        grid=(n // tm,),
        in_specs=[
            yspec, yspec, yspec, yspec,
            pl.BlockSpec((tm, D_MIX), lambda i: (i, 0)),
            pl.BlockSpec((None, D_MIX, D_MODEL), lambda i: (layer, 0, 0)),
            pl.BlockSpec((tm, D_MODEL), lambda i: (i, 0)),
        ],
        out_specs=pl.BlockSpec((tm, D_MODEL), lambda i: (i, 0)),
        out_shape=jax.ShapeDtypeStruct((n, D_MODEL), F32),
        scratch_shapes=[pltpu.VMEM((tm, D_MIX), BF16)],
        compiler_params=_cparams(("parallel",)),
        name="gate_outproj",
    )(ya, yb, yc, yd, proj, w_all, x)


def _attn_kernel(q_ref, k_ref, vt_ref, o_ref, m_ref, l_ref, acc_ref, *, tq):
    i = pl.program_id(2)
    half = tq // 2
    nt_dims = (((1,), (1,)), ((), ()))

    m_ref[...] = jnp.full(m_ref.shape, NEG_BIG, F32)
    l_ref[...] = jnp.zeros(l_ref.shape, F32)
    acc_ref[...] = jnp.zeros(acc_ref.shape, F32)

    def update(h, cols, st, vt):
        m = m_ref[h, :, cols]
        m_new = jnp.maximum(m, jnp.max(st, axis=0, keepdims=True))
        alpha = jnp.exp2(m - m_new)
        p = jnp.exp2(st - m_new)
        m_ref[h, :, cols] = m_new
        l_ref[h, :, cols] = alpha * l_ref[h, :, cols] + jnp.sum(p, axis=0, keepdims=True)
        acc_ref[h, :, cols] = alpha * acc_ref[h, :, cols] + jnp.dot(vt, p.astype(BF16), preferred_element_type=F32)

    def scores(h, krows, qrows):
        lanes = slice(h * HEAD_PAD, (h + 1) * HEAD_PAD)
        return lax.dot_general(k_ref[krows, lanes], q_ref[qrows, lanes], nt_dims, preferred_element_type=F32)

    def body(j, carry):
        off = pl.multiple_of(j * tq, tq)
        for h in range(2):
            st = scores(h, pl.ds(off, tq), slice(0, tq))
            update(h, slice(0, tq), st, vt_ref[h * HEAD_DIM:(h + 1) * HEAD_DIM, pl.ds(off, tq)])
        return carry

    lax.fori_loop(0, i, body, 0)

    off = pl.multiple_of(i * tq, tq)
    tri = lax.broadcasted_iota(jnp.int32, (half, half), 0) <= lax.broadcasted_iota(jnp.int32, (half, half), 1)
    low = lax.broadcasted_iota(jnp.int32, (tq, half), 0) <= lax.broadcasted_iota(jnp.int32, (tq, half), 1) + half
    for h in range(2):
        vrows = slice(h * HEAD_DIM, (h + 1) * HEAD_DIM)
        st = scores(h, pl.ds(off, half), slice(0, half))
        update(h, slice(0, half), jnp.where(tri, st, NEG_BIG), vt_ref[vrows, pl.ds(off, half)])
        st = scores(h, pl.ds(off, tq), slice(half, tq))
        update(h, slice(half, tq), jnp.where(low, st, NEG_BIG), vt_ref[vrows, pl.ds(off, tq)])

    ot = jnp.concatenate([acc_ref[0] / l_ref[0], acc_ref[1] / l_ref[1]], axis=0)
    o_ref[...] = ot.T.astype(BF16)


def _attention(q, k, vt, batch, seq):
    n = q.shape[0]
    tq = min(512, seq)
    nq = seq // tq
    return pl.pallas_call(
        functools.partial(_attn_kernel, tq=tq),
        grid=(batch, N_HEADS // 2, nq),
        in_specs=[
            pl.BlockSpec((tq, 2 * HEAD_PAD), lambda b, p, i: (b * nq + i, p)),
            pl.BlockSpec((seq, 2 * HEAD_PAD), lambda b, p, i: (b, p)),
            pl.BlockSpec((None, 2 * HEAD_DIM, seq), lambda b, p, i: (b, p, 0)),
        ],
        out_specs=pl.BlockSpec((tq, LANES), lambda b, p, i: (b * nq + i, p)),
        out_shape=jax.ShapeDtypeStruct((n, GROUP_W), BF16),
        scratch_shapes=[pltpu.VMEM((2, 1, tq), F32), pltpu.VMEM((2, 1, tq), F32), pltpu.VMEM((2, HEAD_DIM, tq), F32)],
        compiler_params=_cparams(("parallel", "parallel", "arbitrary")),
        name="causal_attention",
    )(q, k, vt)


def _fox_prep_kernel(q_ref, k_ref, v_ref, sm_ref, bf_ref, gq_ref, gk_ref, ones_ref, eye_ref, qo_ref, ko_ref, vt_ref,
                     carry_ref, *, tt):
    @pl.when(pl.program_id(1) == 0)
    def _():
        carry_ref[...] = jnp.zeros_like(carry_ref)

    vt_ref[...] = lax.dot_general(eye_ref[...], v_ref[...], (((1,), (1,)), ((), ())),
                                  preferred_element_type=F32).astype(BF16)

    def headnorm(ref, g_ref):
        z = ref[...].astype(F32)
        ssq = jnp.dot((z * z).astype(BF16), ones_ref[...], preferred_element_type=F32)
        return z * lax.rsqrt(ssq * (1.0 / HEAD_DIM) + EPS) * g_ref[...]

    qn = headnorm(q_ref, gq_ref)
    kn = headnorm(k_ref, gk_ref)

    log_f = _log_sigmoid(sm_ref[...] + bf_ref[...])
    cum = _row_prefix_sum(log_f, tt) + carry_ref[0:1, :]
    carry_ref[0:1, :] = cum[tt - 1:tt, :]
    neg = cum * (-LOG2E)

    lane = lax.broadcasted_iota(jnp.int32, (tt, LANES), 1)
    for h in range(N_HEADS):
        pair, e = h // 2, h % 2
        in_head = (lane >= e * HEAD_DIM) & (lane < (e + 1) * HEAD_DIM)
        base = HEAD_DIM if e == 0 else 0
        cb = jnp.broadcast_to(neg[:, SM_BF + h:SM_BF + h + 1], (tt, LANES))
        hi = cb.astype(BF16).astype(F32)
        r1 = cb - hi
        mid = r1.astype(BF16).astype(F32)
        lo = r1 - mid
        aug_k = jnp.where(lane == base, hi, jnp.where(lane == base + 1, mid, jnp.where(lane == base + 2, lo, 0.0)))
        aug_q = jnp.where((lane >= base) & (lane < base + 3), 1.0, 0.0)
        kpair = kn[:, pair * LANES:(pair + 1) * LANES]
        qpair = qn[:, pair * LANES:(pair + 1) * LANES]
        ko_ref[:, h * HEAD_PAD:(h + 1) * HEAD_PAD] = jnp.where(in_head, kpair, aug_k).astype(BF16)
        qo_ref[:, h * HEAD_PAD:(h + 1) * HEAD_PAD] = jnp.where(in_head, qpair, aug_q).astype(BF16)


def _fox_prep(proj, small, bf_row, gq, gk, ones64, eye, batch, seq):
    n = proj.shape[0]
    tt = min(512, seq)
    nt = seq // tt
    row = lambda shape: pl.BlockSpec(shape, lambda b, t: (0, 0))
    return pl.pallas_call(
        functools.partial(_fox_prep_kernel, tt=tt),
        grid=(batch, nt),
        in_specs=[
            pl.BlockSpec((tt, GROUP_W), lambda b, t: (b * nt + t, COL_BQ // GROUP_W)),
            pl.BlockSpec((tt, GROUP_W), lambda b, t: (b * nt + t, COL_BK // GROUP_W)),
            pl.BlockSpec((tt, GROUP_W), lambda b, t: (b * nt + t, COL_BV // GROUP_W)),
            pl.BlockSpec((tt, LANES), lambda b, t: (b * nt + t, 0)),
            row((1, LANES)), row((1, GROUP_W)), row((1, GROUP_W)), row((GROUP_W, GROUP_W)), row((GROUP_W, GROUP_W)),
        ],
        out_specs=[
            pl.BlockSpec((tt, N_HEADS * HEAD_PAD), lambda b, t: (b * nt + t, 0)),
            pl.BlockSpec((tt, N_HEADS * HEAD_PAD), lambda b, t: (b * nt + t, 0)),
            pl.BlockSpec((None, GROUP_W, tt), lambda b, t: (b, 0, t)),
        ],
        out_shape=[jax.ShapeDtypeStruct((n, N_HEADS * HEAD_PAD), BF16)] * 2
        + [jax.ShapeDtypeStruct((batch, GROUP_W, seq), BF16)],
        scratch_shapes=[pltpu.VMEM((8, LANES), F32)],
        compiler_params=_cparams(("parallel", "arbitrary")),
        name="fox_prep",
    )(proj, proj, proj, small, bf_row, gq, gk, ones64, eye)


def _rope_kernel(pos_ref, invf_ref, c_ref, s1_ref, s2_ref):
    ang = pos_ref[...].astype(F32) * invf_ref[...]
    lane = lax.broadcasted_iota(jnp.int32, ang.shape, 1)
    half = MLA_ROPE // 2
    c, s = jnp.cos(ang), jnp.sin(ang)
    c_ref[...] = jnp.where((lane >= MLA_NOPE) & (lane < MLA_NOPE + MLA_ROPE), c, 1.0)
    s1_ref[...] = jnp.where((lane >= MLA_NOPE) & (lane < MLA_NOPE + half), -s, 0.0)
    s2_ref[...] = jnp.where((lane >= MLA_NOPE + half) & (lane < MLA_NOPE + MLA_ROPE), s, 0.0)


def _rope_tables(pos_col, invf_row):
    n = pos_col.shape[0]
    tm = min(1024, n)
    out = pl.BlockSpec((tm, LANES), lambda i: (i, 0))
    return pl.pallas_call(
        _rope_kernel,
        grid=(n // tm,),
        in_specs=[pl.BlockSpec((tm, 1), lambda i: (i, 0)), pl.BlockSpec((1, LANES), lambda i: (0, 0))],
        out_specs=[out, out, out],
        out_shape=[jax.ShapeDtypeStruct((n, LANES), F32)] * 3,
        compiler_params=_cparams(("parallel",)),
        name="rope_tables",
    )(pos_col, invf_row)


def _mla_prep_kernel(cq_ref, ckv_ref, ckr_ref, c_ref, s1_ref, s2_ref, gcq_ref, wuq_ref, gckv_ref, wuk_ref, wuv_ref,
                     gq_ref, gk_ref, ones_ref, qo_ref, ko_ref, vo_ref):
    d_qk = float(MLA_NOPE + MLA_ROPE)
    cos, s1, s2 = c_ref[...], s1_ref[...], s2_ref[...]
    half = MLA_ROPE // 2

    def rms(z, g):
        return (z * lax.rsqrt(jnp.mean(z * z, axis=-1, keepdims=True) + EPS) * g).astype(BF16)

    def head_ssq(z):
        zz = (z * z).astype(BF16)
        parts = [jnp.dot(zz[:, c * 256:(c + 1) * 256], ones_ref[...], preferred_element_type=F32) for c in range(4)]
        return jnp.concatenate(parts, axis=1)

    def rope(zh):
        return zh * cos + pltpu.roll(zh, HEAD_PAD - half, 1) * s1 + pltpu.roll(zh, half, 1) * s2

    q = jnp.dot(rms(cq_ref[...].astype(F32), gcq_ref[...]), wuq_ref[...], preferred_element_type=F32)
    qn = q * lax.rsqrt(head_ssq(q) * (1.0 / d_qk) + EPS) * gq_ref[...]

    kvn = rms(ckv_ref[...].astype(F32), gckv_ref[...])
    kn = jnp.dot(kvn, wuk_ref[...], preferred_element_type=F32)
    vo_ref[...] = lax.dot_general(wuv_ref[...], kvn, (((1,), (1,)), ((), ())), preferred_element_type=F32).astype(BF16)
    kr = ckr_ref[...].astype(F32)
    ss_r = jnp.sum(kr * kr, axis=-1, keepdims=True)
    kr_placed = pltpu.roll(kr, MLA_NOPE, 1)
    k_scale = lax.rsqrt((head_ssq(kn) + ss_r) * (1.0 / d_qk) + EPS) * gk_ref[...]

    for h in range(N_HEADS):
        sl = slice(h * HEAD_PAD, (h + 1) * HEAD_PAD)
        qo_ref[:, sl] = rope(qn[:, sl]).astype(BF16)
        ko_ref[:, sl] = rope((kn[:, sl] + kr_placed) * k_scale[:, sl]).astype(BF16)


def _mla_prep(proj, cos_t, s1_t, s2_t, gcq, wuq, gckv, wuk, wuv_t, gq, gk, ones128, batch, seq):
    n = proj.shape[0]
    tm = min(512, seq)
    nt = seq // tm
    full = lambda shape: pl.BlockSpec(shape, lambda i: (0, 0))
    tab = pl.BlockSpec((tm, LANES), lambda i: (i, 0))
    wide = pl.BlockSpec((tm, N_HEADS * HEAD_PAD), lambda i: (i, 0))
    return pl.pallas_call(
        _mla_prep_kernel,
        grid=(n // tm,),
        in_specs=[
            pl.BlockSpec((tm, MLA_Q_RANK), lambda i: (i, COL_CQ // MLA_Q_RANK)),
            pl.BlockSpec((tm, MLA_KV_RANK), lambda i: (i, COL_CKV // LANES)),
            pl.BlockSpec((tm, LANES), lambda i: (i, COL_CKR // LANES)),
            tab, tab, tab,
            full((1, MLA_Q_RANK)), full((MLA_Q_RANK, N_HEADS * HEAD_PAD)),
            full((1, MLA_KV_RANK)), full((MLA_KV_RANK, N_HEADS * HEAD_PAD)), full((GROUP_W, MLA_KV_RANK)),
            full((1, N_HEADS * HEAD_PAD)), full((1, N_HEADS * HEAD_PAD)), full((256, 256)),
        ],
        out_specs=[wide, wide, pl.BlockSpec((None, GROUP_W, tm), lambda i: (i // nt, 0, i % nt))],
        out_shape=[jax.ShapeDtypeStruct((n, N_HEADS * HEAD_PAD), BF16)] * 2
        + [jax.ShapeDtypeStruct((batch, GROUP_W, seq), BF16)],
        compiler_params=_cparams(("parallel",)),
        name="mla_prep",
    )(proj, proj, proj, cos_t, s1_t, s2_t, gcq, wuq, gckv, wuk, wuv_t, gq, gk, ones128)


def _gla_kernel(q_ref, k_ref, v_ref, sm_ref, wh_ref, wl_ref, b_ref, go_ref, ones_ref, o_ref, st_ref, *, tc):
    hk = N_HEADS * GLA_DK
    ck = GLA_CHUNK

    @pl.when(pl.program_id(1) == 0)
    def _():
        st_ref[...] = jnp.zeros_like(st_ref)

    sm = sm_ref[...]
    sm_hi = sm.astype(BF16)
    sm_lo = (sm - sm_hi.astype(F32)).astype(BF16)
    z = (jnp.dot(sm_hi, wh_ref[...], preferred_element_type=F32)
         + jnp.dot(sm_lo, wh_ref[...], preferred_element_type=F32)
         + jnp.dot(sm_hi, wl_ref[...], preferred_element_type=F32)) + b_ref[...]
    g_all = _row_prefix_sum(_log_sigmoid(z) * (1.0 / GLA_NORMALIZER), ck)

    lane_k = lax.broadcasted_iota(jnp.int32, (ck, hk), 1) // GLA_DK
    lane_v = lax.broadcasted_iota(jnp.int32, (ck, GROUP_W), 1) // HEAD_DIM
    a_row = lax.broadcasted_iota(jnp.int32, (N_HEADS * ck, ck), 0) % ck
    a_col = lax.broadcasted_iota(jnp.int32, (N_HEADS * ck, ck), 1)
    st_blockdiag = (lax.broadcasted_iota(jnp.int32, (GROUP_W, hk), 0) // HEAD_DIM
                    == lax.broadcasted_iota(jnp.int32, (GROUP_W, hk), 1) // GLA_DK)

    outs = []
    for c in range(tc // ck):
        rows = slice(c * ck, (c + 1) * ck)
        g = g_all[rows]
        q = q_ref[rows, :].astype(F32) * (GLA_DK ** -0.5)
        k = k_ref[rows, :].astype(F32)
        v = v_ref[rows, :]
        g_last = g[ck - 1:ck, :]
        qt = (q * jnp.exp(g)).astype(BF16)
        kt = (k * jnp.exp(-g)).astype(BF16)
        kh = (k * jnp.exp(g_last - g)).astype(BF16)

        q_stack = jnp.concatenate([jnp.where(lane_k == h, qt, jnp.zeros_like(qt)) for h in range(N_HEADS)], axis=0)
        a = lax.dot_general(q_stack, kt, (((1,), (1,)), ((), ())), preferred_element_type=F32)
        a = jnp.where(a_row >= a_col, a, 0.0).astype(BF16)
        r = jnp.dot(a, v, preferred_element_type=F32)
        o = lax.dot_general(qt, st_ref[...].astype(BF16), (((1,), (1,)), ((), ())), preferred_element_type=F32)
        for h in range(N_HEADS):
            o = o + jnp.where(lane_v == h, r[h * ck:(h + 1) * ck, :], 0.0)
        outs.append(o)

        u = lax.dot_general(v, kh, (((0,), (0,)), ((), ())), preferred_element_type=F32)
        st_ref[...] = st_ref[...] * jnp.exp(g_last) + jnp.where(st_blockdiag, u, 0.0)

    o = jnp.concatenate(outs, axis=0)
    ssq = jnp.dot((o * o).astype(BF16), ones_ref[...], preferred_element_type=F32)
    o_ref[...] = (o * lax.rsqrt(ssq * (1.0 / HEAD_DIM) + EPS) * go_ref[...]).astype(BF16)


def _gla(proj, small, w_hi, w_lo, b_up, g_out, ones64, batch, seq):
    n = proj.shape[0]
    tc = min(256, seq)
    nt = seq // tc
    hk = N_HEADS * GLA_DK
    full = lambda shape: pl.BlockSpec(shape, lambda b, t: (0, 0))
    return pl.pallas_call(
        functools.partial(_gla_kernel, tc=tc),
        grid=(batch, nt),
        in_specs=[
            pl.BlockSpec((tc, hk), lambda b, t: (b * nt + t, COL_AQ // hk)),
            pl.BlockSpec((tc, hk), lambda b, t: (b * nt + t, COL_AK // hk)),
            pl.BlockSpec((tc, GROUP_W), lambda b, t: (b * nt + t, COL_AV // GROUP_W)),
            pl.BlockSpec((tc, LANES), lambda b, t: (b * nt + t, 0)),
            full((LANES, hk)), full((LANES, hk)), full((1, hk)), full((1, GROUP_W)), full((GROUP_W, GROUP_W)),
        ],
        out_specs=pl.BlockSpec((tc, GROUP_W), lambda b, t: (b * nt + t, 0)),
        out_shape=jax.ShapeDtypeStruct((n, GROUP_W), BF16),
        scratch_shapes=[pltpu.VMEM((GROUP_W, hk), F32)],
        compiler_params=_cparams(("parallel", "arbitrary")),
        name="gla",
    )(proj, proj, proj, small, w_hi, w_lo, b_up, g_out, ones64)


def _lru_kernel(x_ref, cw_ref, cb_ref, wr_ref, br_ref, wi_ref, bi_ref, lam_ref, o_ref, xpad_ref, h_ref, *, tr):
    @pl.when(pl.program_id(1) == 0)
    def _():
        xpad_ref[0:8, :] = jnp.zeros((8, GROUP_W), F32)
        h_ref[...] = jnp.zeros_like(h_ref)

    xpad_ref[8:8 + tr, :] = x_ref[...].astype(F32)
    xc = cb_ref[...]
    for j in range(CONV_W):
        off = 8 - (CONV_W - 1) + j
        xc = xc + cw_ref[j:j + 1, :] * xpad_ref[off:off + tr, :]
    xpad_ref[0:8, :] = xpad_ref[tr:tr + 8, :]

    xb = xc.astype(BF16)
    r = _sigmoid(jnp.dot(xb, wr_ref[...], preferred_element_type=F32) + br_ref[...])
    i = _sigmoid(jnp.dot(xb, wi_ref[...], preferred_element_type=F32) + bi_ref[...])
    lam = lam_ref[...]
    softplus = jnp.maximum(-lam, 0.0) + jnp.log1p(jnp.exp(-jnp.abs(lam)))
    log_a = (-LRU_C) * r * softplus
    a = jnp.exp(log_a)
    bx = jnp.sqrt(1.0 - jnp.exp(2.0 * log_a)) * (i * xc)

    rid = lax.broadcasted_iota(jnp.int32, (tr, GROUP_W), 0)
    s = 1
    while s < tr:
        keep = rid >= s
        a_prev = jnp.where(keep, pltpu.roll(a, s, 0), 1.0)
        b_prev = jnp.where(keep, pltpu.roll(bx, s, 0), 0.0)
        bx = a * b_prev + bx
        a = a * a_prev
        s *= 2
    h = a * h_ref[0:1, :] + bx
    h_ref[0:1, :] = h[tr - 1:tr, :]
    o_ref[...] = h.astype(BF16)


def _lru(proj, conv_w, conv_b, wr, br, wi, bi, lam, batch, seq):
    n = proj.shape[0]
    tr = min(256, seq)
    nt = seq // tr
    full = lambda shape: pl.BlockSpec(shape, lambda b, t: (0, 0))
    return pl.pallas_call(
        functools.partial(_lru_kernel, tr=tr),
        grid=(batch, nt),
        in_specs=[
            pl.BlockSpec((tr, GROUP_W), lambda b, t: (b * nt + t, COL_DX // GROUP_W)),
            full((8, GROUP_W)), full((1, GROUP_W)),
            full((GROUP_W, GROUP_W)), full((1, GROUP_W)),
            full((GROUP_W, GROUP_W)), full((1, GROUP_W)), full((1, GROUP_W)),
        ],
        out_specs=pl.BlockSpec((tr, GROUP_W), lambda b, t: (b * nt + t, 0)),
        out_shape=jax.ShapeDtypeStruct((n, GROUP_W), BF16),
        scratch_shapes=[pltpu.VMEM((tr + 8, GROUP_W), F32), pltpu.VMEM((8, GROUP_W), F32)],
        compiler_params=_cparams(("parallel", "arbitrary")),
        name="rglru",
    )(proj, conv_w, conv_b, wr, br, wi, bi, lam)


def _pad_cols(w, width):
    return jnp.pad(w, [(0, 0)] * (w.ndim - 1) + [(0, width - w.shape[-1])])


def _block_diag_ones(n_blocks, size):
    return jnp.asarray(np.kron(np.eye(n_blocks, dtype=np.float32), np.ones((size, size), np.float32)), BF16)


def _pad_heads(w, d_real):
    lead = w.shape[:-1]
    w = w.reshape(lead + (N_HEADS, d_real))
    w = jnp.pad(w, [(0, 0)] * len(lead) + [(0, 0), (0, HEAD_PAD - d_real)])
    return w.reshape(lead + (N_HEADS * HEAD_PAD,))


def kernel(x, positions, norm_g, w_in, w_out, gla_w_up, gla_b_up, gla_g_out, fox_b_f, fox_g_q, fox_g_k, mla_g_cq, mla_w_uq, mla_g_ckv, mla_w_ukv, mla_g_q, mla_g_k, lru_conv_w, lru_conv_b, lru_w_r, lru_b_r, lru_w_i, lru_b_i, lru_lam):
    batch, seq, _ = x.shape
    depth = w_in.shape[0]
    n = batch * seq

    sizes = (D_MIX, 256, 256, 512, GLA_RANK, 512, 512, 512, N_HEADS, MLA_Q_RANK, MLA_KV_RANK, MLA_ROPE, GROUP_W)
    offs = np.concatenate([[0], np.cumsum(sizes)])
    seg = [w_in[:, :, offs[i]:offs[i + 1]] for i in range(len(sizes))]
    (w_gate, w_aq, w_ak, w_av, w_ag, w_bq, w_bk, w_bv, w_bf, w_cq, w_ckv, w_ckr, w_dx) = seg
    w_main = jnp.concatenate(
        [w_gate, w_aq, w_ak, w_av, w_bq, w_bk, w_bv, w_cq, w_ckv, _pad_cols(w_ckr, LANES), w_dx], axis=-1).astype(BF16)
    w_small = _pad_cols(jnp.concatenate([w_ag, w_bf], axis=-1), LANES).astype(BF16)
    w_out_b = w_out.astype(BF16)

    ones64 = _block_diag_ones(GROUP_W // HEAD_DIM, HEAD_DIM)
    ones128 = _block_diag_ones(2, HEAD_PAD)

    wup = jnp.pad(gla_w_up, ((0, 0), (0, LANES - GLA_RANK), (0, 0)))
    wup_hi = wup.astype(BF16)
    wup_lo = (wup - wup_hi.astype(F32)).astype(BF16)

    fox_scale = HEAD_DIM ** -0.5 * LOG2E
    mla_scale = (MLA_NOPE + MLA_ROPE) ** -0.5 * LOG2E
    fox_gq = jnp.tile(fox_g_q, (1, N_HEADS)) * fox_scale
    fox_gk = jnp.tile(fox_g_k, (1, N_HEADS))
    fox_bf = jnp.pad(fox_b_f, ((0, 0), (SM_BF, LANES - SM_BF - N_HEADS)))

    wuq = _pad_heads(mla_w_uq, MLA_NOPE + MLA_ROPE).astype(BF16)
    wukv = mla_w_ukv.reshape(depth, MLA_KV_RANK, N_HEADS, 2 * HEAD_DIM)
    wuk = _pad_heads(wukv[..., :MLA_NOPE].reshape(depth, MLA_KV_RANK, N_HEADS * MLA_NOPE), MLA_NOPE).astype(BF16)
    wuv_t = jnp.swapaxes(wukv[..., MLA_NOPE:].reshape(depth, MLA_KV_RANK, GROUP_W), 1, 2).astype(BF16)
    eye = jnp.eye(GROUP_W, dtype=BF16)
    mla_gq = _pad_heads(jnp.tile(mla_g_q, (1, N_HEADS)), MLA_NOPE + MLA_ROPE) * mla_scale
    mla_gk = _pad_heads(jnp.tile(mla_g_k, (1, N_HEADS)), MLA_NOPE + MLA_ROPE)

    eye_h = jnp.eye(N_HEADS, dtype=F32)
    wr_bd = jnp.einsum("lncd,nm->lncmd", lru_w_r, eye_h).reshape(depth, GROUP_W, GROUP_W).astype(BF16)
    wi_bd = jnp.einsum("lncd,nm->lncmd", lru_w_i, eye_h).reshape(depth, GROUP_W, GROUP_W).astype(BF16)
    conv_w = jnp.pad(lru_conv_w, ((0, 0), (0, 8 - CONV_W), (0, 0)))

    half = MLA_ROPE // 2
    inv_freq = ROPE_THETA ** (-jnp.arange(half, dtype=F32) / half)
    invf_row = jnp.pad(jnp.concatenate([inv_freq, inv_freq]), (MLA_NOPE, LANES - MLA_NOPE - MLA_ROPE))[None, :]
    cos_t, s1_t, s2_t = _rope_tables(positions.reshape(n, 1), invf_row)

    xf = x.reshape(n, D_MODEL)
    for l in range(depth):
        proj, small = _norm_inproj(xf, norm_g[l][None, :], w_main, w_small, l)
        ya = _gla(proj, small, wup_hi[l], wup_lo[l], gla_b_up[l][None, :],
                  jnp.tile(gla_g_out[l], N_HEADS)[None, :], ones64, batch, seq)
        fq, fk, fvt = _fox_prep(proj, small, fox_bf[l][None, :], fox_gq[l][None, :], fox_gk[l][None, :], ones64, eye,
                                batch, seq)
        yb = _attention(fq, fk, fvt, batch, seq)
        mq, mk, mvt = _mla_prep(proj, cos_t, s1_t, s2_t, mla_g_cq[l][None, :], wuq[l], mla_g_ckv[l][None, :],
                                wuk[l], wuv_t[l], mla_gq[l][None, :], mla_gk[l][None, :], ones128, batch, seq)
        yc = _attention(mq, mk, mvt, batch, seq)
        yd = _lru(proj, conv_w[l], lru_conv_b[l][None, :], wr_bd[l], lru_b_r[l][None, :], wi_bd[l],
                  lru_b_i[l][None, :], lru_lam[l][None, :], batch, seq)
        xf = _gate_outproj(ya, yb, yc, yd, proj, w_out_b, xf, l)
    return xf.reshape(batch, seq, D_MODEL)
```

```python
import functools
import math

import numpy as np
import jax
import jax.numpy as jnp
from jax import lax
from jax.experimental import pallas as pl
from jax.experimental.pallas import tpu as pltpu

F32 = jnp.float32
BF16 = jnp.bfloat16

D_MODEL = 1024
D_MIX = 2048
GROUP_W = 512
N_HEADS = 8
HEAD_DIM = 64
EPS = 1e-6
GLA_DK = 32
GLA_RANK = 16
GLA_NORMALIZER = 16.0
GLA_CHUNK = 64
MLA_Q_RANK = 256
MLA_KV_RANK = 128
MLA_NOPE = 64
MLA_ROPE = 32
ROPE_THETA = 10000.0
CONV_W = 4
LRU_C = 8.0
LOG2E = 1.4426950408889634
NEG_BIG = -1e30

LANES = 128
HEAD_PAD = 128

COL_GATE = 0
COL_AQ, COL_AK, COL_AV = 2048, 2304, 2560
COL_BQ, COL_BK, COL_BV = 3072, 3584, 4096
COL_CQ, COL_CKV, COL_CKR = 4608, 4864, 4992
COL_DX = 5120
D_PROJ = 5632
SM_AG, SM_BF = 0, 16


def _cparams(sem, vmem_mb=48):
    return pltpu.CompilerParams(dimension_semantics=sem, vmem_limit_bytes=vmem_mb * 1024 * 1024)


def _sigmoid(z):
    return 1.0 / (1.0 + jnp.exp(-z))


def _log_sigmoid(z):
    return jnp.minimum(z, 0.0) - jnp.log1p(jnp.exp(-jnp.abs(z)))


def _row_prefix_sum(v, seg):
    rows = v.shape[0]
    rid = lax.broadcasted_iota(jnp.int32, v.shape, 0) % seg
    s = 1
    while s < seg:
        v = v + jnp.where(rid >= s, pltpu.roll(v, s, 0), 0.0)
        s *= 2
    del rows
    return v


def _inproj_kernel(x_ref, g_ref, w_ref, ws_ref, o_ref, os_ref, h_ref):
    @pl.when(pl.program_id(1) == 0)
    def _():
        x = x_ref[...]
        ms = jnp.mean(x * x, axis=-1, keepdims=True)
        h = (x * lax.rsqrt(ms + EPS) * g_ref[...]).astype(BF16)
        h_ref[...] = h
        os_ref[...] = jnp.dot(h, ws_ref[...], preferred_element_type=F32)

    o_ref[...] = jnp.dot(h_ref[...], w_ref[...], preferred_element_type=F32).astype(BF16)


def _norm_inproj(x, g, w_all, ws_all, layer):
    n = x.shape[0]
    tm = min(1024, n)
    tn = 1408
    return pl.pallas_call(
        _inproj_kernel,
        grid=(n // tm, D_PROJ // tn),
        in_specs=[
            pl.BlockSpec((tm, D_MODEL), lambda i, j: (i, 0)),
            pl.BlockSpec((1, D_MODEL), lambda i, j: (0, 0)),
            pl.BlockSpec((None, D_MODEL, tn), lambda i, j: (layer, 0, j)),
            pl.BlockSpec((None, D_MODEL, LANES), lambda i, j: (layer, 0, 0)),
        ],
        out_specs=[
            pl.BlockSpec((tm, tn), lambda i, j: (i, j)),
            pl.BlockSpec((tm, LANES), lambda i, j: (i, 0)),
        ],
        out_shape=[jax.ShapeDtypeStruct((n, D_PROJ), BF16), jax.ShapeDtypeStruct((n, LANES), F32)],
        scratch_shapes=[pltpu.VMEM((tm, D_MODEL), BF16)],
        compiler_params=_cparams(("parallel", "arbitrary")),
        name="norm_inproj",
    )(x, g, w_all, ws_all)


def _outproj_kernel(ya_ref, yb_ref, yc_ref, yd_ref, gate_ref, w_ref, x_ref, o_ref, z_ref):
    for gi, y_ref in enumerate((ya_ref, yb_ref, yc_ref, yd_ref)):
        lo, hi = gi * GROUP_W, (gi + 1) * GROUP_W
        g = gate_ref[:, lo:hi].astype(F32)
        z_ref[:, lo:hi] = (y_ref[...].astype(F32) * (g * _sigmoid(g))).astype(BF16)
    o_ref[...] = x_ref[...] + jnp.dot(z_ref[...], w_ref[...], preferred_element_type=F32)


def _gate_outproj(ya, yb, yc, yd, proj, w_all, x, layer):
    n = x.shape[0]
    tm = min(512, n)
    yspec = pl.BlockSpec((tm, GROUP_W), lambda i: (i, 0))
    return pl.pallas_call(
        _outproj_kernel,
        grid=(n // tm,),
        in_specs=[
            yspec, yspec, yspec, yspec,
            pl.BlockSpec((tm, D_MIX), lambda i: (i, 0)),
            pl.BlockSpec((None, D_MIX, D_MODEL), lambda i: (layer, 0, 0)),
            pl.BlockSpec((tm, D_MODEL), lambda i: (i, 0)),
        ],
        out_specs=pl.BlockSpec((tm, D_MODEL), lambda i: (i, 0)),
        out_shape=jax.ShapeDtypeStruct((n, D_MODEL), F32),
        scratch_shapes=[pltpu.VMEM((tm, D_MIX), BF16)],
        compiler_params=_cparams(("parallel",)),
        name="gate_outproj",
    )(ya, yb, yc, yd, proj, w_all, x)


def _attn_kernel(q_ref, k_ref, vt_ref, o_ref, m_ref, l_ref, acc_ref, st_ref, *, tq, nq):
    i = pl.program_id(2)
    half = tq // 2
    nt_dims = (((1,), (1,)), ((), ()))

    m_ref[...] = jnp.full(m_ref.shape, NEG_BIG, F32)
    l_ref[...] = jnp.zeros(l_ref.shape, F32)
    acc_ref[...] = jnp.zeros(acc_ref.shape, F32)

    def update(h, cols, st, vt):
        m = m_ref[h, :, cols]
        m_new = jnp.maximum(m, jnp.max(st, axis=0, keepdims=True))
        alpha = jnp.exp2(m - m_new)
        p = jnp.exp2(st - m_new)
        m_ref[h, :, cols] = m_new
        l_ref[h, :, cols] = alpha * l_ref[h, :, cols] + jnp.sum(p, axis=0, keepdims=True)
        acc_ref[h, :, cols] = alpha * acc_ref[h, :, cols] + jnp.dot(vt, p.astype(BF16), preferred_element_type=F32)

    def scores(h, krows, qrows):
        lanes = slice(h * HEAD_PAD, (h + 1) * HEAD_PAD)
        return lax.dot_general(k_ref[krows, lanes], q_ref[qrows, lanes], nt_dims, preferred_element_type=F32)

    def vrows(h):
        return slice(h * HEAD_DIM, (h + 1) * HEAD_DIM)

    for h in range(2):
        st_ref[0, h] = scores(h, slice(0, tq), slice(0, tq))
    for j in range(nq - 1):
        @pl.when(j < i)
        def _(j=j):
            cur, nxt = j % 2, (j + 1) % 2
            for h in range(2):
                s_next = scores(h, slice((j + 1) * tq, (j + 2) * tq), slice(0, tq))
                update(h, slice(0, tq), st_ref[cur, h], vt_ref[vrows(h), j * tq:(j + 1) * tq])
                st_ref[nxt, h] = s_next

    tri = lax.broadcasted_iota(jnp.int32, (half, half), 0) <= lax.broadcasted_iota(jnp.int32, (half, half), 1)
    low = lax.broadcasted_iota(jnp.int32, (tq, half), 0) <= lax.broadcasted_iota(jnp.int32, (tq, half), 1) + half
    for c in range(nq):
        @pl.when(i == c)
        def _(c=c):
            slot, off = c % 2, c * tq
            for h in range(2):
                st = st_ref[slot, h, 0:half, 0:half]
                update(h, slice(0, half), jnp.where(tri, st, NEG_BIG), vt_ref[vrows(h), off:off + half])
                st = st_ref[slot, h, :, half:tq]
                update(h, slice(half, tq), jnp.where(low, st, NEG_BIG), vt_ref[vrows(h), off:off + tq])

    ot = jnp.concatenate([acc_ref[0] / l_ref[0], acc_ref[1] / l_ref[1]], axis=0)
    o_ref[...] = ot.T.astype(BF16)


def _attention(q, k, vt, batch, seq):
    n = q.shape[0]
    tq = min(512, seq)
    nq = seq // tq
    return pl.pallas_call(
        functools.partial(_attn_kernel, tq=tq, nq=nq),
        grid=(batch, N_HEADS // 2, nq),
        in_specs=[
            pl.BlockSpec((tq, 2 * HEAD_PAD), lambda b, p, i: (b * nq + i, p)),
            pl.BlockSpec((seq, 2 * HEAD_PAD), lambda b, p, i: (b, p)),
            pl.BlockSpec((None, 2 * HEAD_DIM, seq), lambda b, p, i: (b, p, 0)),
        ],
        out_specs=pl.BlockSpec((tq, LANES), lambda b, p, i: (b * nq + i, p)),
        out_shape=jax.ShapeDtypeStruct((n, GROUP_W), BF16),
        scratch_shapes=[pltpu.VMEM((2, 1, tq), F32), pltpu.VMEM((2, 1, tq), F32), pltpu.VMEM((2, HEAD_DIM, tq), F32),
                        pltpu.VMEM((2, 2, tq, tq), F32)],
        compiler_params=_cparams(("parallel", "parallel", "arbitrary")),
        name="causal_attention",
    )(q, k, vt)


def _fox_prep_kernel(q_ref, k_ref, v_ref, sm_ref, bf_ref, gq_ref, gk_ref, ones_ref, eye_ref, qo_ref, ko_ref, vt_ref,
                     carry_ref, *, tt):
    @pl.when(pl.program_id(1) == 0)
    def _():
        carry_ref[...] = jnp.zeros_like(carry_ref)

    vt_ref[...] = lax.dot_general(eye_ref[...], v_ref[...], (((1,), (1,)), ((), ())),
                                  preferred_element_type=F32).astype(BF16)

    def headnorm(ref, g_ref):
        z = ref[...].astype(F32)
        ssq = jnp.dot((z * z).astype(BF16), ones_ref[...], preferred_element_type=F32)
        return z * lax.rsqrt(ssq * (1.0 / HEAD_DIM) + EPS) * g_ref[...]

    qn = headnorm(q_ref, gq_ref)
    kn = headnorm(k_ref, gk_ref)

    log_f = _log_sigmoid(sm_ref[...] + bf_ref[...])
    cum = _row_prefix_sum(log_f, tt) + carry_ref[0:1, :]
    carry_ref[0:1, :] = cum[tt - 1:tt, :]
    neg = cum * (-LOG2E)

    lane = lax.broadcasted_iota(jnp.int32, (tt, LANES), 1)
    for h in range(N_HEADS):
        pair, e = h // 2, h % 2
        in_head = (lane >= e * HEAD_DIM) & (lane < (e + 1) * HEAD_DIM)
        base = HEAD_DIM if e == 0 else 0
        cb = jnp.broadcast_to(neg[:, SM_BF + h:SM_BF + h + 1], (tt, LANES))
        hi = cb.astype(BF16).astype(F32)
        r1 = cb - hi
        mid = r1.astype(BF16).astype(F32)
        lo = r1 - mid
        aug_k = jnp.where(lane == base, hi, jnp.where(lane == base + 1, mid, jnp.where(lane == base + 2, lo, 0.0)))
        aug_q = jnp.where((lane >= base) & (lane < base + 3), 1.0, 0.0)
        kpair = kn[:, pair * LANES:(pair + 1) * LANES]
        qpair = qn[:, pair * LANES:(pair + 1) * LANES]
        ko_ref[:, h * HEAD_PAD:(h + 1) * HEAD_PAD] = jnp.where(in_head, kpair, aug_k).astype(BF16)
        qo_ref[:, h * HEAD_PAD:(h + 1) * HEAD_PAD] = jnp.where(in_head, qpair, aug_q).astype(BF16)


def _fox_prep(proj, small, bf_row, gq, gk, ones64, eye, batch, seq):
    n = proj.shape[0]
    tt = min(512, seq)
    nt = seq // tt
    row = lambda shape: pl.BlockSpec(shape, lambda b, t: (0, 0))
    return pl.pallas_call(
        functools.partial(_fox_prep_kernel, tt=tt),
        grid=(batch, nt),
        in_specs=[
            pl.BlockSpec((tt, GROUP_W), lambda b, t: (b * nt + t, COL_BQ // GROUP_W)),
            pl.BlockSpec((tt, GROUP_W), lambda b, t: (b * nt + t, COL_BK // GROUP_W)),
            pl.BlockSpec((tt, GROUP_W), lambda b, t: (b * nt + t, COL_BV // GROUP_W)),
            pl.BlockSpec((tt, LANES), lambda b, t: (b * nt + t, 0)),
            row((1, LANES)), row((1, GROUP_W)), row((1, GROUP_W)), row((GROUP_W, GROUP_W)), row((GROUP_W, GROUP_W)),
        ],
        out_specs=[
            pl.BlockSpec((tt, N_HEADS * HEAD_PAD), lambda b, t: (b * nt + t, 0)),
            pl.BlockSpec((tt, N_HEADS * HEAD_PAD), lambda b, t: (b * nt + t, 0)),
            pl.BlockSpec((None, GROUP_W, tt), lambda b, t: (b, 0, t)),
        ],
        out_shape=[jax.ShapeDtypeStruct((n, N_HEADS * HEAD_PAD), BF16)] * 2
        + [jax.ShapeDtypeStruct((batch, GROUP_W, seq), BF16)],
        scratch_shapes=[pltpu.VMEM((8, LANES), F32)],
        compiler_params=_cparams(("parallel", "arbitrary")),
        name="fox_prep",
    )(proj, proj, proj, small, bf_row, gq, gk, ones64, eye)


def _rope_kernel(pos_ref, invf_ref, c_ref, s1_ref, s2_ref):
    ang = pos_ref[...].astype(F32) * invf_ref[...]
    lane = lax.broadcasted_iota(jnp.int32, ang.shape, 1)
    half = MLA_ROPE // 2
    c, s = jnp.cos(ang), jnp.sin(ang)
    c_ref[...] = jnp.where((lane >= MLA_NOPE) & (lane < MLA_NOPE + MLA_ROPE), c, 1.0)
    s1_ref[...] = jnp.where((lane >= MLA_NOPE) & (lane < MLA_NOPE + half), -s, 0.0)
    s2_ref[...] = jnp.where((lane >= MLA_NOPE + half) & (lane < MLA_NOPE + MLA_ROPE), s, 0.0)


def _rope_tables(pos_col, invf_row):
    n = pos_col.shape[0]
    tm = min(1024, n)
    out = pl.BlockSpec((tm, LANES), lambda i: (i, 0))
    return pl.pallas_call(
        _rope_kernel,
        grid=(n // tm,),
        in_specs=[pl.BlockSpec((tm, 1), lambda i: (i, 0)), pl.BlockSpec((1, LANES), lambda i: (0, 0))],
        out_specs=[out, out, out],
        out_shape=[jax.ShapeDtypeStruct((n, LANES), F32)] * 3,
        compiler_params=_cparams(("parallel",)),
        name="rope_tables",
    )(pos_col, invf_row)


def _mla_prep_kernel(cq_ref, ckv_ref, ckr_ref, c_ref, s1_ref, s2_ref, gcq_ref, wuq_ref, gckv_ref, wuk_ref, wuv_ref,
                     gq_ref, gk_ref, ones_ref, qo_ref, ko_ref, vo_ref):
    d_qk = float(MLA_NOPE + MLA_ROPE)
    cos, s1, s2 = c_ref[...], s1_ref[...], s2_ref[...]
    half = MLA_ROPE // 2

    def rms(z, g):
        return (z * lax.rsqrt(jnp.mean(z * z, axis=-1, keepdims=True) + EPS) * g).astype(BF16)

    def head_ssq(z):
        zz = (z * z).astype(BF16)
        parts = [jnp.dot(zz[:, c * 256:(c + 1) * 256], ones_ref[...], preferred_element_type=F32) for c in range(4)]
        return jnp.concatenate(parts, axis=1)

    def rope(zh):
        return zh * cos + pltpu.roll(zh, HEAD_PAD - half, 1) * s1 + pltpu.roll(zh, half, 1) * s2

    q = jnp.dot(rms(cq_ref[...].astype(F32), gcq_ref[...]), wuq_ref[...], preferred_element_type=F32)
    qn = q * lax.rsqrt(head_ssq(q) * (1.0 / d_qk) + EPS) * gq_ref[...]

    kvn = rms(ckv_ref[...].astype(F32), gckv_ref[...])
    kn = jnp.dot(kvn, wuk_ref[...], preferred_element_type=F32)
    vo_ref[...] = lax.dot_general(wuv_ref[...], kvn, (((1,), (1,)), ((), ())), preferred_element_type=F32).astype(BF16)
    kr = ckr_ref[...].astype(F32)
    ss_r = jnp.sum(kr * kr, axis=-1, keepdims=True)
    kr_placed = pltpu.roll(kr, MLA_NOPE, 1)
    k_scale = lax.rsqrt((head_ssq(kn) + ss_r) * (1.0 / d_qk) + EPS) * gk_ref[...]

    for h in range(N_HEADS):
        sl = slice(h * HEAD_PAD, (h + 1) * HEAD_PAD)
        qo_ref[:, sl] = rope(qn[:, sl]).astype(BF16)
        ko_ref[:, sl] = rope((kn[:, sl] + kr_placed) * k_scale[:, sl]).astype(BF16)


def _mla_prep(proj, cos_t, s1_t, s2_t, gcq, wuq, gckv, wuk, wuv_t, gq, gk, ones128, batch, seq):
    n = proj.shape[0]
    tm = min(512, seq)
    nt = seq // tm
    full = lambda shape: pl.BlockSpec(shape, lambda i: (0, 0))
    tab = pl.BlockSpec((tm, LANES), lambda i: (i, 0))
    wide = pl.BlockSpec((tm, N_HEADS * HEAD_PAD), lambda i: (i, 0))
    return pl.pallas_call(
        _mla_prep_kernel,
        grid=(n // tm,),
        in_specs=[
            pl.BlockSpec((tm, MLA_Q_RANK), lambda i: (i, COL_CQ // MLA_Q_RANK)),
            pl.BlockSpec((tm, MLA_KV_RANK), lambda i: (i, COL_CKV // LANES)),
            pl.BlockSpec((tm, LANES), lambda i: (i, COL_CKR // LANES)),
            tab, tab, tab,
            full((1, MLA_Q_RANK)), full((MLA_Q_RANK, N_HEADS * HEAD_PAD)),
            full((1, MLA_KV_RANK)), full((MLA_KV_RANK, N_HEADS * HEAD_PAD)), full((GROUP_W, MLA_KV_RANK)),
            full((1, N_HEADS * HEAD_PAD)), full((1, N_HEADS * HEAD_PAD)), full((256, 256)),
        ],
        out_specs=[wide, wide, pl.BlockSpec((None, GROUP_W, tm), lambda i: (i // nt, 0, i % nt))],
        out_shape=[jax.ShapeDtypeStruct((n, N_HEADS * HEAD_PAD), BF16)] * 2
        + [jax.ShapeDtypeStruct((batch, GROUP_W, seq), BF16)],
        compiler_params=_cparams(("parallel",)),
        name="mla_prep",
    )(proj, proj, proj, cos_t, s1_t, s2_t, gcq, wuq, gckv, wuk, wuv_t, gq, gk, ones128)


def _gla_kernel(q_ref, k_ref, v_ref, sm_ref, wh_ref, wl_ref, b_ref, go_ref, ones_ref, o_ref, st_ref, *, tc):
    hk = N_HEADS * GLA_DK
    ck = GLA_CHUNK

    @pl.when(pl.program_id(1) == 0)
    def _():
        st_ref[...] = jnp.zeros_like(st_ref)

    sm = sm_ref[...]
    sm_hi = sm.astype(BF16)
    sm_lo = (sm - sm_hi.astype(F32)).astype(BF16)
    z = (jnp.dot(sm_hi, wh_ref[...], preferred_element_type=F32)
         + jnp.dot(sm_lo, wh_ref[...], preferred_element_type=F32)
         + jnp.dot(sm_hi, wl_ref[...], preferred_element_type=F32)) + b_ref[...]
    g_all = _row_prefix_sum(_log_sigmoid(z) * (1.0 / GLA_NORMALIZER), ck)

    lane_k = lax.broadcasted_iota(jnp.int32, (ck, hk), 1) // GLA_DK
    lane_v = lax.broadcasted_iota(jnp.int32, (ck, GROUP_W), 1) // HEAD_DIM
    a_row = lax.broadcasted_iota(jnp.int32, (N_HEADS * ck, ck), 0) % ck
    a_col = lax.broadcasted_iota(jnp.int32, (N_HEADS * ck, ck), 1)
    st_blockdiag = (lax.broadcasted_iota(jnp.int32, (GROUP_W, hk), 0) // HEAD_DIM
                    == lax.broadcasted_iota(jnp.int32, (GROUP_W, hk), 1) // GLA_DK)

    outs = []
    for c in range(tc // ck):
        rows = slice(c * ck, (c + 1) * ck)
        g = g_all[rows]
        q = q_ref[rows, :].astype(F32) * (GLA_DK ** -0.5)
        k = k_ref[rows, :].astype(F32)
        v = v_ref[rows, :]
        g_last = g[ck - 1:ck, :]
        qt = (q * jnp.exp(g)).astype(BF16)
        kt = (k * jnp.exp(-g)).astype(BF16)
        kh = (k * jnp.exp(g_last - g)).astype(BF16)

        q_stack = jnp.concatenate([jnp.where(lane_k == h, qt, jnp.zeros_like(qt)) for h in range(N_HEADS)], axis=0)
        a = lax.dot_general(q_stack, kt, (((1,), (1,)), ((), ())), preferred_element_type=F32)
        a = jnp.where(a_row >= a_col, a, 0.0).astype(BF16)
        r = jnp.dot(a, v, preferred_element_type=F32)
        o = lax.dot_general(qt, st_ref[...].astype(BF16), (((1,), (1,)), ((), ())), preferred_element_type=F32)
        for h in range(N_HEADS):
            o = o + jnp.where(lane_v == h, r[h * ck:(h + 1) * ck, :], 0.0)
        outs.append(o)

        u = lax.dot_general(v, kh, (((0,), (0,)), ((), ())), preferred_element_type=F32)
        st_ref[...] = st_ref[...] * jnp.exp(g_last) + jnp.where(st_blockdiag, u, 0.0)

    o = jnp.concatenate(outs, axis=0)
    ssq = jnp.dot((o * o).astype(BF16), ones_ref[...], preferred_element_type=F32)
    o_ref[...] = (o * lax.rsqrt(ssq * (1.0 / HEAD_DIM) + EPS) * go_ref[...]).astype(BF16)


def _gla(proj, small, w_hi, w_lo, b_up, g_out, ones64, batch, seq):
    n = proj.shape[0]
    tc = min(256, seq)
    nt = seq // tc
    hk = N_HEADS * GLA_DK
    full = lambda shape: pl.BlockSpec(shape, lambda b, t: (0, 0))
    return pl.pallas_call(
        functools.partial(_gla_kernel, tc=tc),
        grid=(batch, nt),
        in_specs=[
            pl.BlockSpec((tc, hk), lambda b, t: (b * nt + t, COL_AQ // hk)),
            pl.BlockSpec((tc, hk), lambda b, t: (b * nt + t, COL_AK // hk)),
            pl.BlockSpec((tc, GROUP_W), lambda b, t: (b * nt + t, COL_AV // GROUP_W)),
            pl.BlockSpec((tc, LANES), lambda b, t: (b * nt + t, 0)),
            full((LANES, hk)), full((LANES, hk)), full((1, hk)), full((1, GROUP_W)), full((GROUP_W, GROUP_W)),
        ],
        out_specs=pl.BlockSpec((tc, GROUP_W), lambda b, t: (b * nt + t, 0)),
        out_shape=jax.ShapeDtypeStruct((n, GROUP_W), BF16),
        scratch_shapes=[pltpu.VMEM((GROUP_W, hk), F32)],
        compiler_params=_cparams(("parallel", "arbitrary")),
        name="gla",
    )(proj, proj, proj, small, w_hi, w_lo, b_up, g_out, ones64)


def _lru_kernel(x_ref, cw_ref, cb_ref, wr_ref, br_ref, wi_ref, bi_ref, lam_ref, o_ref, xpad_ref, h_ref, *, tr):
    @pl.when(pl.program_id(1) == 0)
    def _():
        xpad_ref[0:8, :] = jnp.zeros((8, GROUP_W), F32)
        h_ref[...] = jnp.zeros_like(h_ref)

    xpad_ref[8:8 + tr, :] = x_ref[...].astype(F32)
    xc = cb_ref[...]
    for j in range(CONV_W):
        off = 8 - (CONV_W - 1) + j
        xc = xc + cw_ref[j:j + 1, :] * xpad_ref[off:off + tr, :]
    xpad_ref[0:8, :] = xpad_ref[tr:tr + 8, :]

    xb = xc.astype(BF16)
    r = _sigmoid(jnp.dot(xb, wr_ref[...], preferred_element_type=F32) + br_ref[...])
    i = _sigmoid(jnp.dot(xb, wi_ref[...], preferred_element_type=F32) + bi_ref[...])
    lam = lam_ref[...]
    softplus = jnp.maximum(-lam, 0.0) + jnp.log1p(jnp.exp(-jnp.abs(lam)))
    log_a = (-LRU_C) * r * softplus
    a = jnp.exp(log_a)
    bx = jnp.sqrt(1.0 - jnp.exp(2.0 * log_a)) * (i * xc)

    rid = lax.broadcasted_iota(jnp.int32, (tr, GROUP_W), 0)
    s = 1
    while s < tr:
        keep = rid >= s
        a_prev = jnp.where(keep, pltpu.roll(a, s, 0), 1.0)
        b_prev = jnp.where(keep, pltpu.roll(bx, s, 0), 0.0)
        bx = a * b_prev + bx
        a = a * a_prev
        s *= 2
    h = a * h_ref[0:1, :] + bx
    h_ref[0:1, :] = h[tr - 1:tr, :]
    o_ref[...] = h.astype(BF16)


def _lru(proj, conv_w, conv_b, wr, br, wi, bi, lam, batch, seq):
    n = proj.shape[0]
    tr = min(256, seq)
    nt = seq // tr
    full = lambda shape: pl.BlockSpec(shape, lambda b, t: (0, 0))
    return pl.pallas_call(
        functools.partial(_lru_kernel, tr=tr),
        grid=(batch, nt),
        in_specs=[
            pl.BlockSpec((tr, GROUP_W), lambda b, t: (b * nt + t, COL_DX // GROUP_W)),
            full((8, GROUP_W)), full((1, GROUP_W)),
            full((GROUP_W, GROUP_W)), full((1, GROUP_W)),
            full((GROUP_W, GROUP_W)), full((1, GROUP_W)), full((1, GROUP_W)),
        ],
        out_specs=pl.BlockSpec((tr, GROUP_W), lambda b, t: (b * nt + t, 0)),
        out_shape=jax.ShapeDtypeStruct((n, GROUP_W), BF16),
        scratch_shapes=[pltpu.VMEM((tr + 8, GROUP_W), F32), pltpu.VMEM((8, GROUP_W), F32)],
        compiler_params=_cparams(("parallel", "arbitrary")),
        name="rglru",
    )(proj, conv_w, conv_b, wr, br, wi, bi, lam)


def _pad_cols(w, width):
    return jnp.pad(w, [(0, 0)] * (w.ndim - 1) + [(0, width - w.shape[-1])])


def _block_diag_ones(n_blocks, size):
    return jnp.asarray(np.kron(np.eye(n_blocks, dtype=np.float32), np.ones((size, size), np.float32)), BF16)


def _pad_heads(w, d_real):
    lead = w.shape[:-1]
    w = w.reshape(lead + (N_HEADS, d_real))
    w = jnp.pad(w, [(0, 0)] * len(lead) + [(0, 0), (0, HEAD_PAD - d_real)])
    return w.reshape(lead + (N_HEADS * HEAD_PAD,))


def kernel(x, positions, norm_g, w_in, w_out, gla_w_up, gla_b_up, gla_g_out, fox_b_f, fox_g_q, fox_g_k, mla_g_cq, mla_w_uq, mla_g_ckv, mla_w_ukv, mla_g_q, mla_g_k, lru_conv_w, lru_conv_b, lru_w_r, lru_b_r, lru_w_i, lru_b_i, lru_lam):
    batch, seq, _ = x.shape
    depth = w_in.shape[0]
    n = batch * seq

    sizes = (D_MIX, 256, 256, 512, GLA_RANK, 512, 512, 512, N_HEADS, MLA_Q_RANK, MLA_KV_RANK, MLA_ROPE, GROUP_W)
    offs = np.concatenate([[0], np.cumsum(sizes)])
    seg = [w_in[:, :, offs[i]:offs[i + 1]] for i in range(len(sizes))]
    (w_gate, w_aq, w_ak, w_av, w_ag, w_bq, w_bk, w_bv, w_bf, w_cq, w_ckv, w_ckr, w_dx) = seg
    w_main = jnp.concatenate(
        [w_gate, w_aq, w_ak, w_av, w_bq, w_bk, w_bv, w_cq, w_ckv, _pad_cols(w_ckr, LANES), w_dx], axis=-1).astype(BF16)
    w_small = _pad_cols(jnp.concatenate([w_ag, w_bf], axis=-1), LANES).astype(BF16)
    w_out_b = w_out.astype(BF16)

    ones64 = _block_diag_ones(GROUP_W // HEAD_DIM, HEAD_DIM)
    ones128 = _block_diag_ones(2, HEAD_PAD)

    wup = jnp.pad(gla_w_up, ((0, 0), (0, LANES - GLA_RANK), (0, 0)))
    wup_hi = wup.astype(BF16)
    wup_lo = (wup - wup_hi.astype(F32)).astype(BF16)

    fox_scale = HEAD_DIM ** -0.5 * LOG2E
    mla_scale = (MLA_NOPE + MLA_ROPE) ** -0.5 * LOG2E
    fox_gq = jnp.tile(fox_g_q, (1, N_HEADS)) * fox_scale
    fox_gk = jnp.tile(fox_g_k, (1, N_HEADS))
    fox_bf = jnp.pad(fox_b_f, ((0, 0), (SM_BF, LANES - SM_BF - N_HEADS)))

    wuq = _pad_heads(mla_w_uq, MLA_NOPE + MLA_ROPE).astype(BF16)
    wukv = mla_w_ukv.reshape(depth, MLA_KV_RANK, N_HEADS, 2 * HEAD_DIM)
    wuk = _pad_heads(wukv[..., :MLA_NOPE].reshape(depth, MLA_KV_RANK, N_HEADS * MLA_NOPE), MLA_NOPE).astype(BF16)
    wuv_t = jnp.swapaxes(wukv[..., MLA_NOPE:].reshape(depth, MLA_KV_RANK, GROUP_W), 1, 2).astype(BF16)
    eye = jnp.eye(GROUP_W, dtype=BF16)
    mla_gq = _pad_heads(jnp.tile(mla_g_q, (1, N_HEADS)), MLA_NOPE + MLA_ROPE) * mla_scale
    mla_gk = _pad_heads(jnp.tile(mla_g_k, (1, N_HEADS)), MLA_NOPE + MLA_ROPE)

    eye_h = jnp.eye(N_HEADS, dtype=F32)
    wr_bd = jnp.einsum("lncd,nm->lncmd", lru_w_r, eye_h).reshape(depth, GROUP_W, GROUP_W).astype(BF16)
    wi_bd = jnp.einsum("lncd,nm->lncmd", lru_w_i, eye_h).reshape(depth, GROUP_W, GROUP_W).astype(BF16)
    conv_w = jnp.pad(lru_conv_w, ((0, 0), (0, 8 - CONV_W), (0, 0)))

    half = MLA_ROPE // 2
    inv_freq = ROPE_THETA ** (-jnp.arange(half, dtype=F32) / half)
    invf_row = jnp.pad(jnp.concatenate([inv_freq, inv_freq]), (MLA_NOPE, LANES - MLA_NOPE - MLA_ROPE))[None, :]
    cos_t, s1_t, s2_t = _rope_tables(positions.reshape(n, 1), invf_row)

    xf = x.reshape(n, D_MODEL)
    for l in range(depth):
        proj, small = _norm_inproj(xf, norm_g[l][None, :], w_main, w_small, l)
        ya = _gla(proj, small, wup_hi[l], wup_lo[l], gla_b_up[l][None, :],
                  jnp.tile(gla_g_out[l], N_HEADS)[None, :], ones64, batch, seq)
        fq, fk, fvt = _fox_prep(proj, small, fox_bf[l][None, :], fox_gq[l][None, :], fox_gk[l][None, :], ones64, eye,
                                batch, seq)
        yb = _attention(fq, fk, fvt, batch, seq)
        mq, mk, mvt = _mla_prep(proj, cos_t, s1_t, s2_t, mla_g_cq[l][None, :], wuq[l], mla_g_ckv[l][None, :],
                                wuk[l], wuv_t[l], mla_gq[l][None, :], mla_gk[l][None, :], ones128, batch, seq)
        yc = _attention(mq, mk, mvt, batch, seq)
        yd = _lru(proj, conv_w[l], lru_conv_b[l][None, :], wr_bd[l], lru_b_r[l][None, :], wi_bd[l],
                  lru_b_i[l][None, :], lru_lam[l][None, :], batch, seq)
        xf = _gate_outproj(ya, yb, yc, yd, proj, w_out_b, xf, l)
    return xf.reshape(batch, seq, D_MODEL)
```

```python
import functools
import math

import numpy as np
import jax
import jax.numpy as jnp
from jax import lax
from jax.experimental import pallas as pl
from jax.experimental.pallas import tpu as pltpu

F32 = jnp.float32
BF16 = jnp.bfloat16

D_MODEL = 1024
D_MIX = 2048
GROUP_W = 512
N_HEADS = 8
HEAD_DIM = 64
EPS = 1e-6
GLA_DK = 32
GLA_RANK = 16
GLA_NORMALIZER = 16.0
GLA_CHUNK = 64
MLA_Q_RANK = 256
MLA_KV_RANK = 128
MLA_NOPE = 64
MLA_ROPE = 32
ROPE_THETA = 10000.0
CONV_W = 4
LRU_C = 8.0
LOG2E = 1.4426950408889634
NEG_BIG = -1e30

LANES = 128
HEAD_PAD = 128

COL_GATE = 0
COL_AQ, COL_AK, COL_AV = 2048, 2304, 2560
COL_BQ, COL_BK, COL_BV = 3072, 3584, 4096
COL_CQ, COL_CKV, COL_CKR = 4608, 4864, 4992
COL_DX = 5120
D_PROJ = 5632
SM_AG, SM_BF = 0, 16


def _cparams(sem, vmem_mb=48):
    return pltpu.CompilerParams(dimension_semantics=sem, vmem_limit_bytes=vmem_mb * 1024 * 1024)


def _sigmoid(z):
    return 0.5 * jnp.tanh(0.5 * z) + 0.5


def _log_sigmoid(z):
    return jnp.minimum(z, 0.0) - jnp.log1p(jnp.exp(-jnp.abs(z)))


def _row_prefix_sum(v, seg):
    rows = v.shape[0]
    rid = lax.broadcasted_iota(jnp.int32, v.shape, 0) % seg
    s = 1
    while s < seg:
        v = v + jnp.where(rid >= s, pltpu.roll(v, s, 0), 0.0)
        s *= 2
    del rows
    return v


def _inproj_kernel(x_ref, g_ref, w_ref, ws_ref, o_ref, os_ref, h_ref):
    @pl.when(pl.program_id(1) == 0)
    def _():
        x = x_ref[...]
        ms = jnp.mean(x * x, axis=-1, keepdims=True)
        h = (x * lax.rsqrt(ms + EPS) * g_ref[...]).astype(BF16)
        h_ref[...] = h
        os_ref[...] = jnp.dot(h, ws_ref[...], preferred_element_type=F32)

    o_ref[...] = jnp.dot(h_ref[...], w_ref[...], preferred_element_type=F32).astype(BF16)


def _norm_inproj(x, g, w_all, ws_all, layer):
    n = x.shape[0]
    tm = min(1024, n)
    tn = D_PROJ // 2
    return pl.pallas_call(
        _inproj_kernel,
        grid=(n // tm, D_PROJ // tn),
        in_specs=[
            pl.BlockSpec((tm, D_MODEL), lambda i, j: (i, 0)),
            pl.BlockSpec((1, D_MODEL), lambda i, j: (0, 0)),
            pl.BlockSpec((None, D_MODEL, tn), lambda i, j: (layer, 0, j)),
            pl.BlockSpec((None, D_MODEL, LANES), lambda i, j: (layer, 0, 0)),
        ],
        out_specs=[
            pl.BlockSpec((tm, tn), lambda i, j: (i, j)),
            pl.BlockSpec((tm, LANES), lambda i, j: (i, 0)),
        ],
        out_shape=[jax.ShapeDtypeStruct((n, D_PROJ), BF16), jax.ShapeDtypeStruct((n, LANES), F32)],
        scratch_shapes=[pltpu.VMEM((tm, D_MODEL), BF16)],
        compiler_params=_cparams(("parallel", "arbitrary"), vmem_mb=56),
        name="norm_inproj",
    )(x, g, w_all, ws_all)


def _outproj_kernel(ya_ref, yb_ref, yc_ref, yd_ref, gate_ref, w_ref, x_ref, o_ref, z_ref):
    for gi, y_ref in enumerate((ya_ref, yb_ref, yc_ref, yd_ref)):
        lo, hi = gi * GROUP_W, (gi + 1) * GROUP_W
        g = gate_ref[:, lo:hi].astype(F32)
        z_ref[:, lo:hi] = (y_ref[...].astype(F32) * (g * _sigmoid(g))).astype(BF16)
    o_ref[...] = x_ref[...] + jnp.dot(z_ref[...], w_ref[...], preferred_element_type=F32)


def _gate_outproj(ya, yb, yc, yd, proj, w_all, x, layer):
    n = x.shape[0]
    tm = min(512, n)
    yspec = pl.BlockSpec((tm, GROUP_W), lambda i: (i, 0))
    return pl.pallas_call(
        _outproj_kernel,
        grid=(n // tm,),
        in_specs=[
            yspec, yspec, yspec, yspec,
            pl.BlockSpec((tm, D_MIX), lambda i: (i, 0)),
            pl.BlockSpec((None, D_MIX, D_MODEL), lambda i: (layer, 0, 0)),
            pl.BlockSpec((tm, D_MODEL), lambda i: (i, 0)),
        ],
        out_specs=pl.BlockSpec((tm, D_MODEL), lambda i: (i, 0)),
        out_shape=jax.ShapeDtypeStruct((n, D_MODEL), F32),
        scratch_shapes=[pltpu.VMEM((tm, D_MIX), BF16)],
        compiler_params=_cparams(("parallel",)),
        name="gate_outproj",
    )(ya, yb, yc, yd, proj, w_all, x)


def _attn_kernel(q_ref, k_ref, vt_ref, o_ref, m_ref, l_ref, acc_ref, st_ref, *, tq, nq):
    i = pl.program_id(2)
    half = tq // 2
    nt_dims = (((1,), (1,)), ((), ()))

    m_ref[...] = jnp.full(m_ref.shape, NEG_BIG, F32)
    l_ref[...] = jnp.zeros(l_ref.shape, F32)
    acc_ref[...] = jnp.zeros(acc_ref.shape, F32)

    def update(h, cols, st, vt):
        m = m_ref[h, :, cols]
        m_new = jnp.maximum(m, jnp.max(st, axis=0, keepdims=True))
        alpha = jnp.exp2(m - m_new)
        p = jnp.exp2(st - m_new)
        m_ref[h, :, cols] = m_new
        l_ref[h, :, cols] = alpha * l_ref[h, :, cols] + jnp.sum(p, axis=0, keepdims=True)
        acc_ref[h, :, cols] = alpha * acc_ref[h, :, cols] + jnp.dot(vt, p.astype(BF16), preferred_element_type=F32)

    def scores(h, krows, qrows):
        lanes = slice(h * HEAD_PAD, (h + 1) * HEAD_PAD)
        return lax.dot_general(k_ref[krows, lanes], q_ref[qrows, lanes], nt_dims, preferred_element_type=F32)

    def vrows(h):
        return slice(h * HEAD_DIM, (h + 1) * HEAD_DIM)

    tri = lax.broadcasted_iota(jnp.int32, (half, half), 0) <= lax.broadcasted_iota(jnp.int32, (half, half), 1)
    low = lax.broadcasted_iota(jnp.int32, (tq, half), 0) <= lax.broadcasted_iota(jnp.int32, (tq, half), 1) + half

    def diag_scores(h, off):
        st_ref[0, h, 0:half, 0:half] = scores(h, slice(off, off + half), slice(0, half))
        st_ref[0, h, :, half:tq] = scores(h, slice(off, off + tq), slice(half, tq))

    def diag_update(h, off):
        st = st_ref[0, h, 0:half, 0:half]
        update(h, slice(0, half), jnp.where(tri, st, NEG_BIG), vt_ref[vrows(h), off:off + half])
        st = st_ref[0, h, :, half:tq]
        update(h, slice(half, tq), jnp.where(low, st, NEG_BIG), vt_ref[vrows(h), off:off + tq])

    for c in range(nq):
        @pl.when(i == c)
        def _(c=c):
            for h in range(2):
                diag_scores(h, c * tq)
            for stage in range(c + 1):
                cur, nxt = stage % 2, (stage + 1) % 2
                for h in range(2):
                    if stage < c:
                        s_next = scores(h, slice(stage * tq, (stage + 1) * tq), slice(0, tq))
                    if stage == 0:
                        diag_update(h, c * tq)
                    else:
                        blk = stage - 1
                        update(h, slice(0, tq), st_ref[cur, h], vt_ref[vrows(h), blk * tq:(blk + 1) * tq])
                    if stage < c:
                        st_ref[nxt, h] = s_next

    ot = jnp.concatenate([acc_ref[0] / l_ref[0], acc_ref[1] / l_ref[1]], axis=0)
    o_ref[...] = ot.T.astype(BF16)


def _attention(q, k, vt, batch, seq):
    n = q.shape[0]
    tq = min(512, seq)
    nq = seq // tq
    return pl.pallas_call(
        functools.partial(_attn_kernel, tq=tq, nq=nq),
        grid=(batch, N_HEADS // 2, nq),
        in_specs=[
            pl.BlockSpec((tq, 2 * HEAD_PAD), lambda b, p, i: (b * nq + i, p)),
            pl.BlockSpec((seq, 2 * HEAD_PAD), lambda b, p, i: (b, p)),
            pl.BlockSpec((None, 2 * HEAD_DIM, seq), lambda b, p, i: (b, p, 0)),
        ],
        out_specs=pl.BlockSpec((tq, LANES), lambda b, p, i: (b * nq + i, p)),
        out_shape=jax.ShapeDtypeStruct((n, GROUP_W), BF16),
        scratch_shapes=[pltpu.VMEM((2, 1, tq), F32), pltpu.VMEM((2, 1, tq), F32), pltpu.VMEM((2, HEAD_DIM, tq), F32),
                        pltpu.VMEM((2, 2, tq, tq), F32)],
        compiler_params=_cparams(("parallel", "parallel", "arbitrary")),
        name="causal_attention",
    )(q, k, vt)


def _fox_prep_kernel(q_ref, k_ref, v_ref, sm_ref, bf_ref, gq_ref, gk_ref, ones_ref, eye_ref, qo_ref, ko_ref, vt_ref,
                     carry_ref, *, tt):
    @pl.when(pl.program_id(1) == 0)
    def _():
        carry_ref[...] = jnp.zeros_like(carry_ref)

    vt_ref[...] = lax.dot_general(eye_ref[...], v_ref[...], (((1,), (1,)), ((), ())),
                                  preferred_element_type=F32).astype(BF16)

    def headnorm(ref, g_ref):
        z = ref[...].astype(F32)
        ssq = jnp.dot((z * z).astype(BF16), ones_ref[...], preferred_element_type=F32)
        return z * lax.rsqrt(ssq * (1.0 / HEAD_DIM) + EPS) * g_ref[...]

    qn = headnorm(q_ref, gq_ref)
    kn = headnorm(k_ref, gk_ref)

    log_f = _log_sigmoid(sm_ref[...] + bf_ref[...])
    cum = _row_prefix_sum(log_f, tt) + carry_ref[0:1, :]
    carry_ref[0:1, :] = cum[tt - 1:tt, :]
    neg = cum * (-LOG2E)

    lane = lax.broadcasted_iota(jnp.int32, (tt, LANES), 1)
    for h in range(N_HEADS):
        pair, e = h // 2, h % 2
        in_head = (lane >= e * HEAD_DIM) & (lane < (e + 1) * HEAD_DIM)
        base = HEAD_DIM if e == 0 else 0
        cb = jnp.broadcast_to(neg[:, SM_BF + h:SM_BF + h + 1], (tt, LANES))
        hi = cb.astype(BF16).astype(F32)
        r1 = cb - hi
        mid = r1.astype(BF16).astype(F32)
        lo = r1 - mid
        aug_k = jnp.where(lane == base, hi, jnp.where(lane == base + 1, mid, jnp.where(lane == base + 2, lo, 0.0)))
        aug_q = jnp.where((lane >= base) & (lane < base + 3), 1.0, 0.0)
        kpair = kn[:, pair * LANES:(pair + 1) * LANES]
        qpair = qn[:, pair * LANES:(pair + 1) * LANES]
        ko_ref[:, h * HEAD_PAD:(h + 1) * HEAD_PAD] = jnp.where(in_head, kpair, aug_k).astype(BF16)
        qo_ref[:, h * HEAD_PAD:(h + 1) * HEAD_PAD] = jnp.where(in_head, qpair, aug_q).astype(BF16)


def _fox_prep(proj, small, bf_row, gq, gk, ones64, eye, batch, seq):
    n = proj.shape[0]
    tt = min(512, seq)
    nt = seq // tt
    row = lambda shape: pl.BlockSpec(shape, lambda b, t: (0, 0))
    return pl.pallas_call(
        functools.partial(_fox_prep_kernel, tt=tt),
        grid=(batch, nt),
        in_specs=[
            pl.BlockSpec((tt, GROUP_W), lambda b, t: (b * nt + t, COL_BQ // GROUP_W)),
            pl.BlockSpec((tt, GROUP_W), lambda b, t: (b * nt + t, COL_BK // GROUP_W)),
            pl.BlockSpec((tt, GROUP_W), lambda b, t: (b * nt + t, COL_BV // GROUP_W)),
            pl.BlockSpec((tt, LANES), lambda b, t: (b * nt + t, 0)),
            row((1, LANES)), row((1, GROUP_W)), row((1, GROUP_W)), row((GROUP_W, GROUP_W)), row((GROUP_W, GROUP_W)),
        ],
        out_specs=[
            pl.BlockSpec((tt, N_HEADS * HEAD_PAD), lambda b, t: (b * nt + t, 0)),
            pl.BlockSpec((tt, N_HEADS * HEAD_PAD), lambda b, t: (b * nt + t, 0)),
            pl.BlockSpec((None, GROUP_W, tt), lambda b, t: (b, 0, t)),
        ],
        out_shape=[jax.ShapeDtypeStruct((n, N_HEADS * HEAD_PAD), BF16)] * 2
        + [jax.ShapeDtypeStruct((batch, GROUP_W, seq), BF16)],
        scratch_shapes=[pltpu.VMEM((8, LANES), F32)],
        compiler_params=_cparams(("parallel", "arbitrary")),
        name="fox_prep",
    )(proj, proj, proj, small, bf_row, gq, gk, ones64, eye)


def _rope_kernel(pos_ref, invf_ref, c_ref, s1_ref, s2_ref):
    ang = pos_ref[...].astype(F32) * invf_ref[...]
    lane = lax.broadcasted_iota(jnp.int32, ang.shape, 1)
    half = MLA_ROPE // 2
    c, s = jnp.cos(ang), jnp.sin(ang)
    c_ref[...] = jnp.where((lane >= MLA_NOPE) & (lane < MLA_NOPE + MLA_ROPE), c, 1.0)
    s1_ref[...] = jnp.where((lane >= MLA_NOPE) & (lane < MLA_NOPE + half), -s, 0.0)
    s2_ref[...] = jnp.where((lane >= MLA_NOPE + half) & (lane < MLA_NOPE + MLA_ROPE), s, 0.0)


def _rope_tables(pos_col, invf_row):
    n = pos_col.shape[0]
    tm = min(1024, n)
    out = pl.BlockSpec((tm, LANES), lambda i: (i, 0))
    return pl.pallas_call(
        _rope_kernel,
        grid=(n // tm,),
        in_specs=[pl.BlockSpec((tm, 1), lambda i: (i, 0)), pl.BlockSpec((1, LANES), lambda i: (0, 0))],
        out_specs=[out, out, out],
        out_shape=[jax.ShapeDtypeStruct((n, LANES), F32)] * 3,
        compiler_params=_cparams(("parallel",)),
        name="rope_tables",
    )(pos_col, invf_row)


def _mla_prep_kernel(cq_ref, ckv_ref, ckr_ref, c_ref, s1_ref, s2_ref, gcq_ref, wuq_ref, gckv_ref, wuk_ref, wuv_ref,
                     gq_ref, gk_ref, ones_ref, qo_ref, ko_ref, vo_ref):
    d_qk = float(MLA_NOPE + MLA_ROPE)
    cos, s1, s2 = c_ref[...], s1_ref[...], s2_ref[...]
    half = MLA_ROPE // 2

    def rms(z, g):
        return (z * lax.rsqrt(jnp.mean(z * z, axis=-1, keepdims=True) + EPS) * g).astype(BF16)

    def head_ssq(z):
        zz = (z * z).astype(BF16)
        parts = [jnp.dot(zz[:, c * 256:(c + 1) * 256], ones_ref[...], preferred_element_type=F32) for c in range(4)]
        return jnp.concatenate(parts, axis=1)

    def rope(zh):
        return zh * cos + pltpu.roll(zh, HEAD_PAD - half, 1) * s1 + pltpu.roll(zh, half, 1) * s2

    q = jnp.dot(rms(cq_ref[...].astype(F32), gcq_ref[...]), wuq_ref[...], preferred_element_type=F32)
    qn = q * lax.rsqrt(head_ssq(q) * (1.0 / d_qk) + EPS) * gq_ref[...]

    kvn = rms(ckv_ref[...].astype(F32), gckv_ref[...])
    kn = jnp.dot(kvn, wuk_ref[...], preferred_element_type=F32)
    vo_ref[...] = lax.dot_general(wuv_ref[...], kvn, (((1,), (1,)), ((), ())), preferred_element_type=F32).astype(BF16)
    kr = ckr_ref[...].astype(F32)
    ss_r = jnp.sum(kr * kr, axis=-1, keepdims=True)
    kr_placed = pltpu.roll(kr, MLA_NOPE, 1)
    k_scale = lax.rsqrt((head_ssq(kn) + ss_r) * (1.0 / d_qk) + EPS) * gk_ref[...]

    for h in range(N_HEADS):
        sl = slice(h * HEAD_PAD, (h + 1) * HEAD_PAD)
        qo_ref[:, sl] = rope(qn[:, sl]).astype(BF16)
        ko_ref[:, sl] = rope((kn[:, sl] + kr_placed) * k_scale[:, sl]).astype(BF16)


def _mla_prep(proj, cos_t, s1_t, s2_t, gcq, wuq, gckv, wuk, wuv_t, gq, gk, ones128, batch, seq):
    n = proj.shape[0]
    tm = min(512, seq)
    nt = seq // tm
    full = lambda shape: pl.BlockSpec(shape, lambda i: (0, 0))
    tab = pl.BlockSpec((tm, LANES), lambda i: (i, 0))
    wide = pl.BlockSpec((tm, N_HEADS * HEAD_PAD), lambda i: (i, 0))
    return pl.pallas_call(
        _mla_prep_kernel,
        grid=(n // tm,),
        in_specs=[
            pl.BlockSpec((tm, MLA_Q_RANK), lambda i: (i, COL_CQ // MLA_Q_RANK)),
            pl.BlockSpec((tm, MLA_KV_RANK), lambda i: (i, COL_CKV // LANES)),
            pl.BlockSpec((tm, LANES), lambda i: (i, COL_CKR // LANES)),
            tab, tab, tab,
            full((1, MLA_Q_RANK)), full((MLA_Q_RANK, N_HEADS * HEAD_PAD)),
            full((1, MLA_KV_RANK)), full((MLA_KV_RANK, N_HEADS * HEAD_PAD)), full((GROUP_W, MLA_KV_RANK)),
            full((1, N_HEADS * HEAD_PAD)), full((1, N_HEADS * HEAD_PAD)), full((256, 256)),
        ],
        out_specs=[wide, wide, pl.BlockSpec((None, GROUP_W, tm), lambda i: (i // nt, 0, i % nt))],
        out_shape=[jax.ShapeDtypeStruct((n, N_HEADS * HEAD_PAD), BF16)] * 2
        + [jax.ShapeDtypeStruct((batch, GROUP_W, seq), BF16)],
        compiler_params=_cparams(("parallel",)),
        name="mla_prep",
    )(proj, proj, proj, cos_t, s1_t, s2_t, gcq, wuq, gckv, wuk, wuv_t, gq, gk, ones128)


def _gla_kernel(q_ref, k_ref, v_ref, sm_ref, wh_ref, wl_ref, b_ref, go_ref, ones_ref, o_ref, st_ref, *, tc):
    hk = N_HEADS * GLA_DK
    ck = GLA_CHUNK

    @pl.when(pl.program_id(1) == 0)
    def _():
        st_ref[...] = jnp.zeros_like(st_ref)

    sm = sm_ref[...]
    sm_hi = sm.astype(BF16)
    sm_lo = (sm - sm_hi.astype(F32)).astype(BF16)
    z = (jnp.dot(sm_hi, wh_ref[...], preferred_element_type=F32)
         + jnp.dot(sm_lo, wh_ref[...], preferred_element_type=F32)
         + jnp.dot(sm_hi, wl_ref[...], preferred_element_type=F32)) + b_ref[...]
    g_all = _row_prefix_sum(_log_sigmoid(z) * (1.0 / GLA_NORMALIZER), ck)

    lane_k = lax.broadcasted_iota(jnp.int32, (ck, hk), 1) // GLA_DK
    lane_v = lax.broadcasted_iota(jnp.int32, (ck, GROUP_W), 1) // HEAD_DIM
    a_row = lax.broadcasted_iota(jnp.int32, (N_HEADS * ck, ck), 0) % ck
    a_col = lax.broadcasted_iota(jnp.int32, (N_HEADS * ck, ck), 1)
    st_blockdiag = (lax.broadcasted_iota(jnp.int32, (GROUP_W, hk), 0) // HEAD_DIM
                    == lax.broadcasted_iota(jnp.int32, (GROUP_W, hk), 1) // GLA_DK)

    outs = []
    for c in range(tc // ck):
        rows = slice(c * ck, (c + 1) * ck)
        g = g_all[rows]
        q = q_ref[rows, :].astype(F32) * (GLA_DK ** -0.5)
        k = k_ref[rows, :].astype(F32)
        v = v_ref[rows, :]
        g_last = g[ck - 1:ck, :]
        qt = (q * jnp.exp(g)).astype(BF16)
        kt = (k * jnp.exp(-g)).astype(BF16)
        kh = (k * jnp.exp(g_last - g)).astype(BF16)

        q_stack = jnp.concatenate([jnp.where(lane_k == h, qt, jnp.zeros_like(qt)) for h in range(N_HEADS)], axis=0)
        a = lax.dot_general(q_stack, kt, (((1,), (1,)), ((), ())), preferred_element_type=F32)
        a = jnp.where(a_row >= a_col, a, 0.0).astype(BF16)
        r = jnp.dot(a, v, preferred_element_type=F32)
        o = lax.dot_general(qt, st_ref[...].astype(BF16), (((1,), (1,)), ((), ())), preferred_element_type=F32)
        for h in range(N_HEADS):
            o = o + jnp.where(lane_v == h, r[h * ck:(h + 1) * ck, :], 0.0)
        outs.append(o)

        u = lax.dot_general(v, kh, (((0,), (0,)), ((), ())), preferred_element_type=F32)
        st_ref[...] = st_ref[...] * jnp.exp(g_last) + jnp.where(st_blockdiag, u, 0.0)

    o = jnp.concatenate(outs, axis=0)
    ssq = jnp.dot((o * o).astype(BF16), ones_ref[...], preferred_element_type=F32)
    o_ref[...] = (o * lax.rsqrt(ssq * (1.0 / HEAD_DIM) + EPS) * go_ref[...]).astype(BF16)


def _gla(proj, small, w_hi, w_lo, b_up, g_out, ones64, batch, seq):
    n = proj.shape[0]
    tc = min(256, seq)
    nt = seq // tc
    hk = N_HEADS * GLA_DK
    full = lambda shape: pl.BlockSpec(shape, lambda b, t: (0, 0))
    return pl.pallas_call(
        functools.partial(_gla_kernel, tc=tc),
        grid=(batch, nt),
        in_specs=[
            pl.BlockSpec((tc, hk), lambda b, t: (b * nt + t, COL_AQ // hk)),
            pl.BlockSpec((tc, hk), lambda b, t: (b * nt + t, COL_AK // hk)),
            pl.BlockSpec((tc, GROUP_W), lambda b, t: (b * nt + t, COL_AV // GROUP_W)),
            pl.BlockSpec((tc, LANES), lambda b, t: (b * nt + t, 0)),
            full((LANES, hk)), full((LANES, hk)), full((1, hk)), full((1, GROUP_W)), full((GROUP_W, GROUP_W)),
        ],
        out_specs=pl.BlockSpec((tc, GROUP_W), lambda b, t: (b * nt + t, 0)),
        out_shape=jax.ShapeDtypeStruct((n, GROUP_W), BF16),
        scratch_shapes=[pltpu.VMEM((GROUP_W, hk), F32)],
        compiler_params=_cparams(("parallel", "arbitrary")),
        name="gla",
    )(proj, proj, proj, small, w_hi, w_lo, b_up, g_out, ones64)


def _lru_kernel(x_ref, cw_ref, cb_ref, wr_ref, br_ref, wi_ref, bi_ref, lam_ref, o_ref, xpad_ref, h_ref, *, tr):
    @pl.when(pl.program_id(1) == 0)
    def _():
        xpad_ref[0:8, :] = jnp.zeros((8, GROUP_W), F32)
        h_ref[...] = jnp.zeros_like(h_ref)

    xpad_ref[8:8 + tr, :] = x_ref[...].astype(F32)
    xc = cb_ref[...]
    for j in range(CONV_W):
        off = 8 - (CONV_W - 1) + j
        xc = xc + cw_ref[j:j + 1, :] * xpad_ref[off:off + tr, :]
    xpad_ref[0:8, :] = xpad_ref[tr:tr + 8, :]

    xb = xc.astype(BF16)
    r = _sigmoid(jnp.dot(xb, wr_ref[...], preferred_element_type=F32) + br_ref[...])
    i = _sigmoid(jnp.dot(xb, wi_ref[...], preferred_element_type=F32) + bi_ref[...])
    lam = lam_ref[...]
    softplus = jnp.maximum(-lam, 0.0) + jnp.log1p(jnp.exp(-jnp.abs(lam)))
    log_a = (-LRU_C) * r * softplus
    a = jnp.exp(log_a)
    bx = jnp.sqrt(1.0 - jnp.exp(2.0 * log_a)) * (i * xc)

    a3 = a.reshape(tr // 8, 8, GROUP_W)
    b3 = bx.reshape(tr // 8, 8, GROUP_W)
    sub = lax.broadcasted_iota(jnp.int32, a3.shape, 1)
    for s in (1, 2, 4):
        keep = sub >= s
        a_prev = jnp.where(keep, pltpu.roll(a3, s, 1), 1.0)
        b_prev = jnp.where(keep, pltpu.roll(b3, s, 1), 0.0)
        b3 = a3 * b_prev + b3
        a3 = a3 * a_prev
    h_prev = h_ref[0:1, :]
    for g in range(tr // 8):
        hg = a3[g] * h_prev + b3[g]
        o_ref[g * 8:(g + 1) * 8, :] = hg.astype(BF16)
        h_prev = hg[7:8, :]
    h_ref[0:1, :] = h_prev


def _lru(proj, conv_w, conv_b, wr, br, wi, bi, lam, batch, seq):
    n = proj.shape[0]
    tr = min(256, seq)
    nt = seq // tr
    full = lambda shape: pl.BlockSpec(shape, lambda b, t: (0, 0))
    return pl.pallas_call(
        functools.partial(_lru_kernel, tr=tr),
        grid=(batch, nt),
        in_specs=[
            pl.BlockSpec((tr, GROUP_W), lambda b, t: (b * nt + t, COL_DX // GROUP_W)),
            full((8, GROUP_W)), full((1, GROUP_W)),
            full((GROUP_W, GROUP_W)), full((1, GROUP_W)),
            full((GROUP_W, GROUP_W)), full((1, GROUP_W)), full((1, GROUP_W)),
        ],
        out_specs=pl.BlockSpec((tr, GROUP_W), lambda b, t: (b * nt + t, 0)),
        out_shape=jax.ShapeDtypeStruct((n, GROUP_W), BF16),
        scratch_shapes=[pltpu.VMEM((tr + 8, GROUP_W), F32), pltpu.VMEM((8, GROUP_W), F32)],
        compiler_params=_cparams(("parallel", "arbitrary")),
        name="rglru",
    )(proj, conv_w, conv_b, wr, br, wi, bi, lam)


def _pad_cols(w, width):
    return jnp.pad(w, [(0, 0)] * (w.ndim - 1) + [(0, width - w.shape[-1])])


def _block_diag_ones(n_blocks, size):
    return jnp.asarray(np.kron(np.eye(n_blocks, dtype=np.float32), np.ones((size, size), np.float32)), BF16)


def _pad_heads(w, d_real):
    lead = w.shape[:-1]
    w = w.reshape(lead + (N_HEADS, d_real))
    w = jnp.pad(w, [(0, 0)] * len(lead) + [(0, 0), (0, HEAD_PAD - d_real)])
    return w.reshape(lead + (N_HEADS * HEAD_PAD,))


def kernel(x, positions, norm_g, w_in, w_out, gla_w_up, gla_b_up, gla_g_out, fox_b_f, fox_g_q, fox_g_k, mla_g_cq, mla_w_uq, mla_g_ckv, mla_w_ukv, mla_g_q, mla_g_k, lru_conv_w, lru_conv_b, lru_w_r, lru_b_r, lru_w_i, lru_b_i, lru_lam):
    batch, seq, _ = x.shape
    depth = w_in.shape[0]
    n = batch * seq

    sizes = (D_MIX, 256, 256, 512, GLA_RANK, 512, 512, 512, N_HEADS, MLA_Q_RANK, MLA_KV_RANK, MLA_ROPE, GROUP_W)
    offs = np.concatenate([[0], np.cumsum(sizes)])
    seg = [w_in[:, :, offs[i]:offs[i + 1]] for i in range(len(sizes))]
    (w_gate, w_aq, w_ak, w_av, w_ag, w_bq, w_bk, w_bv, w_bf, w_cq, w_ckv, w_ckr, w_dx) = seg
    w_main = jnp.concatenate(
        [w_gate, w_aq, w_ak, w_av, w_bq, w_bk, w_bv, w_cq, w_ckv, _pad_cols(w_ckr, LANES), w_dx], axis=-1).astype(BF16)
    w_small = _pad_cols(jnp.concatenate([w_ag, w_bf], axis=-1), LANES).astype(BF16)
    w_out_b = w_out.astype(BF16)

    ones64 = _block_diag_ones(GROUP_W // HEAD_DIM, HEAD_DIM)
    ones128 = _block_diag_ones(2, HEAD_PAD)

    wup = jnp.pad(gla_w_up, ((0, 0), (0, LANES - GLA_RANK), (0, 0)))
    wup_hi = wup.astype(BF16)
    wup_lo = (wup - wup_hi.astype(F32)).astype(BF16)

    fox_scale = HEAD_DIM ** -0.5 * LOG2E
    mla_scale = (MLA_NOPE + MLA_ROPE) ** -0.5 * LOG2E
    fox_gq = jnp.tile(fox_g_q, (1, N_HEADS)) * fox_scale
    fox_gk = jnp.tile(fox_g_k, (1, N_HEADS))
    fox_bf = jnp.pad(fox_b_f, ((0, 0), (SM_BF, LANES - SM_BF - N_HEADS)))

    wuq = _pad_heads(mla_w_uq, MLA_NOPE + MLA_ROPE).astype(BF16)
    wukv = mla_w_ukv.reshape(depth, MLA_KV_RANK, N_HEADS, 2 * HEAD_DIM)
    wuk = _pad_heads(wukv[..., :MLA_NOPE].reshape(depth, MLA_KV_RANK, N_HEADS * MLA_NOPE), MLA_NOPE).astype(BF16)
    wuv_t = jnp.swapaxes(wukv[..., MLA_NOPE:].reshape(depth, MLA_KV_RANK, GROUP_W), 1, 2).astype(BF16)
    eye = jnp.eye(GROUP_W, dtype=BF16)
    mla_gq = _pad_heads(jnp.tile(mla_g_q, (1, N_HEADS)), MLA_NOPE + MLA_ROPE) * mla_scale
    mla_gk = _pad_heads(jnp.tile(mla_g_k, (1, N_HEADS)), MLA_NOPE + MLA_ROPE)

    eye_h = jnp.eye(N_HEADS, dtype=F32)
    wr_bd = jnp.einsum("lncd,nm->lncmd", lru_w_r, eye_h).reshape(depth, GROUP_W, GROUP_W).astype(BF16)
    wi_bd = jnp.einsum("lncd,nm->lncmd", lru_w_i, eye_h).reshape(depth, GROUP_W, GROUP_W).astype(BF16)
    conv_w = jnp.pad(lru_conv_w, ((0, 0), (0, 8 - CONV_W), (0, 0)))

    half = MLA_ROPE // 2
    inv_freq = ROPE_THETA ** (-jnp.arange(half, dtype=F32) / half)
    invf_row = jnp.pad(jnp.concatenate([inv_freq, inv_freq]), (MLA_NOPE, LANES - MLA_NOPE - MLA_ROPE))[None, :]
    cos_t, s1_t, s2_t = _rope_tables(positions.reshape(n, 1), invf_row)

    xf = x.reshape(n, D_MODEL)
    for l in range(depth):
        proj, small = _norm_inproj(xf, norm_g[l][None, :], w_main, w_small, l)
        ya = _gla(proj, small, wup_hi[l], wup_lo[l], gla_b_up[l][None, :],
                  jnp.tile(gla_g_out[l], N_HEADS)[None, :], ones64, batch, seq)
        fq, fk, fvt = _fox_prep(proj, small, fox_bf[l][None, :], fox_gq[l][None, :], fox_gk[l][None, :], ones64, eye,
                                batch, seq)
        yb = _attention(fq, fk, fvt, batch, seq)
        mq, mk, mvt = _mla_prep(proj, cos_t, s1_t, s2_t, mla_g_cq[l][None, :], wuq[l], mla_g_ckv[l][None, :],
                                wuk[l], wuv_t[l], mla_gq[l][None, :], mla_gk[l][None, :], ones128, batch, seq)
        yc = _attention(mq, mk, mvt, batch, seq)
        yd = _lru(proj, conv_w[l], lru_conv_b[l][None, :], wr_bd[l], lru_b_r[l][None, :], wi_bd[l],
                  lru_b_i[l][None, :], lru_lam[l][None, :], batch, seq)
        xf = _gate_outproj(ya, yb, yc, yd, proj, w_out_b, xf, l)
    return xf.reshape(batch, seq, D_MODEL)
```

```python
import functools
import math

import numpy as np
import jax
import jax.numpy as jnp
from jax import lax
from jax.experimental import pallas as pl
from jax.experimental.pallas import tpu as pltpu

F32 = jnp.float32
BF16 = jnp.bfloat16

D_MODEL = 1024
D_MIX = 2048
GROUP_W = 512
N_HEADS = 8
HEAD_DIM = 64
EPS = 1e-6
GLA_DK = 32
GLA_RANK = 16
GLA_NORMALIZER = 16.0
GLA_CHUNK = 64
MLA_Q_RANK = 256
MLA_KV_RANK = 128
MLA_NOPE = 64
MLA_ROPE = 32
ROPE_THETA = 10000.0
CONV_W = 4
LRU_C = 8.0
LOG2E = 1.4426950408889634
NEG_BIG = -1e30

LANES = 128
HEAD_PAD = 128
VT_ROWS = 80

COL_GATE = 0
COL_AQ, COL_AK, COL_AV = 2048, 2304, 2560
COL_BQ, COL_BK, COL_BV = 3072, 3584, 4096
COL_CQ, COL_CKV, COL_CKR = 4608, 4864, 4992
COL_DX = 5120
D_PROJ = 5632
SM_AG, SM_BF = 0, 16


def _cparams(sem, vmem_mb=48):
    return pltpu.CompilerParams(dimension_semantics=sem, vmem_limit_bytes=vmem_mb * 1024 * 1024)


def _sigmoid(z):
    return 0.5 * jnp.tanh(0.5 * z) + 0.5


def _with_ones_rows(vt):
    row = lax.broadcasted_iota(jnp.int32, vt.shape, 0) % VT_ROWS
    return jnp.where(row == HEAD_DIM, 1.0, vt)


def _log_sigmoid(z):
    return jnp.minimum(z, 0.0) - jnp.log1p(jnp.exp(-jnp.abs(z)))


def _row_prefix_sum(v, seg):
    rows = v.shape[0]
    rid = lax.broadcasted_iota(jnp.int32, v.shape, 0) % seg
    s = 1
    while s < seg:
        v = v + jnp.where(rid >= s, pltpu.roll(v, s, 0), 0.0)
        s *= 2
    del rows
    return v


def _inproj_kernel(x_ref, g_ref, w_ref, ws_ref, o_ref, os_ref, h_ref):
    @pl.when(pl.program_id(1) == 0)
    def _():
        x = x_ref[...]
        ms = jnp.mean(x * x, axis=-1, keepdims=True)
        h = (x * lax.rsqrt(ms + EPS) * g_ref[...]).astype(BF16)
        h_ref[...] = h
        os_ref[...] = jnp.dot(h, ws_ref[...], preferred_element_type=F32)

    o_ref[...] = jnp.dot(h_ref[...], w_ref[...], preferred_element_type=F32).astype(BF16)


def _norm_inproj(x, g, w_all, ws_all, layer):
    n = x.shape[0]
    tm = min(1024, n)
    tn = D_PROJ // 2
    return pl.pallas_call(
        _inproj_kernel,
        grid=(n // tm, D_PROJ // tn),
        in_specs=[
            pl.BlockSpec((tm, D_MODEL), lambda i, j: (i, 0)),
            pl.BlockSpec((1, D_MODEL), lambda i, j: (0, 0)),
            pl.BlockSpec((None, D_MODEL, tn), lambda i, j: (layer, 0, j)),
            pl.BlockSpec((None, D_MODEL, LANES), lambda i, j: (layer, 0, 0)),
        ],
        out_specs=[
            pl.BlockSpec((tm, tn), lambda i, j: (i, j)),
            pl.BlockSpec((tm, LANES), lambda i, j: (i, 0)),
        ],
        out_shape=[jax.ShapeDtypeStruct((n, D_PROJ), BF16), jax.ShapeDtypeStruct((n, LANES), F32)],
        scratch_shapes=[pltpu.VMEM((tm, D_MODEL), BF16)],
        compiler_params=_cparams(("parallel", "arbitrary"), vmem_mb=56),
        name="norm_inproj",
    )(x, g, w_all, ws_all)


def _outproj_kernel(ya_ref, yb_ref, yc_ref, yd_ref, gate_ref, w_ref, x_ref, o_ref, z_ref):
    for gi, y_ref in enumerate((ya_ref, yb_ref, yc_ref, yd_ref)):
        lo, hi = gi * GROUP_W, (gi + 1) * GROUP_W
        g = gate_ref[:, lo:hi].astype(F32)
        z_ref[:, lo:hi] = (y_ref[...].astype(F32) * (g * _sigmoid(g))).astype(BF16)
    o_ref[...] = x_ref[...] + jnp.dot(z_ref[...], w_ref[...], preferred_element_type=F32)


def _gate_outproj(ya, yb, yc, yd, proj, w_all, x, layer):
    n = x.shape[0]
    tm = min(512, n)
    yspec = pl.BlockSpec((tm, GROUP_W), lambda i: (i, 0))
    return pl.pallas_call(
        _outproj_kernel,
        grid=(n // tm,),
        in_specs=[
            yspec, yspec, yspec, yspec,
            pl.BlockSpec((tm, D_MIX), lambda i: (i, 0)),
            pl.BlockSpec((None, D_MIX, D_MODEL), lambda i: (layer, 0, 0)),
            pl.BlockSpec((tm, D_MODEL), lambda i: (i, 0)),
        ],
        out_specs=pl.BlockSpec((tm, D_MODEL), lambda i: (i, 0)),
        out_shape=jax.ShapeDtypeStruct((n, D_MODEL), F32),
        scratch_shapes=[pltpu.VMEM((tm, D_MIX), BF16)],
        compiler_params=_cparams(("parallel",)),
        name="gate_outproj",
    )(ya, yb, yc, yd, proj, w_all, x)


def _attn_kernel(q_ref, k_ref, vt_ref, o_ref, m_ref, acc_ref, st_ref, mx_ref, *, tq, nq, hg):
    i = pl.program_id(2)
    half = tq // 2
    nt_dims = (((1,), (1,)), ((), ()))

    m_ref[...] = jnp.full(m_ref.shape, NEG_BIG, F32)
    acc_ref[...] = jnp.zeros(acc_ref.shape, F32)

    def scores(h, krows, qrows):
        lanes = slice(h * HEAD_PAD, (h + 1) * HEAD_PAD)
        return lax.dot_general(k_ref[krows, lanes], q_ref[qrows, lanes], nt_dims, preferred_element_type=F32)

    def vrows(h):
        return slice(h * VT_ROWS, (h + 1) * VT_ROWS)

    tri = lax.broadcasted_iota(jnp.int32, (half, half), 0) <= lax.broadcasted_iota(jnp.int32, (half, half), 1)
    low = lax.broadcasted_iota(jnp.int32, (tq, half), 0) <= lax.broadcasted_iota(jnp.int32, (tq, half), 1) + half

    def issue_scores(item, c):
        h, blk = item
        if blk is None:
            off = c * tq
            s_l = jnp.where(tri, scores(h, slice(off, off + half), slice(0, half)), NEG_BIG)
            s_r = jnp.where(low, scores(h, slice(off, off + tq), slice(half, tq)), NEG_BIG)
            mx = jnp.concatenate([jnp.max(s_l, axis=0, keepdims=True), jnp.max(s_r, axis=0, keepdims=True)], axis=1)
            return (s_l, s_r, mx)
        st = scores(h, slice(blk * tq, (blk + 1) * tq), slice(0, tq))
        return (st, jnp.max(st, axis=0, keepdims=True))

    def store_scores(item, slot, vals):
        if item[1] is None:
            st_ref[slot, 0:half, 0:half] = vals[0]
            st_ref[slot, :, half:tq] = vals[1]
        else:
            st_ref[slot] = vals[0]
        mx_ref[slot] = vals[-1]

    def consume(item, slot, c):
        h, blk = item
        m = m_ref[h]
        m_new = jnp.maximum(m, mx_ref[slot])
        alpha = jnp.exp2(m - m_new)
        m_ref[h] = m_new

        def accumulate(cols, st, vt):
            p = jnp.exp2(st - m_new[:, cols]).astype(BF16)
            acc_ref[h, :, cols] = alpha[:, cols] * acc_ref[h, :, cols] + jnp.dot(vt, p, preferred_element_type=F32)

        if blk is None:
            off = c * tq
            accumulate(slice(0, half), st_ref[slot, 0:half, 0:half], vt_ref[vrows(h), off:off + half])
            accumulate(slice(half, tq), st_ref[slot, :, half:tq], vt_ref[vrows(h), off:off + tq])
        else:
            accumulate(slice(0, tq), st_ref[slot], vt_ref[vrows(h), blk * tq:(blk + 1) * tq])

    for c in range(nq):
        @pl.when(i == c)
        def _(c=c):
            items = [(h, None) for h in range(hg)] + [(h, blk) for blk in range(c) for h in range(hg)]
            store_scores(items[0], 0, issue_scores(items[0], c))
            for n, item in enumerate(items):
                nxt = items[n + 1] if n + 1 < len(items) else None
                if nxt is not None:
                    vals = issue_scores(nxt, c)
                consume(item, n % 2, c)
                if nxt is not None:
                    store_scores(nxt, (n + 1) % 2, vals)

    ot = jnp.concatenate([acc_ref[h, 0:HEAD_DIM, :] / acc_ref[h, HEAD_DIM:HEAD_DIM + 1, :] for h in range(hg)], axis=0)
    o_ref[...] = ot.T.astype(BF16)


ATTN_HEADS_PER_STEP = 8


def _attention(q, k, vt, batch, seq):
    n = q.shape[0]
    tq = min(512, seq)
    nq = seq // tq
    hg = ATTN_HEADS_PER_STEP
    return pl.pallas_call(
        functools.partial(_attn_kernel, tq=tq, nq=nq, hg=hg),
        grid=(batch, N_HEADS // hg, nq),
        in_specs=[
            pl.BlockSpec((tq, hg * HEAD_PAD), lambda b, p, i: (b * nq + i, p)),
            pl.BlockSpec((seq, hg * HEAD_PAD), lambda b, p, i: (b, p)),
            pl.BlockSpec((None, hg * VT_ROWS, seq), lambda b, p, i: (b, p, 0)),
        ],
        out_specs=pl.BlockSpec((tq, hg * HEAD_DIM), lambda b, p, i: (b * nq + i, p)),
        out_shape=jax.ShapeDtypeStruct((n, GROUP_W), BF16),
        scratch_shapes=[pltpu.VMEM((hg, 1, tq), F32), pltpu.VMEM((hg, VT_ROWS, tq), F32),
                        pltpu.VMEM((2, tq, tq), F32), pltpu.VMEM((2, 1, tq), F32)],
        compiler_params=_cparams(("parallel", "parallel", "arbitrary")),
        name="causal_attention",
    )(q, k, vt)


def _fox_prep_kernel(q_ref, k_ref, v_ref, sm_ref, bf_ref, gq_ref, gk_ref, ones_ref, eye_ref, qo_ref, ko_ref, vt_ref,
                     carry_ref, *, tt):
    @pl.when(pl.program_id(1) == 0)
    def _():
        carry_ref[...] = jnp.zeros_like(carry_ref)

    vt = lax.dot_general(eye_ref[...], v_ref[...], (((1,), (1,)), ((), ())), preferred_element_type=F32)
    vt_ref[...] = _with_ones_rows(vt).astype(BF16)

    def headnorm(ref, g_ref):
        z = ref[...].astype(F32)
        ssq = jnp.dot((z * z).astype(BF16), ones_ref[...], preferred_element_type=F32)
        return z * lax.rsqrt(ssq * (1.0 / HEAD_DIM) + EPS) * g_ref[...]

    qn = headnorm(q_ref, gq_ref)
    kn = headnorm(k_ref, gk_ref)

    log_f = _log_sigmoid(sm_ref[...] + bf_ref[...])
    cum = _row_prefix_sum(log_f, tt) + carry_ref[0:1, :]
    carry_ref[0:1, :] = cum[tt - 1:tt, :]
    neg = cum * (-LOG2E)

    lane = lax.broadcasted_iota(jnp.int32, (tt, LANES), 1)
    for h in range(N_HEADS):
        pair, e = h // 2, h % 2
        in_head = (lane >= e * HEAD_DIM) & (lane < (e + 1) * HEAD_DIM)
        base = HEAD_DIM if e == 0 else 0
        cb = jnp.broadcast_to(neg[:, SM_BF + h:SM_BF + h + 1], (tt, LANES))
        hi = cb.astype(BF16).astype(F32)
        r1 = cb - hi
        mid = r1.astype(BF16).astype(F32)
        lo = r1 - mid
        aug_k = jnp.where(lane == base, hi, jnp.where(lane == base + 1, mid, jnp.where(lane == base + 2, lo, 0.0)))
        aug_q = jnp.where((lane >= base) & (lane < base + 3), 1.0, 0.0)
        kpair = kn[:, pair * LANES:(pair + 1) * LANES]
        qpair = qn[:, pair * LANES:(pair + 1) * LANES]
        ko_ref[:, h * HEAD_PAD:(h + 1) * HEAD_PAD] = jnp.where(in_head, kpair, aug_k).astype(BF16)
        qo_ref[:, h * HEAD_PAD:(h + 1) * HEAD_PAD] = jnp.where(in_head, qpair, aug_q).astype(BF16)


def _fox_prep(proj, small, bf_row, gq, gk, ones64, eye, batch, seq):
    n = proj.shape[0]
    tt = min(512, seq)
    nt = seq // tt
    row = lambda shape: pl.BlockSpec(shape, lambda b, t: (0, 0))
    return pl.pallas_call(
        functools.partial(_fox_prep_kernel, tt=tt),
        grid=(batch, nt),
        in_specs=[
            pl.BlockSpec((tt, GROUP_W), lambda b, t: (b * nt + t, COL_BQ // GROUP_W)),
            pl.BlockSpec((tt, GROUP_W), lambda b, t: (b * nt + t, COL_BK // GROUP_W)),
            pl.BlockSpec((tt, GROUP_W), lambda b, t: (b * nt + t, COL_BV // GROUP_W)),
            pl.BlockSpec((tt, LANES), lambda b, t: (b * nt + t, 0)),
            row((1, LANES)), row((1, GROUP_W)), row((1, GROUP_W)), row((GROUP_W, GROUP_W)),
            row((N_HEADS * VT_ROWS, GROUP_W)),
        ],
        out_specs=[
            pl.BlockSpec((tt, N_HEADS * HEAD_PAD), lambda b, t: (b * nt + t, 0)),
            pl.BlockSpec((tt, N_HEADS * HEAD_PAD), lambda b, t: (b * nt + t, 0)),
            pl.BlockSpec((None, N_HEADS * VT_ROWS, tt), lambda b, t: (b, 0, t)),
        ],
        out_shape=[jax.ShapeDtypeStruct((n, N_HEADS * HEAD_PAD), BF16)] * 2
        + [jax.ShapeDtypeStruct((batch, N_HEADS * VT_ROWS, seq), BF16)],
        scratch_shapes=[pltpu.VMEM((8, LANES), F32)],
        compiler_params=_cparams(("parallel", "arbitrary")),
        name="fox_prep",
    )(proj, proj, proj, small, bf_row, gq, gk, ones64, eye)


def _rope_kernel(pos_ref, invf_ref, c_ref, s_ref):
    ang = pos_ref[...].astype(F32) * invf_ref[...]
    lane = lax.broadcasted_iota(jnp.int32, ang.shape, 1)
    half = MLA_ROPE // 2
    c, s = jnp.cos(ang), jnp.sin(ang)
    c_ref[...] = jnp.where((lane >= MLA_NOPE) & (lane < MLA_NOPE + MLA_ROPE), c, 1.0)
    s_ref[...] = jnp.where((lane >= MLA_NOPE) & (lane < MLA_NOPE + half), -s,
                           jnp.where((lane >= MLA_NOPE + half) & (lane < MLA_NOPE + MLA_ROPE), s, 0.0))


def _rope_tables(pos_col, invf_row):
    n = pos_col.shape[0]
    tm = min(1024, n)
    out = pl.BlockSpec((tm, LANES), lambda i: (i, 0))
    return pl.pallas_call(
        _rope_kernel,
        grid=(n // tm,),
        in_specs=[pl.BlockSpec((tm, 1), lambda i: (i, 0)), pl.BlockSpec((1, LANES), lambda i: (0, 0))],
        out_specs=[out, out],
        out_shape=[jax.ShapeDtypeStruct((n, LANES), F32)] * 2,
        compiler_params=_cparams(("parallel",)),
        name="rope_tables",
    )(pos_col, invf_row)


def _mla_prep_kernel(cq_ref, ckv_ref, ckr_ref, c_ref, s_ref, gcq_ref, wuq_ref, wuqs_ref, gckv_ref, wuk_ref, wuv_ref,
                     gq_ref, gqs_ref, gk_ref, gks_ref, ones_ref, qo_ref, ko_ref, vo_ref):
    d_qk = float(MLA_NOPE + MLA_ROPE)
    cos, sin = c_ref[...], s_ref[...]
    half = MLA_ROPE // 2

    def rms(z, g):
        return (z * lax.rsqrt(jnp.mean(z * z, axis=-1, keepdims=True) + EPS) * g).astype(BF16)

    def head_ssq(z):
        zz = (z * z).astype(BF16)
        parts = [jnp.dot(zz[:, c * 256:(c + 1) * 256], ones_ref[...], preferred_element_type=F32) for c in range(4)]
        return jnp.concatenate(parts, axis=1)

    cn = rms(cq_ref[...].astype(F32), gcq_ref[...])
    q = jnp.dot(cn, wuq_ref[...], preferred_element_type=F32)
    q_sw = jnp.dot(cn, wuqs_ref[...], preferred_element_type=F32)
    q_inv = lax.rsqrt(head_ssq(q) * (1.0 / d_qk) + EPS)
    qn = q * (q_inv * gq_ref[...])
    qn_sw = q_sw * (q_inv * gqs_ref[...])

    kvn = rms(ckv_ref[...].astype(F32), gckv_ref[...])
    kn = jnp.dot(kvn, wuk_ref[...], preferred_element_type=F32)
    vt = lax.dot_general(wuv_ref[...], kvn, (((1,), (1,)), ((), ())), preferred_element_type=F32)
    vo_ref[...] = _with_ones_rows(vt).astype(BF16)
    kr = ckr_ref[...].astype(F32)
    ss_r = jnp.sum(kr * kr, axis=-1, keepdims=True)
    lane = lax.broadcasted_iota(jnp.int32, kr.shape, 1)
    kr_placed = pltpu.roll(kr, MLA_NOPE, 1)
    kr_swapped = jnp.where(lane < MLA_NOPE + half, pltpu.roll(kr, MLA_NOPE - half, 1), pltpu.roll(kr, MLA_NOPE + half, 1))
    kr_swapped = jnp.where((lane >= MLA_NOPE) & (lane < MLA_NOPE + MLA_ROPE), kr_swapped, 0.0)
    k_inv = lax.rsqrt((head_ssq(kn) + ss_r) * (1.0 / d_qk) + EPS)
    k_scale = k_inv * gk_ref[...]
    k_scale_sw = k_inv * gks_ref[...]

    for h in range(N_HEADS):
        sl = slice(h * HEAD_PAD, (h + 1) * HEAD_PAD)
        qo_ref[:, sl] = (qn[:, sl] * cos + qn_sw[:, sl] * sin).astype(BF16)
        k_h = (kn[:, sl] + kr_placed) * k_scale[:, sl]
        ko_ref[:, sl] = (k_h * cos + (kr_swapped * k_scale_sw[:, sl]) * sin).astype(BF16)


def _mla_prep(proj, cos_t, sin_t, gcq, wuq, wuq_sw, gckv, wuk, wuv_t, gq, gq_sw, gk, gk_sw, ones128, batch, seq):
    n = proj.shape[0]
    tm = min(512, seq)
    nt = seq // tm
    full = lambda shape: pl.BlockSpec(shape, lambda i: (0, 0))
    tab = pl.BlockSpec((tm, LANES), lambda i: (i, 0))
    wide = pl.BlockSpec((tm, N_HEADS * HEAD_PAD), lambda i: (i, 0))
    return pl.pallas_call(
        _mla_prep_kernel,
        grid=(n // tm,),
        in_specs=[
            pl.BlockSpec((tm, MLA_Q_RANK), lambda i: (i, COL_CQ // MLA_Q_RANK)),
            pl.BlockSpec((tm, MLA_KV_RANK), lambda i: (i, COL_CKV // LANES)),
            pl.BlockSpec((tm, LANES), lambda i: (i, COL_CKR // LANES)),
            tab, tab,
            full((1, MLA_Q_RANK)), full((MLA_Q_RANK, N_HEADS * HEAD_PAD)), full((MLA_Q_RANK, N_HEADS * HEAD_PAD)),
            full((1, MLA_KV_RANK)), full((MLA_KV_RANK, N_HEADS * HEAD_PAD)), full((N_HEADS * VT_ROWS, MLA_KV_RANK)),
            full((1, N_HEADS * HEAD_PAD)), full((1, N_HEADS * HEAD_PAD)),
            full((1, N_HEADS * HEAD_PAD)), full((1, N_HEADS * HEAD_PAD)), full((256, 256)),
        ],
        out_specs=[wide, wide, pl.BlockSpec((None, N_HEADS * VT_ROWS, tm), lambda i: (i // nt, 0, i % nt))],
        out_shape=[jax.ShapeDtypeStruct((n, N_HEADS * HEAD_PAD), BF16)] * 2
        + [jax.ShapeDtypeStruct((batch, N_HEADS * VT_ROWS, seq), BF16)],
        compiler_params=_cparams(("parallel",)),
        name="mla_prep",
    )(proj, proj, proj, cos_t, sin_t, gcq, wuq, wuq_sw, gckv, wuk, wuv_t, gq, gq_sw, gk, gk_sw, ones128)


def _gla_kernel(q_ref, k_ref, v_ref, sm_ref, wh_ref, wl_ref, b_ref, go_ref, ones_ref, o_ref, st_ref, *, tc):
    hk = N_HEADS * GLA_DK
    ck = GLA_CHUNK

    @pl.when(pl.program_id(1) == 0)
    def _():
        st_ref[...] = jnp.zeros_like(st_ref)

    sm = sm_ref[...]
    sm_hi = sm.astype(BF16)
    sm_lo = (sm - sm_hi.astype(F32)).astype(BF16)
    z = (jnp.dot(sm_hi, wh_ref[...], preferred_element_type=F32)
         + jnp.dot(sm_lo, wh_ref[...], preferred_element_type=F32)
         + jnp.dot(sm_hi, wl_ref[...], preferred_element_type=F32)) + b_ref[...]
    g_all = _row_prefix_sum(_log_sigmoid(z) * (1.0 / GLA_NORMALIZER), ck)

    lane_k = lax.broadcasted_iota(jnp.int32, (ck, hk), 1) // GLA_DK
    lane_v = lax.broadcasted_iota(jnp.int32, (ck, GROUP_W), 1) // HEAD_DIM
    a_row = lax.broadcasted_iota(jnp.int32, (N_HEADS * ck, ck), 0) % ck
    a_col = lax.broadcasted_iota(jnp.int32, (N_HEADS * ck, ck), 1)
    st_blockdiag = (lax.broadcasted_iota(jnp.int32, (GROUP_W, hk), 0) // HEAD_DIM
                    == lax.broadcasted_iota(jnp.int32, (GROUP_W, hk), 1) // GLA_DK)

    outs = []
    for c in range(tc // ck):
        rows = slice(c * ck, (c + 1) * ck)
        g = g_all[rows]
        q = q_ref[rows, :].astype(F32) * (GLA_DK ** -0.5)
        k = k_ref[rows, :].astype(F32)
        v = v_ref[rows, :]
        g_last = g[ck - 1:ck, :]
        qt = (q * jnp.exp(g)).astype(BF16)
        kt = (k * jnp.exp(-g)).astype(BF16)
        kh = (k * jnp.exp(g_last - g)).astype(BF16)

        q_stack = jnp.concatenate([jnp.where(lane_k == h, qt, jnp.zeros_like(qt)) for h in range(N_HEADS)], axis=0)
        a = lax.dot_general(q_stack, kt, (((1,), (1,)), ((), ())), preferred_element_type=F32)
        a = jnp.where(a_row >= a_col, a, 0.0).astype(BF16)
        r = jnp.dot(a, v, preferred_element_type=F32)
        o = lax.dot_general(qt, st_ref[...].astype(BF16), (((1,), (1,)), ((), ())), preferred_element_type=F32)
        for h in range(N_HEADS):
            o = o + jnp.where(lane_v == h, r[h * ck:(h + 1) * ck, :], 0.0)
        outs.append(o)

        u = lax.dot_general(v, kh, (((0,), (0,)), ((), ())), preferred_element_type=F32)
        st_ref[...] = st_ref[...] * jnp.exp(g_last) + jnp.where(st_blockdiag, u, 0.0)

    o = jnp.concatenate(outs, axis=0)
    ssq = jnp.dot((o * o).astype(BF16), ones_ref[...], preferred_element_type=F32)
    o_ref[...] = (o * lax.rsqrt(ssq * (1.0 / HEAD_DIM) + EPS) * go_ref[...]).astype(BF16)


def _gla(proj, small, w_hi, w_lo, b_up, g_out, ones64, batch, seq):
    n = proj.shape[0]
    tc = min(256, seq)
    nt = seq // tc
    hk = N_HEADS * GLA_DK
    full = lambda shape: pl.BlockSpec(shape, lambda b, t: (0, 0))
    return pl.pallas_call(
        functools.partial(_gla_kernel, tc=tc),
        grid=(batch, nt),
        in_specs=[
            pl.BlockSpec((tc, hk), lambda b, t: (b * nt + t, COL_AQ // hk)),
            pl.BlockSpec((tc, hk), lambda b, t: (b * nt + t, COL_AK // hk)),
            pl.BlockSpec((tc, GROUP_W), lambda b, t: (b * nt + t, COL_AV // GROUP_W)),
            pl.BlockSpec((tc, LANES), lambda b, t: (b * nt + t, 0)),
            full((LANES, hk)), full((LANES, hk)), full((1, hk)), full((1, GROUP_W)), full((GROUP_W, GROUP_W)),
        ],
        out_specs=pl.BlockSpec((tc, GROUP_W), lambda b, t: (b * nt + t, 0)),
        out_shape=jax.ShapeDtypeStruct((n, GROUP_W), BF16),
        scratch_shapes=[pltpu.VMEM((GROUP_W, hk), F32)],
        compiler_params=_cparams(("parallel", "arbitrary")),
        name="gla",
    )(proj, proj, proj, small, w_hi, w_lo, b_up, g_out, ones64)


def _lru_kernel(x_ref, cw_ref, cb_ref, wr_ref, br_ref, wi_ref, bi_ref, lam_ref, o_ref, xpad_ref, h_ref, *, tr):
    @pl.when(pl.program_id(1) == 0)
    def _():
        xpad_ref[0:8, :] = jnp.zeros((8, GROUP_W), F32)
        h_ref[...] = jnp.zeros_like(h_ref)

    xpad_ref[8:8 + tr, :] = x_ref[...].astype(F32)
    xc = cb_ref[...]
    for j in range(CONV_W):
        off = 8 - (CONV_W - 1) + j
        xc = xc + cw_ref[j:j + 1, :] * xpad_ref[off:off + tr, :]
    xpad_ref[0:8, :] = xpad_ref[tr:tr + 8, :]

    xb = xc.astype(BF16)
    r = _sigmoid(jnp.dot(xb, wr_ref[...], preferred_element_type=F32) + br_ref[...])
    i = _sigmoid(jnp.dot(xb, wi_ref[...], preferred_element_type=F32) + bi_ref[...])
    lam = lam_ref[...]
    softplus = jnp.maximum(-lam, 0.0) + jnp.log1p(jnp.exp(-jnp.abs(lam)))
    log_a = (-LRU_C) * r * softplus
    a = jnp.exp(log_a)
    bx = jnp.sqrt(1.0 - jnp.exp(2.0 * log_a)) * (i * xc)

    a3 = a.reshape(tr // 8, 8, GROUP_W)
    b3 = bx.reshape(tr // 8, 8, GROUP_W)
    sub = lax.broadcasted_iota(jnp.int32, a3.shape, 1)
    for s in (1, 2, 4):
        keep = sub >= s
        a_prev = jnp.where(keep, pltpu.roll(a3, s, 1), 1.0)
        b_prev = jnp.where(keep, pltpu.roll(b3, s, 1), 0.0)
        b3 = a3 * b_prev + b3
        a3 = a3 * a_prev
    h_prev = h_ref[0:1, :]
    for g in range(tr // 8):
        hg = a3[g] * h_prev + b3[g]
        o_ref[g * 8:(g + 1) * 8, :] = hg.astype(BF16)
        h_prev = hg[7:8, :]
    h_ref[0:1, :] = h_prev


def _lru(proj, conv_w, conv_b, wr, br, wi, bi, lam, batch, seq):
    n = proj.shape[0]
    tr = min(256, seq)
    nt = seq // tr
    full = lambda shape: pl.BlockSpec(shape, lambda b, t: (0, 0))
    return pl.pallas_call(
        functools.partial(_lru_kernel, tr=tr),
        grid=(batch, nt),
        in_specs=[
            pl.BlockSpec((tr, GROUP_W), lambda b, t: (b * nt + t, COL_DX // GROUP_W)),
            full((8, GROUP_W)), full((1, GROUP_W)),
            full((GROUP_W, GROUP_W)), full((1, GROUP_W)),
            full((GROUP_W, GROUP_W)), full((1, GROUP_W)), full((1, GROUP_W)),
        ],
        out_specs=pl.BlockSpec((tr, GROUP_W), lambda b, t: (b * nt + t, 0)),
        out_shape=jax.ShapeDtypeStruct((n, GROUP_W), BF16),
        scratch_shapes=[pltpu.VMEM((tr + 8, GROUP_W), F32), pltpu.VMEM((8, GROUP_W), F32)],
        compiler_params=_cparams(("parallel", "arbitrary")),
        name="rglru",
    )(proj, conv_w, conv_b, wr, br, wi, bi, lam)


def _pad_cols(w, width):
    return jnp.pad(w, [(0, 0)] * (w.ndim - 1) + [(0, width - w.shape[-1])])


def _block_diag_ones(n_blocks, size):
    return jnp.asarray(np.kron(np.eye(n_blocks, dtype=np.float32), np.ones((size, size), np.float32)), BF16)


def _swap_rotary(w):
    lead = w.shape[:-1]
    w = w.reshape(lead + (N_HEADS, MLA_NOPE + MLA_ROPE))
    half = MLA_ROPE // 2
    z1, z2 = w[..., MLA_NOPE:MLA_NOPE + half], w[..., MLA_NOPE + half:]
    w = jnp.concatenate([jnp.zeros_like(w[..., :MLA_NOPE]), z2, z1], axis=-1)
    return w.reshape(lead + (N_HEADS * (MLA_NOPE + MLA_ROPE),))


def _pad_heads(w, d_real):
    lead = w.shape[:-1]
    w = w.reshape(lead + (N_HEADS, d_real))
    w = jnp.pad(w, [(0, 0)] * len(lead) + [(0, 0), (0, HEAD_PAD - d_real)])
    return w.reshape(lead + (N_HEADS * HEAD_PAD,))


def kernel(x, positions, norm_g, w_in, w_out, gla_w_up, gla_b_up, gla_g_out, fox_b_f, fox_g_q, fox_g_k, mla_g_cq, mla_w_uq, mla_g_ckv, mla_w_ukv, mla_g_q, mla_g_k, lru_conv_w, lru_conv_b, lru_w_r, lru_b_r, lru_w_i, lru_b_i, lru_lam):
    batch, seq, _ = x.shape
    depth = w_in.shape[0]
    n = batch * seq

    sizes = (D_MIX, 256, 256, 512, GLA_RANK, 512, 512, 512, N_HEADS, MLA_Q_RANK, MLA_KV_RANK, MLA_ROPE, GROUP_W)
    offs = np.concatenate([[0], np.cumsum(sizes)])
    seg = [w_in[:, :, offs[i]:offs[i + 1]] for i in range(len(sizes))]
    (w_gate, w_aq, w_ak, w_av, w_ag, w_bq, w_bk, w_bv, w_bf, w_cq, w_ckv, w_ckr, w_dx) = seg
    w_main = jnp.concatenate(
        [w_gate, w_aq, w_ak, w_av, w_bq, w_bk, w_bv, w_cq, w_ckv, _pad_cols(w_ckr, LANES), w_dx], axis=-1).astype(BF16)
    w_small = _pad_cols(jnp.concatenate([w_ag, w_bf], axis=-1), LANES).astype(BF16)
    w_out_b = w_out.astype(BF16)

    ones64 = _block_diag_ones(GROUP_W // HEAD_DIM, HEAD_DIM)
    ones128 = _block_diag_ones(2, HEAD_PAD)

    wup = jnp.pad(gla_w_up, ((0, 0), (0, LANES - GLA_RANK), (0, 0)))
    wup_hi = wup.astype(BF16)
    wup_lo = (wup - wup_hi.astype(F32)).astype(BF16)

    fox_scale = HEAD_DIM ** -0.5 * LOG2E
    mla_scale = (MLA_NOPE + MLA_ROPE) ** -0.5 * LOG2E
    fox_gq = jnp.tile(fox_g_q, (1, N_HEADS)) * fox_scale
    fox_gk = jnp.tile(fox_g_k, (1, N_HEADS))
    fox_bf = jnp.pad(fox_b_f, ((0, 0), (SM_BF, LANES - SM_BF - N_HEADS)))

    wuq = _pad_heads(mla_w_uq, MLA_NOPE + MLA_ROPE).astype(BF16)
    wuq_sw = _pad_heads(_swap_rotary(mla_w_uq), MLA_NOPE + MLA_ROPE).astype(BF16)
    wukv = mla_w_ukv.reshape(depth, MLA_KV_RANK, N_HEADS, 2 * HEAD_DIM)
    wuk = _pad_heads(wukv[..., :MLA_NOPE].reshape(depth, MLA_KV_RANK, N_HEADS * MLA_NOPE), MLA_NOPE).astype(BF16)
    pad_rows = lambda m: jnp.pad(m.reshape(m.shape[:-2] + (N_HEADS, HEAD_DIM, m.shape[-1])),
                                 [(0, 0)] * (m.ndim - 1) + [(0, VT_ROWS - HEAD_DIM), (0, 0)]
                                 ).reshape(m.shape[:-2] + (N_HEADS * VT_ROWS, m.shape[-1]))
    wuv_t = pad_rows(jnp.swapaxes(wukv[..., MLA_NOPE:].reshape(depth, MLA_KV_RANK, GROUP_W), 1, 2)).astype(BF16)
    eye = pad_rows(jnp.eye(GROUP_W, dtype=F32)).astype(BF16)
    mla_gq_all = jnp.tile(mla_g_q, (1, N_HEADS)) * mla_scale
    mla_gk_all = jnp.tile(mla_g_k, (1, N_HEADS))
    mla_gq = _pad_heads(mla_gq_all, MLA_NOPE + MLA_ROPE)
    mla_gk = _pad_heads(mla_gk_all, MLA_NOPE + MLA_ROPE)
    mla_gq_sw = _pad_heads(_swap_rotary(mla_gq_all), MLA_NOPE + MLA_ROPE)
    mla_gk_sw = _pad_heads(_swap_rotary(mla_gk_all), MLA_NOPE + MLA_ROPE)

    eye_h = jnp.eye(N_HEADS, dtype=F32)
    wr_bd = jnp.einsum("lncd,nm->lncmd", lru_w_r, eye_h).reshape(depth, GROUP_W, GROUP_W).astype(BF16)
    wi_bd = jnp.einsum("lncd,nm->lncmd", lru_w_i, eye_h).reshape(depth, GROUP_W, GROUP_W).astype(BF16)
    conv_w = jnp.pad(lru_conv_w, ((0, 0), (0, 8 - CONV_W), (0, 0)))

    half = MLA_ROPE // 2
    inv_freq = ROPE_THETA ** (-jnp.arange(half, dtype=F32) / half)
    invf_row = jnp.pad(jnp.concatenate([inv_freq, inv_freq]), (MLA_NOPE, LANES - MLA_NOPE - MLA_ROPE))[None, :]
    cos_t, sin_t = _rope_tables(positions.reshape(n, 1), invf_row)

    xf = x.reshape(n, D_MODEL)
    for l in range(depth):
        proj, small = _norm_inproj(xf, norm_g[l][None, :], w_main, w_small, l)
        ya = _gla(proj, small, wup_hi[l], wup_lo[l], gla_b_up[l][None, :],
                  jnp.tile(gla_g_out[l], N_HEADS)[None, :], ones64, batch, seq)
        fq, fk, fvt = _fox_prep(proj, small, fox_bf[l][None, :], fox_gq[l][None, :], fox_gk[l][None, :], ones64, eye,
                                batch, seq)
        yb = _attention(fq, fk, fvt, batch, seq)
        mq, mk, mvt = _mla_prep(proj, cos_t, sin_t, mla_g_cq[l][None, :], wuq[l], wuq_sw[l], mla_g_ckv[l][None, :],
                                wuk[l], wuv_t[l], mla_gq[l][None, :], mla_gq_sw[l][None, :], mla_gk[l][None, :],
                                mla_gk_sw[l][None, :], ones128, batch, seq)
        yc = _attention(mq, mk, mvt, batch, seq)
        yd = _lru(proj, conv_w[l], lru_conv_b[l][None, :], wr_bd[l], lru_b_r[l][None, :], wi_bd[l],
                  lru_b_i[l][None, :], lru_lam[l][None, :], batch, seq)
        xf = _gate_outproj(ya, yb, yc, yd, proj, w_out_b, xf, l)
    return xf.reshape(batch, seq, D_MODEL)
```

```python
import functools
import math

import numpy as np
import jax
import jax.numpy as jnp
from jax import lax
from jax.experimental import pallas as pl
from jax.experimental.pallas import tpu as pltpu

F32 = jnp.float32
BF16 = jnp.bfloat16

D_MODEL = 1024
D_MIX = 2048
GROUP_W = 512
N_HEADS = 8
HEAD_DIM = 64
EPS = 1e-6
GLA_DK = 32
GLA_RANK = 16
GLA_NORMALIZER = 16.0
GLA_CHUNK = 64
MLA_Q_RANK = 256
MLA_KV_RANK = 128
MLA_NOPE = 64
MLA_ROPE = 32
ROPE_THETA = 10000.0
CONV_W = 4
LRU_C = 8.0
LOG2E = 1.4426950408889634
NEG_BIG = -1e30

LANES = 128
HEAD_PAD = 128
VT_ROWS = 80

COL_GATE = 0
COL_AQ, COL_AK, COL_AV = 2048, 2304, 2560
COL_BQ, COL_BK, COL_BV = 3072, 3584, 4096
COL_CQ, COL_CKV, COL_CKR = 4608, 4864, 4992
COL_DX = 5120
D_PROJ = 5632
SM_AG, SM_BF = 0, 16


def _cparams(sem, vmem_mb=48):
    return pltpu.CompilerParams(dimension_semantics=sem, vmem_limit_bytes=vmem_mb * 1024 * 1024)


def _sigmoid(z):
    return 0.5 * jnp.tanh(0.5 * z) + 0.5


def _with_ones_rows(vt):
    row = lax.broadcasted_iota(jnp.int32, vt.shape, 0) % VT_ROWS
    return jnp.where(row == HEAD_DIM, 1.0, vt)


def _log_sigmoid(z):
    return jnp.minimum(z, 0.0) - jnp.log1p(jnp.exp(-jnp.abs(z)))


def _row_prefix_sum(v, seg):
    rows = v.shape[0]
    rid = lax.broadcasted_iota(jnp.int32, v.shape, 0) % seg
    s = 1
    while s < seg:
        v = v + jnp.where(rid >= s, pltpu.roll(v, s, 0), 0.0)
        s *= 2
    del rows
    return v


def _inproj_kernel(x_ref, g_ref, w_ref, ws_ref, o_ref, os_ref, h_ref):
    @pl.when(pl.program_id(1) == 0)
    def _():
        x = x_ref[...]
        ms = jnp.mean(x * x, axis=-1, keepdims=True)
        h = (x * lax.rsqrt(ms + EPS) * g_ref[...]).astype(BF16)
        h_ref[...] = h
        os_ref[...] = jnp.dot(h, ws_ref[...], preferred_element_type=F32)

    o_ref[...] = jnp.dot(h_ref[...], w_ref[...], preferred_element_type=F32).astype(BF16)


def _norm_inproj(x, g, w_all, ws_all, layer):
    n = x.shape[0]
    tm = min(1024, n)
    tn = D_PROJ // 2
    return pl.pallas_call(
        _inproj_kernel,
        grid=(n // tm, D_PROJ // tn),
        in_specs=[
            pl.BlockSpec((tm, D_MODEL), lambda i, j: (i, 0)),
            pl.BlockSpec((1, D_MODEL), lambda i, j: (0, 0)),
            pl.BlockSpec((None, D_MODEL, tn), lambda i, j: (layer, 0, j)),
            pl.BlockSpec((None, D_MODEL, LANES), lambda i, j: (layer, 0, 0)),
        ],
        out_specs=[
            pl.BlockSpec((tm, tn), lambda i, j: (i, j)),
            pl.BlockSpec((tm, LANES), lambda i, j: (i, 0)),
        ],
        out_shape=[jax.ShapeDtypeStruct((n, D_PROJ), BF16), jax.ShapeDtypeStruct((n, LANES), F32)],
        scratch_shapes=[pltpu.VMEM((tm, D_MODEL), BF16)],
        compiler_params=_cparams(("parallel", "arbitrary"), vmem_mb=56),
        name="norm_inproj",
    )(x, g, w_all, ws_all)


def _outproj_kernel(ya_ref, yb_ref, yc_ref, yd_ref, gate_ref, w_ref, x_ref, o_ref, z_ref):
    for gi, y_ref in enumerate((ya_ref, yb_ref, yc_ref, yd_ref)):
        lo, hi = gi * GROUP_W, (gi + 1) * GROUP_W
        g = gate_ref[:, lo:hi].astype(F32)
        z_ref[:, lo:hi] = (y_ref[...].astype(F32) * (g * _sigmoid(g))).astype(BF16)
    o_ref[...] = x_ref[...] + jnp.dot(z_ref[...], w_ref[...], preferred_element_type=F32)


def _gate_outproj(ya, yb, yc, yd, proj, w_all, x, layer):
    n = x.shape[0]
    tm = min(512, n)
    yspec = pl.BlockSpec((tm, GROUP_W), lambda i: (i, 0))
    return pl.pallas_call(
        _outproj_kernel,
        grid=(n // tm,),
        in_specs=[
            yspec, yspec, yspec, yspec,
            pl.BlockSpec((tm, D_MIX), lambda i: (i, 0)),
            pl.BlockSpec((None, D_MIX, D_MODEL), lambda i: (layer, 0, 0)),
            pl.BlockSpec((tm, D_MODEL), lambda i: (i, 0)),
        ],
        out_specs=pl.BlockSpec((tm, D_MODEL), lambda i: (i, 0)),
        out_shape=jax.ShapeDtypeStruct((n, D_MODEL), F32),
        scratch_shapes=[pltpu.VMEM((tm, D_MIX), BF16)],
        compiler_params=_cparams(("parallel",)),
        name="gate_outproj",
    )(ya, yb, yc, yd, proj, w_all, x)


def _attn_kernel(q_ref, k_ref, vt_ref, o_ref, m_ref, acc_ref, st_ref, mx_ref, *, tq, nq, hg):
    i = pl.program_id(2)
    half = tq // 2
    nt_dims = (((1,), (1,)), ((), ()))

    m_ref[...] = jnp.full(m_ref.shape, NEG_BIG, F32)
    acc_ref[...] = jnp.zeros(acc_ref.shape, F32)

    def scores(h, krows, qrows):
        lanes = slice(h * HEAD_PAD, (h + 1) * HEAD_PAD)
        return lax.dot_general(k_ref[krows, lanes], q_ref[qrows, lanes], nt_dims, preferred_element_type=F32)

    def vrows(h):
        return slice(h * VT_ROWS, (h + 1) * VT_ROWS)

    tri = lax.broadcasted_iota(jnp.int32, (half, half), 0) <= lax.broadcasted_iota(jnp.int32, (half, half), 1)
    low = lax.broadcasted_iota(jnp.int32, (tq, half), 0) <= lax.broadcasted_iota(jnp.int32, (tq, half), 1) + half

    def issue_scores(item, c):
        h, blk = item
        if blk is None:
            off = c * tq
            s_l = jnp.where(tri, scores(h, slice(off, off + half), slice(0, half)), NEG_BIG)
            s_r = jnp.where(low, scores(h, slice(off, off + tq), slice(half, tq)), NEG_BIG)
            mx = jnp.concatenate([jnp.max(s_l, axis=0, keepdims=True), jnp.max(s_r, axis=0, keepdims=True)], axis=1)
            return (s_l, s_r, mx)
        st = scores(h, slice(blk * tq, (blk + 1) * tq), slice(0, tq))
        return (st, jnp.max(st, axis=0, keepdims=True))

    def store_scores(item, slot, vals):
        if item[1] is None:
            st_ref[slot, 0:half, 0:half] = vals[0]
            st_ref[slot, :, half:tq] = vals[1]
        else:
            st_ref[slot] = vals[0]
        mx_ref[slot] = vals[-1]

    def consume(item, slot, c):
        h, blk = item
        m = m_ref[h]
        m_new = jnp.maximum(m, mx_ref[slot])
        alpha = jnp.exp2(m - m_new)
        m_ref[h] = m_new

        def accumulate(cols, st, vt):
            p = jnp.exp2(st - m_new[:, cols]).astype(BF16)
            acc_ref[h, :, cols] = alpha[:, cols] * acc_ref[h, :, cols] + jnp.dot(vt, p, preferred_element_type=F32)

        if blk is None:
            off = c * tq
            accumulate(slice(0, half), st_ref[slot, 0:half, 0:half], vt_ref[vrows(h), off:off + half])
            accumulate(slice(half, tq), st_ref[slot, :, half:tq], vt_ref[vrows(h), off:off + tq])
        else:
            accumulate(slice(0, tq), st_ref[slot], vt_ref[vrows(h), blk * tq:(blk + 1) * tq])

    for c in range(nq):
        @pl.when(i == c)
        def _(c=c):
            items = [(h, None) for h in range(hg)] + [(h, blk) for blk in range(c) for h in range(hg)]
            store_scores(items[0], 0, issue_scores(items[0], c))
            for n, item in enumerate(items):
                nxt = items[n + 1] if n + 1 < len(items) else None
                if nxt is not None:
                    vals = issue_scores(nxt, c)
                consume(item, n % 2, c)
                if nxt is not None:
                    store_scores(nxt, (n + 1) % 2, vals)

    ot = jnp.concatenate([acc_ref[h, 0:HEAD_DIM, :] / acc_ref[h, HEAD_DIM:HEAD_DIM + 1, :] for h in range(hg)], axis=0)
    o_ref[...] = ot.T.astype(BF16)


ATTN_HEADS_PER_STEP = 8


def _attention(q, k, vt, batch, seq):
    n = q.shape[0]
    tq = min(512, seq)
    nq = seq // tq
    hg = ATTN_HEADS_PER_STEP
    return pl.pallas_call(
        functools.partial(_attn_kernel, tq=tq, nq=nq, hg=hg),
        grid=(batch, N_HEADS // hg, nq),
        in_specs=[
            pl.BlockSpec((tq, hg * HEAD_PAD), lambda b, p, i: (b * nq + i, p)),
            pl.BlockSpec((seq, hg * HEAD_PAD), lambda b, p, i: (b, p)),
            pl.BlockSpec((None, hg * VT_ROWS, seq), lambda b, p, i: (b, p, 0)),
        ],
        out_specs=pl.BlockSpec((tq, hg * HEAD_DIM), lambda b, p, i: (b * nq + i, p)),
        out_shape=jax.ShapeDtypeStruct((n, GROUP_W), BF16),
        scratch_shapes=[pltpu.VMEM((hg, 1, tq), F32), pltpu.VMEM((hg, VT_ROWS, tq), F32),
                        pltpu.VMEM((2, tq, tq), F32), pltpu.VMEM((2, 1, tq), F32)],
        compiler_params=_cparams(("parallel", "parallel", "arbitrary")),
        name="causal_attention",
    )(q, k, vt)


def _fox_prep_kernel(q_ref, k_ref, v_ref, sm_ref, bf_ref, gq_ref, gk_ref, ones_ref, eye_ref, place_ref, augq_ref,
                     qo_ref, ko_ref, vt_ref, carry_ref, *, tt):
    @pl.when(pl.program_id(1) == 0)
    def _():
        carry_ref[...] = jnp.zeros_like(carry_ref)

    vt = lax.dot_general(eye_ref[...], v_ref[...], (((1,), (1,)), ((), ())), preferred_element_type=F32)
    vt_ref[...] = _with_ones_rows(vt).astype(BF16)

    def headnorm(ref, g_ref):
        z = ref[...].astype(F32)
        ssq = jnp.dot((z * z).astype(BF16), ones_ref[...], preferred_element_type=F32)
        return z * lax.rsqrt(ssq + HEAD_DIM * EPS) * g_ref[...]

    qn = headnorm(q_ref, gq_ref)
    kn = headnorm(k_ref, gk_ref)

    log_f = _log_sigmoid(sm_ref[...] + bf_ref[...])
    cum = _row_prefix_sum(log_f, tt) + carry_ref[0:1, :]
    carry_ref[0:1, :] = cum[tt - 1:tt, :]
    neg = cum * (-LOG2E)

    lane = lax.broadcasted_iota(jnp.int32, (tt, LANES), 1)
    hi = neg.astype(BF16).astype(F32)
    r1 = neg - hi
    mid = r1.astype(BF16).astype(F32)
    lo = r1 - mid
    pieces = jnp.where(lane < 2 * SM_BF, hi, jnp.where(lane < 3 * SM_BF, pltpu.roll(mid, SM_BF, 1),
                                                       pltpu.roll(lo, 2 * SM_BF, 1)))
    aug_k = jnp.dot(pieces.astype(BF16), place_ref[...], preferred_element_type=F32)
    aug_q = augq_ref[...]
    for h in range(N_HEADS):
        pair, e = h // 2, h % 2
        in_head = (lane >= e * HEAD_DIM) & (lane < (e + 1) * HEAD_DIM)
        blk = slice(h * HEAD_PAD, (h + 1) * HEAD_PAD)
        kpair = kn[:, pair * LANES:(pair + 1) * LANES]
        qpair = qn[:, pair * LANES:(pair + 1) * LANES]
        ko_ref[:, blk] = jnp.where(in_head, kpair, aug_k[:, blk]).astype(BF16)
        qo_ref[:, blk] = jnp.where(in_head, qpair, aug_q[:, blk]).astype(BF16)


def _fox_prep(proj, small, bf_row, gq, gk, ones64, eye, place, aug_q, batch, seq):
    n = proj.shape[0]
    tt = min(512, seq)
    nt = seq // tt
    row = lambda shape: pl.BlockSpec(shape, lambda b, t: (0, 0))
    return pl.pallas_call(
        functools.partial(_fox_prep_kernel, tt=tt),
        grid=(batch, nt),
        in_specs=[
            pl.BlockSpec((tt, GROUP_W), lambda b, t: (b * nt + t, COL_BQ // GROUP_W)),
            pl.BlockSpec((tt, GROUP_W), lambda b, t: (b * nt + t, COL_BK // GROUP_W)),
            pl.BlockSpec((tt, GROUP_W), lambda b, t: (b * nt + t, COL_BV // GROUP_W)),
            pl.BlockSpec((tt, LANES), lambda b, t: (b * nt + t, 0)),
            row((1, LANES)), row((1, GROUP_W)), row((1, GROUP_W)), row((GROUP_W, GROUP_W)),
            row((N_HEADS * VT_ROWS, GROUP_W)), row((LANES, N_HEADS * HEAD_PAD)), row((1, N_HEADS * HEAD_PAD)),
        ],
        out_specs=[
            pl.BlockSpec((tt, N_HEADS * HEAD_PAD), lambda b, t: (b * nt + t, 0)),
            pl.BlockSpec((tt, N_HEADS * HEAD_PAD), lambda b, t: (b * nt + t, 0)),
            pl.BlockSpec((None, N_HEADS * VT_ROWS, tt), lambda b, t: (b, 0, t)),
        ],
        out_shape=[jax.ShapeDtypeStruct((n, N_HEADS * HEAD_PAD), BF16)] * 2
        + [jax.ShapeDtypeStruct((batch, N_HEADS * VT_ROWS, seq), BF16)],
        scratch_shapes=[pltpu.VMEM((8, LANES), F32)],
        compiler_params=_cparams(("parallel", "arbitrary")),
        name="fox_prep",
    )(proj, proj, proj, small, bf_row, gq, gk, ones64, eye, place, aug_q)


def _rope_kernel(pos_ref, invf_ref, c_ref, s_ref):
    ang = pos_ref[...].astype(F32) * invf_ref[...]
    lane = lax.broadcasted_iota(jnp.int32, ang.shape, 1)
    half = MLA_ROPE // 2
    c, s = jnp.cos(ang), jnp.sin(ang)
    c_ref[...] = jnp.where((lane >= MLA_NOPE) & (lane < MLA_NOPE + MLA_ROPE), c, 1.0)
    s_ref[...] = jnp.where((lane >= MLA_NOPE) & (lane < MLA_NOPE + half), -s,
                           jnp.where((lane >= MLA_NOPE + half) & (lane < MLA_NOPE + MLA_ROPE), s, 0.0))


def _rope_tables(pos_col, invf_row):
    n = pos_col.shape[0]
    tm = min(1024, n)
    out = pl.BlockSpec((tm, LANES), lambda i: (i, 0))
    return pl.pallas_call(
        _rope_kernel,
        grid=(n // tm,),
        in_specs=[pl.BlockSpec((tm, 1), lambda i: (i, 0)), pl.BlockSpec((1, LANES), lambda i: (0, 0))],
        out_specs=[out, out],
        out_shape=[jax.ShapeDtypeStruct((n, LANES), F32)] * 2,
        compiler_params=_cparams(("parallel",)),
        name="rope_tables",
    )(pos_col, invf_row)


def _mla_prep_kernel(cq_ref, ckv_ref, ckr_ref, c_ref, s_ref, gcq_ref, wuq_ref, wuqs_ref, gckv_ref, wuk_ref, wuv_ref,
                     gq_ref, gqs_ref, gk_ref, gks_ref, ones_ref, qo_ref, ko_ref, vo_ref):
    d_qk = float(MLA_NOPE + MLA_ROPE)
    cos, sin = c_ref[...], s_ref[...]
    half = MLA_ROPE // 2

    def rms(z, g):
        return (z * lax.rsqrt(jnp.mean(z * z, axis=-1, keepdims=True) + EPS) * g).astype(BF16)

    def head_ssq(z):
        zz = (z * z).astype(BF16)
        parts = [jnp.dot(zz[:, c * 256:(c + 1) * 256], ones_ref[...], preferred_element_type=F32) for c in range(4)]
        return jnp.concatenate(parts, axis=1)

    cn = rms(cq_ref[...].astype(F32), gcq_ref[...])
    q = jnp.dot(cn, wuq_ref[...], preferred_element_type=F32)
    q_sw = jnp.dot(cn, wuqs_ref[...], preferred_element_type=F32)
    q_inv = lax.rsqrt(head_ssq(q) * (1.0 / d_qk) + EPS)
    qn = q * (q_inv * gq_ref[...])
    qn_sw = q_sw * (q_inv * gqs_ref[...])

    kvn = rms(ckv_ref[...].astype(F32), gckv_ref[...])
    kn = jnp.dot(kvn, wuk_ref[...], preferred_element_type=F32)
    vt = lax.dot_general(wuv_ref[...], kvn, (((1,), (1,)), ((), ())), preferred_element_type=F32)
    vo_ref[...] = _with_ones_rows(vt).astype(BF16)
    kr = ckr_ref[...].astype(F32)
    ss_r = jnp.sum(kr * kr, axis=-1, keepdims=True)
    lane = lax.broadcasted_iota(jnp.int32, kr.shape, 1)
    kr_placed = pltpu.roll(kr, MLA_NOPE, 1)
    kr_swapped = jnp.where(lane < MLA_NOPE + half, pltpu.roll(kr, MLA_NOPE - half, 1), pltpu.roll(kr, MLA_NOPE + half, 1))
    kr_swapped = jnp.where((lane >= MLA_NOPE) & (lane < MLA_NOPE + MLA_ROPE), kr_swapped, 0.0)
    k_inv = lax.rsqrt((head_ssq(kn) + ss_r) * (1.0 / d_qk) + EPS)
    k_scale = k_inv * gk_ref[...]
    k_scale_sw = k_inv * gks_ref[...]

    for h in range(N_HEADS):
        sl = slice(h * HEAD_PAD, (h + 1) * HEAD_PAD)
        qo_ref[:, sl] = (qn[:, sl] * cos + qn_sw[:, sl] * sin).astype(BF16)
        k_h = (kn[:, sl] + kr_placed) * k_scale[:, sl]
        ko_ref[:, sl] = (k_h * cos + (kr_swapped * k_scale_sw[:, sl]) * sin).astype(BF16)


def _mla_prep(proj, cos_t, sin_t, gcq, wuq, wuq_sw, gckv, wuk, wuv_t, gq, gq_sw, gk, gk_sw, ones128, batch, seq):
    n = proj.shape[0]
    tm = min(512, seq)
    nt = seq // tm
    full = lambda shape: pl.BlockSpec(shape, lambda i: (0, 0))
    tab = pl.BlockSpec((tm, LANES), lambda i: (i, 0))
    wide = pl.BlockSpec((tm, N_HEADS * HEAD_PAD), lambda i: (i, 0))
    return pl.pallas_call(
        _mla_prep_kernel,
        grid=(n // tm,),
        in_specs=[
            pl.BlockSpec((tm, MLA_Q_RANK), lambda i: (i, COL_CQ // MLA_Q_RANK)),
            pl.BlockSpec((tm, MLA_KV_RANK), lambda i: (i, COL_CKV // LANES)),
            pl.BlockSpec((tm, LANES), lambda i: (i, COL_CKR // LANES)),
            tab, tab,
            full((1, MLA_Q_RANK)), full((MLA_Q_RANK, N_HEADS * HEAD_PAD)), full((MLA_Q_RANK, N_HEADS * HEAD_PAD)),
            full((1, MLA_KV_RANK)), full((MLA_KV_RANK, N_HEADS * HEAD_PAD)), full((N_HEADS * VT_ROWS, MLA_KV_RANK)),
            full((1, N_HEADS * HEAD_PAD)), full((1, N_HEADS * HEAD_PAD)),
            full((1, N_HEADS * HEAD_PAD)), full((1, N_HEADS * HEAD_PAD)), full((256, 256)),
        ],
        out_specs=[wide, wide, pl.BlockSpec((None, N_HEADS * VT_ROWS, tm), lambda i: (i // nt, 0, i % nt))],
        out_shape=[jax.ShapeDtypeStruct((n, N_HEADS * HEAD_PAD), BF16)] * 2
        + [jax.ShapeDtypeStruct((batch, N_HEADS * VT_ROWS, seq), BF16)],
        compiler_params=_cparams(("parallel",)),
        name="mla_prep",
    )(proj, proj, proj, cos_t, sin_t, gcq, wuq, wuq_sw, gckv, wuk, wuv_t, gq, gq_sw, gk, gk_sw, ones128)


def _gla_kernel(q_ref, k_ref, v_ref, sm_ref, wh_ref, wl_ref, b_ref, go_ref, ones_ref, o_ref, st_ref, *, tc):
    hk = N_HEADS * GLA_DK
    ck = GLA_CHUNK

    @pl.when(pl.program_id(1) == 0)
    def _():
        st_ref[...] = jnp.zeros_like(st_ref)

    sm = sm_ref[...]
    sm_hi = sm.astype(BF16)
    sm_lo = (sm - sm_hi.astype(F32)).astype(BF16)
    z = (jnp.dot(sm_hi, wh_ref[...], preferred_element_type=F32)
         + jnp.dot(sm_lo, wh_ref[...], preferred_element_type=F32)
         + jnp.dot(sm_hi, wl_ref[...], preferred_element_type=F32)) + b_ref[...]
    g_all = _row_prefix_sum(_log_sigmoid(z) * (1.0 / GLA_NORMALIZER), ck)

    lane_k = lax.broadcasted_iota(jnp.int32, (ck, hk), 1) // GLA_DK
    lane_pair = lax.broadcasted_iota(jnp.int32, (ck, LANES), 1)
    a_row = lax.broadcasted_iota(jnp.int32, (N_HEADS * ck, ck), 0) % ck
    a_col = lax.broadcasted_iota(jnp.int32, (N_HEADS * ck, ck), 1)
    st_blockdiag = (lax.broadcasted_iota(jnp.int32, (GROUP_W, hk), 0) // HEAD_DIM
                    == lax.broadcasted_iota(jnp.int32, (GROUP_W, hk), 1) // GLA_DK)

    outs = []
    for c in range(tc // ck):
        rows = slice(c * ck, (c + 1) * ck)
        g = g_all[rows]
        q = q_ref[rows, :].astype(F32) * (GLA_DK ** -0.5)
        k = k_ref[rows, :].astype(F32)
        v = v_ref[rows, :]
        g_last = g[ck - 1:ck, :]
        qt = (q * jnp.exp(g)).astype(BF16)
        kt = (k * jnp.exp(-g)).astype(BF16)
        kh = (k * jnp.exp(g_last - g)).astype(BF16)

        q_stack = jnp.concatenate([jnp.where(lane_k == h, qt, jnp.zeros_like(qt)) for h in range(N_HEADS)], axis=0)
        a = lax.dot_general(q_stack, kt, (((1,), (1,)), ((), ())), preferred_element_type=F32)
        a = jnp.where(a_row >= a_col, a, 0.0).astype(BF16)
        intra = []
        for p in range(N_HEADS // 2):
            r = jnp.dot(a[2 * p * ck:2 * (p + 1) * ck, :], v[:, p * LANES:(p + 1) * LANES], preferred_element_type=F32)
            intra.append(jnp.where(lane_pair < HEAD_DIM, r[0:ck, :], r[ck:2 * ck, :]))
        o = lax.dot_general(qt, st_ref[...].astype(BF16), (((1,), (1,)), ((), ())), preferred_element_type=F32)
        outs.append(o + jnp.concatenate(intra, axis=1))

        u = lax.dot_general(v, kh, (((0,), (0,)), ((), ())), preferred_element_type=F32)
        st_ref[...] = st_ref[...] * jnp.exp(g_last) + jnp.where(st_blockdiag, u, 0.0)

    o = jnp.concatenate(outs, axis=0)
    ssq = jnp.dot((o * o).astype(BF16), ones_ref[...], preferred_element_type=F32)
    o_ref[...] = (o * lax.rsqrt(ssq * (1.0 / HEAD_DIM) + EPS) * go_ref[...]).astype(BF16)


def _gla(proj, small, w_hi, w_lo, b_up, g_out, ones64, batch, seq):
    n = proj.shape[0]
    tc = min(512, seq)
    nt = seq // tc
    hk = N_HEADS * GLA_DK
    full = lambda shape: pl.BlockSpec(shape, lambda b, t: (0, 0))
    return pl.pallas_call(
        functools.partial(_gla_kernel, tc=tc),
        grid=(batch, nt),
        in_specs=[
            pl.BlockSpec((tc, hk), lambda b, t: (b * nt + t, COL_AQ // hk)),
            pl.BlockSpec((tc, hk), lambda b, t: (b * nt + t, COL_AK // hk)),
            pl.BlockSpec((tc, GROUP_W), lambda b, t: (b * nt + t, COL_AV // GROUP_W)),
            pl.BlockSpec((tc, LANES), lambda b, t: (b * nt + t, 0)),
            full((LANES, hk)), full((LANES, hk)), full((1, hk)), full((1, GROUP_W)), full((GROUP_W, GROUP_W)),
        ],
        out_specs=pl.BlockSpec((tc, GROUP_W), lambda b, t: (b * nt + t, 0)),
        out_shape=jax.ShapeDtypeStruct((n, GROUP_W), BF16),
        scratch_shapes=[pltpu.VMEM((GROUP_W, hk), F32)],
        compiler_params=_cparams(("parallel", "arbitrary")),
        name="gla",
    )(proj, proj, proj, small, w_hi, w_lo, b_up, g_out, ones64)


def _lru_kernel(x_ref, cw_ref, cb_ref, wr_ref, br_ref, wi_ref, bi_ref, lam_ref, o_ref, xpad_ref, h_ref, *, tr):
    @pl.when(pl.program_id(1) == 0)
    def _():
        xpad_ref[0:8, :] = jnp.zeros((8, GROUP_W), F32)
        h_ref[...] = jnp.zeros_like(h_ref)

    xpad_ref[8:8 + tr, :] = x_ref[...].astype(F32)
    xc = cb_ref[...]
    for j in range(CONV_W):
        off = 8 - (CONV_W - 1) + j
        xc = xc + cw_ref[j:j + 1, :] * xpad_ref[off:off + tr, :]
    xpad_ref[0:8, :] = xpad_ref[tr:tr + 8, :]

    xb = xc.astype(BF16)
    r = _sigmoid(jnp.dot(xb, wr_ref[...], preferred_element_type=F32) + br_ref[...])
    i = _sigmoid(jnp.dot(xb, wi_ref[...], preferred_element_type=F32) + bi_ref[...])
    lam = lam_ref[...]
    softplus = jnp.maximum(-lam, 0.0) + jnp.log1p(jnp.exp(-jnp.abs(lam)))
    log_a = (-LRU_C) * r * softplus
    a = jnp.exp(log_a)
    bx = jnp.sqrt(1.0 - jnp.exp(2.0 * log_a)) * (i * xc)

    a3 = a.reshape(tr // 8, 8, GROUP_W)
    b3 = bx.reshape(tr // 8, 8, GROUP_W)
    sub = lax.broadcasted_iota(jnp.int32, a3.shape, 1)
    for s in (1, 2, 4):
        keep = sub >= s
        a_prev = jnp.where(keep, pltpu.roll(a3, s, 1), 1.0)
        b_prev = jnp.where(keep, pltpu.roll(b3, s, 1), 0.0)
        b3 = a3 * b_prev + b3
        a3 = a3 * a_prev
    h_prev = h_ref[0:1, :]
    for g in range(tr // 8):
        hg = a3[g] * h_prev + b3[g]
        o_ref[g * 8:(g + 1) * 8, :] = hg.astype(BF16)
        h_prev = hg[7:8, :]
    h_ref[0:1, :] = h_prev


def _lru(proj, conv_w, conv_b, wr, br, wi, bi, lam, batch, seq):
    n = proj.shape[0]
    tr = min(512, seq)
    nt = seq // tr
    full = lambda shape: pl.BlockSpec(shape, lambda b, t: (0, 0))
    return pl.pallas_call(
        functools.partial(_lru_kernel, tr=tr),
        grid=(batch, nt),
        in_specs=[
            pl.BlockSpec((tr, GROUP_W), lambda b, t: (b * nt + t, COL_DX // GROUP_W)),
            full((8, GROUP_W)), full((1, GROUP_W)),
            full((GROUP_W, GROUP_W)), full((1, GROUP_W)),
            full((GROUP_W, GROUP_W)), full((1, GROUP_W)), full((1, GROUP_W)),
        ],
        out_specs=pl.BlockSpec((tr, GROUP_W), lambda b, t: (b * nt + t, 0)),
        out_shape=jax.ShapeDtypeStruct((n, GROUP_W), BF16),
        scratch_shapes=[pltpu.VMEM((tr + 8, GROUP_W), F32), pltpu.VMEM((8, GROUP_W), F32)],
        compiler_params=_cparams(("parallel", "arbitrary")),
        name="rglru",
    )(proj, conv_w, conv_b, wr, br, wi, bi, lam)


def _pad_cols(w, width):
    return jnp.pad(w, [(0, 0)] * (w.ndim - 1) + [(0, width - w.shape[-1])])


def _block_diag_ones(n_blocks, size):
    return jnp.asarray(np.kron(np.eye(n_blocks, dtype=np.float32), np.ones((size, size), np.float32)), BF16)


def _swap_rotary(w):
    lead = w.shape[:-1]
    w = w.reshape(lead + (N_HEADS, MLA_NOPE + MLA_ROPE))
    half = MLA_ROPE // 2
    z1, z2 = w[..., MLA_NOPE:MLA_NOPE + half], w[..., MLA_NOPE + half:]
    w = jnp.concatenate([jnp.zeros_like(w[..., :MLA_NOPE]), z2, z1], axis=-1)
    return w.reshape(lead + (N_HEADS * (MLA_NOPE + MLA_ROPE),))


def _pad_heads(w, d_real):
    lead = w.shape[:-1]
    w = w.reshape(lead + (N_HEADS, d_real))
    w = jnp.pad(w, [(0, 0)] * len(lead) + [(0, 0), (0, HEAD_PAD - d_real)])
    return w.reshape(lead + (N_HEADS * HEAD_PAD,))


def kernel(x, positions, norm_g, w_in, w_out, gla_w_up, gla_b_up, gla_g_out, fox_b_f, fox_g_q, fox_g_k, mla_g_cq, mla_w_uq, mla_g_ckv, mla_w_ukv, mla_g_q, mla_g_k, lru_conv_w, lru_conv_b, lru_w_r, lru_b_r, lru_w_i, lru_b_i, lru_lam):
    batch, seq, _ = x.shape
    depth = w_in.shape[0]
    n = batch * seq

    sizes = (D_MIX, 256, 256, 512, GLA_RANK, 512, 512, 512, N_HEADS, MLA_Q_RANK, MLA_KV_RANK, MLA_ROPE, GROUP_W)
    offs = np.concatenate([[0], np.cumsum(sizes)])
    seg = [w_in[:, :, offs[i]:offs[i + 1]] for i in range(len(sizes))]
    (w_gate, w_aq, w_ak, w_av, w_ag, w_bq, w_bk, w_bv, w_bf, w_cq, w_ckv, w_ckr, w_dx) = seg
    w_main = jnp.concatenate(
        [w_gate, w_aq, w_ak, w_av, w_bq, w_bk, w_bv, w_cq, w_ckv, _pad_cols(w_ckr, LANES), w_dx], axis=-1).astype(BF16)
    w_small = _pad_cols(jnp.concatenate([w_ag, w_bf], axis=-1), LANES).astype(BF16)
    w_out_b = w_out.astype(BF16)

    ones64 = _block_diag_ones(GROUP_W // HEAD_DIM, HEAD_DIM)
    ones128 = _block_diag_ones(2, HEAD_PAD)

    wup = jnp.pad(gla_w_up, ((0, 0), (0, LANES - GLA_RANK), (0, 0)))
    wup_hi = wup.astype(BF16)
    wup_lo = (wup - wup_hi.astype(F32)).astype(BF16)

    fox_scale = LOG2E
    mla_scale = (MLA_NOPE + MLA_ROPE) ** -0.5 * LOG2E
    fox_gq = jnp.tile(fox_g_q, (1, N_HEADS)) * fox_scale
    fox_gk = jnp.tile(fox_g_k, (1, N_HEADS)) * HEAD_DIM ** 0.5
    place = np.zeros((LANES, N_HEADS * HEAD_PAD), np.float32)
    aug_q = np.zeros((1, N_HEADS * HEAD_PAD), np.float32)
    for h in range(N_HEADS):
        base = h * HEAD_PAD + (HEAD_DIM if h % 2 == 0 else 0)
        for piece in range(3):
            place[(piece + 1) * SM_BF + h, base + piece] = 1.0
            aug_q[0, base + piece] = 1.0
    place, aug_q = jnp.asarray(place, BF16), jnp.asarray(aug_q)
    fox_bf = jnp.pad(fox_b_f, ((0, 0), (SM_BF, LANES - SM_BF - N_HEADS)))

    wuq = _pad_heads(mla_w_uq, MLA_NOPE + MLA_ROPE).astype(BF16)
    wuq_sw = _pad_heads(_swap_rotary(mla_w_uq), MLA_NOPE + MLA_ROPE).astype(BF16)
    wukv = mla_w_ukv.reshape(depth, MLA_KV_RANK, N_HEADS, 2 * HEAD_DIM)
    wuk = _pad_heads(wukv[..., :MLA_NOPE].reshape(depth, MLA_KV_RANK, N_HEADS * MLA_NOPE), MLA_NOPE).astype(BF16)
    pad_rows = lambda m: jnp.pad(m.reshape(m.shape[:-2] + (N_HEADS, HEAD_DIM, m.shape[-1])),
                                 [(0, 0)] * (m.ndim - 1) + [(0, VT_ROWS - HEAD_DIM), (0, 0)]
                                 ).reshape(m.shape[:-2] + (N_HEADS * VT_ROWS, m.shape[-1]))
    wuv_t = pad_rows(jnp.swapaxes(wukv[..., MLA_NOPE:].reshape(depth, MLA_KV_RANK, GROUP_W), 1, 2)).astype(BF16)
    eye = pad_rows(jnp.eye(GROUP_W, dtype=F32)).astype(BF16)
    mla_gq_all = jnp.tile(mla_g_q, (1, N_HEADS)) * mla_scale
    mla_gk_all = jnp.tile(mla_g_k, (1, N_HEADS))
    mla_gq = _pad_heads(mla_gq_all, MLA_NOPE + MLA_ROPE)
    mla_gk = _pad_heads(mla_gk_all, MLA_NOPE + MLA_ROPE)
    mla_gq_sw = _pad_heads(_swap_rotary(mla_gq_all), MLA_NOPE + MLA_ROPE)
    mla_gk_sw = _pad_heads(_swap_rotary(mla_gk_all), MLA_NOPE + MLA_ROPE)

    eye_h = jnp.eye(N_HEADS, dtype=F32)
    wr_bd = jnp.einsum("lncd,nm->lncmd", lru_w_r, eye_h).reshape(depth, GROUP_W, GROUP_W).astype(BF16)
    wi_bd = jnp.einsum("lncd,nm->lncmd", lru_w_i, eye_h).reshape(depth, GROUP_W, GROUP_W).astype(BF16)
    conv_w = jnp.pad(lru_conv_w, ((0, 0), (0, 8 - CONV_W), (0, 0)))

    half = MLA_ROPE // 2
    inv_freq = ROPE_THETA ** (-jnp.arange(half, dtype=F32) / half)
    invf_row = jnp.pad(jnp.concatenate([inv_freq, inv_freq]), (MLA_NOPE, LANES - MLA_NOPE - MLA_ROPE))[None, :]
    cos_t, sin_t = _rope_tables(positions.reshape(n, 1), invf_row)

    xf = x.reshape(n, D_MODEL)
    for l in range(depth):
        proj, small = _norm_inproj(xf, norm_g[l][None, :], w_main, w_small, l)
        ya = _gla(proj, small, wup_hi[l], wup_lo[l], gla_b_up[l][None, :],
                  jnp.tile(gla_g_out[l], N_HEADS)[None, :], ones64, batch, seq)
        fq, fk, fvt = _fox_prep(proj, small, fox_bf[l][None, :], fox_gq[l][None, :], fox_gk[l][None, :], ones64, eye,
                                place, aug_q, batch, seq)
        yb = _attention(fq, fk, fvt, batch, seq)
        mq, mk, mvt = _mla_prep(proj, cos_t, sin_t, mla_g_cq[l][None, :], wuq[l], wuq_sw[l], mla_g_ckv[l][None, :],
                                wuk[l], wuv_t[l], mla_gq[l][None, :], mla_gq_sw[l][None, :], mla_gk[l][None, :],
                                mla_gk_sw[l][None, :], ones128, batch, seq)
        yc = _attention(mq, mk, mvt, batch, seq)
        yd = _lru(proj, conv_w[l], lru_conv_b[l][None, :], wr_bd[l], lru_b_r[l][None, :], wi_bd[l],
                  lru_b_i[l][None, :], lru_lam[l][None, :], batch, seq)
        xf = _gate_outproj(ya, yb, yc, yd, proj, w_out_b, xf, l)
    return xf.reshape(batch, seq, D_MODEL)
```

```python
import functools
import math

import numpy as np
import jax
import jax.numpy as jnp
from jax import lax
from jax.experimental import pallas as pl
from jax.experimental.pallas import tpu as pltpu

F32 = jnp.float32
BF16 = jnp.bfloat16

D_MODEL = 1024
D_MIX = 2048
GROUP_W = 512
N_HEADS = 8
HEAD_DIM = 64
EPS = 1e-6
GLA_DK = 32
GLA_RANK = 16
GLA_NORMALIZER = 16.0
GLA_CHUNK = 64
GLA_LEVELS = 6
GLA_SAFE_LOG_DECAY = 40.0
MLA_Q_RANK = 256
MLA_KV_RANK = 128
MLA_NOPE = 64
MLA_ROPE = 32
ROPE_THETA = 10000.0
CONV_W = 4
LRU_C = 8.0
LOG2E = 1.4426950408889634
NEG_BIG = -1e30

LANES = 128
HEAD_PAD = 128
VT_ROWS = 80

COL_GATE = 0
COL_AQ, COL_AK, COL_AV = 2048, 2304, 2560
COL_BQ, COL_BK, COL_BV = 3072, 3584, 4096
COL_CQ, COL_CKV, COL_CKR = 4608, 4864, 4992
COL_DX = 5120
D_PROJ = 5632
SM_AG, SM_BF = 0, 16


def _cparams(sem, vmem_mb=48):
    return pltpu.CompilerParams(dimension_semantics=sem, vmem_limit_bytes=vmem_mb * 1024 * 1024)


def _sigmoid(z):
    return 0.5 * jnp.tanh(0.5 * z) + 0.5


def _with_ones_rows(vt):
    row = lax.broadcasted_iota(jnp.int32, vt.shape, 0) % VT_ROWS
    return jnp.where(row == HEAD_DIM, 1.0, vt)


def _log_sigmoid(z):
    return jnp.minimum(z, 0.0) - jnp.log1p(jnp.exp(-jnp.abs(z)))


def _row_prefix_sum(v, seg):
    rows = v.shape[0]
    rid = lax.broadcasted_iota(jnp.int32, v.shape, 0) % seg
    s = 1
    while s < seg:
        v = v + jnp.where(rid >= s, pltpu.roll(v, s, 0), 0.0)
        s *= 2
    del rows
    return v


def _inproj_kernel(x_ref, g_ref, w_ref, ws_ref, o_ref, os_ref, h_ref):
    @pl.when(pl.program_id(1) == 0)
    def _():
        x = x_ref[...]
        ms = jnp.mean(x * x, axis=-1, keepdims=True)
        h = (x * lax.rsqrt(ms + EPS) * g_ref[...]).astype(BF16)
        h_ref[...] = h
        os_ref[...] = jnp.dot(h, ws_ref[...], preferred_element_type=F32)

    o_ref[...] = jnp.dot(h_ref[...], w_ref[...], preferred_element_type=F32).astype(BF16)


def _norm_inproj(x, g, w_all, ws_all, layer):
    n = x.shape[0]
    tm = min(1024, n)
    tn = D_PROJ // 2
    return pl.pallas_call(
        _inproj_kernel,
        grid=(n // tm, D_PROJ // tn),
        in_specs=[
            pl.BlockSpec((tm, D_MODEL), lambda i, j: (i, 0)),
            pl.BlockSpec((1, D_MODEL), lambda i, j: (0, 0)),
            pl.BlockSpec((None, D_MODEL, tn), lambda i, j: (layer, 0, j)),
            pl.BlockSpec((None, D_MODEL, LANES), lambda i, j: (layer, 0, 0)),
        ],
        out_specs=[
            pl.BlockSpec((tm, tn), lambda i, j: (i, j)),
            pl.BlockSpec((tm, LANES), lambda i, j: (i, 0)),
        ],
        out_shape=[jax.ShapeDtypeStruct((n, D_PROJ), BF16), jax.ShapeDtypeStruct((n, LANES), F32)],
        scratch_shapes=[pltpu.VMEM((tm, D_MODEL), BF16)],
        compiler_params=_cparams(("parallel", "arbitrary"), vmem_mb=56),
        name="norm_inproj",
    )(x, g, w_all, ws_all)


def _outproj_kernel(ya_ref, yb_ref, yc_ref, yd_ref, gate_ref, w_ref, x_ref, o_ref, z_ref):
    for gi, y_ref in enumerate((ya_ref, yb_ref, yc_ref, yd_ref)):
        lo, hi = gi * GROUP_W, (gi + 1) * GROUP_W
        g = gate_ref[:, lo:hi].astype(F32)
        z_ref[:, lo:hi] = (y_ref[...].astype(F32) * (g * _sigmoid(g))).astype(BF16)
    o_ref[...] = x_ref[...] + jnp.dot(z_ref[...], w_ref[...], preferred_element_type=F32)


def _gate_outproj(ya, yb, yc, yd, proj, w_all, x, layer):
    n = x.shape[0]
    tm = min(512, n)
    yspec = pl.BlockSpec((tm, GROUP_W), lambda i: (i, 0))
    return pl.pallas_call(
        _outproj_kernel,
        grid=(n // tm,),
        in_specs=[
            yspec, yspec, yspec, yspec,
            pl.BlockSpec((tm, D_MIX), lambda i: (i, 0)),
            pl.BlockSpec((None, D_MIX, D_MODEL), lambda i: (layer, 0, 0)),
            pl.BlockSpec((tm, D_MODEL), lambda i: (i, 0)),
        ],
        out_specs=pl.BlockSpec((tm, D_MODEL), lambda i: (i, 0)),
        out_shape=jax.ShapeDtypeStruct((n, D_MODEL), F32),
        scratch_shapes=[pltpu.VMEM((tm, D_MIX), BF16)],
        compiler_params=_cparams(("parallel",)),
        name="gate_outproj",
    )(ya, yb, yc, yd, proj, w_all, x)


def _attn_kernel(q_ref, k_ref, vt_ref, o_ref, m_ref, acc_ref, st_ref, mx_ref, *, tq, nq, hg):
    i = pl.program_id(2)
    half = tq // 2
    nt_dims = (((1,), (1,)), ((), ()))

    m_ref[...] = jnp.full(m_ref.shape, NEG_BIG, F32)
    acc_ref[...] = jnp.zeros(acc_ref.shape, F32)

    def scores(h, krows, qrows):
        lanes = slice(h * HEAD_PAD, (h + 1) * HEAD_PAD)
        return lax.dot_general(k_ref[krows, lanes], q_ref[qrows, lanes], nt_dims, preferred_element_type=F32)

    def vrows(h):
        return slice(h * VT_ROWS, (h + 1) * VT_ROWS)

    tri = lax.broadcasted_iota(jnp.int32, (half, half), 0) <= lax.broadcasted_iota(jnp.int32, (half, half), 1)
    low = lax.broadcasted_iota(jnp.int32, (tq, half), 0) <= lax.broadcasted_iota(jnp.int32, (tq, half), 1) + half

    def issue_scores(item, c):
        h, blk = item
        if blk is None:
            off = c * tq
            s_l = jnp.where(tri, scores(h, slice(off, off + half), slice(0, half)), NEG_BIG)
            s_r = jnp.where(low, scores(h, slice(off, off + tq), slice(half, tq)), NEG_BIG)
            mx = jnp.concatenate([jnp.max(s_l, axis=0, keepdims=True), jnp.max(s_r, axis=0, keepdims=True)], axis=1)
            return (s_l, s_r, mx)
        st = scores(h, slice(blk * tq, (blk + 1) * tq), slice(0, tq))
        return (st, jnp.max(st, axis=0, keepdims=True))

    def store_scores(item, slot, vals):
        if item[1] is None:
            st_ref[slot, 0:half, 0:half] = vals[0]
            st_ref[slot, :, half:tq] = vals[1]
        else:
            st_ref[slot] = vals[0]
        mx_ref[slot] = vals[-1]

    def consume(item, slot, c):
        h, blk = item
        m = m_ref[h]
        m_new = jnp.maximum(m, mx_ref[slot])
        alpha = jnp.exp2(m - m_new)
        m_ref[h] = m_new

        def accumulate(cols, st, vt):
            p = jnp.exp2(st - m_new[:, cols]).astype(BF16)
            acc_ref[h, :, cols] = alpha[:, cols] * acc_ref[h, :, cols] + jnp.dot(vt, p, preferred_element_type=F32)

        if blk is None:
            off = c * tq
            accumulate(slice(0, half), st_ref[slot, 0:half, 0:half], vt_ref[vrows(h), off:off + half])
            accumulate(slice(half, tq), st_ref[slot, :, half:tq], vt_ref[vrows(h), off:off + tq])
        else:
            accumulate(slice(0, tq), st_ref[slot], vt_ref[vrows(h), blk * tq:(blk + 1) * tq])

    for c in range(nq):
        @pl.when(i == c)
        def _(c=c):
            items = [(h, None) for h in range(hg)] + [(h, blk) for blk in range(c) for h in range(hg)]
            store_scores(items[0], 0, issue_scores(items[0], c))
            for n, item in enumerate(items):
                nxt = items[n + 1] if n + 1 < len(items) else None
                if nxt is not None:
                    vals = issue_scores(nxt, c)
                consume(item, n % 2, c)
                if nxt is not None:
                    store_scores(nxt, (n + 1) % 2, vals)

    ot = jnp.concatenate([acc_ref[h, 0:HEAD_DIM, :] / acc_ref[h, HEAD_DIM:HEAD_DIM + 1, :] for h in range(hg)], axis=0)
    o_ref[...] = ot.T.astype(BF16)


ATTN_HEADS_PER_STEP = 8


def _attention(q, k, vt, batch, seq):
    n = q.shape[0]
    tq = min(512, seq)
    nq = seq // tq
    hg = ATTN_HEADS_PER_STEP
    return pl.pallas_call(
        functools.partial(_attn_kernel, tq=tq, nq=nq, hg=hg),
        grid=(batch, N_HEADS // hg, nq),
        in_specs=[
            pl.BlockSpec((tq, hg * HEAD_PAD), lambda b, p, i: (b * nq + i, p)),
            pl.BlockSpec((seq, hg * HEAD_PAD), lambda b, p, i: (b, p)),
            pl.BlockSpec((None, hg * VT_ROWS, seq), lambda b, p, i: (b, p, 0)),
        ],
        out_specs=pl.BlockSpec((tq, hg * HEAD_DIM), lambda b, p, i: (b * nq + i, p)),
        out_shape=jax.ShapeDtypeStruct((n, GROUP_W), BF16),
        scratch_shapes=[pltpu.VMEM((hg, 1, tq), F32), pltpu.VMEM((hg, VT_ROWS, tq), F32),
                        pltpu.VMEM((2, tq, tq), F32), pltpu.VMEM((2, 1, tq), F32)],
        compiler_params=_cparams(("parallel", "parallel", "arbitrary")),
        name="causal_attention",
    )(q, k, vt)


def _fox_prep_kernel(q_ref, k_ref, v_ref, sm_ref, bf_ref, gq_ref, gk_ref, ones_ref, eye_ref, place_ref, augq_ref,
                     qo_ref, ko_ref, vt_ref, carry_ref, *, tt):
    @pl.when(pl.program_id(1) == 0)
    def _():
        carry_ref[...] = jnp.zeros_like(carry_ref)

    vt = lax.dot_general(eye_ref[...], v_ref[...], (((1,), (1,)), ((), ())), preferred_element_type=F32)
    vt_ref[...] = _with_ones_rows(vt).astype(BF16)

    def headnorm(ref, g_ref):
        z = ref[...].astype(F32)
        ssq = jnp.dot((z * z).astype(BF16), ones_ref[...], preferred_element_type=F32)
        return z * lax.rsqrt(ssq + HEAD_DIM * EPS) * g_ref[...]

    qn = headnorm(q_ref, gq_ref)
    kn = headnorm(k_ref, gk_ref)

    log_f = _log_sigmoid(sm_ref[...] + bf_ref[...])
    cum = _row_prefix_sum(log_f, tt) + carry_ref[0:1, :]
    carry_ref[0:1, :] = cum[tt - 1:tt, :]
    neg = cum * (-LOG2E)

    lane = lax.broadcasted_iota(jnp.int32, (tt, LANES), 1)
    hi = neg.astype(BF16).astype(F32)
    r1 = neg - hi
    mid = r1.astype(BF16).astype(F32)
    lo = r1 - mid
    pieces = jnp.where(lane < 2 * SM_BF, hi, jnp.where(lane < 3 * SM_BF, pltpu.roll(mid, SM_BF, 1),
                                                       pltpu.roll(lo, 2 * SM_BF, 1)))
    aug_k = jnp.dot(pieces.astype(BF16), place_ref[...], preferred_element_type=F32)
    aug_q = augq_ref[...]
    for h in range(N_HEADS):
        pair, e = h // 2, h % 2
        in_head = (lane >= e * HEAD_DIM) & (lane < (e + 1) * HEAD_DIM)
        blk = slice(h * HEAD_PAD, (h + 1) * HEAD_PAD)
        kpair = kn[:, pair * LANES:(pair + 1) * LANES]
        qpair = qn[:, pair * LANES:(pair + 1) * LANES]
        ko_ref[:, blk] = jnp.where(in_head, kpair, aug_k[:, blk]).astype(BF16)
        qo_ref[:, blk] = jnp.where(in_head, qpair, aug_q[:, blk]).astype(BF16)


def _fox_prep(proj, small, bf_row, gq, gk, ones64, eye, place, aug_q, batch, seq):
    n = proj.shape[0]
    tt = min(512, seq)
    nt = seq // tt
    row = lambda shape: pl.BlockSpec(shape, lambda b, t: (0, 0))
    return pl.pallas_call(
        functools.partial(_fox_prep_kernel, tt=tt),
        grid=(batch, nt),
        in_specs=[
            pl.BlockSpec((tt, GROUP_W), lambda b, t: (b * nt + t, COL_BQ // GROUP_W)),
            pl.BlockSpec((tt, GROUP_W), lambda b, t: (b * nt + t, COL_BK // GROUP_W)),
            pl.BlockSpec((tt, GROUP_W), lambda b, t: (b * nt + t, COL_BV // GROUP_W)),
            pl.BlockSpec((tt, LANES), lambda b, t: (b * nt + t, 0)),
            row((1, LANES)), row((1, GROUP_W)), row((1, GROUP_W)), row((GROUP_W, GROUP_W)),
            row((N_HEADS * VT_ROWS, GROUP_W)), row((LANES, N_HEADS * HEAD_PAD)), row((1, N_HEADS * HEAD_PAD)),
        ],
        out_specs=[
            pl.BlockSpec((tt, N_HEADS * HEAD_PAD), lambda b, t: (b * nt + t, 0)),
            pl.BlockSpec((tt, N_HEADS * HEAD_PAD), lambda b, t: (b * nt + t, 0)),
            pl.BlockSpec((None, N_HEADS * VT_ROWS, tt), lambda b, t: (b, 0, t)),
        ],
        out_shape=[jax.ShapeDtypeStruct((n, N_HEADS * HEAD_PAD), BF16)] * 2
        + [jax.ShapeDtypeStruct((batch, N_HEADS * VT_ROWS, seq), BF16)],
        scratch_shapes=[pltpu.VMEM((8, LANES), F32)],
        compiler_params=_cparams(("parallel", "arbitrary")),
        name="fox_prep",
    )(proj, proj, proj, small, bf_row, gq, gk, ones64, eye, place, aug_q)


def _rope_kernel(pos_ref, invf_ref, c_ref, s_ref):
    ang = pos_ref[...].astype(F32) * invf_ref[...]
    lane = lax.broadcasted_iota(jnp.int32, ang.shape, 1)
    half = MLA_ROPE // 2
    c, s = jnp.cos(ang), jnp.sin(ang)
    c_ref[...] = jnp.where((lane >= MLA_NOPE) & (lane < MLA_NOPE + MLA_ROPE), c, 1.0)
    s_ref[...] = jnp.where((lane >= MLA_NOPE) & (lane < MLA_NOPE + half), -s,
                           jnp.where((lane >= MLA_NOPE + half) & (lane < MLA_NOPE + MLA_ROPE), s, 0.0))


def _rope_tables(pos_col, invf_row):
    n = pos_col.shape[0]
    tm = min(1024, n)
    out = pl.BlockSpec((tm, LANES), lambda i: (i, 0))
    return pl.pallas_call(
        _rope_kernel,
        grid=(n // tm,),
        in_specs=[pl.BlockSpec((tm, 1), lambda i: (i, 0)), pl.BlockSpec((1, LANES), lambda i: (0, 0))],
        out_specs=[out, out],
        out_shape=[jax.ShapeDtypeStruct((n, LANES), F32)] * 2,
        compiler_params=_cparams(("parallel",)),
        name="rope_tables",
    )(pos_col, invf_row)


def _mla_prep_kernel(cq_ref, ckv_ref, ckr_ref, c_ref, s_ref, gcq_ref, wuq_ref, wuqs_ref, gckv_ref, wuk_ref, wuv_ref,
                     gq_ref, gqs_ref, gk_ref, gks_ref, ones_ref, qo_ref, ko_ref, vo_ref):
    d_qk = float(MLA_NOPE + MLA_ROPE)
    cos, sin = c_ref[...], s_ref[...]
    half = MLA_ROPE // 2

    def rms(z, g):
        return (z * lax.rsqrt(jnp.mean(z * z, axis=-1, keepdims=True) + EPS) * g).astype(BF16)

    def head_ssq(z):
        zz = (z * z).astype(BF16)
        parts = [jnp.dot(zz[:, c * 256:(c + 1) * 256], ones_ref[...], preferred_element_type=F32) for c in range(4)]
        return jnp.concatenate(parts, axis=1)

    cn = rms(cq_ref[...].astype(F32), gcq_ref[...])
    q = jnp.dot(cn, wuq_ref[...], preferred_element_type=F32)
    q_sw = jnp.dot(cn, wuqs_ref[...], preferred_element_type=F32)
    q_inv = lax.rsqrt(head_ssq(q) * (1.0 / d_qk) + EPS)
    qn = q * (q_inv * gq_ref[...])
    qn_sw = q_sw * (q_inv * gqs_ref[...])

    kvn = rms(ckv_ref[...].astype(F32), gckv_ref[...])
    kn = jnp.dot(kvn, wuk_ref[...], preferred_element_type=F32)
    vt = lax.dot_general(wuv_ref[...], kvn, (((1,), (1,)), ((), ())), preferred_element_type=F32)
    vo_ref[...] = _with_ones_rows(vt).astype(BF16)
    kr = ckr_ref[...].astype(F32)
    ss_r = jnp.sum(kr * kr, axis=-1, keepdims=True)
    lane = lax.broadcasted_iota(jnp.int32, kr.shape, 1)
    kr_placed = pltpu.roll(kr, MLA_NOPE, 1)
    kr_swapped = jnp.where(lane < MLA_NOPE + half, pltpu.roll(kr, MLA_NOPE - half, 1), pltpu.roll(kr, MLA_NOPE + half, 1))
    kr_swapped = jnp.where((lane >= MLA_NOPE) & (lane < MLA_NOPE + MLA_ROPE), kr_swapped, 0.0)
    k_inv = lax.rsqrt((head_ssq(kn) + ss_r) * (1.0 / d_qk) + EPS)
    k_scale = k_inv * gk_ref[...]
    k_scale_sw = k_inv * gks_ref[...]

    for h in range(N_HEADS):
        sl = slice(h * HEAD_PAD, (h + 1) * HEAD_PAD)
        qo_ref[:, sl] = (qn[:, sl] * cos + qn_sw[:, sl] * sin).astype(BF16)
        k_h = (kn[:, sl] + kr_placed) * k_scale[:, sl]
        ko_ref[:, sl] = (k_h * cos + (kr_swapped * k_scale_sw[:, sl]) * sin).astype(BF16)


def _mla_prep(proj, cos_t, sin_t, gcq, wuq, wuq_sw, gckv, wuk, wuv_t, gq, gq_sw, gk, gk_sw, ones128, batch, seq):
    n = proj.shape[0]
    tm = min(512, seq)
    nt = seq // tm
    full = lambda shape: pl.BlockSpec(shape, lambda i: (0, 0))
    tab = pl.BlockSpec((tm, LANES), lambda i: (i, 0))
    wide = pl.BlockSpec((tm, N_HEADS * HEAD_PAD), lambda i: (i, 0))
    return pl.pallas_call(
        _mla_prep_kernel,
        grid=(n // tm,),
        in_specs=[
            pl.BlockSpec((tm, MLA_Q_RANK), lambda i: (i, COL_CQ // MLA_Q_RANK)),
            pl.BlockSpec((tm, MLA_KV_RANK), lambda i: (i, COL_CKV // LANES)),
            pl.BlockSpec((tm, LANES), lambda i: (i, COL_CKR // LANES)),
            tab, tab,
            full((1, MLA_Q_RANK)), full((MLA_Q_RANK, N_HEADS * HEAD_PAD)), full((MLA_Q_RANK, N_HEADS * HEAD_PAD)),
            full((1, MLA_KV_RANK)), full((MLA_KV_RANK, N_HEADS * HEAD_PAD)), full((N_HEADS * VT_ROWS, MLA_KV_RANK)),
            full((1, N_HEADS * HEAD_PAD)), full((1, N_HEADS * HEAD_PAD)),
            full((1, N_HEADS * HEAD_PAD)), full((1, N_HEADS * HEAD_PAD)), full((256, 256)),
        ],
        out_specs=[wide, wide, pl.BlockSpec((None, N_HEADS * VT_ROWS, tm), lambda i: (i // nt, 0, i % nt))],
        out_shape=[jax.ShapeDtypeStruct((n, N_HEADS * HEAD_PAD), BF16)] * 2
        + [jax.ShapeDtypeStruct((batch, N_HEADS * VT_ROWS, seq), BF16)],
        compiler_params=_cparams(("parallel",)),
        name="mla_prep",
    )(proj, proj, proj, cos_t, sin_t, gcq, wuq, wuq_sw, gckv, wuk, wuv_t, gq, gq_sw, gk, gk_sw, ones128)


def _gla_kernel(q_ref, k_ref, v_ref, sm_ref, wh_ref, wl_ref, b_ref, go_ref, ones_ref, sel_ref, o_ref, st_ref, a_ref,
                *, tc):
    hk = N_HEADS * GLA_DK
    ck = GLA_CHUNK

    @pl.when(pl.program_id(1) == 0)
    def _():
        st_ref[...] = jnp.zeros_like(st_ref)
        a_ref[...] = jnp.zeros_like(a_ref)

    sm = sm_ref[...]
    sm_hi = sm.astype(BF16)
    sm_lo = (sm - sm_hi.astype(F32)).astype(BF16)
    z = (jnp.dot(sm_hi, wh_ref[...], preferred_element_type=F32)
         + jnp.dot(sm_lo, wh_ref[...], preferred_element_type=F32)
         + jnp.dot(sm_hi, wl_ref[...], preferred_element_type=F32)) + b_ref[...]
    g_all = _row_prefix_sum(_log_sigmoid(z) * (1.0 / GLA_NORMALIZER), ck)

    lane_k = lax.broadcasted_iota(jnp.int32, (ck, hk), 1) // GLA_DK
    lane_pair = lax.broadcasted_iota(jnp.int32, (ck, LANES), 1)
    a_row = lax.broadcasted_iota(jnp.int32, (N_HEADS * ck, ck), 0) % ck
    a_col = lax.broadcasted_iota(jnp.int32, (N_HEADS * ck, ck), 1)
    st_blockdiag = (lax.broadcasted_iota(jnp.int32, (GROUP_W, hk), 0) // HEAD_DIM
                    == lax.broadcasted_iota(jnp.int32, (GROUP_W, hk), 1) // GLA_DK)

    def stack_heads(x):
        return jnp.concatenate([jnp.where(lane_k == h, x, jnp.zeros_like(x)) for h in range(N_HEADS)], axis=0)

    n_chunks = tc // ck
    unsafe = jnp.min(g_all) < -GLA_SAFE_LOG_DECAY

    @pl.when(unsafe)
    def _():
        t_idx = lax.broadcasted_iota(jnp.int32, (ck, hk), 0)
        for c in range(n_chunks):
            rows = slice(c * ck, (c + 1) * ck)
            g = g_all[rows]
            q = q_ref[rows, :].astype(F32) * (GLA_DK ** -0.5)
            k = k_ref[rows, :].astype(F32)
            g_hi = g.astype(BF16)
            g_r1 = g - g_hi.astype(F32)
            g_mid = g_r1.astype(BF16)
            g_lo = (g_r1 - g_mid.astype(F32)).astype(BF16)
            g3 = jnp.concatenate([g_hi, g_mid, g_lo], axis=0)
            nt_dims = (((1,), (1,)), ((), ()))
            a = lax.dot_general(stack_heads(q.astype(BF16)), k.astype(BF16), nt_dims, preferred_element_type=F32)
            a = jnp.where(a_row == a_col, a, 0.0)
            for lvl in range(GLA_LEVELS):
                half = ck >> (lvl + 1)
                g_mid_row = jnp.dot(sel_ref[lvl], g3, preferred_element_type=F32)
                d = g - g_mid_row
                second = (t_idx // half) % 2 == 1
                qd = jnp.where(second, q * jnp.exp(jnp.minimum(d, 0.0)), 0.0).astype(BF16)
                kd = jnp.where(second, 0.0, k * jnp.exp(jnp.minimum(-d, 0.0))).astype(BF16)
                part = lax.dot_general(stack_heads(qd), kd, nt_dims, preferred_element_type=F32)
                a = a + jnp.where(a_row // (2 * half) == a_col // (2 * half), part, 0.0)
            a_ref[c] = a.astype(BF16)

    qts, intras, incs, decays = [], [], [], []
    for c in range(n_chunks):
        rows = slice(c * ck, (c + 1) * ck)
        g = g_all[rows]
        q = q_ref[rows, :].astype(F32) * (GLA_DK ** -0.5)
        k = k_ref[rows, :].astype(F32)
        v = v_ref[rows, :]
        g_last = g[ck - 1:ck, :]
        qt = (q * jnp.exp(g)).astype(BF16)
        kt = (k * jnp.exp(-g)).astype(BF16)
        kh = (k * jnp.exp(g_last - g)).astype(BF16)

        u = lax.dot_general(v, kh, (((0,), (0,)), ((), ())), preferred_element_type=F32)
        incs.append(jnp.where(st_blockdiag, u, 0.0))
        decays.append(jnp.exp(g_last))

        a = lax.dot_general(stack_heads(qt), kt, (((1,), (1,)), ((), ())), preferred_element_type=F32)
        a = jnp.where(unsafe, a_ref[c], jnp.where(a_row >= a_col, a, 0.0).astype(BF16))
        intra = []
        for p in range(N_HEADS // 2):
            r = jnp.dot(a[2 * p * ck:2 * (p + 1) * ck, :], v[:, p * LANES:(p + 1) * LANES], preferred_element_type=F32)
            intra.append(jnp.where(lane_pair < HEAD_DIM, r[0:ck, :], r[ck:2 * ck, :]))
        intras.append(jnp.concatenate(intra, axis=1))
        qts.append(qt)

    state = st_ref[...]
    states = []
    for c in range(n_chunks):
        states.append(state.astype(BF16))
        state = state * decays[c] + incs[c]
    st_ref[...] = state

    outs = []
    for c in range(n_chunks):
        o = lax.dot_general(qts[c], states[c], (((1,), (1,)), ((), ())), preferred_element_type=F32)
        outs.append(o + intras[c])

    o = jnp.concatenate(outs, axis=0)
    ssq = jnp.dot((o * o).astype(BF16), ones_ref[...], preferred_element_type=F32)
    o_ref[...] = (o * lax.rsqrt(ssq * (1.0 / HEAD_DIM) + EPS) * go_ref[...]).astype(BF16)


def _gla(proj, small, w_hi, w_lo, b_up, g_out, ones64, sel, batch, seq):
    n = proj.shape[0]
    tc = min(512, seq)
    nt = seq // tc
    hk = N_HEADS * GLA_DK
    full = lambda shape: pl.BlockSpec(shape, lambda b, t: (0, 0))
    return pl.pallas_call(
        functools.partial(_gla_kernel, tc=tc),
        grid=(batch, nt),
        in_specs=[
            pl.BlockSpec((tc, hk), lambda b, t: (b * nt + t, COL_AQ // hk)),
            pl.BlockSpec((tc, hk), lambda b, t: (b * nt + t, COL_AK // hk)),
            pl.BlockSpec((tc, GROUP_W), lambda b, t: (b * nt + t, COL_AV // GROUP_W)),
            pl.BlockSpec((tc, LANES), lambda b, t: (b * nt + t, 0)),
            full((LANES, hk)), full((LANES, hk)), full((1, hk)), full((1, GROUP_W)), full((GROUP_W, GROUP_W)),
            pl.BlockSpec((GLA_LEVELS, GLA_CHUNK, 3 * GLA_CHUNK), lambda b, t: (0, 0, 0)),
        ],
        out_specs=pl.BlockSpec((tc, GROUP_W), lambda b, t: (b * nt + t, 0)),
        out_shape=jax.ShapeDtypeStruct((n, GROUP_W), BF16),
        scratch_shapes=[pltpu.VMEM((GROUP_W, hk), F32), pltpu.VMEM((tc // GLA_CHUNK, N_HEADS * GLA_CHUNK, GLA_CHUNK), BF16)],
        compiler_params=_cparams(("parallel", "arbitrary")),
        name="gla",
    )(proj, proj, proj, small, w_hi, w_lo, b_up, g_out, ones64, sel)


def _lru_kernel(x_ref, cw_ref, cb_ref, wr_ref, br_ref, wi_ref, bi_ref, lam_ref, o_ref, xpad_ref, h_ref, *, tr):
    @pl.when(pl.program_id(1) == 0)
    def _():
        xpad_ref[0:8, :] = jnp.zeros((8, GROUP_W), F32)
        h_ref[...] = jnp.zeros_like(h_ref)

    xpad_ref[8:8 + tr, :] = x_ref[...].astype(F32)
    xc = cb_ref[...]
    for j in range(CONV_W):
        off = 8 - (CONV_W - 1) + j
        xc = xc + cw_ref[j:j + 1, :] * xpad_ref[off:off + tr, :]
    xpad_ref[0:8, :] = xpad_ref[tr:tr + 8, :]

    xb = xc.astype(BF16)
    r = _sigmoid(jnp.dot(xb, wr_ref[...], preferred_element_type=F32) + br_ref[...])
    i = _sigmoid(jnp.dot(xb, wi_ref[...], preferred_element_type=F32) + bi_ref[...])
    lam = lam_ref[...]
    softplus = jnp.maximum(-lam, 0.0) + jnp.log1p(jnp.exp(-jnp.abs(lam)))
    log_a = (-LRU_C) * r * softplus
    a = jnp.exp(log_a)
    bx = jnp.sqrt(1.0 - jnp.exp(2.0 * log_a)) * (i * xc)

    a3 = a.reshape(tr // 8, 8, GROUP_W)
    b3 = bx.reshape(tr // 8, 8, GROUP_W)
    sub = lax.broadcasted_iota(jnp.int32, a3.shape, 1)
    for s in (1, 2, 4):
        keep = sub >= s
        a_prev = jnp.where(keep, pltpu.roll(a3, s, 1), 1.0)
        b_prev = jnp.where(keep, pltpu.roll(b3, s, 1), 0.0)
        b3 = a3 * b_prev + b3
        a3 = a3 * a_prev
    h_prev = h_ref[0:1, :]
    for g in range(tr // 8):
        hg = a3[g] * h_prev + b3[g]
        o_ref[g * 8:(g + 1) * 8, :] = hg.astype(BF16)
        h_prev = hg[7:8, :]
    h_ref[0:1, :] = h_prev


def _lru(proj, conv_w, conv_b, wr, br, wi, bi, lam, batch, seq):
    n = proj.shape[0]
    tr = min(512, seq)
    nt = seq // tr
    full = lambda shape: pl.BlockSpec(shape, lambda b, t: (0, 0))
    return pl.pallas_call(
        functools.partial(_lru_kernel, tr=tr),
        grid=(batch, nt),
        in_specs=[
            pl.BlockSpec((tr, GROUP_W), lambda b, t: (b * nt + t, COL_DX // GROUP_W)),
            full((8, GROUP_W)), full((1, GROUP_W)),
            full((GROUP_W, GROUP_W)), full((1, GROUP_W)),
            full((GROUP_W, GROUP_W)), full((1, GROUP_W)), full((1, GROUP_W)),
        ],
        out_specs=pl.BlockSpec((tr, GROUP_W), lambda b, t: (b * nt + t, 0)),
        out_shape=jax.ShapeDtypeStruct((n, GROUP_W), BF16),
        scratch_shapes=[pltpu.VMEM((tr + 8, GROUP_W), F32), pltpu.VMEM((8, GROUP_W), F32)],
        compiler_params=_cparams(("parallel", "arbitrary")),
        name="rglru",
    )(proj, conv_w, conv_b, wr, br, wi, bi, lam)


def _pad_cols(w, width):
    return jnp.pad(w, [(0, 0)] * (w.ndim - 1) + [(0, width - w.shape[-1])])


def _block_diag_ones(n_blocks, size):
    return jnp.asarray(np.kron(np.eye(n_blocks, dtype=np.float32), np.ones((size, size), np.float32)), BF16)


def _swap_rotary(w):
    lead = w.shape[:-1]
    w = w.reshape(lead + (N_HEADS, MLA_NOPE + MLA_ROPE))
    half = MLA_ROPE // 2
    z1, z2 = w[..., MLA_NOPE:MLA_NOPE + half], w[..., MLA_NOPE + half:]
    w = jnp.concatenate([jnp.zeros_like(w[..., :MLA_NOPE]), z2, z1], axis=-1)
    return w.reshape(lead + (N_HEADS * (MLA_NOPE + MLA_ROPE),))


def _pad_heads(w, d_real):
    lead = w.shape[:-1]
    w = w.reshape(lead + (N_HEADS, d_real))
    w = jnp.pad(w, [(0, 0)] * len(lead) + [(0, 0), (0, HEAD_PAD - d_real)])
    return w.reshape(lead + (N_HEADS * HEAD_PAD,))


def kernel(x, positions, norm_g, w_in, w_out, gla_w_up, gla_b_up, gla_g_out, fox_b_f, fox_g_q, fox_g_k, mla_g_cq, mla_w_uq, mla_g_ckv, mla_w_ukv, mla_g_q, mla_g_k, lru_conv_w, lru_conv_b, lru_w_r, lru_b_r, lru_w_i, lru_b_i, lru_lam):
    batch, seq, _ = x.shape
    depth = w_in.shape[0]
    n = batch * seq

    sizes = (D_MIX, 256, 256, 512, GLA_RANK, 512, 512, 512, N_HEADS, MLA_Q_RANK, MLA_KV_RANK, MLA_ROPE, GROUP_W)
    offs = np.concatenate([[0], np.cumsum(sizes)])
    seg = [w_in[:, :, offs[i]:offs[i + 1]] for i in range(len(sizes))]
    (w_gate, w_aq, w_ak, w_av, w_ag, w_bq, w_bk, w_bv, w_bf, w_cq, w_ckv, w_ckr, w_dx) = seg
    w_main = jnp.concatenate(
        [w_gate, w_aq, w_ak, w_av, w_bq, w_bk, w_bv, w_cq, w_ckv, _pad_cols(w_ckr, LANES), w_dx], axis=-1).astype(BF16)
    w_small = _pad_cols(jnp.concatenate([w_ag, w_bf], axis=-1), LANES).astype(BF16)
    w_out_b = w_out.astype(BF16)

    ones64 = _block_diag_ones(GROUP_W // HEAD_DIM, HEAD_DIM)
    ones128 = _block_diag_ones(2, HEAD_PAD)

    wup = jnp.pad(gla_w_up, ((0, 0), (0, LANES - GLA_RANK), (0, 0)))
    wup_hi = wup.astype(BF16)
    wup_lo = (wup - wup_hi.astype(F32)).astype(BF16)

    sel = np.zeros((GLA_LEVELS, GLA_CHUNK, 3 * GLA_CHUNK), np.float32)
    for lvl in range(GLA_LEVELS):
        half = GLA_CHUNK >> (lvl + 1)
        for t in range(GLA_CHUNK):
            mid = (t // (2 * half)) * 2 * half + half - 1
            for piece in range(3):
                sel[lvl, t, piece * GLA_CHUNK + mid] = 1.0
    gla_sel = jnp.asarray(sel, BF16)

    fox_scale = LOG2E
    mla_scale = (MLA_NOPE + MLA_ROPE) ** -0.5 * LOG2E
    fox_gq = jnp.tile(fox_g_q, (1, N_HEADS)) * fox_scale
    fox_gk = jnp.tile(fox_g_k, (1, N_HEADS)) * HEAD_DIM ** 0.5
    place = np.zeros((LANES, N_HEADS * HEAD_PAD), np.float32)
    aug_q = np.zeros((1, N_HEADS * HEAD_PAD), np.float32)
    for h in range(N_HEADS):
        base = h * HEAD_PAD + (HEAD_DIM if h % 2 == 0 else 0)
        for piece in range(3):
            place[(piece + 1) * SM_BF + h, base + piece] = 1.0
            aug_q[0, base + piece] = 1.0
    place, aug_q = jnp.asarray(place, BF16), jnp.asarray(aug_q)
    fox_bf = jnp.pad(fox_b_f, ((0, 0), (SM_BF, LANES - SM_BF - N_HEADS)))

    wuq = _pad_heads(mla_w_uq, MLA_NOPE + MLA_ROPE).astype(BF16)
    wuq_sw = _pad_heads(_swap_rotary(mla_w_uq), MLA_NOPE + MLA_ROPE).astype(BF16)
    wukv = mla_w_ukv.reshape(depth, MLA_KV_RANK, N_HEADS, 2 * HEAD_DIM)
    wuk = _pad_heads(wukv[..., :MLA_NOPE].reshape(depth, MLA_KV_RANK, N_HEADS * MLA_NOPE), MLA_NOPE).astype(BF16)
    pad_rows = lambda m: jnp.pad(m.reshape(m.shape[:-2] + (N_HEADS, HEAD_DIM, m.shape[-1])),
                                 [(0, 0)] * (m.ndim - 1) + [(0, VT_ROWS - HEAD_DIM), (0, 0)]
                                 ).reshape(m.shape[:-2] + (N_HEADS * VT_ROWS, m.shape[-1]))
    wuv_t = pad_rows(jnp.swapaxes(wukv[..., MLA_NOPE:].reshape(depth, MLA_KV_RANK, GROUP_W), 1, 2)).astype(BF16)
    eye = pad_rows(jnp.eye(GROUP_W, dtype=F32)).astype(BF16)
    mla_gq_all = jnp.tile(mla_g_q, (1, N_HEADS)) * mla_scale
    mla_gk_all = jnp.tile(mla_g_k, (1, N_HEADS))
    mla_gq = _pad_heads(mla_gq_all, MLA_NOPE + MLA_ROPE)
    mla_gk = _pad_heads(mla_gk_all, MLA_NOPE + MLA_ROPE)
    mla_gq_sw = _pad_heads(_swap_rotary(mla_gq_all), MLA_NOPE + MLA_ROPE)
    mla_gk_sw = _pad_heads(_swap_rotary(mla_gk_all), MLA_NOPE + MLA_ROPE)

    eye_h = jnp.eye(N_HEADS, dtype=F32)
    wr_bd = jnp.einsum("lncd,nm->lncmd", lru_w_r, eye_h).reshape(depth, GROUP_W, GROUP_W).astype(BF16)
    wi_bd = jnp.einsum("lncd,nm->lncmd", lru_w_i, eye_h).reshape(depth, GROUP_W, GROUP_W).astype(BF16)
    conv_w = jnp.pad(lru_conv_w, ((0, 0), (0, 8 - CONV_W), (0, 0)))

    half = MLA_ROPE // 2
    inv_freq = ROPE_THETA ** (-jnp.arange(half, dtype=F32) / half)
    invf_row = jnp.pad(jnp.concatenate([inv_freq, inv_freq]), (MLA_NOPE, LANES - MLA_NOPE - MLA_ROPE))[None, :]
    cos_t, sin_t = _rope_tables(positions.reshape(n, 1), invf_row)

    xf = x.reshape(n, D_MODEL)
    for l in range(depth):
        proj, small = _norm_inproj(xf, norm_g[l][None, :], w_main, w_small, l)
        ya = _gla(proj, small, wup_hi[l], wup_lo[l], gla_b_up[l][None, :],
                  jnp.tile(gla_g_out[l], N_HEADS)[None, :], ones64, gla_sel, batch, seq)
        fq, fk, fvt = _fox_prep(proj, small, fox_bf[l][None, :], fox_gq[l][None, :], fox_gk[l][None, :], ones64, eye,
                                place, aug_q, batch, seq)
        yb = _attention(fq, fk, fvt, batch, seq)
        mq, mk, mvt = _mla_prep(proj, cos_t, sin_t, mla_g_cq[l][None, :], wuq[l], wuq_sw[l], mla_g_ckv[l][None, :],
                                wuk[l], wuv_t[l], mla_gq[l][None, :], mla_gq_sw[l][None, :], mla_gk[l][None, :],
                                mla_gk_sw[l][None, :], ones128, batch, seq)
        yc = _attention(mq, mk, mvt, batch, seq)
        yd = _lru(proj, conv_w[l], lru_conv_b[l][None, :], wr_bd[l], lru_b_r[l][None, :], wi_bd[l],
                  lru_b_i[l][None, :], lru_lam[l][None, :], batch, seq)
        xf = _gate_outproj(ya, yb, yc, yd, proj, w_out_b, xf, l)
    return xf.reshape(batch, seq, D_MODEL)
```

```python
import functools
import math

import numpy as np
import jax
import jax.numpy as jnp
from jax import lax
from jax.experimental import pallas as pl
from jax.experimental.pallas import tpu as pltpu

F32 = jnp.float32
BF16 = jnp.bfloat16

D_MODEL = 1024
D_MIX = 2048
GROUP_W = 512
N_HEADS = 8
HEAD_DIM = 64
EPS = 1e-6
GLA_DK = 32
GLA_RANK = 16
GLA_NORMALIZER = 16.0
GLA_CHUNK = 64
GLA_LEVELS = 6
GLA_SAFE_LOG_DECAY = 40.0
MLA_Q_RANK = 256
MLA_KV_RANK = 128
MLA_NOPE = 64
MLA_ROPE = 32
ROPE_THETA = 10000.0
CONV_W = 4
LRU_C = 8.0
LOG2E = 1.4426950408889634
NEG_BIG = -1e30

LANES = 128
HEAD_PAD = 128
VT_ROWS = 80

COL_GATE = 0
COL_AQ, COL_AK, COL_AV = 2048, 2304, 2560
COL_BQ, COL_BK, COL_BV = 3072, 3584, 4096
COL_CQ, COL_CKV, COL_CKR = 4608, 4864, 4992
COL_DX = 5120
D_PROJ = 5632
SM_AG, SM_BF = 0, 16


def _cparams(sem, vmem_mb=48):
    return pltpu.CompilerParams(dimension_semantics=sem, vmem_limit_bytes=vmem_mb * 1024 * 1024)


def _sigmoid(z):
    return 0.5 * jnp.tanh(0.5 * z) + 0.5


def _with_ones_rows(vt):
    row = lax.broadcasted_iota(jnp.int32, vt.shape, 0) % VT_ROWS
    return jnp.where(row == HEAD_DIM, 1.0, vt)


def _log_sigmoid(z):
    return jnp.minimum(z, 0.0) - jnp.log1p(jnp.exp(-jnp.abs(z)))


def _row_prefix_sum(v, seg):
    rows = v.shape[0]
    rid = lax.broadcasted_iota(jnp.int32, v.shape, 0) % seg
    s = 1
    while s < seg:
        v = v + jnp.where(rid >= s, pltpu.roll(v, s, 0), 0.0)
        s *= 2
    del rows
    return v


def _inproj_kernel(x_ref, g_ref, w_ref, ws_ref, o_ref, os_ref, h_ref):
    @pl.when(pl.program_id(1) == 0)
    def _():
        x = x_ref[...]
        ms = jnp.mean(x * x, axis=-1, keepdims=True)
        h = (x * lax.rsqrt(ms + EPS) * g_ref[...]).astype(BF16)
        h_ref[...] = h
        os_ref[...] = jnp.dot(h, ws_ref[...], preferred_element_type=F32)

    o_ref[...] = jnp.dot(h_ref[...], w_ref[...], preferred_element_type=F32).astype(BF16)


def _norm_inproj(x, g, w_all, ws_all, layer):
    n = x.shape[0]
    tm = min(1024, n)
    tn = D_PROJ // 2
    return pl.pallas_call(
        _inproj_kernel,
        grid=(n // tm, D_PROJ // tn),
        in_specs=[
            pl.BlockSpec((tm, D_MODEL), lambda i, j: (i, 0)),
            pl.BlockSpec((1, D_MODEL), lambda i, j: (0, 0)),
            pl.BlockSpec((None, D_MODEL, tn), lambda i, j: (layer, 0, j)),
            pl.BlockSpec((None, D_MODEL, LANES), lambda i, j: (layer, 0, 0)),
        ],
        out_specs=[
            pl.BlockSpec((tm, tn), lambda i, j: (i, j)),
            pl.BlockSpec((tm, LANES), lambda i, j: (i, 0)),
        ],
        out_shape=[jax.ShapeDtypeStruct((n, D_PROJ), BF16), jax.ShapeDtypeStruct((n, LANES), F32)],
        scratch_shapes=[pltpu.VMEM((tm, D_MODEL), BF16)],
        compiler_params=_cparams(("parallel", "arbitrary"), vmem_mb=56),
        name="norm_inproj",
    )(x, g, w_all, ws_all)


def _outproj_kernel(ya_ref, yb_ref, yc_ref, yd_ref, gate_ref, w_ref, x_ref, o_ref, z_ref):
    for gi, y_ref in enumerate((ya_ref, yb_ref, yc_ref, yd_ref)):
        lo, hi = gi * GROUP_W, (gi + 1) * GROUP_W
        g = gate_ref[:, lo:hi].astype(F32)
        z_ref[:, lo:hi] = (y_ref[...].astype(F32) * (g * _sigmoid(g))).astype(BF16)
    o_ref[...] = x_ref[...] + jnp.dot(z_ref[...], w_ref[...], preferred_element_type=F32)


def _gate_outproj(ya, yb, yc, yd, proj, w_all, x, layer):
    n = x.shape[0]
    tm = min(512, n)
    yspec = pl.BlockSpec((tm, GROUP_W), lambda i: (i, 0))
    return pl.pallas_call(
        _outproj_kernel,
        grid=(n // tm,),
        in_specs=[
            yspec, yspec, yspec, yspec,
            pl.BlockSpec((tm, D_MIX), lambda i: (i, 0)),
            pl.BlockSpec((None, D_MIX, D_MODEL), lambda i: (layer, 0, 0)),
            pl.BlockSpec((tm, D_MODEL), lambda i: (i, 0)),
        ],
        out_specs=pl.BlockSpec((tm, D_MODEL), lambda i: (i, 0)),
        out_shape=jax.ShapeDtypeStruct((n, D_MODEL), F32),
        scratch_shapes=[pltpu.VMEM((tm, D_MIX), BF16)],
        compiler_params=_cparams(("parallel",)),
        name="gate_outproj",
    )(ya, yb, yc, yd, proj, w_all, x)


def _attn_kernel(q_ref, k_ref, vt_ref, o_ref, m_ref, acc_ref, st_ref, mx_ref, *, tq, nq, hg):
    i = pl.program_id(2)
    half = tq // 2
    nt_dims = (((1,), (1,)), ((), ()))

    m_ref[...] = jnp.full(m_ref.shape, NEG_BIG, F32)
    acc_ref[...] = jnp.zeros(acc_ref.shape, F32)

    def scores(h, krows, qrows):
        lanes = slice(h * HEAD_PAD, (h + 1) * HEAD_PAD)
        return lax.dot_general(k_ref[krows, lanes], q_ref[qrows, lanes], nt_dims, preferred_element_type=F32)

    def vrows(h):
        return slice(h * VT_ROWS, (h + 1) * VT_ROWS)

    tri = lax.broadcasted_iota(jnp.int32, (half, half), 0) <= lax.broadcasted_iota(jnp.int32, (half, half), 1)
    low = lax.broadcasted_iota(jnp.int32, (tq, half), 0) <= lax.broadcasted_iota(jnp.int32, (tq, half), 1) + half

    def issue_scores(item, c):
        h, blk = item
        if blk is None:
            off = c * tq
            s_l = jnp.where(tri, scores(h, slice(off, off + half), slice(0, half)), NEG_BIG)
            s_r = jnp.where(low, scores(h, slice(off, off + tq), slice(half, tq)), NEG_BIG)
            mx = jnp.concatenate([jnp.max(s_l, axis=0, keepdims=True), jnp.max(s_r, axis=0, keepdims=True)], axis=1)
            return (s_l, s_r, mx)
        st = scores(h, slice(blk * tq, (blk + 1) * tq), slice(0, tq))
        return (st, jnp.max(st, axis=0, keepdims=True))

    def store_scores(item, slot, vals):
        if item[1] is None:
            st_ref[slot, 0:half, 0:half] = vals[0]
            st_ref[slot, :, half:tq] = vals[1]
        else:
            st_ref[slot] = vals[0]
        mx_ref[slot] = vals[-1]

    def consume(item, slot, c):
        h, blk = item
        m = m_ref[h]
        m_new = jnp.maximum(m, mx_ref[slot])
        alpha = jnp.exp2(m - m_new)
        m_ref[h] = m_new

        def accumulate(cols, st, vt):
            p = jnp.exp2(st - m_new[:, cols]).astype(BF16)
            acc_ref[h, :, cols] = alpha[:, cols] * acc_ref[h, :, cols] + jnp.dot(vt, p, preferred_element_type=F32)

        if blk is None:
            off = c * tq
            accumulate(slice(0, half), st_ref[slot, 0:half, 0:half], vt_ref[vrows(h), off:off + half])
            accumulate(slice(half, tq), st_ref[slot, :, half:tq], vt_ref[vrows(h), off:off + tq])
        else:
            accumulate(slice(0, tq), st_ref[slot], vt_ref[vrows(h), blk * tq:(blk + 1) * tq])

    for c in range(nq):
        @pl.when(i == c)
        def _(c=c):
            items = [(h, None) for h in range(hg)] + [(h, blk) for blk in range(c) for h in range(hg)]
            store_scores(items[0], 0, issue_scores(items[0], c))
            for n, item in enumerate(items):
                nxt = items[n + 1] if n + 1 < len(items) else None
                if nxt is not None:
                    vals = issue_scores(nxt, c)
                consume(item, n % 2, c)
                if nxt is not None:
                    store_scores(nxt, (n + 1) % 2, vals)

    ot = jnp.concatenate([acc_ref[h, 0:HEAD_DIM, :] / acc_ref[h, HEAD_DIM:HEAD_DIM + 1, :] for h in range(hg)], axis=0)
    o_ref[...] = ot.T.astype(BF16)


ATTN_HEADS_PER_STEP = 8


def _attention(q, k, vt, batch, seq):
    n = q.shape[0]
    tq = min(512, seq)
    nq = seq // tq
    hg = ATTN_HEADS_PER_STEP
    return pl.pallas_call(
        functools.partial(_attn_kernel, tq=tq, nq=nq, hg=hg),
        grid=(batch, N_HEADS // hg, nq),
        in_specs=[
            pl.BlockSpec((tq, hg * HEAD_PAD), lambda b, p, i: (b * nq + i, p)),
            pl.BlockSpec((seq, hg * HEAD_PAD), lambda b, p, i: (b, p)),
            pl.BlockSpec((None, hg * VT_ROWS, seq), lambda b, p, i: (b, p, 0)),
        ],
        out_specs=pl.BlockSpec((tq, hg * HEAD_DIM), lambda b, p, i: (b * nq + i, p)),
        out_shape=jax.ShapeDtypeStruct((n, GROUP_W), BF16),
        scratch_shapes=[pltpu.VMEM((hg, 1, tq), F32), pltpu.VMEM((hg, VT_ROWS, tq), F32),
                        pltpu.VMEM((2, tq, tq), F32), pltpu.VMEM((2, 1, tq), F32)],
        compiler_params=_cparams(("parallel", "parallel", "arbitrary")),
        name="causal_attention",
    )(q, k, vt)


def _fox_prep_kernel(q_ref, k_ref, v_ref, sm_ref, bf_ref, gq_ref, gk_ref, ones_ref, eye_ref, place_ref, augq_ref,
                     qo_ref, ko_ref, vt_ref, carry_ref, *, tt):
    @pl.when(pl.program_id(1) == 0)
    def _():
        carry_ref[...] = jnp.zeros_like(carry_ref)

    vt = lax.dot_general(eye_ref[...], v_ref[...], (((1,), (1,)), ((), ())), preferred_element_type=F32)
    vt_ref[...] = _with_ones_rows(vt).astype(BF16)

    def headnorm(ref, g_ref):
        z = ref[...].astype(F32)
        ssq = jnp.dot((z * z).astype(BF16), ones_ref[...], preferred_element_type=F32)
        return z * lax.rsqrt(ssq + HEAD_DIM * EPS) * g_ref[...]

    qn = headnorm(q_ref, gq_ref)
    kn = headnorm(k_ref, gk_ref)

    log_f = _log_sigmoid(sm_ref[...] + bf_ref[...])
    cum = _row_prefix_sum(log_f, tt) + carry_ref[0:1, :]
    carry_ref[0:1, :] = cum[tt - 1:tt, :]
    neg = cum * (-LOG2E)

    lane = lax.broadcasted_iota(jnp.int32, (tt, LANES), 1)
    hi = neg.astype(BF16).astype(F32)
    r1 = neg - hi
    mid = r1.astype(BF16).astype(F32)
    lo = r1 - mid
    pieces = jnp.where(lane < 2 * SM_BF, hi, jnp.where(lane < 3 * SM_BF, pltpu.roll(mid, SM_BF, 1),
                                                       pltpu.roll(lo, 2 * SM_BF, 1)))
    aug_k = jnp.dot(pieces.astype(BF16), place_ref[...], preferred_element_type=F32)
    aug_q = augq_ref[...]
    for h in range(N_HEADS):
        pair, e = h // 2, h % 2
        in_head = (lane >= e * HEAD_DIM) & (lane < (e + 1) * HEAD_DIM)
        blk = slice(h * HEAD_PAD, (h + 1) * HEAD_PAD)
        kpair = kn[:, pair * LANES:(pair + 1) * LANES]
        qpair = qn[:, pair * LANES:(pair + 1) * LANES]
        ko_ref[:, blk] = jnp.where(in_head, kpair, aug_k[:, blk]).astype(BF16)
        qo_ref[:, blk] = jnp.where(in_head, qpair, aug_q[:, blk]).astype(BF16)


def _fox_prep(proj, small, bf_row, gq, gk, ones64, eye, place, aug_q, batch, seq):
    n = proj.shape[0]
    tt = min(512, seq)
    nt = seq // tt
    row = lambda shape: pl.BlockSpec(shape, lambda b, t: (0, 0))
    return pl.pallas_call(
        functools.partial(_fox_prep_kernel, tt=tt),
        grid=(batch, nt),
        in_specs=[
            pl.BlockSpec((tt, GROUP_W), lambda b, t: (b * nt + t, COL_BQ // GROUP_W)),
            pl.BlockSpec((tt, GROUP_W), lambda b, t: (b * nt + t, COL_BK // GROUP_W)),
            pl.BlockSpec((tt, GROUP_W), lambda b, t: (b * nt + t, COL_BV // GROUP_W)),
            pl.BlockSpec((tt, LANES), lambda b, t: (b * nt + t, 0)),
            row((1, LANES)), row((1, GROUP_W)), row((1, GROUP_W)), row((GROUP_W, GROUP_W)),
            row((N_HEADS * VT_ROWS, GROUP_W)), row((LANES, N_HEADS * HEAD_PAD)), row((1, N_HEADS * HEAD_PAD)),
        ],
        out_specs=[
            pl.BlockSpec((tt, N_HEADS * HEAD_PAD), lambda b, t: (b * nt + t, 0)),
            pl.BlockSpec((tt, N_HEADS * HEAD_PAD), lambda b, t: (b * nt + t, 0)),
            pl.BlockSpec((None, N_HEADS * VT_ROWS, tt), lambda b, t: (b, 0, t)),
        ],
        out_shape=[jax.ShapeDtypeStruct((n, N_HEADS * HEAD_PAD), BF16)] * 2
        + [jax.ShapeDtypeStruct((batch, N_HEADS * VT_ROWS, seq), BF16)],
        scratch_shapes=[pltpu.VMEM((8, LANES), F32)],
        compiler_params=_cparams(("parallel", "arbitrary")),
        name="fox_prep",
    )(proj, proj, proj, small, bf_row, gq, gk, ones64, eye, place, aug_q)


def _rope_kernel(pos_ref, invf_ref, c_ref, s_ref):
    ang = pos_ref[...].astype(F32) * invf_ref[...]
    lane = lax.broadcasted_iota(jnp.int32, ang.shape, 1)
    half = MLA_ROPE // 2
    c, s = jnp.cos(ang), jnp.sin(ang)
    c_ref[...] = jnp.where((lane >= MLA_NOPE) & (lane < MLA_NOPE + MLA_ROPE), c, 1.0)
    s_ref[...] = jnp.where((lane >= MLA_NOPE) & (lane < MLA_NOPE + half), -s,
                           jnp.where((lane >= MLA_NOPE + half) & (lane < MLA_NOPE + MLA_ROPE), s, 0.0))


def _rope_tables(pos_col, invf_row):
    n = pos_col.shape[0]
    tm = min(1024, n)
    out = pl.BlockSpec((tm, LANES), lambda i: (i, 0))
    return pl.pallas_call(
        _rope_kernel,
        grid=(n // tm,),
        in_specs=[pl.BlockSpec((tm, 1), lambda i: (i, 0)), pl.BlockSpec((1, LANES), lambda i: (0, 0))],
        out_specs=[out, out],
        out_shape=[jax.ShapeDtypeStruct((n, LANES), F32)] * 2,
        compiler_params=_cparams(("parallel",)),
        name="rope_tables",
    )(pos_col, invf_row)


def _mla_prep_kernel(cq_ref, ckv_ref, ckr_ref, c_ref, s_ref, gcq_ref, wuq_ref, wuqs_ref, gckv_ref, wuk_ref, wuv_ref,
                     gq_ref, gqs_ref, gk_ref, gks_ref, ones_ref, qo_ref, ko_ref, vo_ref):
    d_qk = float(MLA_NOPE + MLA_ROPE)
    cos, sin = c_ref[...], s_ref[...]
    half = MLA_ROPE // 2

    def rms(z, g):
        return (z * lax.rsqrt(jnp.mean(z * z, axis=-1, keepdims=True) + EPS) * g).astype(BF16)

    def head_ssq(z):
        zz = (z * z).astype(BF16)
        parts = [jnp.dot(zz[:, c * 256:(c + 1) * 256], ones_ref[...], preferred_element_type=F32) for c in range(4)]
        return jnp.concatenate(parts, axis=1)

    cn = rms(cq_ref[...].astype(F32), gcq_ref[...])
    q = jnp.dot(cn, wuq_ref[...], preferred_element_type=F32)
    q_sw = jnp.dot(cn, wuqs_ref[...], preferred_element_type=F32)
    q_inv = lax.rsqrt(head_ssq(q) * (1.0 / d_qk) + EPS)
    qn = q * (q_inv * gq_ref[...])
    qn_sw = q_sw * (q_inv * gqs_ref[...])

    kvn = rms(ckv_ref[...].astype(F32), gckv_ref[...])
    kn = jnp.dot(kvn, wuk_ref[...], preferred_element_type=F32)
    vt = lax.dot_general(wuv_ref[...], kvn, (((1,), (1,)), ((), ())), preferred_element_type=F32)
    vo_ref[...] = _with_ones_rows(vt).astype(BF16)
    kr = ckr_ref[...].astype(F32)
    ss_r = jnp.sum(kr * kr, axis=-1, keepdims=True)
    lane = lax.broadcasted_iota(jnp.int32, kr.shape, 1)
    kr_placed = pltpu.roll(kr, MLA_NOPE, 1)
    kr_swapped = jnp.where(lane < MLA_NOPE + half, pltpu.roll(kr, MLA_NOPE - half, 1), pltpu.roll(kr, MLA_NOPE + half, 1))
    kr_swapped = jnp.where((lane >= MLA_NOPE) & (lane < MLA_NOPE + MLA_ROPE), kr_swapped, 0.0)
    k_inv = lax.rsqrt((head_ssq(kn) + ss_r) * (1.0 / d_qk) + EPS)
    k_scale = k_inv * gk_ref[...]
    k_scale_sw = k_inv * gks_ref[...]

    for h in range(N_HEADS):
        sl = slice(h * HEAD_PAD, (h + 1) * HEAD_PAD)
        qo_ref[:, sl] = (qn[:, sl] * cos + qn_sw[:, sl] * sin).astype(BF16)
        k_h = (kn[:, sl] + kr_placed) * k_scale[:, sl]
        ko_ref[:, sl] = (k_h * cos + (kr_swapped * k_scale_sw[:, sl]) * sin).astype(BF16)


def _mla_prep(proj, cos_t, sin_t, gcq, wuq, wuq_sw, gckv, wuk, wuv_t, gq, gq_sw, gk, gk_sw, ones128, batch, seq):
    n = proj.shape[0]
    tm = min(512, seq)
    nt = seq // tm
    full = lambda shape: pl.BlockSpec(shape, lambda i: (0, 0))
    tab = pl.BlockSpec((tm, LANES), lambda i: (i, 0))
    wide = pl.BlockSpec((tm, N_HEADS * HEAD_PAD), lambda i: (i, 0))
    return pl.pallas_call(
        _mla_prep_kernel,
        grid=(n // tm,),
        in_specs=[
            pl.BlockSpec((tm, MLA_Q_RANK), lambda i: (i, COL_CQ // MLA_Q_RANK)),
            pl.BlockSpec((tm, MLA_KV_RANK), lambda i: (i, COL_CKV // LANES)),
            pl.BlockSpec((tm, LANES), lambda i: (i, COL_CKR // LANES)),
            tab, tab,
            full((1, MLA_Q_RANK)), full((MLA_Q_RANK, N_HEADS * HEAD_PAD)), full((MLA_Q_RANK, N_HEADS * HEAD_PAD)),
            full((1, MLA_KV_RANK)), full((MLA_KV_RANK, N_HEADS * HEAD_PAD)), full((N_HEADS * VT_ROWS, MLA_KV_RANK)),
            full((1, N_HEADS * HEAD_PAD)), full((1, N_HEADS * HEAD_PAD)),
            full((1, N_HEADS * HEAD_PAD)), full((1, N_HEADS * HEAD_PAD)), full((256, 256)),
        ],
        out_specs=[wide, wide, pl.BlockSpec((None, N_HEADS * VT_ROWS, tm), lambda i: (i // nt, 0, i % nt))],
        out_shape=[jax.ShapeDtypeStruct((n, N_HEADS * HEAD_PAD), BF16)] * 2
        + [jax.ShapeDtypeStruct((batch, N_HEADS * VT_ROWS, seq), BF16)],
        compiler_params=_cparams(("parallel",)),
        name="mla_prep",
    )(proj, proj, proj, cos_t, sin_t, gcq, wuq, wuq_sw, gckv, wuk, wuv_t, gq, gq_sw, gk, gk_sw, ones128)


def _gla_kernel(q_ref, k_ref, v_ref, sm_ref, wh_ref, wl_ref, b_ref, go_ref, ones_ref, sel_ref, o_ref, st_ref,
                inter_ref, *, tc):
    hk = N_HEADS * GLA_DK
    ck = GLA_CHUNK

    @pl.when(pl.program_id(1) == 0)
    def _():
        st_ref[...] = jnp.zeros_like(st_ref)

    sm = sm_ref[...]
    sm_hi = sm.astype(BF16)
    sm_lo = (sm - sm_hi.astype(F32)).astype(BF16)
    z = (jnp.dot(sm_hi, wh_ref[...], preferred_element_type=F32)
         + jnp.dot(sm_lo, wh_ref[...], preferred_element_type=F32)
         + jnp.dot(sm_hi, wl_ref[...], preferred_element_type=F32)) + b_ref[...]
    g_all = _row_prefix_sum(_log_sigmoid(z) * (1.0 / GLA_NORMALIZER), ck)

    lane_k = lax.broadcasted_iota(jnp.int32, (ck, hk), 1) // GLA_DK
    lane_pair = lax.broadcasted_iota(jnp.int32, (ck, LANES), 1)
    a_row = lax.broadcasted_iota(jnp.int32, (N_HEADS * ck, ck), 0) % ck
    a_col = lax.broadcasted_iota(jnp.int32, (N_HEADS * ck, ck), 1)
    st_blockdiag = (lax.broadcasted_iota(jnp.int32, (GROUP_W, hk), 0) // HEAD_DIM
                    == lax.broadcasted_iota(jnp.int32, (GROUP_W, hk), 1) // GLA_DK)

    def stack_heads(x):
        return jnp.concatenate([jnp.where(lane_k == h, x, jnp.zeros_like(x)) for h in range(N_HEADS)], axis=0)

    def intra_output(a, v):
        parts = []
        for p in range(N_HEADS // 2):
            r = jnp.dot(a[2 * p * ck:2 * (p + 1) * ck, :], v[:, p * LANES:(p + 1) * LANES], preferred_element_type=F32)
            parts.append(jnp.where(lane_pair < HEAD_DIM, r[0:ck, :], r[ck:2 * ck, :]))
        return jnp.concatenate(parts, axis=1)

    def write_normed(o):
        ssq = jnp.dot((o * o).astype(BF16), ones_ref[...], preferred_element_type=F32)
        o_ref[...] = (o * lax.rsqrt(ssq * (1.0 / HEAD_DIM) + EPS) * go_ref[...]).astype(BF16)

    n_chunks = tc // ck
    qts, intras, incs, decays = [], [], [], []
    for c in range(n_chunks):
        rows = slice(c * ck, (c + 1) * ck)
        g = g_all[rows]
        q = q_ref[rows, :].astype(F32) * (GLA_DK ** -0.5)
        k = k_ref[rows, :].astype(F32)
        v = v_ref[rows, :]
        g_last = g[ck - 1:ck, :]
        qt = (q * jnp.exp(g)).astype(BF16)
        kt = (k * jnp.exp(-g)).astype(BF16)
        kh = (k * jnp.exp(g_last - g)).astype(BF16)

        u = lax.dot_general(v, kh, (((0,), (0,)), ((), ())), preferred_element_type=F32)
        incs.append(jnp.where(st_blockdiag, u, 0.0))
        decays.append(jnp.exp(g_last))

        a = lax.dot_general(stack_heads(qt), kt, (((1,), (1,)), ((), ())), preferred_element_type=F32)
        intras.append(intra_output(jnp.where(a_row >= a_col, a, 0.0).astype(BF16), v))
        qts.append(qt)

    state = st_ref[...]
    states = []
    for c in range(n_chunks):
        states.append(state.astype(BF16))
        state = state * decays[c] + incs[c]
    st_ref[...] = state

    outs = []
    for c in range(n_chunks):
        o = lax.dot_general(qts[c], states[c], (((1,), (1,)), ((), ())), preferred_element_type=F32)
        inter_ref[c * ck:(c + 1) * ck, :] = o
        outs.append(o + intras[c])
    write_normed(jnp.concatenate(outs, axis=0))

    @pl.when(jnp.min(g_all) < -GLA_SAFE_LOG_DECAY)
    def _():
        t_idx = lax.broadcasted_iota(jnp.int32, (ck, hk), 0)
        nt_dims = (((1,), (1,)), ((), ()))
        redo = []
        for c in range(n_chunks):
            rows = slice(c * ck, (c + 1) * ck)
            g = g_all[rows]
            q = q_ref[rows, :].astype(F32) * (GLA_DK ** -0.5)
            k = k_ref[rows, :].astype(F32)
            g_hi = g.astype(BF16)
            g_r1 = g - g_hi.astype(F32)
            g_mid = g_r1.astype(BF16)
            g_lo = (g_r1 - g_mid.astype(F32)).astype(BF16)
            g3 = jnp.concatenate([g_hi, g_mid, g_lo], axis=0)
            a = lax.dot_general(stack_heads(q.astype(BF16)), k.astype(BF16), nt_dims, preferred_element_type=F32)
            a = jnp.where(a_row == a_col, a, 0.0)
            for lvl in range(GLA_LEVELS):
                half = ck >> (lvl + 1)
                g_mid_row = jnp.dot(sel_ref[lvl], g3, preferred_element_type=F32)
                d = g - g_mid_row
                second = (t_idx // half) % 2 == 1
                qd = jnp.where(second, q * jnp.exp(jnp.minimum(d, 0.0)), 0.0).astype(BF16)
                kd = jnp.where(second, 0.0, k * jnp.exp(jnp.minimum(-d, 0.0))).astype(BF16)
                part = lax.dot_general(stack_heads(qd), kd, nt_dims, preferred_element_type=F32)
                a = a + jnp.where(a_row // (2 * half) == a_col // (2 * half), part, 0.0)
            redo.append(inter_ref[rows, :] + intra_output(a.astype(BF16), v_ref[rows, :]))
        write_normed(jnp.concatenate(redo, axis=0))


def _gla(proj, small, w_hi, w_lo, b_up, g_out, ones64, sel, batch, seq):
    n = proj.shape[0]
    tc = min(512, seq)
    nt = seq // tc
    hk = N_HEADS * GLA_DK
    full = lambda shape: pl.BlockSpec(shape, lambda b, t: (0, 0))
    return pl.pallas_call(
        functools.partial(_gla_kernel, tc=tc),
        grid=(batch, nt),
        in_specs=[
            pl.BlockSpec((tc, hk), lambda b, t: (b * nt + t, COL_AQ // hk)),
            pl.BlockSpec((tc, hk), lambda b, t: (b * nt + t, COL_AK // hk)),
            pl.BlockSpec((tc, GROUP_W), lambda b, t: (b * nt + t, COL_AV // GROUP_W)),
            pl.BlockSpec((tc, LANES), lambda b, t: (b * nt + t, 0)),
            full((LANES, hk)), full((LANES, hk)), full((1, hk)), full((1, GROUP_W)), full((GROUP_W, GROUP_W)),
            pl.BlockSpec((GLA_LEVELS, GLA_CHUNK, 3 * GLA_CHUNK), lambda b, t: (0, 0, 0)),
        ],
        out_specs=pl.BlockSpec((tc, GROUP_W), lambda b, t: (b * nt + t, 0)),
        out_shape=jax.ShapeDtypeStruct((n, GROUP_W), BF16),
        scratch_shapes=[pltpu.VMEM((GROUP_W, hk), F32), pltpu.VMEM((tc, GROUP_W), F32)],
        compiler_params=_cparams(("parallel", "arbitrary")),
        name="gla",
    )(proj, proj, proj, small, w_hi, w_lo, b_up, g_out, ones64, sel)


def _lru_kernel(x_ref, cw_ref, cb_ref, wr_ref, br_ref, wi_ref, bi_ref, lam_ref, o_ref, xpad_ref, h_ref, *, tr):
    @pl.when(pl.program_id(1) == 0)
    def _():
        xpad_ref[0:8, :] = jnp.zeros((8, GROUP_W), F32)
        h_ref[...] = jnp.zeros_like(h_ref)

    xpad_ref[8:8 + tr, :] = x_ref[...].astype(F32)
    xc = cb_ref[...]
    for j in range(CONV_W):
        off = 8 - (CONV_W - 1) + j
        xc = xc + cw_ref[j:j + 1, :] * xpad_ref[off:off + tr, :]
    xpad_ref[0:8, :] = xpad_ref[tr:tr + 8, :]

    xb = xc.astype(BF16)
    r = _sigmoid(jnp.dot(xb, wr_ref[...], preferred_element_type=F32) + br_ref[...])
    i = _sigmoid(jnp.dot(xb, wi_ref[...], preferred_element_type=F32) + bi_ref[...])
    lam = lam_ref[...]
    softplus = jnp.maximum(-lam, 0.0) + jnp.log1p(jnp.exp(-jnp.abs(lam)))
    log_a = (-LRU_C) * r * softplus
    a = jnp.exp(log_a)
    bx = jnp.sqrt(1.0 - jnp.exp(2.0 * log_a)) * (i * xc)

    a3 = a.reshape(tr // 8, 8, GROUP_W)
    b3 = bx.reshape(tr // 8, 8, GROUP_W)
    sub = lax.broadcasted_iota(jnp.int32, a3.shape, 1)
    for s in (1, 2, 4):
        keep = sub >= s
        a_prev = jnp.where(keep, pltpu.roll(a3, s, 1), 1.0)
        b_prev = jnp.where(keep, pltpu.roll(b3, s, 1), 0.0)
        b3 = a3 * b_prev + b3
        a3 = a3 * a_prev
    h_prev = h_ref[0:1, :]
    for g in range(tr // 8):
        hg = a3[g] * h_prev + b3[g]
        o_ref[g * 8:(g + 1) * 8, :] = hg.astype(BF16)
        h_prev = hg[7:8, :]
    h_ref[0:1, :] = h_prev


def _lru(proj, conv_w, conv_b, wr, br, wi, bi, lam, batch, seq):
    n = proj.shape[0]
    tr = min(512, seq)
    nt = seq // tr
    full = lambda shape: pl.BlockSpec(shape, lambda b, t: (0, 0))
    return pl.pallas_call(
        functools.partial(_lru_kernel, tr=tr),
        grid=(batch, nt),
        in_specs=[
            pl.BlockSpec((tr, GROUP_W), lambda b, t: (b * nt + t, COL_DX // GROUP_W)),
            full((8, GROUP_W)), full((1, GROUP_W)),
            full((GROUP_W, GROUP_W)), full((1, GROUP_W)),
            full((GROUP_W, GROUP_W)), full((1, GROUP_W)), full((1, GROUP_W)),
        ],
        out_specs=pl.BlockSpec((tr, GROUP_W), lambda b, t: (b * nt + t, 0)),
        out_shape=jax.ShapeDtypeStruct((n, GROUP_W), BF16),
        scratch_shapes=[pltpu.VMEM((tr + 8, GROUP_W), F32), pltpu.VMEM((8, GROUP_W), F32)],
        compiler_params=_cparams(("parallel", "arbitrary")),
        name="rglru",
    )(proj, conv_w, conv_b, wr, br, wi, bi, lam)


def _pad_cols(w, width):
    return jnp.pad(w, [(0, 0)] * (w.ndim - 1) + [(0, width - w.shape[-1])])


def _block_diag_ones(n_blocks, size):
    return jnp.asarray(np.kron(np.eye(n_blocks, dtype=np.float32), np.ones((size, size), np.float32)), BF16)


def _swap_rotary(w):
    lead = w.shape[:-1]
    w = w.reshape(lead + (N_HEADS, MLA_NOPE + MLA_ROPE))
    half = MLA_ROPE // 2
    z1, z2 = w[..., MLA_NOPE:MLA_NOPE + half], w[..., MLA_NOPE + half:]
    w = jnp.concatenate([jnp.zeros_like(w[..., :MLA_NOPE]), z2, z1], axis=-1)
    return w.reshape(lead + (N_HEADS * (MLA_NOPE + MLA_ROPE),))


def _pad_heads(w, d_real):
    lead = w.shape[:-1]
    w = w.reshape(lead + (N_HEADS, d_real))
    w = jnp.pad(w, [(0, 0)] * len(lead) + [(0, 0), (0, HEAD_PAD - d_real)])
    return w.reshape(lead + (N_HEADS * HEAD_PAD,))


def kernel(x, positions, norm_g, w_in, w_out, gla_w_up, gla_b_up, gla_g_out, fox_b_f, fox_g_q, fox_g_k, mla_g_cq, mla_w_uq, mla_g_ckv, mla_w_ukv, mla_g_q, mla_g_k, lru_conv_w, lru_conv_b, lru_w_r, lru_b_r, lru_w_i, lru_b_i, lru_lam):
    batch, seq, _ = x.shape
    depth = w_in.shape[0]
    n = batch * seq

    sizes = (D_MIX, 256, 256, 512, GLA_RANK, 512, 512, 512, N_HEADS, MLA_Q_RANK, MLA_KV_RANK, MLA_ROPE, GROUP_W)
    offs = np.concatenate([[0], np.cumsum(sizes)])
    seg = [w_in[:, :, offs[i]:offs[i + 1]] for i in range(len(sizes))]
    (w_gate, w_aq, w_ak, w_av, w_ag, w_bq, w_bk, w_bv, w_bf, w_cq, w_ckv, w_ckr, w_dx) = seg
    w_main = jnp.concatenate(
        [w_gate, w_aq, w_ak, w_av, w_bq, w_bk, w_bv, w_cq, w_ckv, _pad_cols(w_ckr, LANES), w_dx], axis=-1).astype(BF16)
    w_small = _pad_cols(jnp.concatenate([w_ag, w_bf], axis=-1), LANES).astype(BF16)
    w_out_b = w_out.astype(BF16)

    ones64 = _block_diag_ones(GROUP_W // HEAD_DIM, HEAD_DIM)
    ones128 = _block_diag_ones(2, HEAD_PAD)

    wup = jnp.pad(gla_w_up, ((0, 0), (0, LANES - GLA_RANK), (0, 0)))
    wup_hi = wup.astype(BF16)
    wup_lo = (wup - wup_hi.astype(F32)).astype(BF16)

    sel = np.zeros((GLA_LEVELS, GLA_CHUNK, 3 * GLA_CHUNK), np.float32)
    for lvl in range(GLA_LEVELS):
        half = GLA_CHUNK >> (lvl + 1)
        for t in range(GLA_CHUNK):
            mid = (t // (2 * half)) * 2 * half + half - 1
            for piece in range(3):
                sel[lvl, t, piece * GLA_CHUNK + mid] = 1.0
    gla_sel = jnp.asarray(sel, BF16)

    fox_scale = LOG2E
    mla_scale = (MLA_NOPE + MLA_ROPE) ** -0.5 * LOG2E
    fox_gq = jnp.tile(fox_g_q, (1, N_HEADS)) * fox_scale
    fox_gk = jnp.tile(fox_g_k, (1, N_HEADS)) * HEAD_DIM ** 0.5
    place = np.zeros((LANES, N_HEADS * HEAD_PAD), np.float32)
    aug_q = np.zeros((1, N_HEADS * HEAD_PAD), np.float32)
    for h in range(N_HEADS):
        base = h * HEAD_PAD + (HEAD_DIM if h % 2 == 0 else 0)
        for piece in range(3):
            place[(piece + 1) * SM_BF + h, base + piece] = 1.0
            aug_q[0, base + piece] = 1.0
    place, aug_q = jnp.asarray(place, BF16), jnp.asarray(aug_q)
    fox_bf = jnp.pad(fox_b_f, ((0, 0), (SM_BF, LANES - SM_BF - N_HEADS)))

    wuq = _pad_heads(mla_w_uq, MLA_NOPE + MLA_ROPE).astype(BF16)
    wuq_sw = _pad_heads(_swap_rotary(mla_w_uq), MLA_NOPE + MLA_ROPE).astype(BF16)
    wukv = mla_w_ukv.reshape(depth, MLA_KV_RANK, N_HEADS, 2 * HEAD_DIM)
    wuk = _pad_heads(wukv[..., :MLA_NOPE].reshape(depth, MLA_KV_RANK, N_HEADS * MLA_NOPE), MLA_NOPE).astype(BF16)
    pad_rows = lambda m: jnp.pad(m.reshape(m.shape[:-2] + (N_HEADS, HEAD_DIM, m.shape[-1])),
                                 [(0, 0)] * (m.ndim - 1) + [(0, VT_ROWS - HEAD_DIM), (0, 0)]
                                 ).reshape(m.shape[:-2] + (N_HEADS * VT_ROWS, m.shape[-1]))
    wuv_t = pad_rows(jnp.swapaxes(wukv[..., MLA_NOPE:].reshape(depth, MLA_KV_RANK, GROUP_W), 1, 2)).astype(BF16)
    eye = pad_rows(jnp.eye(GROUP_W, dtype=F32)).astype(BF16)
    mla_gq_all = jnp.tile(mla_g_q, (1, N_HEADS)) * mla_scale
    mla_gk_all = jnp.tile(mla_g_k, (1, N_HEADS))
    mla_gq = _pad_heads(mla_gq_all, MLA_NOPE + MLA_ROPE)
    mla_gk = _pad_heads(mla_gk_all, MLA_NOPE + MLA_ROPE)
    mla_gq_sw = _pad_heads(_swap_rotary(mla_gq_all), MLA_NOPE + MLA_ROPE)
    mla_gk_sw = _pad_heads(_swap_rotary(mla_gk_all), MLA_NOPE + MLA_ROPE)

    eye_h = jnp.eye(N_HEADS, dtype=F32)
    wr_bd = jnp.einsum("lncd,nm->lncmd", lru_w_r, eye_h).reshape(depth, GROUP_W, GROUP_W).astype(BF16)
    wi_bd = jnp.einsum("lncd,nm->lncmd", lru_w_i, eye_h).reshape(depth, GROUP_W, GROUP_W).astype(BF16)
    conv_w = jnp.pad(lru_conv_w, ((0, 0), (0, 8 - CONV_W), (0, 0)))

    half = MLA_ROPE // 2
    inv_freq = ROPE_THETA ** (-jnp.arange(half, dtype=F32) / half)
    invf_row = jnp.pad(jnp.concatenate([inv_freq, inv_freq]), (MLA_NOPE, LANES - MLA_NOPE - MLA_ROPE))[None, :]
    cos_t, sin_t = _rope_tables(positions.reshape(n, 1), invf_row)

    xf = x.reshape(n, D_MODEL)
    for l in range(depth):
        proj, small = _norm_inproj(xf, norm_g[l][None, :], w_main, w_small, l)
        ya = _gla(proj, small, wup_hi[l], wup_lo[l], gla_b_up[l][None, :],
                  jnp.tile(gla_g_out[l], N_HEADS)[None, :], ones64, gla_sel, batch, seq)
        fq, fk, fvt = _fox_prep(proj, small, fox_bf[l][None, :], fox_gq[l][None, :], fox_gk[l][None, :], ones64, eye,
                                place, aug_q, batch, seq)
        yb = _attention(fq, fk, fvt, batch, seq)
        mq, mk, mvt = _mla_prep(proj, cos_t, sin_t, mla_g_cq[l][None, :], wuq[l], wuq_sw[l], mla_g_ckv[l][None, :],
                                wuk[l], wuv_t[l], mla_gq[l][None, :], mla_gq_sw[l][None, :], mla_gk[l][None, :],
                                mla_gk_sw[l][None, :], ones128, batch, seq)
        yc = _attention(mq, mk, mvt, batch, seq)
        yd = _lru(proj, conv_w[l], lru_conv_b[l][None, :], wr_bd[l], lru_b_r[l][None, :], wi_bd[l],
                  lru_b_i[l][None, :], lru_lam[l][None, :], batch, seq)
        xf = _gate_outproj(ya, yb, yc, yd, proj, w_out_b, xf, l)
    return xf.reshape(batch, seq, D_MODEL)
```

```python
import functools
import math

import numpy as np
import jax
import jax.numpy as jnp
from jax import lax
from jax.experimental import pallas as pl
from jax.experimental.pallas import tpu as pltpu

F32 = jnp.float32
BF16 = jnp.bfloat16

D_MODEL = 1024
D_MIX = 2048
GROUP_W = 512
N_HEADS = 8
HEAD_DIM = 64
EPS = 1e-6
GLA_DK = 32
GLA_RANK = 16
GLA_NORMALIZER = 16.0
GLA_CHUNK = 64
GLA_LEVELS = 6
GLA_SAFE_LOG_DECAY = 40.0
MLA_Q_RANK = 256
MLA_KV_RANK = 128
MLA_NOPE = 64
MLA_ROPE = 32
ROPE_THETA = 10000.0
CONV_W = 4
LRU_C = 8.0
LOG2E = 1.4426950408889634
NEG_BIG = -1e30

LANES = 128
HEAD_PAD = 128
VT_ROWS = 80

COL_GATE = 0
COL_AQ, COL_AK, COL_AV = 2048, 2304, 2560
COL_BQ, COL_BK, COL_BV = 3072, 3584, 4096
COL_CQ, COL_CKV, COL_CKR = 4608, 4864, 4992
COL_DX = 5120
D_PROJ = 5632
SM_AG, SM_BF = 0, 16


def _cparams(sem, vmem_mb=48):
    return pltpu.CompilerParams(dimension_semantics=sem, vmem_limit_bytes=vmem_mb * 1024 * 1024)


def _sigmoid(z):
    return 0.5 * jnp.tanh(0.5 * z) + 0.5


def _with_ones_rows(vt):
    row = lax.broadcasted_iota(jnp.int32, vt.shape, 0) % VT_ROWS
    return jnp.where(row == HEAD_DIM, 1.0, vt)


def _log_sigmoid(z):
    return jnp.minimum(z, 0.0) - jnp.log1p(jnp.exp(-jnp.abs(z)))


def _row_prefix_sum(v, seg):
    rows = v.shape[0]
    rid = lax.broadcasted_iota(jnp.int32, v.shape, 0) % seg
    s = 1
    while s < seg:
        v = v + jnp.where(rid >= s, pltpu.roll(v, s, 0), 0.0)
        s *= 2
    del rows
    return v


def _inproj_kernel(x_ref, g_ref, w_ref, ws_ref, o_ref, os_ref, h_ref):
    @pl.when(pl.program_id(1) == 0)
    def _():
        x = x_ref[...]
        ms = jnp.mean(x * x, axis=-1, keepdims=True)
        h = (x * lax.rsqrt(ms + EPS) * g_ref[...]).astype(BF16)
        h_ref[...] = h
        os_ref[...] = jnp.dot(h, ws_ref[...], preferred_element_type=F32)

    o_ref[...] = jnp.dot(h_ref[...], w_ref[...], preferred_element_type=F32).astype(BF16)


def _norm_inproj(x, g, w_all, ws_all, layer):
    n = x.shape[0]
    tm = min(1024, n)
    tn = D_PROJ // 2
    return pl.pallas_call(
        _inproj_kernel,
        grid=(n // tm, D_PROJ // tn),
        in_specs=[
            pl.BlockSpec((tm, D_MODEL), lambda i, j: (i, 0)),
            pl.BlockSpec((1, D_MODEL), lambda i, j: (0, 0)),
            pl.BlockSpec((None, D_MODEL, tn), lambda i, j: (layer, 0, j)),
            pl.BlockSpec((None, D_MODEL, LANES), lambda i, j: (layer, 0, 0)),
        ],
        out_specs=[
            pl.BlockSpec((tm, tn), lambda i, j: (i, j)),
            pl.BlockSpec((tm, LANES), lambda i, j: (i, 0)),
        ],
        out_shape=[jax.ShapeDtypeStruct((n, D_PROJ), BF16), jax.ShapeDtypeStruct((n, LANES), F32)],
        scratch_shapes=[pltpu.VMEM((tm, D_MODEL), BF16)],
        compiler_params=_cparams(("parallel", "arbitrary"), vmem_mb=56),
        name="norm_inproj",
    )(x, g, w_all, ws_all)


def _outproj_kernel(ya_ref, yb_ref, yc_ref, yd_ref, gate_ref, w_ref, x_ref, o_ref, z_ref):
    for gi, y_ref in enumerate((ya_ref, yb_ref, yc_ref, yd_ref)):
        lo, hi = gi * GROUP_W, (gi + 1) * GROUP_W
        g = gate_ref[:, lo:hi].astype(F32)
        z_ref[:, lo:hi] = (y_ref[...].astype(F32) * (g * _sigmoid(g))).astype(BF16)
    o_ref[...] = x_ref[...] + jnp.dot(z_ref[...], w_ref[...], preferred_element_type=F32)


def _gate_outproj(ya, yb, yc, yd, proj, w_all, x, layer):
    n = x.shape[0]
    tm = min(512, n)
    yspec = pl.BlockSpec((tm, GROUP_W), lambda i: (i, 0))
    return pl.pallas_call(
        _outproj_kernel,
        grid=(n // tm,),
        in_specs=[
            yspec, yspec, yspec, yspec,
            pl.BlockSpec((tm, D_MIX), lambda i: (i, 0)),
            pl.BlockSpec((None, D_MIX, D_MODEL), lambda i: (layer, 0, 0)),
            pl.BlockSpec((tm, D_MODEL), lambda i: (i, 0)),
        ],
        out_specs=pl.BlockSpec((tm, D_MODEL), lambda i: (i, 0)),
        out_shape=jax.ShapeDtypeStruct((n, D_MODEL), F32),
        scratch_shapes=[pltpu.VMEM((tm, D_MIX), BF16)],
        compiler_params=_cparams(("parallel",)),
        name="gate_outproj",
    )(ya, yb, yc, yd, proj, w_all, x)


def _attn_kernel(q_ref, k_ref, vt_ref, o_ref, m_ref, acc_ref, st_ref, mx_ref, *, tq, nq, hg):
    i = pl.program_id(2)
    nt_dims = (((1,), (1,)), ((), ()))

    m_ref[...] = jnp.full(m_ref.shape, NEG_BIG, F32)
    acc_ref[...] = jnp.zeros(acc_ref.shape, F32)

    def vrows(h):
        return slice(h * VT_ROWS, (h + 1) * VT_ROWS)

    causal = lax.broadcasted_iota(jnp.int32, (tq, tq), 0) <= lax.broadcasted_iota(jnp.int32, (tq, tq), 1)

    def issue_scores(item):
        h, blk, nblk, diagonal = item
        lanes = slice(h * HEAD_PAD, (h + 1) * HEAD_PAD)
        st = lax.dot_general(k_ref[blk * tq:(blk + nblk) * tq, lanes], q_ref[:, lanes], nt_dims,
                             preferred_element_type=F32)
        if diagonal:
            st = jnp.where(causal, st, NEG_BIG)
        return st, jnp.max(st, axis=0, keepdims=True)

    def store_scores(item, slot, vals):
        st_ref[slot, 0:item[2] * tq, :] = vals[0]
        mx_ref[slot] = vals[1]

    def consume(item, slot):
        h, blk, nblk, _ = item
        m = m_ref[h]
        m_new = jnp.maximum(m, mx_ref[slot])
        alpha = jnp.exp2(m - m_new)
        m_ref[h] = m_new
        p = jnp.exp2(st_ref[slot, 0:nblk * tq, :] - m_new).astype(BF16)
        pv = jnp.dot(vt_ref[vrows(h), blk * tq:(blk + nblk) * tq], p, preferred_element_type=F32)
        acc_ref[h] = alpha * acc_ref[h] + pv

    for c in range(nq):
        @pl.when(i == c)
        def _(c=c):
            groups = [(b0, min(ATTN_BLOCKS_PER_ITEM, c - b0)) for b0 in range(0, c, ATTN_BLOCKS_PER_ITEM)]
            items = ([(h, c, 1, True) for h in range(hg)]
                     + [(h, b0, nb, False) for b0, nb in groups for h in range(hg)])
            store_scores(items[0], 0, issue_scores(items[0]))
            for n, item in enumerate(items):
                nxt = items[n + 1] if n + 1 < len(items) else None
                if nxt is not None:
                    vals = issue_scores(nxt)
                consume(item, n % 2)
                if nxt is not None:
                    store_scores(nxt, (n + 1) % 2, vals)

    ot = jnp.concatenate([acc_ref[h, 0:HEAD_DIM, :] / acc_ref[h, HEAD_DIM:HEAD_DIM + 1, :] for h in range(hg)], axis=0)
    o_ref[...] = ot.T.astype(BF16)


ATTN_HEADS_PER_STEP = 8
ATTN_BLOCKS_PER_ITEM = 2


def _attention(q, k, vt, batch, seq):
    n = q.shape[0]
    tq = min(512, seq)
    nq = seq // tq
    hg = ATTN_HEADS_PER_STEP
    return pl.pallas_call(
        functools.partial(_attn_kernel, tq=tq, nq=nq, hg=hg),
        grid=(batch, N_HEADS // hg, nq),
        in_specs=[
            pl.BlockSpec((tq, hg * HEAD_PAD), lambda b, p, i: (b * nq + i, p)),
            pl.BlockSpec((seq, hg * HEAD_PAD), lambda b, p, i: (b, p)),
            pl.BlockSpec((None, hg * VT_ROWS, seq), lambda b, p, i: (b, p, 0)),
        ],
        out_specs=pl.BlockSpec((tq, hg * HEAD_DIM), lambda b, p, i: (b * nq + i, p)),
        out_shape=jax.ShapeDtypeStruct((n, GROUP_W), BF16),
        scratch_shapes=[pltpu.VMEM((hg, 1, tq), F32), pltpu.VMEM((hg, VT_ROWS, tq), F32),
                        pltpu.VMEM((2, ATTN_BLOCKS_PER_ITEM * tq, tq), F32), pltpu.VMEM((2, 1, tq), F32)],
        compiler_params=_cparams(("parallel", "parallel", "arbitrary")),
        name="causal_attention",
    )(q, k, vt)


def _fox_prep_kernel(q_ref, k_ref, v_ref, sm_ref, bf_ref, gq_ref, gk_ref, ones_ref, eye_ref, place_ref, augq_ref,
                     qo_ref, ko_ref, vt_ref, carry_ref, *, tt):
    @pl.when(pl.program_id(1) == 0)
    def _():
        carry_ref[...] = jnp.zeros_like(carry_ref)

    vt = lax.dot_general(eye_ref[...], v_ref[...], (((1,), (1,)), ((), ())), preferred_element_type=F32)
    vt_ref[...] = _with_ones_rows(vt).astype(BF16)

    def headnorm(ref, g_ref):
        z = ref[...].astype(F32)
        ssq = jnp.dot((z * z).astype(BF16), ones_ref[...], preferred_element_type=F32)
        return z * lax.rsqrt(ssq + HEAD_DIM * EPS) * g_ref[...]

    qn = headnorm(q_ref, gq_ref)
    kn = headnorm(k_ref, gk_ref)

    log_f = _log_sigmoid(sm_ref[...] + bf_ref[...])
    cum = _row_prefix_sum(log_f, tt) + carry_ref[0:1, :]
    carry_ref[0:1, :] = cum[tt - 1:tt, :]
    neg = cum * (-LOG2E)

    lane = lax.broadcasted_iota(jnp.int32, (tt, LANES), 1)
    hi = neg.astype(BF16).astype(F32)
    r1 = neg - hi
    mid = r1.astype(BF16).astype(F32)
    lo = r1 - mid
    pieces = jnp.where(lane < 2 * SM_BF, hi, jnp.where(lane < 3 * SM_BF, pltpu.roll(mid, SM_BF, 1),
                                                       pltpu.roll(lo, 2 * SM_BF, 1)))
    aug_k = jnp.dot(pieces.astype(BF16), place_ref[...], preferred_element_type=F32)
    aug_q = augq_ref[...]
    for h in range(N_HEADS):
        pair, e = h // 2, h % 2
        in_head = (lane >= e * HEAD_DIM) & (lane < (e + 1) * HEAD_DIM)
        blk = slice(h * HEAD_PAD, (h + 1) * HEAD_PAD)
        kpair = kn[:, pair * LANES:(pair + 1) * LANES]
        qpair = qn[:, pair * LANES:(pair + 1) * LANES]
        ko_ref[:, blk] = jnp.where(in_head, kpair, aug_k[:, blk]).astype(BF16)
        qo_ref[:, blk] = jnp.where(in_head, qpair, aug_q[:, blk]).astype(BF16)


def _fox_prep(proj, small, bf_row, gq, gk, ones64, eye, place, aug_q, batch, seq):
    n = proj.shape[0]
    tt = min(512, seq)
    nt = seq // tt
    row = lambda shape: pl.BlockSpec(shape, lambda b, t: (0, 0))
    return pl.pallas_call(
        functools.partial(_fox_prep_kernel, tt=tt),
        grid=(batch, nt),
        in_specs=[
            pl.BlockSpec((tt, GROUP_W), lambda b, t: (b * nt + t, COL_BQ // GROUP_W)),
            pl.BlockSpec((tt, GROUP_W), lambda b, t: (b * nt + t, COL_BK // GROUP_W)),
            pl.BlockSpec((tt, GROUP_W), lambda b, t: (b * nt + t, COL_BV // GROUP_W)),
            pl.BlockSpec((tt, LANES), lambda b, t: (b * nt + t, 0)),
            row((1, LANES)), row((1, GROUP_W)), row((1, GROUP_W)), row((GROUP_W, GROUP_W)),
            row((N_HEADS * VT_ROWS, GROUP_W)), row((LANES, N_HEADS * HEAD_PAD)), row((1, N_HEADS * HEAD_PAD)),
        ],
        out_specs=[
            pl.BlockSpec((tt, N_HEADS * HEAD_PAD), lambda b, t: (b * nt + t, 0)),
            pl.BlockSpec((tt, N_HEADS * HEAD_PAD), lambda b, t: (b * nt + t, 0)),
            pl.BlockSpec((None, N_HEADS * VT_ROWS, tt), lambda b, t: (b, 0, t)),
        ],
        out_shape=[jax.ShapeDtypeStruct((n, N_HEADS * HEAD_PAD), BF16)] * 2
        + [jax.ShapeDtypeStruct((batch, N_HEADS * VT_ROWS, seq), BF16)],
        scratch_shapes=[pltpu.VMEM((8, LANES), F32)],
        compiler_params=_cparams(("parallel", "arbitrary")),
        name="fox_prep",
    )(proj, proj, proj, small, bf_row, gq, gk, ones64, eye, place, aug_q)


def _rope_kernel(pos_ref, invf_ref, c_ref, s_ref):
    ang = pos_ref[...].astype(F32) * invf_ref[...]
    lane = lax.broadcasted_iota(jnp.int32, ang.shape, 1)
    half = MLA_ROPE // 2
    c, s = jnp.cos(ang), jnp.sin(ang)
    c_ref[...] = jnp.where((lane >= MLA_NOPE) & (lane < MLA_NOPE + MLA_ROPE), c, 1.0)
    s_ref[...] = jnp.where((lane >= MLA_NOPE) & (lane < MLA_NOPE + half), -s,
                           jnp.where((lane >= MLA_NOPE + half) & (lane < MLA_NOPE + MLA_ROPE), s, 0.0))


def _rope_tables(pos_col, invf_row):
    n = pos_col.shape[0]
    tm = min(1024, n)
    out = pl.BlockSpec((tm, LANES), lambda i: (i, 0))
    return pl.pallas_call(
        _rope_kernel,
        grid=(n // tm,),
        in_specs=[pl.BlockSpec((tm, 1), lambda i: (i, 0)), pl.BlockSpec((1, LANES), lambda i: (0, 0))],
        out_specs=[out, out],
        out_shape=[jax.ShapeDtypeStruct((n, LANES), F32)] * 2,
        compiler_params=_cparams(("parallel",)),
        name="rope_tables",
    )(pos_col, invf_row)


def _mla_prep_kernel(cq_ref, ckv_ref, ckr_ref, c_ref, s_ref, gcq_ref, wuq_ref, wuqs_ref, gckv_ref, wuk_ref, wuv_ref,
                     gq_ref, gqs_ref, gk_ref, gks_ref, ones_ref, qo_ref, ko_ref, vo_ref):
    d_qk = float(MLA_NOPE + MLA_ROPE)
    cos, sin = c_ref[...], s_ref[...]
    half = MLA_ROPE // 2

    def rms(z, g):
        return (z * lax.rsqrt(jnp.mean(z * z, axis=-1, keepdims=True) + EPS) * g).astype(BF16)

    def head_ssq(z):
        zz = (z * z).astype(BF16)
        parts = [jnp.dot(zz[:, c * 256:(c + 1) * 256], ones_ref[...], preferred_element_type=F32) for c in range(4)]
        return jnp.concatenate(parts, axis=1)

    cn = rms(cq_ref[...].astype(F32), gcq_ref[...])
    q = jnp.dot(cn, wuq_ref[...], preferred_element_type=F32)
    q_sw = jnp.dot(cn, wuqs_ref[...], preferred_element_type=F32)
    q_inv = lax.rsqrt(head_ssq(q) * (1.0 / d_qk) + EPS)
    qn = q * (q_inv * gq_ref[...])
    qn_sw = q_sw * (q_inv * gqs_ref[...])

    kvn = rms(ckv_ref[...].astype(F32), gckv_ref[...])
    kn = jnp.dot(kvn, wuk_ref[...], preferred_element_type=F32)
    vt = lax.dot_general(wuv_ref[...], kvn, (((1,), (1,)), ((), ())), preferred_element_type=F32)
    vo_ref[...] = _with_ones_rows(vt).astype(BF16)
    kr = ckr_ref[...].astype(F32)
    ss_r = jnp.sum(kr * kr, axis=-1, keepdims=True)
    lane = lax.broadcasted_iota(jnp.int32, kr.shape, 1)
    kr_placed = pltpu.roll(kr, MLA_NOPE, 1)
    kr_swapped = jnp.where(lane < MLA_NOPE + half, pltpu.roll(kr, MLA_NOPE - half, 1), pltpu.roll(kr, MLA_NOPE + half, 1))
    kr_swapped = jnp.where((lane >= MLA_NOPE) & (lane < MLA_NOPE + MLA_ROPE), kr_swapped, 0.0)
    k_inv = lax.rsqrt((head_ssq(kn) + ss_r) * (1.0 / d_qk) + EPS)
    k_scale = k_inv * gk_ref[...]
    k_scale_sw = k_inv * gks_ref[...]

    for h in range(N_HEADS):
        sl = slice(h * HEAD_PAD, (h + 1) * HEAD_PAD)
        qo_ref[:, sl] = (qn[:, sl] * cos + qn_sw[:, sl] * sin).astype(BF16)
        k_h = (kn[:, sl] + kr_placed) * k_scale[:, sl]
        ko_ref[:, sl] = (k_h * cos + (kr_swapped * k_scale_sw[:, sl]) * sin).astype(BF16)


def _mla_prep(proj, cos_t, sin_t, gcq, wuq, wuq_sw, gckv, wuk, wuv_t, gq, gq_sw, gk, gk_sw, ones128, batch, seq):
    n = proj.shape[0]
    tm = min(512, seq)
    nt = seq // tm
    full = lambda shape: pl.BlockSpec(shape, lambda i: (0, 0))
    tab = pl.BlockSpec((tm, LANES), lambda i: (i, 0))
    wide = pl.BlockSpec((tm, N_HEADS * HEAD_PAD), lambda i: (i, 0))
    return pl.pallas_call(
        _mla_prep_kernel,
        grid=(n // tm,),
        in_specs=[
            pl.BlockSpec((tm, MLA_Q_RANK), lambda i: (i, COL_CQ // MLA_Q_RANK)),
            pl.BlockSpec((tm, MLA_KV_RANK), lambda i: (i, COL_CKV // LANES)),
            pl.BlockSpec((tm, LANES), lambda i: (i, COL_CKR // LANES)),
            tab, tab,
            full((1, MLA_Q_RANK)), full((MLA_Q_RANK, N_HEADS * HEAD_PAD)), full((MLA_Q_RANK, N_HEADS * HEAD_PAD)),
            full((1, MLA_KV_RANK)), full((MLA_KV_RANK, N_HEADS * HEAD_PAD)), full((N_HEADS * VT_ROWS, MLA_KV_RANK)),
            full((1, N_HEADS * HEAD_PAD)), full((1, N_HEADS * HEAD_PAD)),
            full((1, N_HEADS * HEAD_PAD)), full((1, N_HEADS * HEAD_PAD)), full((256, 256)),
        ],
        out_specs=[wide, wide, pl.BlockSpec((None, N_HEADS * VT_ROWS, tm), lambda i: (i // nt, 0, i % nt))],
        out_shape=[jax.ShapeDtypeStruct((n, N_HEADS * HEAD_PAD), BF16)] * 2
        + [jax.ShapeDtypeStruct((batch, N_HEADS * VT_ROWS, seq), BF16)],
        compiler_params=_cparams(("parallel",)),
        name="mla_prep",
    )(proj, proj, proj, cos_t, sin_t, gcq, wuq, wuq_sw, gckv, wuk, wuv_t, gq, gq_sw, gk, gk_sw, ones128)


def _gla_kernel(q_ref, k_ref, v_ref, sm_ref, wh_ref, wl_ref, b_ref, go_ref, ones_ref, sel_ref, o_ref, st_ref,
                inter_ref, *, tc):
    hk = N_HEADS * GLA_DK
    ck = GLA_CHUNK

    @pl.when(pl.program_id(1) == 0)
    def _():
        st_ref[...] = jnp.zeros_like(st_ref)

    sm = sm_ref[...]
    sm_hi = sm.astype(BF16)
    sm_lo = (sm - sm_hi.astype(F32)).astype(BF16)
    z = (jnp.dot(sm_hi, wh_ref[...], preferred_element_type=F32)
         + jnp.dot(sm_lo, wh_ref[...], preferred_element_type=F32)
         + jnp.dot(sm_hi, wl_ref[...], preferred_element_type=F32)) + b_ref[...]
    g_all = _row_prefix_sum(_log_sigmoid(z) * (1.0 / GLA_NORMALIZER), ck)

    lane_k = lax.broadcasted_iota(jnp.int32, (ck, hk), 1) // GLA_DK
    lane_pair = lax.broadcasted_iota(jnp.int32, (ck, LANES), 1)
    a_row = lax.broadcasted_iota(jnp.int32, (N_HEADS * ck, ck), 0) % ck
    a_col = lax.broadcasted_iota(jnp.int32, (N_HEADS * ck, ck), 1)
    st_blockdiag = (lax.broadcasted_iota(jnp.int32, (GROUP_W, hk), 0) // HEAD_DIM
                    == lax.broadcasted_iota(jnp.int32, (GROUP_W, hk), 1) // GLA_DK)

    def stack_heads(x):
        return jnp.concatenate([jnp.where(lane_k == h, x, jnp.zeros_like(x)) for h in range(N_HEADS)], axis=0)

    def intra_output(a, v):
        parts = []
        for p in range(N_HEADS // 2):
            r = jnp.dot(a[2 * p * ck:2 * (p + 1) * ck, :], v[:, p * LANES:(p + 1) * LANES], preferred_element_type=F32)
            parts.append(jnp.where(lane_pair < HEAD_DIM, r[0:ck, :], r[ck:2 * ck, :]))
        return jnp.concatenate(parts, axis=1)

    def write_normed(o):
        ssq = jnp.dot((o * o).astype(BF16), ones_ref[...], preferred_element_type=F32)
        o_ref[...] = (o * lax.rsqrt(ssq * (1.0 / HEAD_DIM) + EPS) * go_ref[...]).astype(BF16)

    n_chunks = tc // ck
    qts, intras, incs, decays = [], [], [], []
    for c in range(n_chunks):
        rows = slice(c * ck, (c + 1) * ck)
        g = g_all[rows]
        q = q_ref[rows, :].astype(F32) * (GLA_DK ** -0.5)
        k = k_ref[rows, :].astype(F32)
        v = v_ref[rows, :]
        g_last = g[ck - 1:ck, :]
        qt = (q * jnp.exp(g)).astype(BF16)
        kt = (k * jnp.exp(-g)).astype(BF16)
        kh = (k * jnp.exp(g_last - g)).astype(BF16)

        u = lax.dot_general(v, kh, (((0,), (0,)), ((), ())), preferred_element_type=F32)
        incs.append(jnp.where(st_blockdiag, u, 0.0))
        decays.append(jnp.exp(g_last))

        a = lax.dot_general(stack_heads(qt), kt, (((1,), (1,)), ((), ())), preferred_element_type=F32)
        intras.append(intra_output(jnp.where(a_row >= a_col, a, 0.0).astype(BF16), v))
        qts.append(qt)

    state = st_ref[...]
    states = []
    for c in range(n_chunks):
        states.append(state.astype(BF16))
        state = state * decays[c] + incs[c]
    st_ref[...] = state

    outs = []
    for c in range(n_chunks):
        o = lax.dot_general(qts[c], states[c], (((1,), (1,)), ((), ())), preferred_element_type=F32)
        inter_ref[c * ck:(c + 1) * ck, :] = o
        outs.append(o + intras[c])
    write_normed(jnp.concatenate(outs, axis=0))

    @pl.when(jnp.min(g_all) < -GLA_SAFE_LOG_DECAY)
    def _():
        t_idx = lax.broadcasted_iota(jnp.int32, (ck, hk), 0)
        nt_dims = (((1,), (1,)), ((), ()))
        redo = []
        for c in range(n_chunks):
            rows = slice(c * ck, (c + 1) * ck)
            g = g_all[rows]
            q = q_ref[rows, :].astype(F32) * (GLA_DK ** -0.5)
            k = k_ref[rows, :].astype(F32)
            g_hi = g.astype(BF16)
            g_r1 = g - g_hi.astype(F32)
            g_mid = g_r1.astype(BF16)
            g_lo = (g_r1 - g_mid.astype(F32)).astype(BF16)
            g3 = jnp.concatenate([g_hi, g_mid, g_lo], axis=0)
            a = lax.dot_general(stack_heads(q.astype(BF16)), k.astype(BF16), nt_dims, preferred_element_type=F32)
            a = jnp.where(a_row == a_col, a, 0.0)
            for lvl in range(GLA_LEVELS):
                half = ck >> (lvl + 1)
                g_mid_row = jnp.dot(sel_ref[lvl], g3, preferred_element_type=F32)
                d = g - g_mid_row
                second = (t_idx // half) % 2 == 1
                qd = jnp.where(second, q * jnp.exp(jnp.minimum(d, 0.0)), 0.0).astype(BF16)
                kd = jnp.where(second, 0.0, k * jnp.exp(jnp.minimum(-d, 0.0))).astype(BF16)
                part = lax.dot_general(stack_heads(qd), kd, nt_dims, preferred_element_type=F32)
                a = a + jnp.where(a_row // (2 * half) == a_col // (2 * half), part, 0.0)
            redo.append(inter_ref[rows, :] + intra_output(a.astype(BF16), v_ref[rows, :]))
        write_normed(jnp.concatenate(redo, axis=0))


def _gla(proj, small, w_hi, w_lo, b_up, g_out, ones64, sel, batch, seq):
    n = proj.shape[0]
    tc = min(512, seq)
    nt = seq // tc
    hk = N_HEADS * GLA_DK
    full = lambda shape: pl.BlockSpec(shape, lambda b, t: (0, 0))
    return pl.pallas_call(
        functools.partial(_gla_kernel, tc=tc),
        grid=(batch, nt),
        in_specs=[
            pl.BlockSpec((tc, hk), lambda b, t: (b * nt + t, COL_AQ // hk)),
            pl.BlockSpec((tc, hk), lambda b, t: (b * nt + t, COL_AK // hk)),
            pl.BlockSpec((tc, GROUP_W), lambda b, t: (b * nt + t, COL_AV // GROUP_W)),
            pl.BlockSpec((tc, LANES), lambda b, t: (b * nt + t, 0)),
            full((LANES, hk)), full((LANES, hk)), full((1, hk)), full((1, GROUP_W)), full((GROUP_W, GROUP_W)),
            pl.BlockSpec((GLA_LEVELS, GLA_CHUNK, 3 * GLA_CHUNK), lambda b, t: (0, 0, 0)),
        ],
        out_specs=pl.BlockSpec((tc, GROUP_W), lambda b, t: (b * nt + t, 0)),
        out_shape=jax.ShapeDtypeStruct((n, GROUP_W), BF16),
        scratch_shapes=[pltpu.VMEM((GROUP_W, hk), F32), pltpu.VMEM((tc, GROUP_W), F32)],
        compiler_params=_cparams(("parallel", "arbitrary")),
        name="gla",
    )(proj, proj, proj, small, w_hi, w_lo, b_up, g_out, ones64, sel)


def _lru_kernel(x_ref, cw_ref, cb_ref, wr_ref, br_ref, wi_ref, bi_ref, lam_ref, o_ref, xpad_ref, h_ref, *, tr):
    @pl.when(pl.program_id(1) == 0)
    def _():
        xpad_ref[0:8, :] = jnp.zeros((8, GROUP_W), F32)
        h_ref[...] = jnp.zeros_like(h_ref)

    xpad_ref[8:8 + tr, :] = x_ref[...].astype(F32)
    xc = cb_ref[...]
    for j in range(CONV_W):
        off = 8 - (CONV_W - 1) + j
        xc = xc + cw_ref[j:j + 1, :] * xpad_ref[off:off + tr, :]
    xpad_ref[0:8, :] = xpad_ref[tr:tr + 8, :]

    xb = xc.astype(BF16)
    r = _sigmoid(jnp.dot(xb, wr_ref[...], preferred_element_type=F32) + br_ref[...])
    i = _sigmoid(jnp.dot(xb, wi_ref[...], preferred_element_type=F32) + bi_ref[...])
    lam = lam_ref[...]
    softplus = jnp.maximum(-lam, 0.0) + jnp.log1p(jnp.exp(-jnp.abs(lam)))
    log_a = (-LRU_C) * r * softplus
    a = jnp.exp(log_a)
    bx = jnp.sqrt(1.0 - jnp.exp(2.0 * log_a)) * (i * xc)

    a3 = a.reshape(tr // 8, 8, GROUP_W)
    b3 = bx.reshape(tr // 8, 8, GROUP_W)
    sub = lax.broadcasted_iota(jnp.int32, a3.shape, 1)
    for s in (1, 2, 4):
        keep = sub >= s
        a_prev = jnp.where(keep, pltpu.roll(a3, s, 1), 1.0)
        b_prev = jnp.where(keep, pltpu.roll(b3, s, 1), 0.0)
        b3 = a3 * b_prev + b3
        a3 = a3 * a_prev
    h_prev = h_ref[0:1, :]
    for g in range(tr // 8):
        hg = a3[g] * h_prev + b3[g]
        o_ref[g * 8:(g + 1) * 8, :] = hg.astype(BF16)
        h_prev = hg[7:8, :]
    h_ref[0:1, :] = h_prev


def _lru(proj, conv_w, conv_b, wr, br, wi, bi, lam, batch, seq):
    n = proj.shape[0]
    tr = min(512, seq)
    nt = seq // tr
    full = lambda shape: pl.BlockSpec(shape, lambda b, t: (0, 0))
    return pl.pallas_call(
        functools.partial(_lru_kernel, tr=tr),
        grid=(batch, nt),
        in_specs=[
            pl.BlockSpec((tr, GROUP_W), lambda b, t: (b * nt + t, COL_DX // GROUP_W)),
            full((8, GROUP_W)), full((1, GROUP_W)),
            full((GROUP_W, GROUP_W)), full((1, GROUP_W)),
            full((GROUP_W, GROUP_W)), full((1, GROUP_W)), full((1, GROUP_W)),
        ],
        out_specs=pl.BlockSpec((tr, GROUP_W), lambda b, t: (b * nt + t, 0)),
        out_shape=jax.ShapeDtypeStruct((n, GROUP_W), BF16),
        scratch_shapes=[pltpu.VMEM((tr + 8, GROUP_W), F32), pltpu.VMEM((8, GROUP_W), F32)],
        compiler_params=_cparams(("parallel", "arbitrary")),
        name="rglru",
    )(proj, conv_w, conv_b, wr, br, wi, bi, lam)


def _pad_cols(w, width):
    return jnp.pad(w, [(0, 0)] * (w.ndim - 1) + [(0, width - w.shape[-1])])


def _block_diag_ones(n_blocks, size):
    return jnp.asarray(np.kron(np.eye(n_blocks, dtype=np.float32), np.ones((size, size), np.float32)), BF16)


def _swap_rotary(w):
    lead = w.shape[:-1]
    w = w.reshape(lead + (N_HEADS, MLA_NOPE + MLA_ROPE))
    half = MLA_ROPE // 2
    z1, z2 = w[..., MLA_NOPE:MLA_NOPE + half], w[..., MLA_NOPE + half:]
    w = jnp.concatenate([jnp.zeros_like(w[..., :MLA_NOPE]), z2, z1], axis=-1)
    return w.reshape(lead + (N_HEADS * (MLA_NOPE + MLA_ROPE),))


def _pad_heads(w, d_real):
    lead = w.shape[:-1]
    w = w.reshape(lead + (N_HEADS, d_real))
    w = jnp.pad(w, [(0, 0)] * len(lead) + [(0, 0), (0, HEAD_PAD - d_real)])
    return w.reshape(lead + (N_HEADS * HEAD_PAD,))


def kernel(x, positions, norm_g, w_in, w_out, gla_w_up, gla_b_up, gla_g_out, fox_b_f, fox_g_q, fox_g_k, mla_g_cq, mla_w_uq, mla_g_ckv, mla_w_ukv, mla_g_q, mla_g_k, lru_conv_w, lru_conv_b, lru_w_r, lru_b_r, lru_w_i, lru_b_i, lru_lam):
    batch, seq, _ = x.shape
    depth = w_in.shape[0]
    n = batch * seq

    sizes = (D_MIX, 256, 256, 512, GLA_RANK, 512, 512, 512, N_HEADS, MLA_Q_RANK, MLA_KV_RANK, MLA_ROPE, GROUP_W)
    offs = np.concatenate([[0], np.cumsum(sizes)])
    w_in_b = w_in.astype(BF16)
    seg = [w_in_b[:, :, offs[i]:offs[i + 1]] for i in range(len(sizes))]
    (w_gate, w_aq, w_ak, w_av, w_ag, w_bq, w_bk, w_bv, w_bf, w_cq, w_ckv, w_ckr, w_dx) = seg
    w_main = jnp.concatenate(
        [w_gate, w_aq, w_ak, w_av, w_bq, w_bk, w_bv, w_cq, w_ckv, _pad_cols(w_ckr, LANES), w_dx], axis=-1)
    w_small = _pad_cols(jnp.concatenate([w_ag, w_bf], axis=-1), LANES)
    w_out_b = w_out.astype(BF16)

    ones64 = _block_diag_ones(GROUP_W // HEAD_DIM, HEAD_DIM)
    ones128 = _block_diag_ones(2, HEAD_PAD)

    wup = jnp.pad(gla_w_up, ((0, 0), (0, LANES - GLA_RANK), (0, 0)))
    wup_hi = wup.astype(BF16)
    wup_lo = (wup - wup_hi.astype(F32)).astype(BF16)

    sel = np.zeros((GLA_LEVELS, GLA_CHUNK, 3 * GLA_CHUNK), np.float32)
    for lvl in range(GLA_LEVELS):
        half = GLA_CHUNK >> (lvl + 1)
        for t in range(GLA_CHUNK):
            mid = (t // (2 * half)) * 2 * half + half - 1
            for piece in range(3):
                sel[lvl, t, piece * GLA_CHUNK + mid] = 1.0
    gla_sel = jnp.asarray(sel, BF16)

    fox_scale = LOG2E
    mla_scale = (MLA_NOPE + MLA_ROPE) ** -0.5 * LOG2E
    fox_gq = jnp.tile(fox_g_q, (1, N_HEADS)) * fox_scale
    fox_gk = jnp.tile(fox_g_k, (1, N_HEADS)) * HEAD_DIM ** 0.5
    place = np.zeros((LANES, N_HEADS * HEAD_PAD), np.float32)
    aug_q = np.zeros((1, N_HEADS * HEAD_PAD), np.float32)
    for h in range(N_HEADS):
        base = h * HEAD_PAD + (HEAD_DIM if h % 2 == 0 else 0)
        for piece in range(3):
            place[(piece + 1) * SM_BF + h, base + piece] = 1.0
            aug_q[0, base + piece] = 1.0
    place, aug_q = jnp.asarray(place, BF16), jnp.asarray(aug_q)
    fox_bf = jnp.pad(fox_b_f, ((0, 0), (SM_BF, LANES - SM_BF - N_HEADS)))

    wuq = _pad_heads(mla_w_uq, MLA_NOPE + MLA_ROPE).astype(BF16)
    wuq_sw = _pad_heads(_swap_rotary(mla_w_uq), MLA_NOPE + MLA_ROPE).astype(BF16)
    wukv = mla_w_ukv.reshape(depth, MLA_KV_RANK, N_HEADS, 2 * HEAD_DIM)
    wuk = _pad_heads(wukv[..., :MLA_NOPE].reshape(depth, MLA_KV_RANK, N_HEADS * MLA_NOPE), MLA_NOPE).astype(BF16)
    pad_rows = lambda m: jnp.pad(m.reshape(m.shape[:-2] + (N_HEADS, HEAD_DIM, m.shape[-1])),
                                 [(0, 0)] * (m.ndim - 1) + [(0, VT_ROWS - HEAD_DIM), (0, 0)]
                                 ).reshape(m.shape[:-2] + (N_HEADS * VT_ROWS, m.shape[-1]))
    wuv_t = pad_rows(jnp.swapaxes(wukv[..., MLA_NOPE:].reshape(depth, MLA_KV_RANK, GROUP_W), 1, 2)).astype(BF16)
    eye = pad_rows(jnp.eye(GROUP_W, dtype=F32)).astype(BF16)
    mla_gq_all = jnp.tile(mla_g_q, (1, N_HEADS)) * mla_scale
    mla_gk_all = jnp.tile(mla_g_k, (1, N_HEADS))
    mla_gq = _pad_heads(mla_gq_all, MLA_NOPE + MLA_ROPE)
    mla_gk = _pad_heads(mla_gk_all, MLA_NOPE + MLA_ROPE)
    mla_gq_sw = _pad_heads(_swap_rotary(mla_gq_all), MLA_NOPE + MLA_ROPE)
    mla_gk_sw = _pad_heads(_swap_rotary(mla_gk_all), MLA_NOPE + MLA_ROPE)

    eye_h = jnp.eye(N_HEADS, dtype=F32)
    wr_bd = jnp.einsum("lncd,nm->lncmd", lru_w_r, eye_h).reshape(depth, GROUP_W, GROUP_W).astype(BF16)
    wi_bd = jnp.einsum("lncd,nm->lncmd", lru_w_i, eye_h).reshape(depth, GROUP_W, GROUP_W).astype(BF16)
    conv_w = jnp.pad(lru_conv_w, ((0, 0), (0, 8 - CONV_W), (0, 0)))

    half = MLA_ROPE // 2
    inv_freq = ROPE_THETA ** (-jnp.arange(half, dtype=F32) / half)
    invf_row = jnp.pad(jnp.concatenate([inv_freq, inv_freq]), (MLA_NOPE, LANES - MLA_NOPE - MLA_ROPE))[None, :]
    cos_t, sin_t = _rope_tables(positions.reshape(n, 1), invf_row)

    xf = x.reshape(n, D_MODEL)
    for l in range(depth):
        proj, small = _norm_inproj(xf, norm_g[l][None, :], w_main, w_small, l)
        ya = _gla(proj, small, wup_hi[l], wup_lo[l], gla_b_up[l][None, :],
                  jnp.tile(gla_g_out[l], N_HEADS)[None, :], ones64, gla_sel, batch, seq)
        fq, fk, fvt = _fox_prep(proj, small, fox_bf[l][None, :], fox_gq[l][None, :], fox_gk[l][None, :], ones64, eye,
                                place, aug_q, batch, seq)
        yb = _attention(fq, fk, fvt, batch, seq)
        mq, mk, mvt = _mla_prep(proj, cos_t, sin_t, mla_g_cq[l][None, :], wuq[l], wuq_sw[l], mla_g_ckv[l][None, :],
                                wuk[l], wuv_t[l], mla_gq[l][None, :], mla_gq_sw[l][None, :], mla_gk[l][None, :],
                                mla_gk_sw[l][None, :], ones128, batch, seq)
        yc = _attention(mq, mk, mvt, batch, seq)
        yd = _lru(proj, conv_w[l], lru_conv_b[l][None, :], wr_bd[l], lru_b_r[l][None, :], wi_bd[l],
                  lru_b_i[l][None, :], lru_lam[l][None, :], batch, seq)
        xf = _gate_outproj(ya, yb, yc, yd, proj, w_out_b, xf, l)
    return xf.reshape(batch, seq, D_MODEL)
```

```python
import functools
import math

import numpy as np
import jax
import jax.numpy as jnp
from jax import lax
from jax.experimental import pallas as pl
from jax.experimental.pallas import tpu as pltpu

F32 = jnp.float32
BF16 = jnp.bfloat16

D_MODEL = 1024
D_MIX = 2048
GROUP_W = 512
N_HEADS = 8
HEAD_DIM = 64
EPS = 1e-6
GLA_DK = 32
GLA_RANK = 16
GLA_NORMALIZER = 16.0
GLA_CHUNK = 64
GLA_LEVELS = 6
GLA_SAFE_LOG_DECAY = 40.0
MLA_Q_RANK = 256
MLA_KV_RANK = 128
MLA_NOPE = 64
MLA_ROPE = 32
ROPE_THETA = 10000.0
CONV_W = 4
LRU_C = 8.0
LOG2E = 1.4426950408889634
NEG_BIG = -1e30

LANES = 128
HEAD_PAD = 128
VT_ROWS = 80

COL_GATE = 0
COL_AQ, COL_AK, COL_AV = 2048, 2304, 2560
COL_BQ, COL_BK, COL_BV = 3072, 3584, 4096
COL_CQ, COL_CKV, COL_CKR = 4608, 4864, 4992
COL_DX = 5120
D_PROJ = 5632
SM_AG, SM_BF = 0, 16


def _cparams(sem, vmem_mb=48):
    return pltpu.CompilerParams(dimension_semantics=sem, vmem_limit_bytes=vmem_mb * 1024 * 1024)


def _sigmoid(z):
    return 0.5 * jnp.tanh(0.5 * z) + 0.5


def _with_ones_rows(vt):
    row = lax.broadcasted_iota(jnp.int32, vt.shape, 0) % VT_ROWS
    return jnp.where(row == HEAD_DIM, 1.0, vt)


def _log_sigmoid(z):
    return jnp.minimum(z, 0.0) - jnp.log1p(jnp.exp(-jnp.abs(z)))


def _row_prefix_sum(v, seg):
    rows = v.shape[0]
    rid = lax.broadcasted_iota(jnp.int32, v.shape, 0) % seg
    s = 1
    while s < seg:
        v = v + jnp.where(rid >= s, pltpu.roll(v, s, 0), 0.0)
        s *= 2
    del rows
    return v


def _inproj_kernel(x_ref, g_ref, w_ref, ws_ref, o_ref, os_ref, h_ref):
    @pl.when(pl.program_id(1) == 0)
    def _():
        x = x_ref[...]
        ms = jnp.mean(x * x, axis=-1, keepdims=True)
        h = (x * lax.rsqrt(ms + EPS) * g_ref[...]).astype(BF16)
        h_ref[...] = h
        os_ref[...] = jnp.dot(h, ws_ref[...], preferred_element_type=F32)

    o_ref[...] = jnp.dot(h_ref[...], w_ref[...], preferred_element_type=F32).astype(BF16)


def _norm_inproj(x, g, w_all, ws_all, layer):
    n = x.shape[0]
    tm = min(1024, n)
    tn = D_PROJ // 2
    return pl.pallas_call(
        _inproj_kernel,
        grid=(n // tm, D_PROJ // tn),
        in_specs=[
            pl.BlockSpec((tm, D_MODEL), lambda i, j: (i, 0)),
            pl.BlockSpec((1, D_MODEL), lambda i, j: (0, 0)),
            pl.BlockSpec((None, D_MODEL, tn), lambda i, j: (layer, 0, j)),
            pl.BlockSpec((None, D_MODEL, LANES), lambda i, j: (layer, 0, 0)),
        ],
        out_specs=[
            pl.BlockSpec((tm, tn), lambda i, j: (i, j)),
            pl.BlockSpec((tm, LANES), lambda i, j: (i, 0)),
        ],
        out_shape=[jax.ShapeDtypeStruct((n, D_PROJ), BF16), jax.ShapeDtypeStruct((n, LANES), F32)],
        scratch_shapes=[pltpu.VMEM((tm, D_MODEL), BF16)],
        compiler_params=_cparams(("parallel", "arbitrary"), vmem_mb=56),
        name="norm_inproj",
    )(x, g, w_all, ws_all)


def _outproj_kernel(ya_ref, yb_ref, yc_ref, yd_ref, gate_ref, w_ref, x_ref, o_ref, z_ref):
    for gi, y_ref in enumerate((ya_ref, yb_ref, yc_ref, yd_ref)):
        lo, hi = gi * GROUP_W, (gi + 1) * GROUP_W
        g = gate_ref[:, lo:hi].astype(F32)
        z_ref[:, lo:hi] = (y_ref[...].astype(F32) * (g * _sigmoid(g))).astype(BF16)
    o_ref[...] = x_ref[...] + jnp.dot(z_ref[...], w_ref[...], preferred_element_type=F32)


def _gate_outproj(ya, yb, yc, yd, proj, w_all, x, layer):
    n = x.shape[0]
    tm = min(512, n)
    yspec = pl.BlockSpec((tm, GROUP_W), lambda i: (i, 0))
    return pl.pallas_call(
        _outproj_kernel,
        grid=(n // tm,),
        in_specs=[
            yspec, yspec, yspec, yspec,
            pl.BlockSpec((tm, D_MIX), lambda i: (i, 0)),
            pl.BlockSpec((None, D_MIX, D_MODEL), lambda i: (layer, 0, 0)),
            pl.BlockSpec((tm, D_MODEL), lambda i: (i, 0)),
        ],
        out_specs=pl.BlockSpec((tm, D_MODEL), lambda i: (i, 0)),
        out_shape=jax.ShapeDtypeStruct((n, D_MODEL), F32),
        scratch_shapes=[pltpu.VMEM((tm, D_MIX), BF16)],
        compiler_params=_cparams(("parallel",)),
        name="gate_outproj",
    )(ya, yb, yc, yd, proj, w_all, x)


def _attn_kernel(q_ref, k_ref, vt_ref, o_ref, m_ref, acc_ref, st_ref, mx_ref, *, tq, nq, hg):
    i = pl.program_id(2)
    half = tq // 2
    nt_dims = (((1,), (1,)), ((), ()))

    m_ref[...] = jnp.full(m_ref.shape, NEG_BIG, F32)
    acc_ref[...] = jnp.zeros(acc_ref.shape, F32)

    def scores(h, krows, qrows):
        lanes = slice(h * HEAD_PAD, (h + 1) * HEAD_PAD)
        return lax.dot_general(k_ref[krows, lanes], q_ref[qrows, lanes], nt_dims, preferred_element_type=F32)

    def vrows(h):
        return slice(h * VT_ROWS, (h + 1) * VT_ROWS)

    tri = lax.broadcasted_iota(jnp.int32, (half, half), 0) <= lax.broadcasted_iota(jnp.int32, (half, half), 1)
    low = lax.broadcasted_iota(jnp.int32, (tq, half), 0) <= lax.broadcasted_iota(jnp.int32, (tq, half), 1) + half

    def issue_scores(item, c):
        h, blk = item
        if blk is None:
            off = c * tq
            s_l = jnp.where(tri, scores(h, slice(off, off + half), slice(0, half)), NEG_BIG)
            s_r = jnp.where(low, scores(h, slice(off, off + tq), slice(half, tq)), NEG_BIG)
            mx = jnp.concatenate([jnp.max(s_l, axis=0, keepdims=True), jnp.max(s_r, axis=0, keepdims=True)], axis=1)
            return (s_l, s_r, mx)
        st = scores(h, slice(blk * tq, (blk + 1) * tq), slice(0, tq))
        return (st, jnp.max(st, axis=0, keepdims=True))

    def store_scores(item, slot, vals):
        if item[1] is None:
            st_ref[slot, 0:half, 0:half] = vals[0]
            st_ref[slot, :, half:tq] = vals[1]
        else:
            st_ref[slot] = vals[0]
        mx_ref[slot] = vals[-1]

    def consume(item, slot, c):
        h, blk = item
        m = m_ref[h]
        m_new = jnp.maximum(m, mx_ref[slot])
        alpha = jnp.exp2(m - m_new)
        m_ref[h] = m_new

        def accumulate(cols, st, vt):
            p = jnp.exp2(st - m_new[:, cols]).astype(BF16)
            acc_ref[h, :, cols] = alpha[:, cols] * acc_ref[h, :, cols] + jnp.dot(vt, p, preferred_element_type=F32)

        if blk is None:
            off = c * tq
            accumulate(slice(0, half), st_ref[slot, 0:half, 0:half], vt_ref[vrows(h), off:off + half])
            accumulate(slice(half, tq), st_ref[slot, :, half:tq], vt_ref[vrows(h), off:off + tq])
        else:
            accumulate(slice(0, tq), st_ref[slot], vt_ref[vrows(h), blk * tq:(blk + 1) * tq])

    for c in range(nq):
        @pl.when(i == c)
        def _(c=c):
            items = [(h, None) for h in range(hg)] + [(h, blk) for blk in range(c) for h in range(hg)]
            store_scores(items[0], 0, issue_scores(items[0], c))
            for n, item in enumerate(items):
                nxt = items[n + 1] if n + 1 < len(items) else None
                if nxt is not None:
                    vals = issue_scores(nxt, c)
                consume(item, n % 2, c)
                if nxt is not None:
                    store_scores(nxt, (n + 1) % 2, vals)

    ot = jnp.concatenate([acc_ref[h, 0:HEAD_DIM, :] / acc_ref[h, HEAD_DIM:HEAD_DIM + 1, :] for h in range(hg)], axis=0)
    o_ref[...] = ot.T.astype(BF16)


ATTN_HEADS_PER_STEP = 8


def _attention(q, k, vt, batch, seq):
    n = q.shape[0]
    tq = min(512, seq)
    nq = seq // tq
    hg = ATTN_HEADS_PER_STEP
    return pl.pallas_call(
        functools.partial(_attn_kernel, tq=tq, nq=nq, hg=hg),
        grid=(batch, N_HEADS // hg, nq),
        in_specs=[
            pl.BlockSpec((tq, hg * HEAD_PAD), lambda b, p, i: (b * nq + i, p)),
            pl.BlockSpec((seq, hg * HEAD_PAD), lambda b, p, i: (b, p)),
            pl.BlockSpec((None, hg * VT_ROWS, seq), lambda b, p, i: (b, p, 0)),
        ],
        out_specs=pl.BlockSpec((tq, hg * HEAD_DIM), lambda b, p, i: (b * nq + i, p)),
        out_shape=jax.ShapeDtypeStruct((n, GROUP_W), BF16),
        scratch_shapes=[pltpu.VMEM((hg, 1, tq), F32), pltpu.VMEM((hg, VT_ROWS, tq), F32),
                        pltpu.VMEM((2, tq, tq), F32), pltpu.VMEM((2, 1, tq), F32)],
        compiler_params=_cparams(("parallel", "parallel", "arbitrary")),
        name="causal_attention",
    )(q, k, vt)


def _fox_prep_kernel(q_ref, k_ref, v_ref, sm_ref, bf_ref, gq_ref, gk_ref, ones_ref, eye_ref, place_ref, augq_ref,
                     qo_ref, ko_ref, vt_ref, carry_ref, *, tt):
    @pl.when(pl.program_id(1) == 0)
    def _():
        carry_ref[...] = jnp.zeros_like(carry_ref)

    vt = lax.dot_general(eye_ref[...], v_ref[...], (((1,), (1,)), ((), ())), preferred_element_type=F32)
    vt_ref[...] = _with_ones_rows(vt).astype(BF16)

    def headnorm(ref, g_ref):
        z = ref[...].astype(F32)
        ssq = jnp.dot((z * z).astype(BF16), ones_ref[...], preferred_element_type=F32)
        return z * lax.rsqrt(ssq + HEAD_DIM * EPS) * g_ref[...]

    qn = headnorm(q_ref, gq_ref)
    kn = headnorm(k_ref, gk_ref)

    log_f = _log_sigmoid(sm_ref[...] + bf_ref[...])
    cum = _row_prefix_sum(log_f, tt) + carry_ref[0:1, :]
    carry_ref[0:1, :] = cum[tt - 1:tt, :]
    neg = cum * (-LOG2E)

    lane = lax.broadcasted_iota(jnp.int32, (tt, LANES), 1)
    hi = neg.astype(BF16).astype(F32)
    r1 = neg - hi
    mid = r1.astype(BF16).astype(F32)
    lo = r1 - mid
    pieces = jnp.where(lane < 2 * SM_BF, hi, jnp.where(lane < 3 * SM_BF, pltpu.roll(mid, SM_BF, 1),
                                                       pltpu.roll(lo, 2 * SM_BF, 1)))
    aug_k = jnp.dot(pieces.astype(BF16), place_ref[...], preferred_element_type=F32)
    aug_q = augq_ref[...]
    for h in range(N_HEADS):
        pair, e = h // 2, h % 2
        in_head = (lane >= e * HEAD_DIM) & (lane < (e + 1) * HEAD_DIM)
        blk = slice(h * HEAD_PAD, (h + 1) * HEAD_PAD)
        kpair = kn[:, pair * LANES:(pair + 1) * LANES]
        qpair = qn[:, pair * LANES:(pair + 1) * LANES]
        ko_ref[:, blk] = jnp.where(in_head, kpair, aug_k[:, blk]).astype(BF16)
        qo_ref[:, blk] = jnp.where(in_head, qpair, aug_q[:, blk]).astype(BF16)


def _fox_prep(proj, small, bf_row, gq, gk, ones64, eye, place, aug_q, batch, seq):
    n = proj.shape[0]
    tt = min(1024, seq)
    nt = seq // tt
    row = lambda shape: pl.BlockSpec(shape, lambda b, t: (0, 0))
    return pl.pallas_call(
        functools.partial(_fox_prep_kernel, tt=tt),
        grid=(batch, nt),
        in_specs=[
            pl.BlockSpec((tt, GROUP_W), lambda b, t: (b * nt + t, COL_BQ // GROUP_W)),
            pl.BlockSpec((tt, GROUP_W), lambda b, t: (b * nt + t, COL_BK // GROUP_W)),
            pl.BlockSpec((tt, GROUP_W), lambda b, t: (b * nt + t, COL_BV // GROUP_W)),
            pl.BlockSpec((tt, LANES), lambda b, t: (b * nt + t, 0)),
            row((1, LANES)), row((1, GROUP_W)), row((1, GROUP_W)), row((GROUP_W, GROUP_W)),
            row((N_HEADS * VT_ROWS, GROUP_W)), row((LANES, N_HEADS * HEAD_PAD)), row((1, N_HEADS * HEAD_PAD)),
        ],
        out_specs=[
            pl.BlockSpec((tt, N_HEADS * HEAD_PAD), lambda b, t: (b * nt + t, 0)),
            pl.BlockSpec((tt, N_HEADS * HEAD_PAD), lambda b, t: (b * nt + t, 0)),
            pl.BlockSpec((None, N_HEADS * VT_ROWS, tt), lambda b, t: (b, 0, t)),
        ],
        out_shape=[jax.ShapeDtypeStruct((n, N_HEADS * HEAD_PAD), BF16)] * 2
        + [jax.ShapeDtypeStruct((batch, N_HEADS * VT_ROWS, seq), BF16)],
        scratch_shapes=[pltpu.VMEM((8, LANES), F32)],
        compiler_params=_cparams(("parallel", "arbitrary")),
        name="fox_prep",
    )(proj, proj, proj, small, bf_row, gq, gk, ones64, eye, place, aug_q)


def _rope_kernel(pos_ref, invf_ref, c_ref, s_ref):
    ang = pos_ref[...].astype(F32) * invf_ref[...]
    lane = lax.broadcasted_iota(jnp.int32, ang.shape, 1)
    half = MLA_ROPE // 2
    c, s = jnp.cos(ang), jnp.sin(ang)
    c_ref[...] = jnp.where((lane >= MLA_NOPE) & (lane < MLA_NOPE + MLA_ROPE), c, 1.0)
    s_ref[...] = jnp.where((lane >= MLA_NOPE) & (lane < MLA_NOPE + half), -s,
                           jnp.where((lane >= MLA_NOPE + half) & (lane < MLA_NOPE + MLA_ROPE), s, 0.0))


def _rope_tables(pos_col, invf_row):
    n = pos_col.shape[0]
    tm = min(1024, n)
    out = pl.BlockSpec((tm, LANES), lambda i: (i, 0))
    return pl.pallas_call(
        _rope_kernel,
        grid=(n // tm,),
        in_specs=[pl.BlockSpec((tm, 1), lambda i: (i, 0)), pl.BlockSpec((1, LANES), lambda i: (0, 0))],
        out_specs=[out, out],
        out_shape=[jax.ShapeDtypeStruct((n, LANES), F32)] * 2,
        compiler_params=_cparams(("parallel",)),
        name="rope_tables",
    )(pos_col, invf_row)


def _mla_prep_kernel(cq_ref, ckv_ref, ckr_ref, c_ref, s_ref, gcq_ref, wuq_ref, wuqs_ref, gckv_ref, wuk_ref, wuv_ref,
                     gq_ref, gqs_ref, gk_ref, gks_ref, ones_ref, qo_ref, ko_ref, vo_ref):
    d_qk = float(MLA_NOPE + MLA_ROPE)
    cos, sin = c_ref[...], s_ref[...]
    half = MLA_ROPE // 2

    def rms(z, g):
        return (z * lax.rsqrt(jnp.mean(z * z, axis=-1, keepdims=True) + EPS) * g).astype(BF16)

    def head_ssq(z):
        zz = (z * z).astype(BF16)
        parts = [jnp.dot(zz[:, c * 256:(c + 1) * 256], ones_ref[...], preferred_element_type=F32) for c in range(4)]
        return jnp.concatenate(parts, axis=1)

    cn = rms(cq_ref[...].astype(F32), gcq_ref[...])
    q = jnp.dot(cn, wuq_ref[...], preferred_element_type=F32)
    q_sw = jnp.dot(cn, wuqs_ref[...], preferred_element_type=F32)
    q_inv = lax.rsqrt(head_ssq(q) * (1.0 / d_qk) + EPS)
    qn = q * (q_inv * gq_ref[...])
    qn_sw = q_sw * (q_inv * gqs_ref[...])

    kvn = rms(ckv_ref[...].astype(F32), gckv_ref[...])
    kn = jnp.dot(kvn, wuk_ref[...], preferred_element_type=F32)
    vt = lax.dot_general(wuv_ref[...], kvn, (((1,), (1,)), ((), ())), preferred_element_type=F32)
    vo_ref[...] = _with_ones_rows(vt).astype(BF16)
    kr = ckr_ref[...].astype(F32)
    ss_r = jnp.sum(kr * kr, axis=-1, keepdims=True)
    lane = lax.broadcasted_iota(jnp.int32, kr.shape, 1)
    kr_placed = pltpu.roll(kr, MLA_NOPE, 1)
    kr_swapped = jnp.where(lane < MLA_NOPE + half, pltpu.roll(kr, MLA_NOPE - half, 1), pltpu.roll(kr, MLA_NOPE + half, 1))
    kr_swapped = jnp.where((lane >= MLA_NOPE) & (lane < MLA_NOPE + MLA_ROPE), kr_swapped, 0.0)
    k_inv = lax.rsqrt((head_ssq(kn) + ss_r) * (1.0 / d_qk) + EPS)
    k_scale = k_inv * gk_ref[...]
    k_scale_sw = k_inv * gks_ref[...]

    for h in range(N_HEADS):
        sl = slice(h * HEAD_PAD, (h + 1) * HEAD_PAD)
        qo_ref[:, sl] = (qn[:, sl] * cos + qn_sw[:, sl] * sin).astype(BF16)
        k_h = (kn[:, sl] + kr_placed) * k_scale[:, sl]
        ko_ref[:, sl] = (k_h * cos + (kr_swapped * k_scale_sw[:, sl]) * sin).astype(BF16)


def _mla_prep(proj, cos_t, sin_t, gcq, wuq, wuq_sw, gckv, wuk, wuv_t, gq, gq_sw, gk, gk_sw, ones128, batch, seq):
    n = proj.shape[0]
    tm = min(1024, seq)
    nt = seq // tm
    full = lambda shape: pl.BlockSpec(shape, lambda i: (0, 0))
    tab = pl.BlockSpec((tm, LANES), lambda i: (i, 0))
    wide = pl.BlockSpec((tm, N_HEADS * HEAD_PAD), lambda i: (i, 0))
    return pl.pallas_call(
        _mla_prep_kernel,
        grid=(n // tm,),
        in_specs=[
            pl.BlockSpec((tm, MLA_Q_RANK), lambda i: (i, COL_CQ // MLA_Q_RANK)),
            pl.BlockSpec((tm, MLA_KV_RANK), lambda i: (i, COL_CKV // LANES)),
            pl.BlockSpec((tm, LANES), lambda i: (i, COL_CKR // LANES)),
            tab, tab,
            full((1, MLA_Q_RANK)), full((MLA_Q_RANK, N_HEADS * HEAD_PAD)), full((MLA_Q_RANK, N_HEADS * HEAD_PAD)),
            full((1, MLA_KV_RANK)), full((MLA_KV_RANK, N_HEADS * HEAD_PAD)), full((N_HEADS * VT_ROWS, MLA_KV_RANK)),
            full((1, N_HEADS * HEAD_PAD)), full((1, N_HEADS * HEAD_PAD)),
            full((1, N_HEADS * HEAD_PAD)), full((1, N_HEADS * HEAD_PAD)), full((256, 256)),
        ],
        out_specs=[wide, wide, pl.BlockSpec((None, N_HEADS * VT_ROWS, tm), lambda i: (i // nt, 0, i % nt))],
        out_shape=[jax.ShapeDtypeStruct((n, N_HEADS * HEAD_PAD), BF16)] * 2
        + [jax.ShapeDtypeStruct((batch, N_HEADS * VT_ROWS, seq), BF16)],
        compiler_params=_cparams(("parallel",)),
        name="mla_prep",
    )(proj, proj, proj, cos_t, sin_t, gcq, wuq, wuq_sw, gckv, wuk, wuv_t, gq, gq_sw, gk, gk_sw, ones128)


def _gla_kernel(q_ref, k_ref, v_ref, sm_ref, wh_ref, wl_ref, b_ref, go_ref, ones_ref, sel_ref, o_ref, st_ref,
                inter_ref, *, tc):
    hk = N_HEADS * GLA_DK
    ck = GLA_CHUNK

    @pl.when(pl.program_id(1) == 0)
    def _():
        st_ref[...] = jnp.zeros_like(st_ref)

    sm = sm_ref[...]
    sm_hi = sm.astype(BF16)
    sm_lo = (sm - sm_hi.astype(F32)).astype(BF16)
    z = (jnp.dot(sm_hi, wh_ref[...], preferred_element_type=F32)
         + jnp.dot(sm_lo, wh_ref[...], preferred_element_type=F32)
         + jnp.dot(sm_hi, wl_ref[...], preferred_element_type=F32)) + b_ref[...]
    g_all = _row_prefix_sum(_log_sigmoid(z) * (1.0 / GLA_NORMALIZER), ck)

    lane_k = lax.broadcasted_iota(jnp.int32, (ck, hk), 1) // GLA_DK
    lane_pair = lax.broadcasted_iota(jnp.int32, (ck, LANES), 1)
    a_row = lax.broadcasted_iota(jnp.int32, (N_HEADS * ck, ck), 0) % ck
    a_col = lax.broadcasted_iota(jnp.int32, (N_HEADS * ck, ck), 1)
    st_blockdiag = (lax.broadcasted_iota(jnp.int32, (GROUP_W, hk), 0) // HEAD_DIM
                    == lax.broadcasted_iota(jnp.int32, (GROUP_W, hk), 1) // GLA_DK)

    def stack_heads(x):
        return jnp.concatenate([jnp.where(lane_k == h, x, jnp.zeros_like(x)) for h in range(N_HEADS)], axis=0)

    def intra_output(a, v):
        parts = []
        for p in range(N_HEADS // 2):
            r = jnp.dot(a[2 * p * ck:2 * (p + 1) * ck, :], v[:, p * LANES:(p + 1) * LANES], preferred_element_type=F32)
            parts.append(jnp.where(lane_pair < HEAD_DIM, r[0:ck, :], r[ck:2 * ck, :]))
        return jnp.concatenate(parts, axis=1)

    def write_normed(o):
        ssq = jnp.dot((o * o).astype(BF16), ones_ref[...], preferred_element_type=F32)
        o_ref[...] = (o * lax.rsqrt(ssq * (1.0 / HEAD_DIM) + EPS) * go_ref[...]).astype(BF16)

    n_chunks = tc // ck
    qts, intras, incs, decays = [], [], [], []
    for c in range(n_chunks):
        rows = slice(c * ck, (c + 1) * ck)
        g = g_all[rows]
        q = q_ref[rows, :].astype(F32) * (GLA_DK ** -0.5)
        k = k_ref[rows, :].astype(F32)
        v = v_ref[rows, :]
        g_last = g[ck - 1:ck, :]
        qt = (q * jnp.exp(g)).astype(BF16)
        kt = (k * jnp.exp(-g)).astype(BF16)
        kh = (k * jnp.exp(g_last - g)).astype(BF16)

        u = lax.dot_general(v, kh, (((0,), (0,)), ((), ())), preferred_element_type=F32)
        incs.append(jnp.where(st_blockdiag, u, 0.0))
        decays.append(jnp.exp(g_last))

        a = lax.dot_general(stack_heads(qt), kt, (((1,), (1,)), ((), ())), preferred_element_type=F32)
        intras.append(intra_output(jnp.where(a_row >= a_col, a, 0.0).astype(BF16), v))
        qts.append(qt)

    state = st_ref[...]
    states = []
    for c in range(n_chunks):
        states.append(state.astype(BF16))
        state = state * decays[c] + incs[c]
    st_ref[...] = state

    outs = []
    for c in range(n_chunks):
        o = lax.dot_general(qts[c], states[c], (((1,), (1,)), ((), ())), preferred_element_type=F32)
        inter_ref[c * ck:(c + 1) * ck, :] = o
        outs.append(o + intras[c])
    write_normed(jnp.concatenate(outs, axis=0))

    @pl.when(jnp.min(g_all) < -GLA_SAFE_LOG_DECAY)
    def _():
        t_idx = lax.broadcasted_iota(jnp.int32, (ck, hk), 0)
        nt_dims = (((1,), (1,)), ((), ()))
        redo = []
        for c in range(n_chunks):
            rows = slice(c * ck, (c + 1) * ck)
            g = g_all[rows]
            q = q_ref[rows, :].astype(F32) * (GLA_DK ** -0.5)
            k = k_ref[rows, :].astype(F32)
            g_hi = g.astype(BF16)
            g_r1 = g - g_hi.astype(F32)
            g_mid = g_r1.astype(BF16)
            g_lo = (g_r1 - g_mid.astype(F32)).astype(BF16)
            g3 = jnp.concatenate([g_hi, g_mid, g_lo], axis=0)
            a = lax.dot_general(stack_heads(q.astype(BF16)), k.astype(BF16), nt_dims, preferred_element_type=F32)
            a = jnp.where(a_row == a_col, a, 0.0)
            for lvl in range(GLA_LEVELS):
                half = ck >> (lvl + 1)
                g_mid_row = jnp.dot(sel_ref[lvl], g3, preferred_element_type=F32)
                d = g - g_mid_row
                second = (t_idx // half) % 2 == 1
                qd = jnp.where(second, q * jnp.exp(jnp.minimum(d, 0.0)), 0.0).astype(BF16)
                kd = jnp.where(second, 0.0, k * jnp.exp(jnp.minimum(-d, 0.0))).astype(BF16)
                part = lax.dot_general(stack_heads(qd), kd, nt_dims, preferred_element_type=F32)
                a = a + jnp.where(a_row // (2 * half) == a_col // (2 * half), part, 0.0)
            redo.append(inter_ref[rows, :] + intra_output(a.astype(BF16), v_ref[rows, :]))
        write_normed(jnp.concatenate(redo, axis=0))


def _gla(proj, small, w_hi, w_lo, b_up, g_out, ones64, sel, batch, seq):
    n = proj.shape[0]
    tc = min(512, seq)
    nt = seq // tc
    hk = N_HEADS * GLA_DK
    full = lambda shape: pl.BlockSpec(shape, lambda b, t: (0, 0))
    return pl.pallas_call(
        functools.partial(_gla_kernel, tc=tc),
        grid=(batch, nt),
        in_specs=[
            pl.BlockSpec((tc, hk), lambda b, t: (b * nt + t, COL_AQ // hk)),
            pl.BlockSpec((tc, hk), lambda b, t: (b * nt + t, COL_AK // hk)),
            pl.BlockSpec((tc, GROUP_W), lambda b, t: (b * nt + t, COL_AV // GROUP_W)),
            pl.BlockSpec((tc, LANES), lambda b, t: (b * nt + t, 0)),
            full((LANES, hk)), full((LANES, hk)), full((1, hk)), full((1, GROUP_W)), full((GROUP_W, GROUP_W)),
            pl.BlockSpec((GLA_LEVELS, GLA_CHUNK, 3 * GLA_CHUNK), lambda b, t: (0, 0, 0)),
        ],
        out_specs=pl.BlockSpec((tc, GROUP_W), lambda b, t: (b * nt + t, 0)),
        out_shape=jax.ShapeDtypeStruct((n, GROUP_W), BF16),
        scratch_shapes=[pltpu.VMEM((GROUP_W, hk), F32), pltpu.VMEM((tc, GROUP_W), F32)],
        compiler_params=_cparams(("parallel", "arbitrary")),
        name="gla",
    )(proj, proj, proj, small, w_hi, w_lo, b_up, g_out, ones64, sel)


def _lru_kernel(x_ref, cw_ref, cb_ref, wr_ref, br_ref, wi_ref, bi_ref, lam_ref, o_ref, xprev_ref, h_ref, *, tr):
    @pl.when(pl.program_id(1) == 0)
    def _():
        xprev_ref[...] = jnp.zeros_like(xprev_ref)
        h_ref[...] = jnp.zeros_like(h_ref)

    x3 = x_ref[...].astype(F32).reshape(tr // 8, 8, GROUP_W)
    xg = jnp.concatenate([xprev_ref[...][None], x3], axis=0)
    xprev_ref[...] = x3[tr // 8 - 1]
    sub3 = lax.broadcasted_iota(jnp.int32, x3.shape, 1)
    xc3 = cb_ref[...][None] + cw_ref[CONV_W - 1:CONV_W, :][None] * x3
    for s in range(1, CONV_W):
        rot = pltpu.roll(xg, s, 1)
        shifted = jnp.where(sub3 >= s, rot[1:], rot[:-1])
        xc3 = xc3 + cw_ref[CONV_W - 1 - s:CONV_W - s, :][None] * shifted
    xc = xc3.reshape(tr, GROUP_W)

    xb = xc.astype(BF16)
    r = _sigmoid(jnp.dot(xb, wr_ref[...], preferred_element_type=F32) + br_ref[...])
    i = _sigmoid(jnp.dot(xb, wi_ref[...], preferred_element_type=F32) + bi_ref[...])
    lam = lam_ref[...]
    softplus = jnp.maximum(-lam, 0.0) + jnp.log1p(jnp.exp(-jnp.abs(lam)))
    log_a = (-LRU_C) * r * softplus
    a = jnp.exp(log_a)
    bx = jnp.sqrt(1.0 - jnp.exp(2.0 * log_a)) * (i * xc)

    a3 = a.reshape(tr // 8, 8, GROUP_W)
    b3 = bx.reshape(tr // 8, 8, GROUP_W)
    sub = lax.broadcasted_iota(jnp.int32, a3.shape, 1)
    for s in (1, 2, 4):
        keep = sub >= s
        a_prev = jnp.where(keep, pltpu.roll(a3, s, 1), 1.0)
        b_prev = jnp.where(keep, pltpu.roll(b3, s, 1), 0.0)
        b3 = a3 * b_prev + b3
        a3 = a3 * a_prev
    h_prev = h_ref[0:1, :]
    for g in range(tr // 8):
        hg = a3[g] * h_prev + b3[g]
        o_ref[g * 8:(g + 1) * 8, :] = hg.astype(BF16)
        h_prev = hg[7:8, :]
    h_ref[0:1, :] = h_prev


def _lru(proj, conv_w, conv_b, wr, br, wi, bi, lam, batch, seq):
    n = proj.shape[0]
    tr = min(1024, seq)
    nt = seq // tr
    full = lambda shape: pl.BlockSpec(shape, lambda b, t: (0, 0))
    return pl.pallas_call(
        functools.partial(_lru_kernel, tr=tr),
        grid=(batch, nt),
        in_specs=[
            pl.BlockSpec((tr, GROUP_W), lambda b, t: (b * nt + t, COL_DX // GROUP_W)),
            full((8, GROUP_W)), full((1, GROUP_W)),
            full((GROUP_W, GROUP_W)), full((1, GROUP_W)),
            full((GROUP_W, GROUP_W)), full((1, GROUP_W)), full((1, GROUP_W)),
        ],
        out_specs=pl.BlockSpec((tr, GROUP_W), lambda b, t: (b * nt + t, 0)),
        out_shape=jax.ShapeDtypeStruct((n, GROUP_W), BF16),
        scratch_shapes=[pltpu.VMEM((8, GROUP_W), F32), pltpu.VMEM((8, GROUP_W), F32)],
        compiler_params=_cparams(("parallel", "arbitrary")),
        name="rglru",
    )(proj, conv_w, conv_b, wr, br, wi, bi, lam)


def _pad_cols(w, width):
    return jnp.pad(w, [(0, 0)] * (w.ndim - 1) + [(0, width - w.shape[-1])])


def _block_diag_ones(n_blocks, size):
    return jnp.asarray(np.kron(np.eye(n_blocks, dtype=np.float32), np.ones((size, size), np.float32)), BF16)


def _swap_rotary(w):
    lead = w.shape[:-1]
    w = w.reshape(lead + (N_HEADS, MLA_NOPE + MLA_ROPE))
    half = MLA_ROPE // 2
    z1, z2 = w[..., MLA_NOPE:MLA_NOPE + half], w[..., MLA_NOPE + half:]
    w = jnp.concatenate([jnp.zeros_like(w[..., :MLA_NOPE]), z2, z1], axis=-1)
    return w.reshape(lead + (N_HEADS * (MLA_NOPE + MLA_ROPE),))


def _pad_heads(w, d_real):
    lead = w.shape[:-1]
    w = w.reshape(lead + (N_HEADS, d_real))
    w = jnp.pad(w, [(0, 0)] * len(lead) + [(0, 0), (0, HEAD_PAD - d_real)])
    return w.reshape(lead + (N_HEADS * HEAD_PAD,))


def kernel(x, positions, norm_g, w_in, w_out, gla_w_up, gla_b_up, gla_g_out, fox_b_f, fox_g_q, fox_g_k, mla_g_cq, mla_w_uq, mla_g_ckv, mla_w_ukv, mla_g_q, mla_g_k, lru_conv_w, lru_conv_b, lru_w_r, lru_b_r, lru_w_i, lru_b_i, lru_lam):
    batch, seq, _ = x.shape
    depth = w_in.shape[0]
    n = batch * seq

    sizes = (D_MIX, 256, 256, 512, GLA_RANK, 512, 512, 512, N_HEADS, MLA_Q_RANK, MLA_KV_RANK, MLA_ROPE, GROUP_W)
    offs = np.concatenate([[0], np.cumsum(sizes)])
    seg = [w_in[:, :, offs[i]:offs[i + 1]] for i in range(len(sizes))]
    (w_gate, w_aq, w_ak, w_av, w_ag, w_bq, w_bk, w_bv, w_bf, w_cq, w_ckv, w_ckr, w_dx) = seg
    w_main = jnp.concatenate(
        [w_gate, w_aq, w_ak, w_av, w_bq, w_bk, w_bv, w_cq, w_ckv, _pad_cols(w_ckr, LANES), w_dx], axis=-1).astype(BF16)
    w_small = _pad_cols(jnp.concatenate([w_ag, w_bf], axis=-1), LANES).astype(BF16)
    w_out_b = w_out.astype(BF16)

    ones64 = _block_diag_ones(GROUP_W // HEAD_DIM, HEAD_DIM)
    ones128 = _block_diag_ones(2, HEAD_PAD)

    wup = jnp.pad(gla_w_up, ((0, 0), (0, LANES - GLA_RANK), (0, 0)))
    wup_hi = wup.astype(BF16)
    wup_lo = (wup - wup_hi.astype(F32)).astype(BF16)

    sel = np.zeros((GLA_LEVELS, GLA_CHUNK, 3 * GLA_CHUNK), np.float32)
    for lvl in range(GLA_LEVELS):
        half = GLA_CHUNK >> (lvl + 1)
        for t in range(GLA_CHUNK):
            mid = (t // (2 * half)) * 2 * half + half - 1
            for piece in range(3):
                sel[lvl, t, piece * GLA_CHUNK + mid] = 1.0
    gla_sel = jnp.asarray(sel, BF16)

    fox_scale = LOG2E
    mla_scale = (MLA_NOPE + MLA_ROPE) ** -0.5 * LOG2E
    fox_gq = jnp.tile(fox_g_q, (1, N_HEADS)) * fox_scale
    fox_gk = jnp.tile(fox_g_k, (1, N_HEADS)) * HEAD_DIM ** 0.5
    place = np.zeros((LANES, N_HEADS * HEAD_PAD), np.float32)
    aug_q = np.zeros((1, N_HEADS * HEAD_PAD), np.float32)
    for h in range(N_HEADS):
        base = h * HEAD_PAD + (HEAD_DIM if h % 2 == 0 else 0)
        for piece in range(3):
            place[(piece + 1) * SM_BF + h, base + piece] = 1.0
            aug_q[0, base + piece] = 1.0
    place, aug_q = jnp.asarray(place, BF16), jnp.asarray(aug_q)
    fox_bf = jnp.pad(fox_b_f, ((0, 0), (SM_BF, LANES - SM_BF - N_HEADS)))

    wuq = _pad_heads(mla_w_uq, MLA_NOPE + MLA_ROPE).astype(BF16)
    wuq_sw = _pad_heads(_swap_rotary(mla_w_uq), MLA_NOPE + MLA_ROPE).astype(BF16)
    wukv = mla_w_ukv.reshape(depth, MLA_KV_RANK, N_HEADS, 2 * HEAD_DIM)
    wuk = _pad_heads(wukv[..., :MLA_NOPE].reshape(depth, MLA_KV_RANK, N_HEADS * MLA_NOPE), MLA_NOPE).astype(BF16)
    pad_rows = lambda m: jnp.pad(m.reshape(m.shape[:-2] + (N_HEADS, HEAD_DIM, m.shape[-1])),
                                 [(0, 0)] * (m.ndim - 1) + [(0, VT_ROWS - HEAD_DIM), (0, 0)]
                                 ).reshape(m.shape[:-2] + (N_HEADS * VT_ROWS, m.shape[-1]))
    wuv_t = pad_rows(jnp.swapaxes(wukv[..., MLA_NOPE:].reshape(depth, MLA_KV_RANK, GROUP_W), 1, 2)).astype(BF16)
    eye = pad_rows(jnp.eye(GROUP_W, dtype=F32)).astype(BF16)
    mla_gq_all = jnp.tile(mla_g_q, (1, N_HEADS)) * mla_scale
    mla_gk_all = jnp.tile(mla_g_k, (1, N_HEADS))
    mla_gq = _pad_heads(mla_gq_all, MLA_NOPE + MLA_ROPE)
    mla_gk = _pad_heads(mla_gk_all, MLA_NOPE + MLA_ROPE)
    mla_gq_sw = _pad_heads(_swap_rotary(mla_gq_all), MLA_NOPE + MLA_ROPE)
    mla_gk_sw = _pad_heads(_swap_rotary(mla_gk_all), MLA_NOPE + MLA_ROPE)

    eye_h = jnp.eye(N_HEADS, dtype=F32)
    wr_bd = jnp.einsum("lncd,nm->lncmd", lru_w_r, eye_h).reshape(depth, GROUP_W, GROUP_W).astype(BF16)
    wi_bd = jnp.einsum("lncd,nm->lncmd", lru_w_i, eye_h).reshape(depth, GROUP_W, GROUP_W).astype(BF16)
    conv_w = jnp.pad(lru_conv_w, ((0, 0), (0, 8 - CONV_W), (0, 0)))

    half = MLA_ROPE // 2
    inv_freq = ROPE_THETA ** (-jnp.arange(half, dtype=F32) / half)
    invf_row = jnp.pad(jnp.concatenate([inv_freq, inv_freq]), (MLA_NOPE, LANES - MLA_NOPE - MLA_ROPE))[None, :]
    cos_t, sin_t = _rope_tables(positions.reshape(n, 1), invf_row)

    xf = x.reshape(n, D_MODEL)
    for l in range(depth):
        proj, small = _norm_inproj(xf, norm_g[l][None, :], w_main, w_small, l)
        ya = _gla(proj, small, wup_hi[l], wup_lo[l], gla_b_up[l][None, :],
                  jnp.tile(gla_g_out[l], N_HEADS)[None, :], ones64, gla_sel, batch, seq)
        fq, fk, fvt = _fox_prep(proj, small, fox_bf[l][None, :], fox_gq[l][None, :], fox_gk[l][None, :], ones64, eye,
                                place, aug_q, batch, seq)
        yb = _attention(fq, fk, fvt, batch, seq)
        mq, mk, mvt = _mla_prep(proj, cos_t, sin_t, mla_g_cq[l][None, :], wuq[l], wuq_sw[l], mla_g_ckv[l][None, :],
                                wuk[l], wuv_t[l], mla_gq[l][None, :], mla_gq_sw[l][None, :], mla_gk[l][None, :],
                                mla_gk_sw[l][None, :], ones128, batch, seq)
        yc = _attention(mq, mk, mvt, batch, seq)
        yd = _lru(proj, conv_w[l], lru_conv_b[l][None, :], wr_bd[l], lru_b_r[l][None, :], wi_bd[l],
                  lru_b_i[l][None, :], lru_lam[l][None, :], batch, seq)
        xf = _gate_outproj(ya, yb, yc, yd, proj, w_out_b, xf, l)
    return xf.reshape(batch, seq, D_MODEL)
```

```python
import functools
import math

import numpy as np
import jax
import jax.numpy as jnp
from jax import lax
from jax.experimental import pallas as pl
from jax.experimental.pallas import tpu as pltpu

F32 = jnp.float32
BF16 = jnp.bfloat16

D_MODEL = 1024
D_MIX = 2048
GROUP_W = 512
N_HEADS = 8
HEAD_DIM = 64
EPS = 1e-6
GLA_DK = 32
GLA_RANK = 16
GLA_NORMALIZER = 16.0
GLA_CHUNK = 64
GLA_LEVELS = 6
GLA_SAFE_LOG_DECAY = 40.0
MLA_Q_RANK = 256
MLA_KV_RANK = 128
MLA_NOPE = 64
MLA_ROPE = 32
ROPE_THETA = 10000.0
CONV_W = 4
LRU_C = 8.0
LOG2E = 1.4426950408889634
NEG_BIG = -1e30

LANES = 128
HEAD_PAD = 128
VT_ROWS = 80

COL_GATE = 0
COL_AQ, COL_AK, COL_AV = 2048, 2304, 2560
COL_BQ, COL_BK, COL_BV = 3072, 3584, 4096
COL_CQ, COL_CKV, COL_CKR = 4608, 4864, 4992
COL_DX = 5120
D_PROJ = 5632
SM_AG, SM_BF = 32, 48
PIECE_STRIDE = 16


def _cparams(sem, vmem_mb=48):
    return pltpu.CompilerParams(dimension_semantics=sem, vmem_limit_bytes=vmem_mb * 1024 * 1024)


def _sigmoid(z):
    return 0.5 * jnp.tanh(0.5 * z) + 0.5


def _with_ones_rows(vt):
    row = lax.broadcasted_iota(jnp.int32, vt.shape, 0) % VT_ROWS
    return jnp.where(row == HEAD_DIM, 1.0, vt)


def _log_sigmoid(z):
    return jnp.minimum(z, 0.0) - jnp.log1p(jnp.exp(-jnp.abs(z)))


def _row_prefix_sum(v, seg):
    rows = v.shape[0]
    rid = lax.broadcasted_iota(jnp.int32, v.shape, 0) % seg
    s = 1
    while s < seg:
        v = v + jnp.where(rid >= s, pltpu.roll(v, s, 0), 0.0)
        s *= 2
    del rows
    return v


def _inproj_kernel(x_ref, g_ref, w_ref, o_ref, os_ref, h_ref, *, small_col):
    @pl.when(pl.program_id(1) == 0)
    def _():
        x = x_ref[...]
        ms = jnp.mean(x * x, axis=-1, keepdims=True)
        h_ref[...] = (x * lax.rsqrt(ms + EPS) * g_ref[...]).astype(BF16)

    res = jnp.dot(h_ref[...], w_ref[...], preferred_element_type=F32)
    o_ref[...] = res.astype(BF16)

    @pl.when(pl.program_id(1) == pl.num_programs(1) - 1)
    def _():
        os_ref[...] = res[:, small_col:small_col + LANES]


def _norm_inproj(x, g, w_all, layer):
    n = x.shape[0]
    tm = min(1024, n)
    tn = D_PROJ // 2
    assert COL_CKR >= D_PROJ - tn
    return pl.pallas_call(
        functools.partial(_inproj_kernel, small_col=COL_CKR - (D_PROJ - tn)),
        grid=(n // tm, D_PROJ // tn),
        in_specs=[
            pl.BlockSpec((tm, D_MODEL), lambda i, j: (i, 0)),
            pl.BlockSpec((1, D_MODEL), lambda i, j: (0, 0)),
            pl.BlockSpec((None, D_MODEL, tn), lambda i, j: (layer, 0, j)),
        ],
        out_specs=[
            pl.BlockSpec((tm, tn), lambda i, j: (i, j)),
            pl.BlockSpec((tm, LANES), lambda i, j: (i, 0)),
        ],
        out_shape=[jax.ShapeDtypeStruct((n, D_PROJ), BF16), jax.ShapeDtypeStruct((n, LANES), F32)],
        scratch_shapes=[pltpu.VMEM((tm, D_MODEL), BF16)],
        compiler_params=_cparams(("parallel", "arbitrary"), vmem_mb=56),
        name="norm_inproj",
    )(x, g, w_all)


def _outproj_kernel(ya_ref, yb_ref, yc_ref, yd_ref, gate_ref, w_ref, x_ref, o_ref, z_ref):
    for gi, y_ref in enumerate((ya_ref, yb_ref, yc_ref, yd_ref)):
        lo, hi = gi * GROUP_W, (gi + 1) * GROUP_W
        g = gate_ref[:, lo:hi].astype(F32)
        z_ref[:, lo:hi] = (y_ref[...].astype(F32) * (g * _sigmoid(g))).astype(BF16)
    o_ref[...] = x_ref[...] + jnp.dot(z_ref[...], w_ref[...], preferred_element_type=F32)


def _gate_outproj(ya, yb, yc, yd, proj, w_all, x, layer):
    n = x.shape[0]
    tm = min(1024, n)
    yspec = pl.BlockSpec((tm, GROUP_W), lambda i: (i, 0))
    return pl.pallas_call(
        _outproj_kernel,
        grid=(n // tm,),
        in_specs=[
            yspec, yspec, yspec, yspec,
            pl.BlockSpec((tm, D_MIX), lambda i: (i, 0)),
            pl.BlockSpec((None, D_MIX, D_MODEL), lambda i: (layer, 0, 0)),
            pl.BlockSpec((tm, D_MODEL), lambda i: (i, 0)),
        ],
        out_specs=pl.BlockSpec((tm, D_MODEL), lambda i: (i, 0)),
        out_shape=jax.ShapeDtypeStruct((n, D_MODEL), F32),
        scratch_shapes=[pltpu.VMEM((tm, D_MIX), BF16)],
        compiler_params=_cparams(("parallel",)),
        name="gate_outproj",
    )(ya, yb, yc, yd, proj, w_all, x)


def _attn_kernel(q_ref, k_ref, vt_ref, o_ref, m_ref, acc_ref, st_ref, mx_ref, *, tq, nq, hg):
    i = pl.program_id(2)
    half = tq // 2
    nt_dims = (((1,), (1,)), ((), ()))

    m_ref[...] = jnp.full(m_ref.shape, NEG_BIG, F32)
    acc_ref[...] = jnp.zeros(acc_ref.shape, F32)

    def scores(h, krows, qrows):
        lanes = slice(h * HEAD_PAD, (h + 1) * HEAD_PAD)
        return lax.dot_general(k_ref[krows, lanes], q_ref[qrows, lanes], nt_dims, preferred_element_type=F32)

    def vrows(h):
        return slice(h * VT_ROWS, (h + 1) * VT_ROWS)

    tri = lax.broadcasted_iota(jnp.int32, (half, half), 0) <= lax.broadcasted_iota(jnp.int32, (half, half), 1)
    low = lax.broadcasted_iota(jnp.int32, (tq, half), 0) <= lax.broadcasted_iota(jnp.int32, (tq, half), 1) + half

    def issue_scores(item, c):
        h, blk = item
        if blk is None:
            off = c * tq
            s_l = jnp.where(tri, scores(h, slice(off, off + half), slice(0, half)), NEG_BIG)
            s_r = jnp.where(low, scores(h, slice(off, off + tq), slice(half, tq)), NEG_BIG)
            mx = jnp.concatenate([jnp.max(s_l, axis=0, keepdims=True), jnp.max(s_r, axis=0, keepdims=True)], axis=1)
            return (s_l, s_r, mx)
        st = scores(h, slice(blk * tq, (blk + 1) * tq), slice(0, tq))
        return (st, jnp.max(st, axis=0, keepdims=True))

    def store_scores(item, slot, vals):
        if item[1] is None:
            st_ref[slot, 0:half, 0:half] = vals[0]
            st_ref[slot, :, half:tq] = vals[1]
        else:
            st_ref[slot] = vals[0]
        mx_ref[slot] = vals[-1]

    def consume(item, slot, c):
        h, blk = item
        m = m_ref[h]
        m_new = jnp.maximum(m, mx_ref[slot])
        alpha = jnp.exp2(m - m_new)
        m_ref[h] = m_new

        def accumulate(cols, st, vt):
            p = jnp.exp2(st - m_new[:, cols]).astype(BF16)
            acc_ref[h, :, cols] = alpha[:, cols] * acc_ref[h, :, cols] + jnp.dot(vt, p, preferred_element_type=F32)

        if blk is None:
            off = c * tq
            accumulate(slice(0, half), st_ref[slot, 0:half, 0:half], vt_ref[vrows(h), off:off + half])
            accumulate(slice(half, tq), st_ref[slot, :, half:tq], vt_ref[vrows(h), off:off + tq])
        else:
            accumulate(slice(0, tq), st_ref[slot], vt_ref[vrows(h), blk * tq:(blk + 1) * tq])

    for c in range(nq):
        @pl.when(i == c)
        def _(c=c):
            items = [(h, None) for h in range(hg)] + [(h, blk) for blk in range(c) for h in range(hg)]
            store_scores(items[0], 0, issue_scores(items[0], c))
            for n, item in enumerate(items):
                nxt = items[n + 1] if n + 1 < len(items) else None
                if nxt is not None:
                    vals = issue_scores(nxt, c)
                consume(item, n % 2, c)
                if nxt is not None:
                    store_scores(nxt, (n + 1) % 2, vals)

    ot = jnp.concatenate([acc_ref[h, 0:HEAD_DIM, :] / acc_ref[h, HEAD_DIM:HEAD_DIM + 1, :] for h in range(hg)], axis=0)
    o_ref[...] = ot.T.astype(BF16)


ATTN_HEADS_PER_STEP = 8


def _attention(q, k, vt, batch, seq):
    n = q.shape[0]
    tq = min(512, seq)
    nq = seq // tq
    hg = ATTN_HEADS_PER_STEP
    return pl.pallas_call(
        functools.partial(_attn_kernel, tq=tq, nq=nq, hg=hg),
        grid=(batch, N_HEADS // hg, nq),
        in_specs=[
            pl.BlockSpec((tq, hg * HEAD_PAD), lambda b, p, i: (b * nq + i, p)),
            pl.BlockSpec((seq, hg * HEAD_PAD), lambda b, p, i: (b, p)),
            pl.BlockSpec((None, hg * VT_ROWS, seq), lambda b, p, i: (b, p, 0)),
        ],
        out_specs=pl.BlockSpec((tq, hg * HEAD_DIM), lambda b, p, i: (b * nq + i, p)),
        out_shape=jax.ShapeDtypeStruct((n, GROUP_W), BF16),
        scratch_shapes=[pltpu.VMEM((hg, 1, tq), F32), pltpu.VMEM((hg, VT_ROWS, tq), F32),
                        pltpu.VMEM((2, tq, tq), F32), pltpu.VMEM((2, 1, tq), F32)],
        compiler_params=_cparams(("parallel", "parallel", "arbitrary")),
        name="causal_attention",
    )(q, k, vt)


def _fox_prep_kernel(q_ref, k_ref, v_ref, sm_ref, bf_ref, gq_ref, gk_ref, ones_ref, eye_ref, place_ref, augq_ref,
                     qo_ref, ko_ref, vt_ref, carry_ref, *, tt):
    @pl.when(pl.program_id(1) == 0)
    def _():
        carry_ref[...] = jnp.zeros_like(carry_ref)

    vt = lax.dot_general(eye_ref[...], v_ref[...], (((1,), (1,)), ((), ())), preferred_element_type=F32)
    vt_ref[...] = _with_ones_rows(vt).astype(BF16)

    def headnorm(ref, g_ref):
        z = ref[...].astype(F32)
        ssq = jnp.dot((z * z).astype(BF16), ones_ref[...], preferred_element_type=F32)
        return z * lax.rsqrt(ssq + HEAD_DIM * EPS) * g_ref[...]

    qn = headnorm(q_ref, gq_ref)
    kn = headnorm(k_ref, gk_ref)

    log_f = _log_sigmoid(sm_ref[...] + bf_ref[...])
    cum = _row_prefix_sum(log_f, tt) + carry_ref[0:1, :]
    carry_ref[0:1, :] = cum[tt - 1:tt, :]
    neg = cum * (-LOG2E)

    lane = lax.broadcasted_iota(jnp.int32, (tt, LANES), 1)
    hi = neg.astype(BF16).astype(F32)
    r1 = neg - hi
    mid = r1.astype(BF16).astype(F32)
    lo = r1 - mid
    pieces = jnp.where(lane < SM_BF + PIECE_STRIDE, hi,
                       jnp.where(lane < SM_BF + 2 * PIECE_STRIDE, pltpu.roll(mid, PIECE_STRIDE, 1),
                                 pltpu.roll(lo, 2 * PIECE_STRIDE, 1)))
    aug_k = jnp.dot(pieces.astype(BF16), place_ref[...], preferred_element_type=F32)
    aug_q = augq_ref[...]
    for h in range(N_HEADS):
        pair, e = h // 2, h % 2
        in_head = (lane >= e * HEAD_DIM) & (lane < (e + 1) * HEAD_DIM)
        blk = slice(h * HEAD_PAD, (h + 1) * HEAD_PAD)
        kpair = kn[:, pair * LANES:(pair + 1) * LANES]
        qpair = qn[:, pair * LANES:(pair + 1) * LANES]
        ko_ref[:, blk] = jnp.where(in_head, kpair, aug_k[:, blk]).astype(BF16)
        qo_ref[:, blk] = jnp.where(in_head, qpair, aug_q[:, blk]).astype(BF16)


def _fox_prep(proj, small, bf_row, gq, gk, ones64, eye, place, aug_q, batch, seq):
    n = proj.shape[0]
    tt = min(1024, seq)
    nt = seq // tt
    row = lambda shape: pl.BlockSpec(shape, lambda b, t: (0, 0))
    return pl.pallas_call(
        functools.partial(_fox_prep_kernel, tt=tt),
        grid=(batch, nt),
        in_specs=[
            pl.BlockSpec((tt, GROUP_W), lambda b, t: (b * nt + t, COL_BQ // GROUP_W)),
            pl.BlockSpec((tt, GROUP_W), lambda b, t: (b * nt + t, COL_BK // GROUP_W)),
            pl.BlockSpec((tt, GROUP_W), lambda b, t: (b * nt + t, COL_BV // GROUP_W)),
            pl.BlockSpec((tt, LANES), lambda b, t: (b * nt + t, 0)),
            row((1, LANES)), row((1, GROUP_W)), row((1, GROUP_W)), row((GROUP_W, GROUP_W)),
            row((N_HEADS * VT_ROWS, GROUP_W)), row((LANES, N_HEADS * HEAD_PAD)), row((1, N_HEADS * HEAD_PAD)),
        ],
        out_specs=[
            pl.BlockSpec((tt, N_HEADS * HEAD_PAD), lambda b, t: (b * nt + t, 0)),
            pl.BlockSpec((tt, N_HEADS * HEAD_PAD), lambda b, t: (b * nt + t, 0)),
            pl.BlockSpec((None, N_HEADS * VT_ROWS, tt), lambda b, t: (b, 0, t)),
        ],
        out_shape=[jax.ShapeDtypeStruct((n, N_HEADS * HEAD_PAD), BF16)] * 2
        + [jax.ShapeDtypeStruct((batch, N_HEADS * VT_ROWS, seq), BF16)],
        scratch_shapes=[pltpu.VMEM((8, LANES), F32)],
        compiler_params=_cparams(("parallel", "arbitrary")),
        name="fox_prep",
    )(proj, proj, proj, small, bf_row, gq, gk, ones64, eye, place, aug_q)


def _rope_kernel(pos_ref, invf_ref, c_ref, s_ref):
    ang = pos_ref[...].astype(F32) * invf_ref[...]
    lane = lax.broadcasted_iota(jnp.int32, ang.shape, 1)
    half = MLA_ROPE // 2
    c, s = jnp.cos(ang), jnp.sin(ang)
    c_ref[...] = jnp.where((lane >= MLA_NOPE) & (lane < MLA_NOPE + MLA_ROPE), c, 1.0)
    s_ref[...] = jnp.where((lane >= MLA_NOPE) & (lane < MLA_NOPE + half), -s,
                           jnp.where((lane >= MLA_NOPE + half) & (lane < MLA_NOPE + MLA_ROPE), s, 0.0))


def _rope_tables(pos_col, invf_row):
    n = pos_col.shape[0]
    tm = min(1024, n)
    out = pl.BlockSpec((tm, LANES), lambda i: (i, 0))
    return pl.pallas_call(
        _rope_kernel,
        grid=(n // tm,),
        in_specs=[pl.BlockSpec((tm, 1), lambda i: (i, 0)), pl.BlockSpec((1, LANES), lambda i: (0, 0))],
        out_specs=[out, out],
        out_shape=[jax.ShapeDtypeStruct((n, LANES), F32)] * 2,
        compiler_params=_cparams(("parallel",)),
        name="rope_tables",
    )(pos_col, invf_row)


def _mla_prep_kernel(cq_ref, ckv_ref, ckr_ref, c_ref, s_ref, gcq_ref, wuq_ref, wuqs_ref, gckv_ref, wuk_ref, wuv_ref,
                     gq_ref, gqs_ref, gk_ref, gks_ref, ones_ref, qo_ref, ko_ref, vo_ref):
    d_qk = float(MLA_NOPE + MLA_ROPE)
    cos, sin = c_ref[...], s_ref[...]
    half = MLA_ROPE // 2

    def rms(z, g):
        return (z * lax.rsqrt(jnp.mean(z * z, axis=-1, keepdims=True) + EPS) * g).astype(BF16)

    def head_ssq(z):
        zz = (z * z).astype(BF16)
        parts = [jnp.dot(zz[:, c * 256:(c + 1) * 256], ones_ref[...], preferred_element_type=F32) for c in range(4)]
        return jnp.concatenate(parts, axis=1)

    cn = rms(cq_ref[...].astype(F32), gcq_ref[...])
    q = jnp.dot(cn, wuq_ref[...], preferred_element_type=F32)
    q_sw = jnp.dot(cn, wuqs_ref[...], preferred_element_type=F32)
    q_inv = lax.rsqrt(head_ssq(q) * (1.0 / d_qk) + EPS)
    qn = q * (q_inv * gq_ref[...])
    qn_sw = q_sw * (q_inv * gqs_ref[...])

    kvn = rms(ckv_ref[...].astype(F32), gckv_ref[...])
    kn = jnp.dot(kvn, wuk_ref[...], preferred_element_type=F32)
    vt = lax.dot_general(wuv_ref[...], kvn, (((1,), (1,)), ((), ())), preferred_element_type=F32)
    vo_ref[...] = _with_ones_rows(vt).astype(BF16)
    lane = lax.broadcasted_iota(jnp.int32, ckr_ref.shape, 1)
    kr = jnp.where(lane < MLA_ROPE, ckr_ref[...].astype(F32), 0.0)
    ss_r = jnp.sum(kr * kr, axis=-1, keepdims=True)
    kr_placed = pltpu.roll(kr, MLA_NOPE, 1)
    kr_swapped = jnp.where(lane < MLA_NOPE + half, pltpu.roll(kr, MLA_NOPE - half, 1), pltpu.roll(kr, MLA_NOPE + half, 1))
    kr_swapped = jnp.where((lane >= MLA_NOPE) & (lane < MLA_NOPE + MLA_ROPE), kr_swapped, 0.0)
    k_inv = lax.rsqrt((head_ssq(kn) + ss_r) * (1.0 / d_qk) + EPS)
    k_scale = k_inv * gk_ref[...]
    k_scale_sw = k_inv * gks_ref[...]

    for h in range(N_HEADS):
        sl = slice(h * HEAD_PAD, (h + 1) * HEAD_PAD)
        qo_ref[:, sl] = (qn[:, sl] * cos + qn_sw[:, sl] * sin).astype(BF16)
        k_h = (kn[:, sl] + kr_placed) * k_scale[:, sl]
        ko_ref[:, sl] = (k_h * cos + (kr_swapped * k_scale_sw[:, sl]) * sin).astype(BF16)


def _mla_prep(proj, cos_t, sin_t, gcq, wuq, wuq_sw, gckv, wuk, wuv_t, gq, gq_sw, gk, gk_sw, ones128, batch, seq):
    n = proj.shape[0]
    tm = min(1024, seq)
    nt = seq // tm
    full = lambda shape: pl.BlockSpec(shape, lambda i: (0, 0))
    tab = pl.BlockSpec((tm, LANES), lambda i: (i, 0))
    wide = pl.BlockSpec((tm, N_HEADS * HEAD_PAD), lambda i: (i, 0))
    return pl.pallas_call(
        _mla_prep_kernel,
        grid=(n // tm,),
        in_specs=[
            pl.BlockSpec((tm, MLA_Q_RANK), lambda i: (i, COL_CQ // MLA_Q_RANK)),
            pl.BlockSpec((tm, MLA_KV_RANK), lambda i: (i, COL_CKV // LANES)),
            pl.BlockSpec((tm, LANES), lambda i: (i, COL_CKR // LANES)),
            tab, tab,
            full((1, MLA_Q_RANK)), full((MLA_Q_RANK, N_HEADS * HEAD_PAD)), full((MLA_Q_RANK, N_HEADS * HEAD_PAD)),
            full((1, MLA_KV_RANK)), full((MLA_KV_RANK, N_HEADS * HEAD_PAD)), full((N_HEADS * VT_ROWS, MLA_KV_RANK)),
            full((1, N_HEADS * HEAD_PAD)), full((1, N_HEADS * HEAD_PAD)),
            full((1, N_HEADS * HEAD_PAD)), full((1, N_HEADS * HEAD_PAD)), full((256, 256)),
        ],
        out_specs=[wide, wide, pl.BlockSpec((None, N_HEADS * VT_ROWS, tm), lambda i: (i // nt, 0, i % nt))],
        out_shape=[jax.ShapeDtypeStruct((n, N_HEADS * HEAD_PAD), BF16)] * 2
        + [jax.ShapeDtypeStruct((batch, N_HEADS * VT_ROWS, seq), BF16)],
        compiler_params=_cparams(("parallel",)),
        name="mla_prep",
    )(proj, proj, proj, cos_t, sin_t, gcq, wuq, wuq_sw, gckv, wuk, wuv_t, gq, gq_sw, gk, gk_sw, ones128)


def _gla_kernel(q_ref, k_ref, v_ref, sm_ref, wh_ref, wl_ref, b_ref, go_ref, ones_ref, sel_ref, o_ref, st_ref,
                inter_ref, *, tc):
    hk = N_HEADS * GLA_DK
    ck = GLA_CHUNK

    @pl.when(pl.program_id(1) == 0)
    def _():
        st_ref[...] = jnp.zeros_like(st_ref)

    sm = sm_ref[...]
    sm_hi = sm.astype(BF16)
    sm_lo = (sm - sm_hi.astype(F32)).astype(BF16)
    z = (jnp.dot(sm_hi, wh_ref[...], preferred_element_type=F32)
         + jnp.dot(sm_lo, wh_ref[...], preferred_element_type=F32)
         + jnp.dot(sm_hi, wl_ref[...], preferred_element_type=F32)) + b_ref[...]
    g_all = _row_prefix_sum(_log_sigmoid(z) * (1.0 / GLA_NORMALIZER), ck)

    lane_k = lax.broadcasted_iota(jnp.int32, (ck, hk), 1) // GLA_DK
    lane_pair = lax.broadcasted_iota(jnp.int32, (ck, LANES), 1)
    a_row = lax.broadcasted_iota(jnp.int32, (N_HEADS * ck, ck), 0) % ck
    a_col = lax.broadcasted_iota(jnp.int32, (N_HEADS * ck, ck), 1)
    st_blockdiag = (lax.broadcasted_iota(jnp.int32, (GROUP_W, hk), 0) // HEAD_DIM
                    == lax.broadcasted_iota(jnp.int32, (GROUP_W, hk), 1) // GLA_DK)

    def stack_heads(x):
        return jnp.concatenate([jnp.where(lane_k == h, x, jnp.zeros_like(x)) for h in range(N_HEADS)], axis=0)

    def intra_output(a, v):
        parts = []
        for p in range(N_HEADS // 2):
            r = jnp.dot(a[2 * p * ck:2 * (p + 1) * ck, :], v[:, p * LANES:(p + 1) * LANES], preferred_element_type=F32)
            parts.append(jnp.where(lane_pair < HEAD_DIM, r[0:ck, :], r[ck:2 * ck, :]))
        return jnp.concatenate(parts, axis=1)

    def write_normed(o):
        ssq = jnp.dot((o * o).astype(BF16), ones_ref[...], preferred_element_type=F32)
        o_ref[...] = (o * lax.rsqrt(ssq * (1.0 / HEAD_DIM) + EPS) * go_ref[...]).astype(BF16)

    n_chunks = tc // ck
    qts, intras, incs, decays = [], [], [], []
    for c in range(n_chunks):
        rows = slice(c * ck, (c + 1) * ck)
        g = g_all[rows]
        q = q_ref[rows, :].astype(F32) * (GLA_DK ** -0.5)
        k = k_ref[rows, :].astype(F32)
        v = v_ref[rows, :]
        g_last = g[ck - 1:ck, :]
        qt = (q * jnp.exp(g)).astype(BF16)
        kt = (k * jnp.exp(-g)).astype(BF16)
        kh = (k * jnp.exp(g_last - g)).astype(BF16)

        u = lax.dot_general(v, kh, (((0,), (0,)), ((), ())), preferred_element_type=F32)
        incs.append(jnp.where(st_blockdiag, u, 0.0))
        decays.append(jnp.exp(g_last))

        a = lax.dot_general(stack_heads(qt), kt, (((1,), (1,)), ((), ())), preferred_element_type=F32)
        intras.append(intra_output(jnp.where(a_row >= a_col, a, 0.0).astype(BF16), v))
        qts.append(qt)

    state = st_ref[...]
    states = []
    for c in range(n_chunks):
        states.append(state.astype(BF16))
        state = state * decays[c] + incs[c]
    st_ref[...] = state

    outs = []
    for c in range(n_chunks):
        o = lax.dot_general(qts[c], states[c], (((1,), (1,)), ((), ())), preferred_element_type=F32)
        inter_ref[c * ck:(c + 1) * ck, :] = o
        outs.append(o + intras[c])
    write_normed(jnp.concatenate(outs, axis=0))

    @pl.when(jnp.min(g_all) < -GLA_SAFE_LOG_DECAY)
    def _():
        t_idx = lax.broadcasted_iota(jnp.int32, (ck, hk), 0)
        nt_dims = (((1,), (1,)), ((), ()))
        redo = []
        for c in range(n_chunks):
            rows = slice(c * ck, (c + 1) * ck)
            g = g_all[rows]
            q = q_ref[rows, :].astype(F32) * (GLA_DK ** -0.5)
            k = k_ref[rows, :].astype(F32)
            g_hi = g.astype(BF16)
            g_r1 = g - g_hi.astype(F32)
            g_mid = g_r1.astype(BF16)
            g_lo = (g_r1 - g_mid.astype(F32)).astype(BF16)
            g3 = jnp.concatenate([g_hi, g_mid, g_lo], axis=0)
            a = lax.dot_general(stack_heads(q.astype(BF16)), k.astype(BF16), nt_dims, preferred_element_type=F32)
            a = jnp.where(a_row == a_col, a, 0.0)
            for lvl in range(GLA_LEVELS):
                half = ck >> (lvl + 1)
                g_mid_row = jnp.dot(sel_ref[lvl], g3, preferred_element_type=F32)
                d = g - g_mid_row
                second = (t_idx // half) % 2 == 1
                qd = jnp.where(second, q * jnp.exp(jnp.minimum(d, 0.0)), 0.0).astype(BF16)
                kd = jnp.where(second, 0.0, k * jnp.exp(jnp.minimum(-d, 0.0))).astype(BF16)
                part = lax.dot_general(stack_heads(qd), kd, nt_dims, preferred_element_type=F32)
                a = a + jnp.where(a_row // (2 * half) == a_col // (2 * half), part, 0.0)
            redo.append(inter_ref[rows, :] + intra_output(a.astype(BF16), v_ref[rows, :]))
        write_normed(jnp.concatenate(redo, axis=0))


def _gla(proj, small, w_hi, w_lo, b_up, g_out, ones64, sel, batch, seq):
    n = proj.shape[0]
    tc = min(512, seq)
    nt = seq // tc
    hk = N_HEADS * GLA_DK
    full = lambda shape: pl.BlockSpec(shape, lambda b, t: (0, 0))
    return pl.pallas_call(
        functools.partial(_gla_kernel, tc=tc),
        grid=(batch, nt),
        in_specs=[
            pl.BlockSpec((tc, hk), lambda b, t: (b * nt + t, COL_AQ // hk)),
            pl.BlockSpec((tc, hk), lambda b, t: (b * nt + t, COL_AK // hk)),
            pl.BlockSpec((tc, GROUP_W), lambda b, t: (b * nt + t, COL_AV // GROUP_W)),
            pl.BlockSpec((tc, LANES), lambda b, t: (b * nt + t, 0)),
            full((LANES, hk)), full((LANES, hk)), full((1, hk)), full((1, GROUP_W)), full((GROUP_W, GROUP_W)),
            pl.BlockSpec((GLA_LEVELS, GLA_CHUNK, 3 * GLA_CHUNK), lambda b, t: (0, 0, 0)),
        ],
        out_specs=pl.BlockSpec((tc, GROUP_W), lambda b, t: (b * nt + t, 0)),
        out_shape=jax.ShapeDtypeStruct((n, GROUP_W), BF16),
        scratch_shapes=[pltpu.VMEM((GROUP_W, hk), F32), pltpu.VMEM((tc, GROUP_W), F32)],
        compiler_params=_cparams(("parallel", "arbitrary")),
        name="gla",
    )(proj, proj, proj, small, w_hi, w_lo, b_up, g_out, ones64, sel)


def _lru_kernel(x_ref, cw_ref, cb_ref, wr_ref, br_ref, wi_ref, bi_ref, lam_ref, o_ref, xprev_ref, h_ref, *, tr):
    @pl.when(pl.program_id(1) == 0)
    def _():
        xprev_ref[...] = jnp.zeros_like(xprev_ref)
        h_ref[...] = jnp.zeros_like(h_ref)

    x3 = x_ref[...].astype(F32).reshape(tr // 8, 8, GROUP_W)
    xg = jnp.concatenate([xprev_ref[...][None], x3], axis=0)
    xprev_ref[...] = x3[tr // 8 - 1]
    sub3 = lax.broadcasted_iota(jnp.int32, x3.shape, 1)
    xc3 = cb_ref[...][None] + cw_ref[CONV_W - 1:CONV_W, :][None] * x3
    for s in range(1, CONV_W):
        rot = pltpu.roll(xg, s, 1)
        shifted = jnp.where(sub3 >= s, rot[1:], rot[:-1])
        xc3 = xc3 + cw_ref[CONV_W - 1 - s:CONV_W - s, :][None] * shifted
    xc = xc3.reshape(tr, GROUP_W)

    xb = xc.astype(BF16)
    r = _sigmoid(jnp.dot(xb, wr_ref[...], preferred_element_type=F32) + br_ref[...])
    i = _sigmoid(jnp.dot(xb, wi_ref[...], preferred_element_type=F32) + bi_ref[...])
    lam = lam_ref[...]
    softplus = jnp.maximum(-lam, 0.0) + jnp.log1p(jnp.exp(-jnp.abs(lam)))
    log_a = (-LRU_C) * r * softplus
    a = jnp.exp(log_a)
    bx = jnp.sqrt(1.0 - jnp.exp(2.0 * log_a)) * (i * xc)

    a3 = a.reshape(tr // 8, 8, GROUP_W)
    b3 = bx.reshape(tr // 8, 8, GROUP_W)
    sub = lax.broadcasted_iota(jnp.int32, a3.shape, 1)
    for s in (1, 2, 4):
        keep = sub >= s
        a_prev = jnp.where(keep, pltpu.roll(a3, s, 1), 1.0)
        b_prev = jnp.where(keep, pltpu.roll(b3, s, 1), 0.0)
        b3 = a3 * b_prev + b3
        a3 = a3 * a_prev
    h_prev = h_ref[0:1, :]
    for g in range(tr // 8):
        hg = a3[g] * h_prev + b3[g]
        o_ref[g * 8:(g + 1) * 8, :] = hg.astype(BF16)
        h_prev = hg[7:8, :]
    h_ref[0:1, :] = h_prev


def _lru(proj, conv_w, conv_b, wr, br, wi, bi, lam, batch, seq):
    n = proj.shape[0]
    tr = min(1024, seq)
    nt = seq // tr
    full = lambda shape: pl.BlockSpec(shape, lambda b, t: (0, 0))
    return pl.pallas_call(
        functools.partial(_lru_kernel, tr=tr),
        grid=(batch, nt),
        in_specs=[
            pl.BlockSpec((tr, GROUP_W), lambda b, t: (b * nt + t, COL_DX // GROUP_W)),
            full((8, GROUP_W)), full((1, GROUP_W)),
            full((GROUP_W, GROUP_W)), full((1, GROUP_W)),
            full((GROUP_W, GROUP_W)), full((1, GROUP_W)), full((1, GROUP_W)),
        ],
        out_specs=pl.BlockSpec((tr, GROUP_W), lambda b, t: (b * nt + t, 0)),
        out_shape=jax.ShapeDtypeStruct((n, GROUP_W), BF16),
        scratch_shapes=[pltpu.VMEM((8, GROUP_W), F32), pltpu.VMEM((8, GROUP_W), F32)],
        compiler_params=_cparams(("parallel", "arbitrary")),
        name="rglru",
    )(proj, conv_w, conv_b, wr, br, wi, bi, lam)


def _pad_cols(w, width):
    return jnp.pad(w, [(0, 0)] * (w.ndim - 1) + [(0, width - w.shape[-1])])


def _block_diag_ones(n_blocks, size):
    return jnp.asarray(np.kron(np.eye(n_blocks, dtype=np.float32), np.ones((size, size), np.float32)), BF16)


def _swap_rotary(w):
    lead = w.shape[:-1]
    w = w.reshape(lead + (N_HEADS, MLA_NOPE + MLA_ROPE))
    half = MLA_ROPE // 2
    z1, z2 = w[..., MLA_NOPE:MLA_NOPE + half], w[..., MLA_NOPE + half:]
    w = jnp.concatenate([jnp.zeros_like(w[..., :MLA_NOPE]), z2, z1], axis=-1)
    return w.reshape(lead + (N_HEADS * (MLA_NOPE + MLA_ROPE),))


def _pad_heads(w, d_real):
    lead = w.shape[:-1]
    w = w.reshape(lead + (N_HEADS, d_real))
    w = jnp.pad(w, [(0, 0)] * len(lead) + [(0, 0), (0, HEAD_PAD - d_real)])
    return w.reshape(lead + (N_HEADS * HEAD_PAD,))


def kernel(x, positions, norm_g, w_in, w_out, gla_w_up, gla_b_up, gla_g_out, fox_b_f, fox_g_q, fox_g_k, mla_g_cq, mla_w_uq, mla_g_ckv, mla_w_ukv, mla_g_q, mla_g_k, lru_conv_w, lru_conv_b, lru_w_r, lru_b_r, lru_w_i, lru_b_i, lru_lam):
    batch, seq, _ = x.shape
    depth = w_in.shape[0]
    n = batch * seq

    sizes = (D_MIX, 256, 256, 512, GLA_RANK, 512, 512, 512, N_HEADS, MLA_Q_RANK, MLA_KV_RANK, MLA_ROPE, GROUP_W)
    offs = np.concatenate([[0], np.cumsum(sizes)])
    seg = [w_in[:, :, offs[i]:offs[i + 1]] for i in range(len(sizes))]
    (w_gate, w_aq, w_ak, w_av, w_ag, w_bq, w_bk, w_bv, w_bf, w_cq, w_ckv, w_ckr, w_dx) = seg
    assert (SM_AG, SM_BF) == (MLA_ROPE, MLA_ROPE + GLA_RANK)
    w_kr_gates = _pad_cols(jnp.concatenate([w_ckr, w_ag, w_bf], axis=-1), LANES)
    w_main = jnp.concatenate(
        [w_gate, w_aq, w_ak, w_av, w_bq, w_bk, w_bv, w_cq, w_ckv, w_kr_gates, w_dx], axis=-1).astype(BF16)
    w_out_b = w_out.astype(BF16)

    ones64 = _block_diag_ones(GROUP_W // HEAD_DIM, HEAD_DIM)
    ones128 = _block_diag_ones(2, HEAD_PAD)

    wup = jnp.pad(gla_w_up, ((0, 0), (SM_AG, LANES - SM_AG - GLA_RANK), (0, 0)))
    wup_hi = wup.astype(BF16)
    wup_lo = (wup - wup_hi.astype(F32)).astype(BF16)

    sel = np.zeros((GLA_LEVELS, GLA_CHUNK, 3 * GLA_CHUNK), np.float32)
    for lvl in range(GLA_LEVELS):
        half = GLA_CHUNK >> (lvl + 1)
        for t in range(GLA_CHUNK):
            mid = (t // (2 * half)) * 2 * half + half - 1
            for piece in range(3):
                sel[lvl, t, piece * GLA_CHUNK + mid] = 1.0
    gla_sel = jnp.asarray(sel, BF16)

    fox_scale = LOG2E
    mla_scale = (MLA_NOPE + MLA_ROPE) ** -0.5 * LOG2E
    fox_gq = jnp.tile(fox_g_q, (1, N_HEADS)) * fox_scale
    fox_gk = jnp.tile(fox_g_k, (1, N_HEADS)) * HEAD_DIM ** 0.5
    place = np.zeros((LANES, N_HEADS * HEAD_PAD), np.float32)
    aug_q = np.zeros((1, N_HEADS * HEAD_PAD), np.float32)
    for h in range(N_HEADS):
        base = h * HEAD_PAD + (HEAD_DIM if h % 2 == 0 else 0)
        for piece in range(3):
            place[SM_BF + piece * PIECE_STRIDE + h, base + piece] = 1.0
            aug_q[0, base + piece] = 1.0
    place, aug_q = jnp.asarray(place, BF16), jnp.asarray(aug_q)
    fox_bf = jnp.pad(fox_b_f, ((0, 0), (SM_BF, LANES - SM_BF - N_HEADS)))

    wuq = _pad_heads(mla_w_uq, MLA_NOPE + MLA_ROPE).astype(BF16)
    wuq_sw = _pad_heads(_swap_rotary(mla_w_uq), MLA_NOPE + MLA_ROPE).astype(BF16)
    wukv = mla_w_ukv.reshape(depth, MLA_KV_RANK, N_HEADS, 2 * HEAD_DIM)
    wuk = _pad_heads(wukv[..., :MLA_NOPE].reshape(depth, MLA_KV_RANK, N_HEADS * MLA_NOPE), MLA_NOPE).astype(BF16)
    pad_rows = lambda m: jnp.pad(m.reshape(m.shape[:-2] + (N_HEADS, HEAD_DIM, m.shape[-1])),
                                 [(0, 0)] * (m.ndim - 1) + [(0, VT_ROWS - HEAD_DIM), (0, 0)]
                                 ).reshape(m.shape[:-2] + (N_HEADS * VT_ROWS, m.shape[-1]))
    wuv_t = pad_rows(jnp.swapaxes(wukv[..., MLA_NOPE:].reshape(depth, MLA_KV_RANK, GROUP_W), 1, 2)).astype(BF16)
    eye = pad_rows(jnp.eye(GROUP_W, dtype=F32)).astype(BF16)
    mla_gq_all = jnp.tile(mla_g_q, (1, N_HEADS)) * mla_scale
    mla_gk_all = jnp.tile(mla_g_k, (1, N_HEADS))
    mla_gq = _pad_heads(mla_gq_all, MLA_NOPE + MLA_ROPE)
    mla_gk = _pad_heads(mla_gk_all, MLA_NOPE + MLA_ROPE)
    mla_gq_sw = _pad_heads(_swap_rotary(mla_gq_all), MLA_NOPE + MLA_ROPE)
    mla_gk_sw = _pad_heads(_swap_rotary(mla_gk_all), MLA_NOPE + MLA_ROPE)

    eye_h = jnp.eye(N_HEADS, dtype=F32)
    wr_bd = jnp.einsum("lncd,nm->lncmd", lru_w_r, eye_h).reshape(depth, GROUP_W, GROUP_W).astype(BF16)
    wi_bd = jnp.einsum("lncd,nm->lncmd", lru_w_i, eye_h).reshape(depth, GROUP_W, GROUP_W).astype(BF16)
    conv_w = jnp.pad(lru_conv_w, ((0, 0), (0, 8 - CONV_W), (0, 0)))

    half = MLA_ROPE // 2
    inv_freq = ROPE_THETA ** (-jnp.arange(half, dtype=F32) / half)
    invf_row = jnp.pad(jnp.concatenate([inv_freq, inv_freq]), (MLA_NOPE, LANES - MLA_NOPE - MLA_ROPE))[None, :]
    cos_t, sin_t = _rope_tables(positions.reshape(n, 1), invf_row)

    xf = x.reshape(n, D_MODEL)
    for l in range(depth):
        proj, small = _norm_inproj(xf, norm_g[l][None, :], w_main, l)
        ya = _gla(proj, small, wup_hi[l], wup_lo[l], gla_b_up[l][None, :],
                  jnp.tile(gla_g_out[l], N_HEADS)[None, :], ones64, gla_sel, batch, seq)
        fq, fk, fvt = _fox_prep(proj, small, fox_bf[l][None, :], fox_gq[l][None, :], fox_gk[l][None, :], ones64, eye,
                                place, aug_q, batch, seq)
        yb = _attention(fq, fk, fvt, batch, seq)
        mq, mk, mvt = _mla_prep(proj, cos_t, sin_t, mla_g_cq[l][None, :], wuq[l], wuq_sw[l], mla_g_ckv[l][None, :],
                                wuk[l], wuv_t[l], mla_gq[l][None, :], mla_gq_sw[l][None, :], mla_gk[l][None, :],
                                mla_gk_sw[l][None, :], ones128, batch, seq)
        yc = _attention(mq, mk, mvt, batch, seq)
        yd = _lru(proj, conv_w[l], lru_conv_b[l][None, :], wr_bd[l], lru_b_r[l][None, :], wi_bd[l],
                  lru_b_i[l][None, :], lru_lam[l][None, :], batch, seq)
        xf = _gate_outproj(ya, yb, yc, yd, proj, w_out_b, xf, l)
    return xf.reshape(batch, seq, D_MODEL)
```

```python
import functools
import math

import numpy as np
import jax
import jax.numpy as jnp
from jax import lax
from jax.experimental import pallas as pl
from jax.experimental.pallas import tpu as pltpu

F32 = jnp.float32
BF16 = jnp.bfloat16

D_MODEL = 1024
D_MIX = 2048
GROUP_W = 512
N_HEADS = 8
HEAD_DIM = 64
EPS = 1e-6
GLA_DK = 32
GLA_RANK = 16
GLA_NORMALIZER = 16.0
GLA_CHUNK = 64
GLA_LEVELS = 6
GLA_SAFE_LOG_DECAY = 40.0
MLA_Q_RANK = 256
MLA_KV_RANK = 128
MLA_NOPE = 64
MLA_ROPE = 32
ROPE_THETA = 10000.0
CONV_W = 4
LRU_C = 8.0
LOG2E = 1.4426950408889634
NEG_BIG = -1e30

LANES = 128
HEAD_PAD = 128
VT_ROWS = 80

COL_GATE = 0
COL_AQ, COL_AK, COL_AV = 2048, 2304, 2560
COL_BQ, COL_BK, COL_BV = 3072, 3584, 4096
COL_CQ, COL_CKV, COL_CKR = 4608, 4864, 4992
COL_DX = 5120
D_PROJ = 5632
SM_AG, SM_BF = 32, 48
PIECE_STRIDE = 16


def _cparams(sem, vmem_mb=48):
    return pltpu.CompilerParams(dimension_semantics=sem, vmem_limit_bytes=vmem_mb * 1024 * 1024)


def _sigmoid(z):
    return 0.5 * jnp.tanh(0.5 * z) + 0.5


def _with_ones_rows(vt):
    row = lax.broadcasted_iota(jnp.int32, vt.shape, 0) % VT_ROWS
    return jnp.where(row == HEAD_DIM, 1.0, vt)


def _log_sigmoid(z):
    return jnp.minimum(z, 0.0) - jnp.log1p(jnp.exp(-jnp.abs(z)))


def _row_prefix_sum(v, seg):
    rows = v.shape[0]
    rid = lax.broadcasted_iota(jnp.int32, v.shape, 0) % seg
    s = 1
    while s < seg:
        v = v + jnp.where(rid >= s, pltpu.roll(v, s, 0), 0.0)
        s *= 2
    del rows
    return v


def _inproj_kernel(x_ref, g_ref, w_ref, o_ref, os_ref, h_ref, *, small_col):
    @pl.when(pl.program_id(1) == 0)
    def _():
        x = x_ref[...]
        ms = jnp.mean(x * x, axis=-1, keepdims=True)
        h_ref[...] = (x * lax.rsqrt(ms + EPS) * g_ref[...]).astype(BF16)

    res = jnp.dot(h_ref[...], w_ref[...], preferred_element_type=F32)
    o_ref[...] = res.astype(BF16)

    @pl.when(pl.program_id(1) == pl.num_programs(1) - 1)
    def _():
        os_ref[...] = res[:, small_col:small_col + LANES]


def _norm_inproj(x, g, w_all, layer):
    n = x.shape[0]
    tm = min(1024, n)
    tn = D_PROJ // 2
    assert COL_CKR >= D_PROJ - tn
    return pl.pallas_call(
        functools.partial(_inproj_kernel, small_col=COL_CKR - (D_PROJ - tn)),
        grid=(n // tm, D_PROJ // tn),
        in_specs=[
            pl.BlockSpec((tm, D_MODEL), lambda i, j: (i, 0)),
            pl.BlockSpec((1, D_MODEL), lambda i, j: (0, 0)),
            pl.BlockSpec((None, D_MODEL, tn), lambda i, j: (layer, 0, j)),
        ],
        out_specs=[
            pl.BlockSpec((tm, tn), lambda i, j: (i, j)),
            pl.BlockSpec((tm, LANES), lambda i, j: (i, 0)),
        ],
        out_shape=[jax.ShapeDtypeStruct((n, D_PROJ), BF16), jax.ShapeDtypeStruct((n, LANES), F32)],
        scratch_shapes=[pltpu.VMEM((tm, D_MODEL), BF16)],
        compiler_params=_cparams(("parallel", "arbitrary"), vmem_mb=56),
        name="norm_inproj",
    )(x, g, w_all)


def _outproj_kernel(ya_ref, yb_ref, yc_ref, yd_ref, gate_ref, w_ref, x_ref, o_ref, z_ref):
    for gi, y_ref in enumerate((ya_ref, yb_ref, yc_ref, yd_ref)):
        lo, hi = gi * GROUP_W, (gi + 1) * GROUP_W
        g = gate_ref[:, lo:hi].astype(F32)
        z_ref[:, lo:hi] = (y_ref[...].astype(F32) * (g * _sigmoid(g))).astype(BF16)
    o_ref[...] = x_ref[...] + jnp.dot(z_ref[...], w_ref[...], preferred_element_type=F32)


def _gate_outproj(ya, yb, yc, yd, proj, w_all, x, layer):
    n = x.shape[0]
    tm = min(1024, n)
    yspec = pl.BlockSpec((tm, GROUP_W), lambda i: (i, 0))
    return pl.pallas_call(
        _outproj_kernel,
        grid=(n // tm,),
        in_specs=[
            yspec, yspec, yspec, yspec,
            pl.BlockSpec((tm, D_MIX), lambda i: (i, 0)),
            pl.BlockSpec((None, D_MIX, D_MODEL), lambda i: (layer, 0, 0)),
            pl.BlockSpec((tm, D_MODEL), lambda i: (i, 0)),
        ],
        out_specs=pl.BlockSpec((tm, D_MODEL), lambda i: (i, 0)),
        out_shape=jax.ShapeDtypeStruct((n, D_MODEL), F32),
        scratch_shapes=[pltpu.VMEM((tm, D_MIX), BF16)],
        compiler_params=_cparams(("parallel",)),
        name="gate_outproj",
    )(ya, yb, yc, yd, proj, w_all, x)


def _attn_kernel(q_ref, k_ref, vt_ref, o_ref, m_ref, acc_ref, st_ref, mx_ref, *, tq, nq, hg):
    i = pl.program_id(2)
    half = tq // 2
    nt_dims = (((1,), (1,)), ((), ()))

    m_ref[...] = jnp.full(m_ref.shape, NEG_BIG, F32)
    acc_ref[...] = jnp.zeros(acc_ref.shape, F32)

    def scores(h, krows, qrows):
        lanes = slice(h * HEAD_PAD, (h + 1) * HEAD_PAD)
        return lax.dot_general(k_ref[krows, lanes], q_ref[qrows, lanes], nt_dims, preferred_element_type=F32)

    def vrows(h):
        return slice(h * VT_ROWS, (h + 1) * VT_ROWS)

    tri = lax.broadcasted_iota(jnp.int32, (half, half), 0) <= lax.broadcasted_iota(jnp.int32, (half, half), 1)
    low = lax.broadcasted_iota(jnp.int32, (tq, half), 0) <= lax.broadcasted_iota(jnp.int32, (tq, half), 1) + half

    def issue_scores(item, c):
        h, blk = item
        if blk is None:
            off = c * tq
            s_l = jnp.where(tri, scores(h, slice(off, off + half), slice(0, half)), NEG_BIG)
            s_r = jnp.where(low, scores(h, slice(off, off + tq), slice(half, tq)), NEG_BIG)
            mx = jnp.concatenate([jnp.max(s_l, axis=0, keepdims=True), jnp.max(s_r, axis=0, keepdims=True)], axis=1)
            return (s_l, s_r, mx)
        st = scores(h, slice(blk * tq, (blk + 1) * tq), slice(0, tq))
        return (st, jnp.max(st, axis=0, keepdims=True))

    def store_scores(item, slot, vals):
        if item[1] is None:
            st_ref[slot, 0:half, 0:half] = vals[0]
            st_ref[slot, :, half:tq] = vals[1]
        else:
            st_ref[slot] = vals[0]
        mx_ref[slot] = vals[-1]

    def consume(item, slot, c):
        h, blk = item
        m = m_ref[h]
        m_new = jnp.maximum(m, mx_ref[slot])
        alpha = jnp.exp2(m - m_new)
        m_ref[h] = m_new

        def accumulate(cols, st, vt):
            p = jnp.exp2(st - m_new[:, cols]).astype(BF16)
            acc_ref[h, :, cols] = alpha[:, cols] * acc_ref[h, :, cols] + jnp.dot(vt, p, preferred_element_type=F32)

        if blk is None:
            off = c * tq
            accumulate(slice(0, half), st_ref[slot, 0:half, 0:half], vt_ref[vrows(h), off:off + half])
            accumulate(slice(half, tq), st_ref[slot, :, half:tq], vt_ref[vrows(h), off:off + tq])
        else:
            accumulate(slice(0, tq), st_ref[slot], vt_ref[vrows(h), blk * tq:(blk + 1) * tq])

    for c in range(nq):
        @pl.when(i == c)
        def _(c=c):
            items = [(h, None) for h in range(hg)] + [(h, blk) for blk in range(c) for h in range(hg)]
            store_scores(items[0], 0, issue_scores(items[0], c))
            for n, item in enumerate(items):
                nxt = items[n + 1] if n + 1 < len(items) else None
                if nxt is not None:
                    vals = issue_scores(nxt, c)
                consume(item, n % 2, c)
                if nxt is not None:
                    store_scores(nxt, (n + 1) % 2, vals)

    ot = jnp.concatenate([acc_ref[h, 0:HEAD_DIM, :] / acc_ref[h, HEAD_DIM:HEAD_DIM + 1, :] for h in range(hg)], axis=0)
    o_ref[...] = ot.T.astype(BF16)


ATTN_HEADS_PER_STEP = 8


def _attention(q, k, vt, batch, seq):
    n = q.shape[0]
    tq = min(512, seq)
    nq = seq // tq
    hg = ATTN_HEADS_PER_STEP
    return pl.pallas_call(
        functools.partial(_attn_kernel, tq=tq, nq=nq, hg=hg),
        grid=(batch, N_HEADS // hg, nq),
        in_specs=[
            pl.BlockSpec((tq, hg * HEAD_PAD), lambda b, p, i: (b * nq + i, p)),
            pl.BlockSpec((seq, hg * HEAD_PAD), lambda b, p, i: (b, p)),
            pl.BlockSpec((None, hg * VT_ROWS, seq), lambda b, p, i: (b, p, 0)),
        ],
        out_specs=pl.BlockSpec((tq, hg * HEAD_DIM), lambda b, p, i: (b * nq + i, p)),
        out_shape=jax.ShapeDtypeStruct((n, GROUP_W), BF16),
        scratch_shapes=[pltpu.VMEM((hg, 1, tq), F32), pltpu.VMEM((hg, VT_ROWS, tq), F32),
                        pltpu.VMEM((2, tq, tq), F32), pltpu.VMEM((2, 1, tq), F32)],
        compiler_params=_cparams(("parallel", "parallel", "arbitrary")),
        name="causal_attention",
    )(q, k, vt)


def _fox_prep_kernel(q_ref, k_ref, v_ref, sm_ref, bf_ref, gq_ref, gk_ref, ones_ref, eye_ref, place_ref, augq_ref,
                     qo_ref, ko_ref, vt_ref, carry_ref, *, tt):
    @pl.when(pl.program_id(1) == 0)
    def _():
        carry_ref[...] = jnp.zeros_like(carry_ref)

    vt = lax.dot_general(eye_ref[...], v_ref[...], (((1,), (1,)), ((), ())), preferred_element_type=F32)
    vt_ref[...] = _with_ones_rows(vt).astype(BF16)

    def headnorm(ref, g_ref):
        z = ref[...].astype(F32)
        ssq = jnp.dot((z * z).astype(BF16), ones_ref[...], preferred_element_type=F32)
        return z * lax.rsqrt(ssq + HEAD_DIM * EPS) * g_ref[...]

    qn = headnorm(q_ref, gq_ref)
    kn = headnorm(k_ref, gk_ref)

    log_f = _log_sigmoid(sm_ref[...] + bf_ref[...])
    cum = _row_prefix_sum(log_f, tt) + carry_ref[0:1, :]
    carry_ref[0:1, :] = cum[tt - 1:tt, :]
    neg = cum * (-LOG2E)

    lane = lax.broadcasted_iota(jnp.int32, (tt, LANES), 1)
    hi = neg.astype(BF16).astype(F32)
    r1 = neg - hi
    mid = r1.astype(BF16).astype(F32)
    lo = r1 - mid
    pieces = jnp.where(lane < SM_BF + PIECE_STRIDE, hi,
                       jnp.where(lane < SM_BF + 2 * PIECE_STRIDE, pltpu.roll(mid, PIECE_STRIDE, 1),
                                 pltpu.roll(lo, 2 * PIECE_STRIDE, 1)))
    aug_k = jnp.dot(pieces.astype(BF16), place_ref[...], preferred_element_type=F32)
    aug_q = augq_ref[...]
    for h in range(N_HEADS):
        pair, e = h // 2, h % 2
        in_head = (lane >= e * HEAD_DIM) & (lane < (e + 1) * HEAD_DIM)
        blk = slice(h * HEAD_PAD, (h + 1) * HEAD_PAD)
        kpair = kn[:, pair * LANES:(pair + 1) * LANES]
        qpair = qn[:, pair * LANES:(pair + 1) * LANES]
        ko_ref[:, blk] = jnp.where(in_head, kpair, aug_k[:, blk]).astype(BF16)
        qo_ref[:, blk] = jnp.where(in_head, qpair, aug_q[:, blk]).astype(BF16)


def _fox_prep(proj, small, bf_row, gq, gk, ones64, eye, place, aug_q, batch, seq):
    n = proj.shape[0]
    tt = min(1024, seq)
    nt = seq // tt
    row = lambda shape: pl.BlockSpec(shape, lambda b, t: (0, 0))
    return pl.pallas_call(
        functools.partial(_fox_prep_kernel, tt=tt),
        grid=(batch, nt),
        in_specs=[
            pl.BlockSpec((tt, GROUP_W), lambda b, t: (b * nt + t, COL_BQ // GROUP_W)),
            pl.BlockSpec((tt, GROUP_W), lambda b, t: (b * nt + t, COL_BK // GROUP_W)),
            pl.BlockSpec((tt, GROUP_W), lambda b, t: (b * nt + t, COL_BV // GROUP_W)),
            pl.BlockSpec((tt, LANES), lambda b, t: (b * nt + t, 0)),
            row((1, LANES)), row((1, GROUP_W)), row((1, GROUP_W)), row((GROUP_W, GROUP_W)),
            row((N_HEADS * VT_ROWS, GROUP_W)), row((LANES, N_HEADS * HEAD_PAD)), row((1, N_HEADS * HEAD_PAD)),
        ],
        out_specs=[
            pl.BlockSpec((tt, N_HEADS * HEAD_PAD), lambda b, t: (b * nt + t, 0)),
            pl.BlockSpec((tt, N_HEADS * HEAD_PAD), lambda b, t: (b * nt + t, 0)),
            pl.BlockSpec((None, N_HEADS * VT_ROWS, tt), lambda b, t: (b, 0, t)),
        ],
        out_shape=[jax.ShapeDtypeStruct((n, N_HEADS * HEAD_PAD), BF16)] * 2
        + [jax.ShapeDtypeStruct((batch, N_HEADS * VT_ROWS, seq), BF16)],
        scratch_shapes=[pltpu.VMEM((8, LANES), F32)],
        compiler_params=_cparams(("parallel", "arbitrary")),
        name="fox_prep",
    )(proj, proj, proj, small, bf_row, gq, gk, ones64, eye, place, aug_q)


def _rope_kernel(pos_ref, invf_ref, c_ref, s_ref):
    ang = pos_ref[...].astype(F32) * invf_ref[...]
    lane = lax.broadcasted_iota(jnp.int32, ang.shape, 1)
    half = MLA_ROPE // 2
    c, s = jnp.cos(ang), jnp.sin(ang)
    c_ref[...] = jnp.where((lane >= MLA_NOPE) & (lane < MLA_NOPE + MLA_ROPE), c, 1.0)
    s_ref[...] = jnp.where((lane >= MLA_NOPE) & (lane < MLA_NOPE + half), -s,
                           jnp.where((lane >= MLA_NOPE + half) & (lane < MLA_NOPE + MLA_ROPE), s, 0.0))


def _rope_tables(pos_col, invf_row):
    n = pos_col.shape[0]
    tm = min(1024, n)
    out = pl.BlockSpec((tm, LANES), lambda i: (i, 0))
    return pl.pallas_call(
        _rope_kernel,
        grid=(n // tm,),
        in_specs=[pl.BlockSpec((tm, 1), lambda i: (i, 0)), pl.BlockSpec((1, LANES), lambda i: (0, 0))],
        out_specs=[out, out],
        out_shape=[jax.ShapeDtypeStruct((n, LANES), F32)] * 2,
        compiler_params=_cparams(("parallel",)),
        name="rope_tables",
    )(pos_col, invf_row)


def _mla_prep_kernel(cq_ref, ckv_ref, ckr_ref, c_ref, s_ref, gcq_ref, wuq_ref, wuqs_ref, gckv_ref, wuk_ref, wuv_ref,
                     gq_ref, gqs_ref, gk_ref, gks_ref, ones_ref, qo_ref, ko_ref, vo_ref):
    d_qk = float(MLA_NOPE + MLA_ROPE)
    cos, sin = c_ref[...], s_ref[...]
    half = MLA_ROPE // 2

    def rms(z, g):
        return (z * lax.rsqrt(jnp.mean(z * z, axis=-1, keepdims=True) + EPS) * g).astype(BF16)

    def head_ssq(z):
        zz = (z * z).astype(BF16)
        parts = [jnp.dot(zz[:, c * 256:(c + 1) * 256], ones_ref[...], preferred_element_type=F32) for c in range(4)]
        return jnp.concatenate(parts, axis=1)

    cn = rms(cq_ref[...].astype(F32), gcq_ref[...])
    q = jnp.dot(cn, wuq_ref[...], preferred_element_type=F32)
    q_sw = jnp.dot(cn, wuqs_ref[...], preferred_element_type=F32)
    q_inv = lax.rsqrt(head_ssq(q) + d_qk * EPS)

    kvn = rms(ckv_ref[...].astype(F32), gckv_ref[...])
    kn = jnp.dot(kvn, wuk_ref[...], preferred_element_type=F32)
    vt = lax.dot_general(wuv_ref[...], kvn, (((1,), (1,)), ((), ())), preferred_element_type=F32)
    vo_ref[...] = _with_ones_rows(vt).astype(BF16)
    lane = lax.broadcasted_iota(jnp.int32, ckr_ref.shape, 1)
    kr = jnp.where(lane < MLA_ROPE, ckr_ref[...].astype(F32), 0.0)
    ss_r = jnp.sum(kr * kr, axis=-1, keepdims=True)
    kr_placed = pltpu.roll(kr, MLA_NOPE, 1)
    kr_swapped = jnp.where(lane < MLA_NOPE + half, pltpu.roll(kr, MLA_NOPE - half, 1), pltpu.roll(kr, MLA_NOPE + half, 1))
    kr_swapped = jnp.where((lane >= MLA_NOPE) & (lane < MLA_NOPE + MLA_ROPE), kr_swapped, 0.0)
    k_inv = lax.rsqrt(head_ssq(kn) + ss_r + d_qk * EPS)

    gc_q, gs_q = gq_ref[...] * cos, gqs_ref[...] * sin
    gc_k = gk_ref[...] * cos
    kr_rot = kr_placed * gc_k + kr_swapped * (gks_ref[...] * sin)
    for h in range(N_HEADS):
        sl = slice(h * HEAD_PAD, (h + 1) * HEAD_PAD)
        qo_ref[:, sl] = (q_inv[:, sl] * (q[:, sl] * gc_q + q_sw[:, sl] * gs_q)).astype(BF16)
        ko_ref[:, sl] = (k_inv[:, sl] * (kn[:, sl] * gc_k + kr_rot)).astype(BF16)


def _mla_prep(proj, cos_t, sin_t, gcq, wuq, wuq_sw, gckv, wuk, wuv_t, gq, gq_sw, gk, gk_sw, ones128, batch, seq):
    n = proj.shape[0]
    tm = min(1024, seq)
    nt = seq // tm
    full = lambda shape: pl.BlockSpec(shape, lambda i: (0, 0))
    tab = pl.BlockSpec((tm, LANES), lambda i: (i, 0))
    wide = pl.BlockSpec((tm, N_HEADS * HEAD_PAD), lambda i: (i, 0))
    return pl.pallas_call(
        _mla_prep_kernel,
        grid=(n // tm,),
        in_specs=[
            pl.BlockSpec((tm, MLA_Q_RANK), lambda i: (i, COL_CQ // MLA_Q_RANK)),
            pl.BlockSpec((tm, MLA_KV_RANK), lambda i: (i, COL_CKV // LANES)),
            pl.BlockSpec((tm, LANES), lambda i: (i, COL_CKR // LANES)),
            tab, tab,
            full((1, MLA_Q_RANK)), full((MLA_Q_RANK, N_HEADS * HEAD_PAD)), full((MLA_Q_RANK, N_HEADS * HEAD_PAD)),
            full((1, MLA_KV_RANK)), full((MLA_KV_RANK, N_HEADS * HEAD_PAD)), full((N_HEADS * VT_ROWS, MLA_KV_RANK)),
            full((1, HEAD_PAD)), full((1, HEAD_PAD)), full((1, HEAD_PAD)), full((1, HEAD_PAD)), full((256, 256)),
        ],
        out_specs=[wide, wide, pl.BlockSpec((None, N_HEADS * VT_ROWS, tm), lambda i: (i // nt, 0, i % nt))],
        out_shape=[jax.ShapeDtypeStruct((n, N_HEADS * HEAD_PAD), BF16)] * 2
        + [jax.ShapeDtypeStruct((batch, N_HEADS * VT_ROWS, seq), BF16)],
        compiler_params=_cparams(("parallel",)),
        name="mla_prep",
    )(proj, proj, proj, cos_t, sin_t, gcq, wuq, wuq_sw, gckv, wuk, wuv_t, gq, gq_sw, gk, gk_sw, ones128)


def _gla_kernel(q_ref, k_ref, v_ref, sm_ref, wh_ref, wl_ref, b_ref, go_ref, ones_ref, sel_ref, o_ref, st_ref,
                inter_ref, *, tc):
    hk = N_HEADS * GLA_DK
    ck = GLA_CHUNK

    @pl.when(pl.program_id(1) == 0)
    def _():
        st_ref[...] = jnp.zeros_like(st_ref)

    sm = sm_ref[...]
    sm_hi = sm.astype(BF16)
    sm_lo = (sm - sm_hi.astype(F32)).astype(BF16)
    z = (jnp.dot(sm_hi, wh_ref[...], preferred_element_type=F32)
         + jnp.dot(sm_lo, wh_ref[...], preferred_element_type=F32)
         + jnp.dot(sm_hi, wl_ref[...], preferred_element_type=F32)) + b_ref[...]
    g_all = _row_prefix_sum(_log_sigmoid(z) * (1.0 / GLA_NORMALIZER), ck)

    lane_k = lax.broadcasted_iota(jnp.int32, (ck, hk), 1) // GLA_DK
    lane_pair = lax.broadcasted_iota(jnp.int32, (ck, LANES), 1)
    a_row = lax.broadcasted_iota(jnp.int32, (N_HEADS * ck, ck), 0) % ck
    a_col = lax.broadcasted_iota(jnp.int32, (N_HEADS * ck, ck), 1)
    st_blockdiag = (lax.broadcasted_iota(jnp.int32, (GROUP_W, hk), 0) // HEAD_DIM
                    == lax.broadcasted_iota(jnp.int32, (GROUP_W, hk), 1) // GLA_DK)

    def stack_heads(x):
        return jnp.concatenate([jnp.where(lane_k == h, x, jnp.zeros_like(x)) for h in range(N_HEADS)], axis=0)

    def intra_output(a, v):
        parts = []
        for p in range(N_HEADS // 2):
            r = jnp.dot(a[2 * p * ck:2 * (p + 1) * ck, :], v[:, p * LANES:(p + 1) * LANES], preferred_element_type=F32)
            parts.append(jnp.where(lane_pair < HEAD_DIM, r[0:ck, :], r[ck:2 * ck, :]))
        return jnp.concatenate(parts, axis=1)

    def write_normed(o):
        ssq = jnp.dot((o * o).astype(BF16), ones_ref[...], preferred_element_type=F32)
        o_ref[...] = (o * lax.rsqrt(ssq * (1.0 / HEAD_DIM) + EPS) * go_ref[...]).astype(BF16)

    n_chunks = tc // ck
    qts, intras, incs, decays = [], [], [], []
    for c in range(n_chunks):
        rows = slice(c * ck, (c + 1) * ck)
        g = g_all[rows]
        q = q_ref[rows, :].astype(F32) * (GLA_DK ** -0.5)
        k = k_ref[rows, :].astype(F32)
        v = v_ref[rows, :]
        g_last = g[ck - 1:ck, :]
        qt = (q * jnp.exp(g)).astype(BF16)
        kt = (k * jnp.exp(-g)).astype(BF16)
        kh = (k * jnp.exp(g_last - g)).astype(BF16)

        u = lax.dot_general(v, kh, (((0,), (0,)), ((), ())), preferred_element_type=F32)
        incs.append(jnp.where(st_blockdiag, u, 0.0))
        decays.append(jnp.exp(g_last))

        a = lax.dot_general(stack_heads(qt), kt, (((1,), (1,)), ((), ())), preferred_element_type=F32)
        intras.append(intra_output(jnp.where(a_row >= a_col, a, 0.0).astype(BF16), v))
        qts.append(qt)

    state = st_ref[...]
    states = []
    for c in range(n_chunks):
        states.append(state.astype(BF16))
        state = state * decays[c] + incs[c]
    st_ref[...] = state

    outs = []
    for c in range(n_chunks):
        o = lax.dot_general(qts[c], states[c], (((1,), (1,)), ((), ())), preferred_element_type=F32)
        inter_ref[c * ck:(c + 1) * ck, :] = o
        outs.append(o + intras[c])
    write_normed(jnp.concatenate(outs, axis=0))

    @pl.when(jnp.min(g_all) < -GLA_SAFE_LOG_DECAY)
    def _():
        t_idx = lax.broadcasted_iota(jnp.int32, (ck, hk), 0)
        nt_dims = (((1,), (1,)), ((), ()))
        redo = []
        for c in range(n_chunks):
            rows = slice(c * ck, (c + 1) * ck)
            g = g_all[rows]
            q = q_ref[rows, :].astype(F32) * (GLA_DK ** -0.5)
            k = k_ref[rows, :].astype(F32)
            g_hi = g.astype(BF16)
            g_r1 = g - g_hi.astype(F32)
            g_mid = g_r1.astype(BF16)
            g_lo = (g_r1 - g_mid.astype(F32)).astype(BF16)
            g3 = jnp.concatenate([g_hi, g_mid, g_lo], axis=0)
            a = lax.dot_general(stack_heads(q.astype(BF16)), k.astype(BF16), nt_dims, preferred_element_type=F32)
            a = jnp.where(a_row == a_col, a, 0.0)
            for lvl in range(GLA_LEVELS):
                half = ck >> (lvl + 1)
                g_mid_row = jnp.dot(sel_ref[lvl], g3, preferred_element_type=F32)
                d = g - g_mid_row
                second = (t_idx // half) % 2 == 1
                qd = jnp.where(second, q * jnp.exp(jnp.minimum(d, 0.0)), 0.0).astype(BF16)
                kd = jnp.where(second, 0.0, k * jnp.exp(jnp.minimum(-d, 0.0))).astype(BF16)
                part = lax.dot_general(stack_heads(qd), kd, nt_dims, preferred_element_type=F32)
                a = a + jnp.where(a_row // (2 * half) == a_col // (2 * half), part, 0.0)
            redo.append(inter_ref[rows, :] + intra_output(a.astype(BF16), v_ref[rows, :]))
        write_normed(jnp.concatenate(redo, axis=0))


def _gla(proj, small, w_hi, w_lo, b_up, g_out, ones64, sel, batch, seq):
    n = proj.shape[0]
    tc = min(512, seq)
    nt = seq // tc
    hk = N_HEADS * GLA_DK
    full = lambda shape: pl.BlockSpec(shape, lambda b, t: (0, 0))
    return pl.pallas_call(
        functools.partial(_gla_kernel, tc=tc),
        grid=(batch, nt),
        in_specs=[
            pl.BlockSpec((tc, hk), lambda b, t: (b * nt + t, COL_AQ // hk)),
            pl.BlockSpec((tc, hk), lambda b, t: (b * nt + t, COL_AK // hk)),
            pl.BlockSpec((tc, GROUP_W), lambda b, t: (b * nt + t, COL_AV // GROUP_W)),
            pl.BlockSpec((tc, LANES), lambda b, t: (b * nt + t, 0)),
            full((LANES, hk)), full((LANES, hk)), full((1, hk)), full((1, GROUP_W)), full((GROUP_W, GROUP_W)),
            pl.BlockSpec((GLA_LEVELS, GLA_CHUNK, 3 * GLA_CHUNK), lambda b, t: (0, 0, 0)),
        ],
        out_specs=pl.BlockSpec((tc, GROUP_W), lambda b, t: (b * nt + t, 0)),
        out_shape=jax.ShapeDtypeStruct((n, GROUP_W), BF16),
        scratch_shapes=[pltpu.VMEM((GROUP_W, hk), F32), pltpu.VMEM((tc, GROUP_W), F32)],
        compiler_params=_cparams(("parallel", "arbitrary")),
        name="gla",
    )(proj, proj, proj, small, w_hi, w_lo, b_up, g_out, ones64, sel)


def _lru_kernel(x_ref, cw_ref, cb_ref, wr_ref, br_ref, wi_ref, bi_ref, lam_ref, o_ref, xprev_ref, h_ref, *, tr):
    @pl.when(pl.program_id(1) == 0)
    def _():
        xprev_ref[...] = jnp.zeros_like(xprev_ref)
        h_ref[...] = jnp.zeros_like(h_ref)

    x3 = x_ref[...].astype(F32).reshape(tr // 8, 8, GROUP_W)
    xg = jnp.concatenate([xprev_ref[...][None], x3], axis=0)
    xprev_ref[...] = x3[tr // 8 - 1]
    sub3 = lax.broadcasted_iota(jnp.int32, x3.shape, 1)
    xc3 = cb_ref[...][None] + cw_ref[CONV_W - 1:CONV_W, :][None] * x3
    for s in range(1, CONV_W):
        rot = pltpu.roll(xg, s, 1)
        shifted = jnp.where(sub3 >= s, rot[1:], rot[:-1])
        xc3 = xc3 + cw_ref[CONV_W - 1 - s:CONV_W - s, :][None] * shifted
    xc = xc3.reshape(tr, GROUP_W)

    xb = xc.astype(BF16)
    r = _sigmoid(jnp.dot(xb, wr_ref[...], preferred_element_type=F32) + br_ref[...])
    i = _sigmoid(jnp.dot(xb, wi_ref[...], preferred_element_type=F32) + bi_ref[...])
    lam = lam_ref[...]
    softplus = jnp.maximum(-lam, 0.0) + jnp.log1p(jnp.exp(-jnp.abs(lam)))
    log_a = (-LRU_C) * r * softplus
    a = jnp.exp(log_a)
    bx = jnp.sqrt(1.0 - jnp.exp(2.0 * log_a)) * (i * xc)

    a3 = a.reshape(tr // 8, 8, GROUP_W)
    b3 = bx.reshape(tr // 8, 8, GROUP_W)
    sub = lax.broadcasted_iota(jnp.int32, a3.shape, 1)
    for s in (1, 2, 4):
        keep = sub >= s
        a_prev = jnp.where(keep, pltpu.roll(a3, s, 1), 1.0)
        b_prev = jnp.where(keep, pltpu.roll(b3, s, 1), 0.0)
        b3 = a3 * b_prev + b3
        a3 = a3 * a_prev
    h_prev = h_ref[0:1, :]
    for g in range(tr // 8):
        hg = a3[g] * h_prev + b3[g]
        o_ref[g * 8:(g + 1) * 8, :] = hg.astype(BF16)
        h_prev = hg[7:8, :]
    h_ref[0:1, :] = h_prev


def _lru(proj, conv_w, conv_b, wr, br, wi, bi, lam, batch, seq):
    n = proj.shape[0]
    tr = min(1024, seq)
    nt = seq // tr
    full = lambda shape: pl.BlockSpec(shape, lambda b, t: (0, 0))
    return pl.pallas_call(
        functools.partial(_lru_kernel, tr=tr),
        grid=(batch, nt),
        in_specs=[
            pl.BlockSpec((tr, GROUP_W), lambda b, t: (b * nt + t, COL_DX // GROUP_W)),
            full((8, GROUP_W)), full((1, GROUP_W)),
            full((GROUP_W, GROUP_W)), full((1, GROUP_W)),
            full((GROUP_W, GROUP_W)), full((1, GROUP_W)), full((1, GROUP_W)),
        ],
        out_specs=pl.BlockSpec((tr, GROUP_W), lambda b, t: (b * nt + t, 0)),
        out_shape=jax.ShapeDtypeStruct((n, GROUP_W), BF16),
        scratch_shapes=[pltpu.VMEM((8, GROUP_W), F32), pltpu.VMEM((8, GROUP_W), F32)],
        compiler_params=_cparams(("parallel", "arbitrary")),
        name="rglru",
    )(proj, conv_w, conv_b, wr, br, wi, bi, lam)


def _pad_cols(w, width):
    return jnp.pad(w, [(0, 0)] * (w.ndim - 1) + [(0, width - w.shape[-1])])


def _block_diag_ones(n_blocks, size):
    return jnp.asarray(np.kron(np.eye(n_blocks, dtype=np.float32), np.ones((size, size), np.float32)), BF16)


def _swap_rotary(w):
    lead = w.shape[:-1]
    w = w.reshape(lead + (N_HEADS, MLA_NOPE + MLA_ROPE))
    half = MLA_ROPE // 2
    z1, z2 = w[..., MLA_NOPE:MLA_NOPE + half], w[..., MLA_NOPE + half:]
    w = jnp.concatenate([jnp.zeros_like(w[..., :MLA_NOPE]), z2, z1], axis=-1)
    return w.reshape(lead + (N_HEADS * (MLA_NOPE + MLA_ROPE),))


def _pad_heads(w, d_real):
    lead = w.shape[:-1]
    w = w.reshape(lead + (N_HEADS, d_real))
    w = jnp.pad(w, [(0, 0)] * len(lead) + [(0, 0), (0, HEAD_PAD - d_real)])
    return w.reshape(lead + (N_HEADS * HEAD_PAD,))


def kernel(x, positions, norm_g, w_in, w_out, gla_w_up, gla_b_up, gla_g_out, fox_b_f, fox_g_q, fox_g_k, mla_g_cq, mla_w_uq, mla_g_ckv, mla_w_ukv, mla_g_q, mla_g_k, lru_conv_w, lru_conv_b, lru_w_r, lru_b_r, lru_w_i, lru_b_i, lru_lam):
    batch, seq, _ = x.shape
    depth = w_in.shape[0]
    n = batch * seq

    sizes = (D_MIX, 256, 256, 512, GLA_RANK, 512, 512, 512, N_HEADS, MLA_Q_RANK, MLA_KV_RANK, MLA_ROPE, GROUP_W)
    offs = np.concatenate([[0], np.cumsum(sizes)])
    seg = [w_in[:, :, offs[i]:offs[i + 1]] for i in range(len(sizes))]
    (w_gate, w_aq, w_ak, w_av, w_ag, w_bq, w_bk, w_bv, w_bf, w_cq, w_ckv, w_ckr, w_dx) = seg
    assert (SM_AG, SM_BF) == (MLA_ROPE, MLA_ROPE + GLA_RANK)
    w_kr_gates = _pad_cols(jnp.concatenate([w_ckr, w_ag, w_bf], axis=-1), LANES)
    w_main = jnp.concatenate(
        [w_gate, w_aq, w_ak, w_av, w_bq, w_bk, w_bv, w_cq, w_ckv, w_kr_gates, w_dx], axis=-1).astype(BF16)
    w_out_b = w_out.astype(BF16)

    ones64 = _block_diag_ones(GROUP_W // HEAD_DIM, HEAD_DIM)
    ones128 = _block_diag_ones(2, HEAD_PAD)

    wup = jnp.pad(gla_w_up, ((0, 0), (SM_AG, LANES - SM_AG - GLA_RANK), (0, 0)))
    wup_hi = wup.astype(BF16)
    wup_lo = (wup - wup_hi.astype(F32)).astype(BF16)

    sel = np.zeros((GLA_LEVELS, GLA_CHUNK, 3 * GLA_CHUNK), np.float32)
    for lvl in range(GLA_LEVELS):
        half = GLA_CHUNK >> (lvl + 1)
        for t in range(GLA_CHUNK):
            mid = (t // (2 * half)) * 2 * half + half - 1
            for piece in range(3):
                sel[lvl, t, piece * GLA_CHUNK + mid] = 1.0
    gla_sel = jnp.asarray(sel, BF16)

    fox_scale = LOG2E
    mla_scale = (MLA_NOPE + MLA_ROPE) ** -0.5 * LOG2E
    fox_gq = jnp.tile(fox_g_q, (1, N_HEADS)) * fox_scale
    fox_gk = jnp.tile(fox_g_k, (1, N_HEADS)) * HEAD_DIM ** 0.5
    place = np.zeros((LANES, N_HEADS * HEAD_PAD), np.float32)
    aug_q = np.zeros((1, N_HEADS * HEAD_PAD), np.float32)
    for h in range(N_HEADS):
        base = h * HEAD_PAD + (HEAD_DIM if h % 2 == 0 else 0)
        for piece in range(3):
            place[SM_BF + piece * PIECE_STRIDE + h, base + piece] = 1.0
            aug_q[0, base + piece] = 1.0
    place, aug_q = jnp.asarray(place, BF16), jnp.asarray(aug_q)
    fox_bf = jnp.pad(fox_b_f, ((0, 0), (SM_BF, LANES - SM_BF - N_HEADS)))

    wuq = _pad_heads(mla_w_uq, MLA_NOPE + MLA_ROPE).astype(BF16)
    wuq_sw = _pad_heads(_swap_rotary(mla_w_uq), MLA_NOPE + MLA_ROPE).astype(BF16)
    wukv = mla_w_ukv.reshape(depth, MLA_KV_RANK, N_HEADS, 2 * HEAD_DIM)
    wuk = _pad_heads(wukv[..., :MLA_NOPE].reshape(depth, MLA_KV_RANK, N_HEADS * MLA_NOPE), MLA_NOPE).astype(BF16)
    pad_rows = lambda m: jnp.pad(m.reshape(m.shape[:-2] + (N_HEADS, HEAD_DIM, m.shape[-1])),
                                 [(0, 0)] * (m.ndim - 1) + [(0, VT_ROWS - HEAD_DIM), (0, 0)]
                                 ).reshape(m.shape[:-2] + (N_HEADS * VT_ROWS, m.shape[-1]))
    wuv_t = pad_rows(jnp.swapaxes(wukv[..., MLA_NOPE:].reshape(depth, MLA_KV_RANK, GROUP_W), 1, 2)).astype(BF16)
    eye = pad_rows(jnp.eye(GROUP_W, dtype=F32)).astype(BF16)
    d_qk = MLA_NOPE + MLA_ROPE
    half = MLA_ROPE // 2
    swap_halves = lambda g: jnp.concatenate(
        [jnp.zeros_like(g[:, :MLA_NOPE]), g[:, MLA_NOPE + half:], g[:, MLA_NOPE:MLA_NOPE + half]], axis=-1)
    mla_gq_all = mla_g_q * (mla_scale * d_qk ** 0.5)
    mla_gk_all = mla_g_k * d_qk ** 0.5
    mla_gq, mla_gq_sw = _pad_cols(mla_gq_all, HEAD_PAD), _pad_cols(swap_halves(mla_gq_all), HEAD_PAD)
    mla_gk, mla_gk_sw = _pad_cols(mla_gk_all, HEAD_PAD), _pad_cols(swap_halves(mla_gk_all), HEAD_PAD)

    eye_h = jnp.eye(N_HEADS, dtype=F32)
    wr_bd = jnp.einsum("lncd,nm->lncmd", lru_w_r, eye_h).reshape(depth, GROUP_W, GROUP_W).astype(BF16)
    wi_bd = jnp.einsum("lncd,nm->lncmd", lru_w_i, eye_h).reshape(depth, GROUP_W, GROUP_W).astype(BF16)
    conv_w = jnp.pad(lru_conv_w, ((0, 0), (0, 8 - CONV_W), (0, 0)))

    inv_freq = ROPE_THETA ** (-jnp.arange(half, dtype=F32) / half)
    invf_row = jnp.pad(jnp.concatenate([inv_freq, inv_freq]), (MLA_NOPE, LANES - MLA_NOPE - MLA_ROPE))[None, :]
    cos_t, sin_t = _rope_tables(positions.reshape(n, 1), invf_row)

    xf = x.reshape(n, D_MODEL)
    for l in range(depth):
        proj, small = _norm_inproj(xf, norm_g[l][None, :], w_main, l)
        ya = _gla(proj, small, wup_hi[l], wup_lo[l], gla_b_up[l][None, :],
                  jnp.tile(gla_g_out[l], N_HEADS)[None, :], ones64, gla_sel, batch, seq)
        fq, fk, fvt = _fox_prep(proj, small, fox_bf[l][None, :], fox_gq[l][None, :], fox_gk[l][None, :], ones64, eye,
                                place, aug_q, batch, seq)
        yb = _attention(fq, fk, fvt, batch, seq)
        mq, mk, mvt = _mla_prep(proj, cos_t, sin_t, mla_g_cq[l][None, :], wuq[l], wuq_sw[l], mla_g_ckv[l][None, :],
                                wuk[l], wuv_t[l], mla_gq[l][None, :], mla_gq_sw[l][None, :], mla_gk[l][None, :],
                                mla_gk_sw[l][None, :], ones128, batch, seq)
        yc = _attention(mq, mk, mvt, batch, seq)
        yd = _lru(proj, conv_w[l], lru_conv_b[l][None, :], wr_bd[l], lru_b_r[l][None, :], wi_bd[l],
                  lru_b_i[l][None, :], lru_lam[l][None, :], batch, seq)
        xf = _gate_outproj(ya, yb, yc, yd, proj, w_out_b, xf, l)
    return xf.reshape(batch, seq, D_MODEL)
```

```python
import functools
import math

import numpy as np
import jax
import jax.numpy as jnp
from jax import lax
from jax.experimental import pallas as pl
from jax.experimental.pallas import tpu as pltpu

F32 = jnp.float32
BF16 = jnp.bfloat16

D_MODEL = 1024
D_MIX = 2048
GROUP_W = 512
N_HEADS = 8
HEAD_DIM = 64
EPS = 1e-6
GLA_DK = 32
GLA_RANK = 16
GLA_NORMALIZER = 16.0
GLA_CHUNK = 64
GLA_LEVELS = 6
GLA_SAFE_LOG_DECAY = 40.0
MLA_Q_RANK = 256
MLA_KV_RANK = 128
MLA_NOPE = 64
MLA_ROPE = 32
ROPE_THETA = 10000.0
CONV_W = 4
LRU_C = 8.0
LOG2E = 1.4426950408889634
NEG_BIG = -1e30
EXPM1_SERIES_RANGE = 1.0 / 64.0

LANES = 128
HEAD_PAD = 128
VT_ROWS = 80

COL_GATE = 0
COL_AQ, COL_AK, COL_AV = 2048, 2304, 2560
COL_BQ, COL_BK, COL_BV = 3072, 3584, 4096
COL_CQ, COL_CKV, COL_CKR = 4608, 4864, 4992
COL_DX = 5120
D_PROJ = 5632
SM_AG, SM_BF = 32, 48
PIECE_STRIDE = 16


def _cparams(sem, vmem_mb=48):
    return pltpu.CompilerParams(dimension_semantics=sem, vmem_limit_bytes=vmem_mb * 1024 * 1024)


def _sigmoid(z):
    return 0.5 * jnp.tanh(0.5 * z) + 0.5


def _with_ones_rows(vt):
    row = lax.broadcasted_iota(jnp.int32, vt.shape, 0) % VT_ROWS
    return jnp.where(row == HEAD_DIM, 1.0, vt)


def _neg_expm1(z):
    series = -z * (1.0 + z * (0.5 + z * (1.0 / 6.0 + z * (1.0 / 24.0))))
    return jnp.where(z > -EXPM1_SERIES_RANGE, series, 1.0 - jnp.exp(z))


def _log_sigmoid(z):
    return jnp.minimum(z, 0.0) - jnp.log1p(jnp.exp(-jnp.abs(z)))


def _row_prefix_sum(v, seg):
    rows, width = v.shape
    g = v.reshape(rows // 8, 8, width)
    sub = lax.broadcasted_iota(jnp.int32, g.shape, 1)
    for s in (1, 2, 4):
        g = g + jnp.where(sub >= s, pltpu.roll(g, s, 1), 0.0)
    per_seg = seg // 8
    g = g.reshape(rows // seg, per_seg, 8, width)
    groups = [g[:, 0]]
    for i in range(1, per_seg):
        groups.append(g[:, i] + groups[-1][:, 7:8, :])
    return jnp.stack(groups, axis=1).reshape(rows, width)


def _inproj_kernel(x_ref, g_ref, w_ref, o_ref, os_ref, h_ref, *, small_col):
    @pl.when(pl.program_id(1) == 0)
    def _():
        x = x_ref[...]
        ms = jnp.mean(x * x, axis=-1, keepdims=True)
        h_ref[...] = (x * lax.rsqrt(ms + EPS) * g_ref[...]).astype(BF16)

    res = jnp.dot(h_ref[...], w_ref[...], preferred_element_type=F32)
    o_ref[...] = res.astype(BF16)

    @pl.when(pl.program_id(1) == pl.num_programs(1) - 1)
    def _():
        os_ref[...] = res[:, small_col:small_col + LANES]


def _norm_inproj(x, g, w_all, layer):
    n = x.shape[0]
    tm = min(1024, n)
    tn = D_PROJ // 2
    assert COL_CKR >= D_PROJ - tn
    return pl.pallas_call(
        functools.partial(_inproj_kernel, small_col=COL_CKR - (D_PROJ - tn)),
        grid=(n // tm, D_PROJ // tn),
        in_specs=[
            pl.BlockSpec((tm, D_MODEL), lambda i, j: (i, 0)),
            pl.BlockSpec((1, D_MODEL), lambda i, j: (0, 0)),
            pl.BlockSpec((None, D_MODEL, tn), lambda i, j: (layer, 0, j)),
        ],
        out_specs=[
            pl.BlockSpec((tm, tn), lambda i, j: (i, j)),
            pl.BlockSpec((tm, LANES), lambda i, j: (i, 0)),
        ],
        out_shape=[jax.ShapeDtypeStruct((n, D_PROJ), BF16), jax.ShapeDtypeStruct((n, LANES), F32)],
        scratch_shapes=[pltpu.VMEM((tm, D_MODEL), BF16)],
        compiler_params=_cparams(("parallel", "arbitrary"), vmem_mb=56),
        name="norm_inproj",
    )(x, g, w_all)


def _outproj_kernel(ya_ref, yb_ref, yc_ref, yd_ref, gate_ref, w_ref, x_ref, o_ref, z_ref):
    for gi, y_ref in enumerate((ya_ref, yb_ref, yc_ref, yd_ref)):
        lo, hi = gi * GROUP_W, (gi + 1) * GROUP_W
        g = gate_ref[:, lo:hi].astype(F32)
        z_ref[:, lo:hi] = (y_ref[...].astype(F32) * (g * _sigmoid(g))).astype(BF16)
    o_ref[...] = x_ref[...] + jnp.dot(z_ref[...], w_ref[...], preferred_element_type=F32)


def _gate_outproj(ya, yb, yc, yd, proj, w_all, x, layer):
    n = x.shape[0]
    tm = min(1024, n)
    yspec = pl.BlockSpec((tm, GROUP_W), lambda i: (i, 0))
    return pl.pallas_call(
        _outproj_kernel,
        grid=(n // tm,),
        in_specs=[
            yspec, yspec, yspec, yspec,
            pl.BlockSpec((tm, D_MIX), lambda i: (i, 0)),
            pl.BlockSpec((None, D_MIX, D_MODEL), lambda i: (layer, 0, 0)),
            pl.BlockSpec((tm, D_MODEL), lambda i: (i, 0)),
        ],
        out_specs=pl.BlockSpec((tm, D_MODEL), lambda i: (i, 0)),
        out_shape=jax.ShapeDtypeStruct((n, D_MODEL), F32),
        scratch_shapes=[pltpu.VMEM((tm, D_MIX), BF16)],
        compiler_params=_cparams(("parallel",)),
        name="gate_outproj",
    )(ya, yb, yc, yd, proj, w_all, x)


def _attn_kernel(q_ref, k_ref, vt_ref, o_ref, m_ref, acc_ref, st_ref, mx_ref, *, tq, nq, hg):
    i = pl.program_id(2)
    half = tq // 2
    nt_dims = (((1,), (1,)), ((), ()))

    m_ref[...] = jnp.full(m_ref.shape, NEG_BIG, F32)
    acc_ref[...] = jnp.zeros(acc_ref.shape, F32)

    def scores(h, krows, qrows):
        lanes = slice(h * HEAD_PAD, (h + 1) * HEAD_PAD)
        return lax.dot_general(k_ref[krows, lanes], q_ref[qrows, lanes], nt_dims, preferred_element_type=F32)

    def vrows(h):
        return slice(h * VT_ROWS, (h + 1) * VT_ROWS)

    tri = lax.broadcasted_iota(jnp.int32, (half, half), 0) <= lax.broadcasted_iota(jnp.int32, (half, half), 1)
    low = lax.broadcasted_iota(jnp.int32, (tq, half), 0) <= lax.broadcasted_iota(jnp.int32, (tq, half), 1) + half

    def issue_scores(item, c):
        h, blk = item
        if blk is None:
            off = c * tq
            s_l = jnp.where(tri, scores(h, slice(off, off + half), slice(0, half)), NEG_BIG)
            s_r = jnp.where(low, scores(h, slice(off, off + tq), slice(half, tq)), NEG_BIG)
            mx = jnp.concatenate([jnp.max(s_l, axis=0, keepdims=True), jnp.max(s_r, axis=0, keepdims=True)], axis=1)
            return (s_l, s_r, mx)
        st = scores(h, slice(blk * tq, (blk + 1) * tq), slice(0, tq))
        return (st, jnp.max(st, axis=0, keepdims=True))

    def store_scores(item, slot, vals):
        if item[1] is None:
            st_ref[slot, 0:half, 0:half] = vals[0]
            st_ref[slot, :, half:tq] = vals[1]
        else:
            st_ref[slot] = vals[0]
        mx_ref[slot] = vals[-1]

    def consume(item, slot, c):
        h, blk = item
        m = m_ref[h]
        m_new = jnp.maximum(m, mx_ref[slot])
        alpha = jnp.exp2(m - m_new)
        m_ref[h] = m_new

        def accumulate(cols, st, vt):
            p = jnp.exp2(st - m_new[:, cols]).astype(BF16)
            acc_ref[h, :, cols] = alpha[:, cols] * acc_ref[h, :, cols] + jnp.dot(vt, p, preferred_element_type=F32)

        if blk is None:
            off = c * tq
            accumulate(slice(0, half), st_ref[slot, 0:half, 0:half], vt_ref[vrows(h), off:off + half])
            accumulate(slice(half, tq), st_ref[slot, :, half:tq], vt_ref[vrows(h), off:off + tq])
        else:
            accumulate(slice(0, tq), st_ref[slot], vt_ref[vrows(h), blk * tq:(blk + 1) * tq])

    for c in range(nq):
        @pl.when(i == c)
        def _(c=c):
            items = [(h, None) for h in range(hg)] + [(h, blk) for blk in range(c) for h in range(hg)]
            store_scores(items[0], 0, issue_scores(items[0], c))
            for n, item in enumerate(items):
                nxt = items[n + 1] if n + 1 < len(items) else None
                if nxt is not None:
                    vals = issue_scores(nxt, c)
                consume(item, n % 2, c)
                if nxt is not None:
                    store_scores(nxt, (n + 1) % 2, vals)

    ot = jnp.concatenate([acc_ref[h, 0:HEAD_DIM, :] / acc_ref[h, HEAD_DIM:HEAD_DIM + 1, :] for h in range(hg)], axis=0)
    o_ref[...] = ot.T.astype(BF16)


ATTN_HEADS_PER_STEP = 8


def _attention(q, k, vt, batch, seq):
    n = q.shape[0]
    tq = min(512, seq)
    nq = seq // tq
    hg = ATTN_HEADS_PER_STEP
    return pl.pallas_call(
        functools.partial(_attn_kernel, tq=tq, nq=nq, hg=hg),
        grid=(batch, N_HEADS // hg, nq),
        in_specs=[
            pl.BlockSpec((tq, hg * HEAD_PAD), lambda b, p, i: (b * nq + i, p)),
            pl.BlockSpec((seq, hg * HEAD_PAD), lambda b, p, i: (b, p)),
            pl.BlockSpec((None, hg * VT_ROWS, seq), lambda b, p, i: (b, p, 0)),
        ],
        out_specs=pl.BlockSpec((tq, hg * HEAD_DIM), lambda b, p, i: (b * nq + i, p)),
        out_shape=jax.ShapeDtypeStruct((n, GROUP_W), BF16),
        scratch_shapes=[pltpu.VMEM((hg, 1, tq), F32), pltpu.VMEM((hg, VT_ROWS, tq), F32),
                        pltpu.VMEM((2, tq, tq), F32), pltpu.VMEM((2, 1, tq), F32)],
        compiler_params=_cparams(("parallel", "parallel", "arbitrary")),
        name="causal_attention",
    )(q, k, vt)


def _fox_prep_kernel(q_ref, k_ref, v_ref, sm_ref, bf_ref, gq_ref, gk_ref, ones_ref, eye_ref, place_ref, augq_ref,
                     qo_ref, ko_ref, vt_ref, carry_ref, *, tt):
    @pl.when(pl.program_id(1) == 0)
    def _():
        carry_ref[...] = jnp.zeros_like(carry_ref)

    vt = lax.dot_general(eye_ref[...], v_ref[...], (((1,), (1,)), ((), ())), preferred_element_type=F32)
    vt_ref[...] = _with_ones_rows(vt).astype(BF16)

    def headnorm(ref, g_ref):
        z = ref[...].astype(F32)
        ssq = jnp.dot((z * z).astype(BF16), ones_ref[...], preferred_element_type=F32)
        return z * lax.rsqrt(ssq + HEAD_DIM * EPS) * g_ref[...]

    qn = headnorm(q_ref, gq_ref)
    kn = headnorm(k_ref, gk_ref)

    log_f = _log_sigmoid(sm_ref[...] + bf_ref[...])
    cum = _row_prefix_sum(log_f, tt) + carry_ref[0:1, :]
    carry_ref[0:1, :] = cum[tt - 1:tt, :]
    neg = cum * (-LOG2E)

    lane = lax.broadcasted_iota(jnp.int32, (tt, LANES), 1)
    hi = neg.astype(BF16).astype(F32)
    r1 = neg - hi
    mid = r1.astype(BF16).astype(F32)
    lo = r1 - mid
    pieces = jnp.where(lane < SM_BF + PIECE_STRIDE, hi,
                       jnp.where(lane < SM_BF + 2 * PIECE_STRIDE, pltpu.roll(mid, PIECE_STRIDE, 1),
                                 pltpu.roll(lo, 2 * PIECE_STRIDE, 1)))
    aug_k = jnp.dot(pieces.astype(BF16), place_ref[...], preferred_element_type=F32)
    aug_q = augq_ref[...]
    for h in range(N_HEADS):
        pair, e = h // 2, h % 2
        in_head = (lane >= e * HEAD_DIM) & (lane < (e + 1) * HEAD_DIM)
        blk = slice(h * HEAD_PAD, (h + 1) * HEAD_PAD)
        kpair = kn[:, pair * LANES:(pair + 1) * LANES]
        qpair = qn[:, pair * LANES:(pair + 1) * LANES]
        ko_ref[:, blk] = jnp.where(in_head, kpair, aug_k[:, blk]).astype(BF16)
        qo_ref[:, blk] = jnp.where(in_head, qpair, aug_q[:, blk]).astype(BF16)


def _fox_prep(proj, small, bf_row, gq, gk, ones64, eye, place, aug_q, batch, seq):
    n = proj.shape[0]
    tt = min(1024, seq)
    nt = seq // tt
    row = lambda shape: pl.BlockSpec(shape, lambda b, t: (0, 0))
    return pl.pallas_call(
        functools.partial(_fox_prep_kernel, tt=tt),
        grid=(batch, nt),
        in_specs=[
            pl.BlockSpec((tt, GROUP_W), lambda b, t: (b * nt + t, COL_BQ // GROUP_W)),
            pl.BlockSpec((tt, GROUP_W), lambda b, t: (b * nt + t, COL_BK // GROUP_W)),
            pl.BlockSpec((tt, GROUP_W), lambda b, t: (b * nt + t, COL_BV // GROUP_W)),
            pl.BlockSpec((tt, LANES), lambda b, t: (b * nt + t, 0)),
            row((1, LANES)), row((1, GROUP_W)), row((1, GROUP_W)), row((GROUP_W, GROUP_W)),
            row((N_HEADS * VT_ROWS, GROUP_W)), row((LANES, N_HEADS * HEAD_PAD)), row((1, N_HEADS * HEAD_PAD)),
        ],
        out_specs=[
            pl.BlockSpec((tt, N_HEADS * HEAD_PAD), lambda b, t: (b * nt + t, 0)),
            pl.BlockSpec((tt, N_HEADS * HEAD_PAD), lambda b, t: (b * nt + t, 0)),
            pl.BlockSpec((None, N_HEADS * VT_ROWS, tt), lambda b, t: (b, 0, t)),
        ],
        out_shape=[jax.ShapeDtypeStruct((n, N_HEADS * HEAD_PAD), BF16)] * 2
        + [jax.ShapeDtypeStruct((batch, N_HEADS * VT_ROWS, seq), BF16)],
        scratch_shapes=[pltpu.VMEM((8, LANES), F32)],
        compiler_params=_cparams(("parallel", "arbitrary")),
        name="fox_prep",
    )(proj, proj, proj, small, bf_row, gq, gk, ones64, eye, place, aug_q)


def _rope_kernel(pos_ref, invf_ref, c_ref, s_ref):
    ang = pos_ref[...].astype(F32) * invf_ref[...]
    lane = lax.broadcasted_iota(jnp.int32, ang.shape, 1)
    half = MLA_ROPE // 2
    c, s = jnp.cos(ang), jnp.sin(ang)
    c_ref[...] = jnp.where((lane >= MLA_NOPE) & (lane < MLA_NOPE + MLA_ROPE), c, 1.0)
    s_ref[...] = jnp.where((lane >= MLA_NOPE) & (lane < MLA_NOPE + half), -s,
                           jnp.where((lane >= MLA_NOPE + half) & (lane < MLA_NOPE + MLA_ROPE), s, 0.0))


def _rope_tables(pos_col, invf_row):
    n = pos_col.shape[0]
    tm = min(1024, n)
    out = pl.BlockSpec((tm, LANES), lambda i: (i, 0))
    return pl.pallas_call(
        _rope_kernel,
        grid=(n // tm,),
        in_specs=[pl.BlockSpec((tm, 1), lambda i: (i, 0)), pl.BlockSpec((1, LANES), lambda i: (0, 0))],
        out_specs=[out, out],
        out_shape=[jax.ShapeDtypeStruct((n, LANES), F32)] * 2,
        compiler_params=_cparams(("parallel",)),
        name="rope_tables",
    )(pos_col, invf_row)


def _mla_prep_kernel(cq_ref, ckv_ref, ckr_ref, c_ref, s_ref, gcq_ref, wuq_ref, wuqs_ref, gckv_ref, wuk_ref, wuv_ref,
                     gq_ref, gqs_ref, gk_ref, gks_ref, ones_ref, qo_ref, ko_ref, vo_ref):
    d_qk = float(MLA_NOPE + MLA_ROPE)
    cos, sin = c_ref[...], s_ref[...]
    half = MLA_ROPE // 2

    def rms(z, g):
        return (z * lax.rsqrt(jnp.mean(z * z, axis=-1, keepdims=True) + EPS) * g).astype(BF16)

    def head_ssq(z):
        zz = (z * z).astype(BF16)
        parts = [jnp.dot(zz[:, c * 256:(c + 1) * 256], ones_ref[...], preferred_element_type=F32) for c in range(4)]
        return jnp.concatenate(parts, axis=1)

    cn = rms(cq_ref[...].astype(F32), gcq_ref[...])
    q = jnp.dot(cn, wuq_ref[...], preferred_element_type=F32)
    q_sw = jnp.dot(cn, wuqs_ref[...], preferred_element_type=F32)
    q_inv = lax.rsqrt(head_ssq(q) + d_qk * EPS)

    kvn = rms(ckv_ref[...].astype(F32), gckv_ref[...])
    kn = jnp.dot(kvn, wuk_ref[...], preferred_element_type=F32)
    vt = lax.dot_general(wuv_ref[...], kvn, (((1,), (1,)), ((), ())), preferred_element_type=F32)
    vo_ref[...] = _with_ones_rows(vt).astype(BF16)
    lane = lax.broadcasted_iota(jnp.int32, ckr_ref.shape, 1)
    kr = jnp.where(lane < MLA_ROPE, ckr_ref[...].astype(F32), 0.0)
    ss_r = jnp.sum(kr * kr, axis=-1, keepdims=True)
    kr_placed = pltpu.roll(kr, MLA_NOPE, 1)
    kr_swapped = jnp.where(lane < MLA_NOPE + half, pltpu.roll(kr, MLA_NOPE - half, 1), pltpu.roll(kr, MLA_NOPE + half, 1))
    kr_swapped = jnp.where((lane >= MLA_NOPE) & (lane < MLA_NOPE + MLA_ROPE), kr_swapped, 0.0)
    k_inv = lax.rsqrt(head_ssq(kn) + ss_r + d_qk * EPS)

    gc_q, gs_q = gq_ref[...] * cos, gqs_ref[...] * sin
    gc_k = gk_ref[...] * cos
    kr_rot = kr_placed * gc_k + kr_swapped * (gks_ref[...] * sin)
    for h in range(N_HEADS):
        sl = slice(h * HEAD_PAD, (h + 1) * HEAD_PAD)
        qo_ref[:, sl] = (q_inv[:, sl] * (q[:, sl] * gc_q + q_sw[:, sl] * gs_q)).astype(BF16)
        ko_ref[:, sl] = (k_inv[:, sl] * (kn[:, sl] * gc_k + kr_rot)).astype(BF16)


def _mla_prep(proj, cos_t, sin_t, gcq, wuq, wuq_sw, gckv, wuk, wuv_t, gq, gq_sw, gk, gk_sw, ones128, batch, seq):
    n = proj.shape[0]
    tm = min(1024, seq)
    nt = seq // tm
    full = lambda shape: pl.BlockSpec(shape, lambda i: (0, 0))
    tab = pl.BlockSpec((tm, LANES), lambda i: (i, 0))
    wide = pl.BlockSpec((tm, N_HEADS * HEAD_PAD), lambda i: (i, 0))
    return pl.pallas_call(
        _mla_prep_kernel,
        grid=(n // tm,),
        in_specs=[
            pl.BlockSpec((tm, MLA_Q_RANK), lambda i: (i, COL_CQ // MLA_Q_RANK)),
            pl.BlockSpec((tm, MLA_KV_RANK), lambda i: (i, COL_CKV // LANES)),
            pl.BlockSpec((tm, LANES), lambda i: (i, COL_CKR // LANES)),
            tab, tab,
            full((1, MLA_Q_RANK)), full((MLA_Q_RANK, N_HEADS * HEAD_PAD)), full((MLA_Q_RANK, N_HEADS * HEAD_PAD)),
            full((1, MLA_KV_RANK)), full((MLA_KV_RANK, N_HEADS * HEAD_PAD)), full((N_HEADS * VT_ROWS, MLA_KV_RANK)),
            full((1, HEAD_PAD)), full((1, HEAD_PAD)), full((1, HEAD_PAD)), full((1, HEAD_PAD)), full((256, 256)),
        ],
        out_specs=[wide, wide, pl.BlockSpec((None, N_HEADS * VT_ROWS, tm), lambda i: (i // nt, 0, i % nt))],
        out_shape=[jax.ShapeDtypeStruct((n, N_HEADS * HEAD_PAD), BF16)] * 2
        + [jax.ShapeDtypeStruct((batch, N_HEADS * VT_ROWS, seq), BF16)],
        compiler_params=_cparams(("parallel",)),
        name="mla_prep",
    )(proj, proj, proj, cos_t, sin_t, gcq, wuq, wuq_sw, gckv, wuk, wuv_t, gq, gq_sw, gk, gk_sw, ones128)


def _gla_kernel(q_ref, k_ref, v_ref, sm_ref, wh_ref, wl_ref, b_ref, go_ref, ones_ref, sel_ref, o_ref, st_ref,
                inter_ref, *, tc):
    hk = N_HEADS * GLA_DK
    ck = GLA_CHUNK

    @pl.when(pl.program_id(1) == 0)
    def _():
        st_ref[...] = jnp.zeros_like(st_ref)

    sm = sm_ref[...]
    sm_hi = sm.astype(BF16)
    sm_lo = (sm - sm_hi.astype(F32)).astype(BF16)
    z = (jnp.dot(sm_hi, wh_ref[...], preferred_element_type=F32)
         + jnp.dot(sm_lo, wh_ref[...], preferred_element_type=F32)
         + jnp.dot(sm_hi, wl_ref[...], preferred_element_type=F32)) + b_ref[...]
    g_all = _row_prefix_sum(_log_sigmoid(z) * (1.0 / GLA_NORMALIZER), ck)

    lane_k = lax.broadcasted_iota(jnp.int32, (ck, hk), 1) // GLA_DK
    lane_pair = lax.broadcasted_iota(jnp.int32, (ck, LANES), 1)
    a_row = lax.broadcasted_iota(jnp.int32, (N_HEADS * ck, ck), 0) % ck
    a_col = lax.broadcasted_iota(jnp.int32, (N_HEADS * ck, ck), 1)
    st_blockdiag = (lax.broadcasted_iota(jnp.int32, (GROUP_W, hk), 0) // HEAD_DIM
                    == lax.broadcasted_iota(jnp.int32, (GROUP_W, hk), 1) // GLA_DK)

    def stack_heads(x):
        return jnp.concatenate([jnp.where(lane_k == h, x, jnp.zeros_like(x)) for h in range(N_HEADS)], axis=0)

    def intra_output(a, v):
        parts = []
        for p in range(N_HEADS // 2):
            r = jnp.dot(a[2 * p * ck:2 * (p + 1) * ck, :], v[:, p * LANES:(p + 1) * LANES], preferred_element_type=F32)
            parts.append(jnp.where(lane_pair < HEAD_DIM, r[0:ck, :], r[ck:2 * ck, :]))
        return jnp.concatenate(parts, axis=1)

    def write_normed(o):
        ssq = jnp.dot((o * o).astype(BF16), ones_ref[...], preferred_element_type=F32)
        o_ref[...] = (o * lax.rsqrt(ssq * (1.0 / HEAD_DIM) + EPS) * go_ref[...]).astype(BF16)

    n_chunks = tc // ck
    qts, intras, incs, decays = [], [], [], []
    for c in range(n_chunks):
        rows = slice(c * ck, (c + 1) * ck)
        g = g_all[rows]
        q = q_ref[rows, :].astype(F32) * (GLA_DK ** -0.5)
        k = k_ref[rows, :].astype(F32)
        v = v_ref[rows, :]
        g_last = g[ck - 1:ck, :]
        qt = (q * jnp.exp(g)).astype(BF16)
        kt = (k * jnp.exp(-g)).astype(BF16)
        kh = (k * jnp.exp(g_last - g)).astype(BF16)

        u = lax.dot_general(v, kh, (((0,), (0,)), ((), ())), preferred_element_type=F32)
        incs.append(jnp.where(st_blockdiag, u, 0.0))
        decays.append(jnp.exp(g_last))

        a = lax.dot_general(stack_heads(qt), kt, (((1,), (1,)), ((), ())), preferred_element_type=F32)
        intras.append(intra_output(jnp.where(a_row >= a_col, a, 0.0).astype(BF16), v))
        qts.append(qt)

    state = st_ref[...]
    states = []
    for c in range(n_chunks):
        states.append(state.astype(BF16))
        state = state * decays[c] + incs[c]
    st_ref[...] = state

    outs = []
    for c in range(n_chunks):
        o = lax.dot_general(qts[c], states[c], (((1,), (1,)), ((), ())), preferred_element_type=F32)
        inter_ref[c * ck:(c + 1) * ck, :] = o
        outs.append(o + intras[c])
    write_normed(jnp.concatenate(outs, axis=0))

    @pl.when(jnp.min(g_all) < -GLA_SAFE_LOG_DECAY)
    def _():
        t_idx = lax.broadcasted_iota(jnp.int32, (ck, hk), 0)
        nt_dims = (((1,), (1,)), ((), ()))
        redo = []
        for c in range(n_chunks):
            rows = slice(c * ck, (c + 1) * ck)
            g = g_all[rows]
            q = q_ref[rows, :].astype(F32) * (GLA_DK ** -0.5)
            k = k_ref[rows, :].astype(F32)
            g_hi = g.astype(BF16)
            g_r1 = g - g_hi.astype(F32)
            g_mid = g_r1.astype(BF16)
            g_lo = (g_r1 - g_mid.astype(F32)).astype(BF16)
            g3 = jnp.concatenate([g_hi, g_mid, g_lo], axis=0)
            a = lax.dot_general(stack_heads(q.astype(BF16)), k.astype(BF16), nt_dims, preferred_element_type=F32)
            a = jnp.where(a_row == a_col, a, 0.0)
            for lvl in range(GLA_LEVELS):
                half = ck >> (lvl + 1)
                g_mid_row = jnp.dot(sel_ref[lvl], g3, preferred_element_type=F32)
                d = g - g_mid_row
                second = (t_idx // half) % 2 == 1
                qd = jnp.where(second, q * jnp.exp(jnp.minimum(d, 0.0)), 0.0).astype(BF16)
                kd = jnp.where(second, 0.0, k * jnp.exp(jnp.minimum(-d, 0.0))).astype(BF16)
                part = lax.dot_general(stack_heads(qd), kd, nt_dims, preferred_element_type=F32)
                a = a + jnp.where(a_row // (2 * half) == a_col // (2 * half), part, 0.0)
            redo.append(inter_ref[rows, :] + intra_output(a.astype(BF16), v_ref[rows, :]))
        write_normed(jnp.concatenate(redo, axis=0))


def _gla(proj, small, w_hi, w_lo, b_up, g_out, ones64, sel, batch, seq):
    n = proj.shape[0]
    tc = min(512, seq)
    nt = seq // tc
    hk = N_HEADS * GLA_DK
    full = lambda shape: pl.BlockSpec(shape, lambda b, t: (0, 0))
    return pl.pallas_call(
        functools.partial(_gla_kernel, tc=tc),
        grid=(batch, nt),
        in_specs=[
            pl.BlockSpec((tc, hk), lambda b, t: (b * nt + t, COL_AQ // hk)),
            pl.BlockSpec((tc, hk), lambda b, t: (b * nt + t, COL_AK // hk)),
            pl.BlockSpec((tc, GROUP_W), lambda b, t: (b * nt + t, COL_AV // GROUP_W)),
            pl.BlockSpec((tc, LANES), lambda b, t: (b * nt + t, 0)),
            full((LANES, hk)), full((LANES, hk)), full((1, hk)), full((1, GROUP_W)), full((GROUP_W, GROUP_W)),
            pl.BlockSpec((GLA_LEVELS, GLA_CHUNK, 3 * GLA_CHUNK), lambda b, t: (0, 0, 0)),
        ],
        out_specs=pl.BlockSpec((tc, GROUP_W), lambda b, t: (b * nt + t, 0)),
        out_shape=jax.ShapeDtypeStruct((n, GROUP_W), BF16),
        scratch_shapes=[pltpu.VMEM((GROUP_W, hk), F32), pltpu.VMEM((tc, GROUP_W), F32)],
        compiler_params=_cparams(("parallel", "arbitrary")),
        name="gla",
    )(proj, proj, proj, small, w_hi, w_lo, b_up, g_out, ones64, sel)


def _lru_kernel(x_ref, cw_ref, cb_ref, wr_ref, br_ref, wi_ref, bi_ref, lam_ref, o_ref, xprev_ref, h_ref, *, tr):
    @pl.when(pl.program_id(1) == 0)
    def _():
        xprev_ref[...] = jnp.zeros_like(xprev_ref)
        h_ref[...] = jnp.zeros_like(h_ref)

    x3 = x_ref[...].astype(F32).reshape(tr // 8, 8, GROUP_W)
    xg = jnp.concatenate([xprev_ref[...][None], x3], axis=0)
    xprev_ref[...] = x3[tr // 8 - 1]
    sub3 = lax.broadcasted_iota(jnp.int32, x3.shape, 1)
    xc3 = cb_ref[...][None] + cw_ref[CONV_W - 1:CONV_W, :][None] * x3
    for s in range(1, CONV_W):
        rot = pltpu.roll(xg, s, 1)
        shifted = jnp.where(sub3 >= s, rot[1:], rot[:-1])
        xc3 = xc3 + cw_ref[CONV_W - 1 - s:CONV_W - s, :][None] * shifted
    xc = xc3.reshape(tr, GROUP_W)

    xb = xc.astype(BF16)
    t_r = jnp.tanh(0.5 * (jnp.dot(xb, wr_ref[...], preferred_element_type=F32) + br_ref[...]))
    t_i = jnp.tanh(0.5 * (jnp.dot(xb, wi_ref[...], preferred_element_type=F32) + bi_ref[...]))
    lam = lam_ref[...]
    softplus = jnp.maximum(-lam, 0.0) + jnp.log1p(jnp.exp(-jnp.abs(lam)))
    half_rate = (-0.5 * LRU_C) * softplus
    log_a = half_rate * t_r + half_rate
    one_minus_a = _neg_expm1(log_a)
    a = 1.0 - one_minus_a
    half_x = 0.5 * xc
    bx = jnp.sqrt(one_minus_a * (2.0 - one_minus_a)) * (half_x * t_i + half_x)

    a3 = a.reshape(tr // 8, 8, GROUP_W)
    b3 = bx.reshape(tr // 8, 8, GROUP_W)
    sub = lax.broadcasted_iota(jnp.int32, a3.shape, 1)
    for s in (1, 2, 4):
        keep = sub >= s
        a_prev = jnp.where(keep, pltpu.roll(a3, s, 1), 1.0)
        b_prev = jnp.where(keep, pltpu.roll(b3, s, 1), 0.0)
        b3 = a3 * b_prev + b3
        a3 = a3 * a_prev
    h_prev = h_ref[0:1, :]
    for g in range(tr // 8):
        hg = a3[g] * h_prev + b3[g]
        o_ref[g * 8:(g + 1) * 8, :] = hg.astype(BF16)
        h_prev = hg[7:8, :]
    h_ref[0:1, :] = h_prev


def _lru(proj, conv_w, conv_b, wr, br, wi, bi, lam, batch, seq):
    n = proj.shape[0]
    tr = min(1024, seq)
    nt = seq // tr
    full = lambda shape: pl.BlockSpec(shape, lambda b, t: (0, 0))
    return pl.pallas_call(
        functools.partial(_lru_kernel, tr=tr),
        grid=(batch, nt),
        in_specs=[
            pl.BlockSpec((tr, GROUP_W), lambda b, t: (b * nt + t, COL_DX // GROUP_W)),
            full((8, GROUP_W)), full((1, GROUP_W)),
            full((GROUP_W, GROUP_W)), full((1, GROUP_W)),
            full((GROUP_W, GROUP_W)), full((1, GROUP_W)), full((1, GROUP_W)),
        ],
        out_specs=pl.BlockSpec((tr, GROUP_W), lambda b, t: (b * nt + t, 0)),
        out_shape=jax.ShapeDtypeStruct((n, GROUP_W), BF16),
        scratch_shapes=[pltpu.VMEM((8, GROUP_W), F32), pltpu.VMEM((8, GROUP_W), F32)],
        compiler_params=_cparams(("parallel", "arbitrary")),
        name="rglru",
    )(proj, conv_w, conv_b, wr, br, wi, bi, lam)


def _pad_cols(w, width):
    return jnp.pad(w, [(0, 0)] * (w.ndim - 1) + [(0, width - w.shape[-1])])


def _block_diag_ones(n_blocks, size):
    return jnp.asarray(np.kron(np.eye(n_blocks, dtype=np.float32), np.ones((size, size), np.float32)), BF16)


def _swap_rotary(w):
    lead = w.shape[:-1]
    w = w.reshape(lead + (N_HEADS, MLA_NOPE + MLA_ROPE))
    half = MLA_ROPE // 2
    z1, z2 = w[..., MLA_NOPE:MLA_NOPE + half], w[..., MLA_NOPE + half:]
    w = jnp.concatenate([jnp.zeros_like(w[..., :MLA_NOPE]), z2, z1], axis=-1)
    return w.reshape(lead + (N_HEADS * (MLA_NOPE + MLA_ROPE),))


def _pad_heads(w, d_real):
    lead = w.shape[:-1]
    w = w.reshape(lead + (N_HEADS, d_real))
    w = jnp.pad(w, [(0, 0)] * len(lead) + [(0, 0), (0, HEAD_PAD - d_real)])
    return w.reshape(lead + (N_HEADS * HEAD_PAD,))


def kernel(x, positions, norm_g, w_in, w_out, gla_w_up, gla_b_up, gla_g_out, fox_b_f, fox_g_q, fox_g_k, mla_g_cq, mla_w_uq, mla_g_ckv, mla_w_ukv, mla_g_q, mla_g_k, lru_conv_w, lru_conv_b, lru_w_r, lru_b_r, lru_w_i, lru_b_i, lru_lam):
    batch, seq, _ = x.shape
    depth = w_in.shape[0]
    n = batch * seq

    sizes = (D_MIX, 256, 256, 512, GLA_RANK, 512, 512, 512, N_HEADS, MLA_Q_RANK, MLA_KV_RANK, MLA_ROPE, GROUP_W)
    offs = np.concatenate([[0], np.cumsum(sizes)])
    seg = [w_in[:, :, offs[i]:offs[i + 1]] for i in range(len(sizes))]
    (w_gate, w_aq, w_ak, w_av, w_ag, w_bq, w_bk, w_bv, w_bf, w_cq, w_ckv, w_ckr, w_dx) = seg
    assert (SM_AG, SM_BF) == (MLA_ROPE, MLA_ROPE + GLA_RANK)
    w_kr_gates = _pad_cols(jnp.concatenate([w_ckr, w_ag, w_bf], axis=-1), LANES)
    w_main = jnp.concatenate(
        [w_gate, w_aq, w_ak, w_av, w_bq, w_bk, w_bv, w_cq, w_ckv, w_kr_gates, w_dx], axis=-1).astype(BF16)
    w_out_b = w_out.astype(BF16)

    ones64 = _block_diag_ones(GROUP_W // HEAD_DIM, HEAD_DIM)
    ones128 = _block_diag_ones(2, HEAD_PAD)

    wup = jnp.pad(gla_w_up, ((0, 0), (SM_AG, LANES - SM_AG - GLA_RANK), (0, 0)))
    wup_hi = wup.astype(BF16)
    wup_lo = (wup - wup_hi.astype(F32)).astype(BF16)

    sel = np.zeros((GLA_LEVELS, GLA_CHUNK, 3 * GLA_CHUNK), np.float32)
    for lvl in range(GLA_LEVELS):
        half = GLA_CHUNK >> (lvl + 1)
        for t in range(GLA_CHUNK):
            mid = (t // (2 * half)) * 2 * half + half - 1
            for piece in range(3):
                sel[lvl, t, piece * GLA_CHUNK + mid] = 1.0
    gla_sel = jnp.asarray(sel, BF16)

    fox_scale = LOG2E
    mla_scale = (MLA_NOPE + MLA_ROPE) ** -0.5 * LOG2E
    fox_gq = jnp.tile(fox_g_q, (1, N_HEADS)) * fox_scale
    fox_gk = jnp.tile(fox_g_k, (1, N_HEADS)) * HEAD_DIM ** 0.5
    place = np.zeros((LANES, N_HEADS * HEAD_PAD), np.float32)
    aug_q = np.zeros((1, N_HEADS * HEAD_PAD), np.float32)
    for h in range(N_HEADS):
        base = h * HEAD_PAD + (HEAD_DIM if h % 2 == 0 else 0)
        for piece in range(3):
            place[SM_BF + piece * PIECE_STRIDE + h, base + piece] = 1.0
            aug_q[0, base + piece] = 1.0
    place, aug_q = jnp.asarray(place, BF16), jnp.asarray(aug_q)
    fox_bf = jnp.pad(fox_b_f, ((0, 0), (SM_BF, LANES - SM_BF - N_HEADS)))

    wuq = _pad_heads(mla_w_uq, MLA_NOPE + MLA_ROPE).astype(BF16)
    wuq_sw = _pad_heads(_swap_rotary(mla_w_uq), MLA_NOPE + MLA_ROPE).astype(BF16)
    wukv = mla_w_ukv.reshape(depth, MLA_KV_RANK, N_HEADS, 2 * HEAD_DIM)
    wuk = _pad_heads(wukv[..., :MLA_NOPE].reshape(depth, MLA_KV_RANK, N_HEADS * MLA_NOPE), MLA_NOPE).astype(BF16)
    pad_rows = lambda m: jnp.pad(m.reshape(m.shape[:-2] + (N_HEADS, HEAD_DIM, m.shape[-1])),
                                 [(0, 0)] * (m.ndim - 1) + [(0, VT_ROWS - HEAD_DIM), (0, 0)]
                                 ).reshape(m.shape[:-2] + (N_HEADS * VT_ROWS, m.shape[-1]))
    wuv_t = pad_rows(jnp.swapaxes(wukv[..., MLA_NOPE:].reshape(depth, MLA_KV_RANK, GROUP_W), 1, 2)).astype(BF16)
    eye = pad_rows(jnp.eye(GROUP_W, dtype=F32)).astype(BF16)
    d_qk = MLA_NOPE + MLA_ROPE
    half = MLA_ROPE // 2
    swap_halves = lambda g: jnp.concatenate(
        [jnp.zeros_like(g[:, :MLA_NOPE]), g[:, MLA_NOPE + half:], g[:, MLA_NOPE:MLA_NOPE + half]], axis=-1)
    mla_gq_all = mla_g_q * (mla_scale * d_qk ** 0.5)
    mla_gk_all = mla_g_k * d_qk ** 0.5
    mla_gq, mla_gq_sw = _pad_cols(mla_gq_all, HEAD_PAD), _pad_cols(swap_halves(mla_gq_all), HEAD_PAD)
    mla_gk, mla_gk_sw = _pad_cols(mla_gk_all, HEAD_PAD), _pad_cols(swap_halves(mla_gk_all), HEAD_PAD)

    eye_h = jnp.eye(N_HEADS, dtype=F32)
    wr_bd = jnp.einsum("lncd,nm->lncmd", lru_w_r, eye_h).reshape(depth, GROUP_W, GROUP_W).astype(BF16)
    wi_bd = jnp.einsum("lncd,nm->lncmd", lru_w_i, eye_h).reshape(depth, GROUP_W, GROUP_W).astype(BF16)
    conv_w = jnp.pad(lru_conv_w, ((0, 0), (0, 8 - CONV_W), (0, 0)))

    inv_freq = ROPE_THETA ** (-jnp.arange(half, dtype=F32) / half)
    invf_row = jnp.pad(jnp.concatenate([inv_freq, inv_freq]), (MLA_NOPE, LANES - MLA_NOPE - MLA_ROPE))[None, :]
    cos_t, sin_t = _rope_tables(positions.reshape(n, 1), invf_row)

    xf = x.reshape(n, D_MODEL)
    for l in range(depth):
        proj, small = _norm_inproj(xf, norm_g[l][None, :], w_main, l)
        ya = _gla(proj, small, wup_hi[l], wup_lo[l], gla_b_up[l][None, :],
                  jnp.tile(gla_g_out[l], N_HEADS)[None, :], ones64, gla_sel, batch, seq)
        fq, fk, fvt = _fox_prep(proj, small, fox_bf[l][None, :], fox_gq[l][None, :], fox_gk[l][None, :], ones64, eye,
                                place, aug_q, batch, seq)
        yb = _attention(fq, fk, fvt, batch, seq)
        mq, mk, mvt = _mla_prep(proj, cos_t, sin_t, mla_g_cq[l][None, :], wuq[l], wuq_sw[l], mla_g_ckv[l][None, :],
                                wuk[l], wuv_t[l], mla_gq[l][None, :], mla_gq_sw[l][None, :], mla_gk[l][None, :],
                                mla_gk_sw[l][None, :], ones128, batch, seq)
        yc = _attention(mq, mk, mvt, batch, seq)
        yd = _lru(proj, conv_w[l], lru_conv_b[l][None, :], wr_bd[l], lru_b_r[l][None, :], wi_bd[l],
                  lru_b_i[l][None, :], lru_lam[l][None, :], batch, seq)
        xf = _gate_outproj(ya, yb, yc, yd, proj, w_out_b, xf, l)
    return xf.reshape(batch, seq, D_MODEL)
```

```python
import functools

import numpy as np
import jax
import jax.numpy as jnp
from jax import lax
from jax.experimental import pallas as pl
from jax.experimental.pallas import tpu as pltpu

F32 = jnp.float32
BF16 = jnp.bfloat16

D_MODEL = 1024
D_MIX = 2048
GROUP_W = 512
N_HEADS = 8
HEAD_DIM = 64
EPS = 1e-6
GLA_DK = 32
GLA_RANK = 16
GLA_NORMALIZER = 16.0
GLA_CHUNK = 64
GLA_LEVELS = 6
GLA_SAFE_LOG_DECAY = 40.0
MLA_Q_RANK = 256
MLA_KV_RANK = 128
MLA_NOPE = 64
MLA_ROPE = 32
ROPE_THETA = 10000.0
CONV_W = 4
LRU_C = 8.0
LOG2E = 1.4426950408889634
NEG_BIG = -1e30
EXPM1_SERIES_RANGE = 1.0 / 64.0

LANES = 128
SUBLANES = 8
MXU_WIDTH = 256
HEAD_PAD = 128
VT_ROWS = 80

COL_AQ, COL_AK, COL_AV = 2048, 2304, 2560
COL_BQ, COL_BK, COL_BV = 3072, 3584, 4096
COL_CQ, COL_CKV, COL_CKR = 4608, 4864, 4992
COL_DX = 5120
D_PROJ = 5632
SM_AG, SM_BF = 32, 48
PIECE_STRIDE = 16


ROWS_INPROJ = 1024
ROWS_OUTPROJ = 1024
ROWS_PREP = 1024
ROWS_GLA = 512
ROWS_LRU = 1024
ATTN_BLOCK = 512
ATTN_HEADS_PER_STEP = 8
VMEM_LIMIT_MB = 48
VMEM_LIMIT_INPROJ_MB = 56


def _cparams(sem, vmem_mb=VMEM_LIMIT_MB):
    return pltpu.CompilerParams(dimension_semantics=sem, vmem_limit_bytes=vmem_mb * 1024 * 1024)


def _sigmoid(z):
    return 0.5 * jnp.tanh(0.5 * z) + 0.5


def _with_ones_rows(vt):
    row = lax.broadcasted_iota(jnp.int32, vt.shape, 0) % VT_ROWS
    return jnp.where(row == HEAD_DIM, 1.0, vt)


def _neg_expm1(z):
    series = -z * (1.0 + z * (0.5 + z * (1.0 / 6.0 + z * (1.0 / 24.0))))
    return jnp.where(z > -EXPM1_SERIES_RANGE, series, 1.0 - jnp.exp(z))


def _log_sigmoid(z):
    return jnp.minimum(z, 0.0) - jnp.log1p(jnp.exp(-jnp.abs(z)))


def _row_prefix_sum(v, seg):
    rows, width = v.shape
    g = v.reshape(rows // SUBLANES, SUBLANES, width)
    sub = lax.broadcasted_iota(jnp.int32, g.shape, 1)
    for s in (1, 2, 4):
        g = g + jnp.where(sub >= s, pltpu.roll(g, s, 1), 0.0)
    per_seg = seg // SUBLANES
    g = g.reshape(rows // seg, per_seg, SUBLANES, width)
    groups = [g[:, 0]]
    for i in range(1, per_seg):
        groups.append(g[:, i] + groups[-1][:, SUBLANES - 1:SUBLANES, :])
    return jnp.stack(groups, axis=1).reshape(rows, width)


def _inproj_kernel(x_ref, g_ref, w_ref, o_ref, os_ref, h_ref, *, small_col):
    @pl.when(pl.program_id(1) == 0)
    def _():
        x = x_ref[...]
        ms = jnp.mean(x * x, axis=-1, keepdims=True)
        h_ref[...] = (x * lax.rsqrt(ms + EPS) * g_ref[...]).astype(BF16)

    res = jnp.dot(h_ref[...], w_ref[...], preferred_element_type=F32)
    o_ref[...] = res.astype(BF16)

    @pl.when(pl.program_id(1) == pl.num_programs(1) - 1)
    def _():
        os_ref[...] = res[:, small_col:small_col + LANES]


def _norm_inproj(x, g, w_all, layer):
    n = x.shape[0]
    tm = min(ROWS_INPROJ, n)
    tn = D_PROJ // 2
    assert COL_CKR >= D_PROJ - tn
    return pl.pallas_call(
        functools.partial(_inproj_kernel, small_col=COL_CKR - (D_PROJ - tn)),
        grid=(n // tm, D_PROJ // tn),
        in_specs=[
            pl.BlockSpec((tm, D_MODEL), lambda i, j: (i, 0)),
            pl.BlockSpec((1, D_MODEL), lambda i, j: (0, 0)),
            pl.BlockSpec((None, D_MODEL, tn), lambda i, j: (layer, 0, j)),
        ],
        out_specs=[
            pl.BlockSpec((tm, tn), lambda i, j: (i, j)),
            pl.BlockSpec((tm, LANES), lambda i, j: (i, 0)),
        ],
        out_shape=[jax.ShapeDtypeStruct((n, D_PROJ), BF16), jax.ShapeDtypeStruct((n, LANES), F32)],
        scratch_shapes=[pltpu.VMEM((tm, D_MODEL), BF16)],
        compiler_params=_cparams(("parallel", "arbitrary"), vmem_mb=VMEM_LIMIT_INPROJ_MB),
        name="norm_inproj",
    )(x, g, w_all)


def _outproj_kernel(ya_ref, yb_ref, yc_ref, yd_ref, gate_ref, w_ref, x_ref, o_ref, z_ref):
    for gi, y_ref in enumerate((ya_ref, yb_ref, yc_ref, yd_ref)):
        lo, hi = gi * GROUP_W, (gi + 1) * GROUP_W
        g = gate_ref[:, lo:hi].astype(F32)
        z_ref[:, lo:hi] = (y_ref[...].astype(F32) * (g * _sigmoid(g))).astype(BF16)
    o_ref[...] = x_ref[...] + jnp.dot(z_ref[...], w_ref[...], preferred_element_type=F32)


def _gate_outproj(ya, yb, yc, yd, proj, w_all, x, layer):
    n = x.shape[0]
    tm = min(ROWS_OUTPROJ, n)
    yspec = pl.BlockSpec((tm, GROUP_W), lambda i: (i, 0))
    return pl.pallas_call(
        _outproj_kernel,
        grid=(n // tm,),
        in_specs=[
            yspec, yspec, yspec, yspec,
            pl.BlockSpec((tm, D_MIX), lambda i: (i, 0)),
            pl.BlockSpec((None, D_MIX, D_MODEL), lambda i: (layer, 0, 0)),
            pl.BlockSpec((tm, D_MODEL), lambda i: (i, 0)),
        ],
        out_specs=pl.BlockSpec((tm, D_MODEL), lambda i: (i, 0)),
        out_shape=jax.ShapeDtypeStruct((n, D_MODEL), F32),
        scratch_shapes=[pltpu.VMEM((tm, D_MIX), BF16)],
        compiler_params=_cparams(("parallel",)),
        name="gate_outproj",
    )(ya, yb, yc, yd, proj, w_all, x)


def _attn_kernel(q_ref, k_ref, vt_ref, o_ref, m_ref, acc_ref, st_ref, mx_ref, *, tq, nq, hg):
    i = pl.program_id(2)
    half = tq // 2
    nt_dims = (((1,), (1,)), ((), ()))

    m_ref[...] = jnp.full(m_ref.shape, NEG_BIG, F32)
    acc_ref[...] = jnp.zeros(acc_ref.shape, F32)

    def scores(h, krows, qrows):
        lanes = slice(h * HEAD_PAD, (h + 1) * HEAD_PAD)
        return lax.dot_general(k_ref[krows, lanes], q_ref[qrows, lanes], nt_dims, preferred_element_type=F32)

    def vrows(h):
        return slice(h * VT_ROWS, (h + 1) * VT_ROWS)

    tri = lax.broadcasted_iota(jnp.int32, (half, half), 0) <= lax.broadcasted_iota(jnp.int32, (half, half), 1)
    low = lax.broadcasted_iota(jnp.int32, (tq, half), 0) <= lax.broadcasted_iota(jnp.int32, (tq, half), 1) + half

    def issue_scores(item, c):
        h, blk = item
        if blk is None:
            off = c * tq
            s_l = jnp.where(tri, scores(h, slice(off, off + half), slice(0, half)), NEG_BIG)
            s_r = jnp.where(low, scores(h, slice(off, off + tq), slice(half, tq)), NEG_BIG)
            mx = jnp.concatenate([jnp.max(s_l, axis=0, keepdims=True), jnp.max(s_r, axis=0, keepdims=True)], axis=1)
            return (s_l, s_r, mx)
        st = scores(h, slice(blk * tq, (blk + 1) * tq), slice(0, tq))
        return (st, jnp.max(st, axis=0, keepdims=True))

    def store_scores(item, slot, vals):
        if item[1] is None:
            st_ref[slot, 0:half, 0:half] = vals[0]
            st_ref[slot, :, half:tq] = vals[1]
        else:
            st_ref[slot] = vals[0]
        mx_ref[slot] = vals[-1]

    def consume(item, slot, c):
        h, blk = item
        m = m_ref[h]
        m_new = jnp.maximum(m, mx_ref[slot])
        alpha = jnp.exp2(m - m_new)
        m_ref[h] = m_new

        def accumulate(cols, st, vt):
            p = jnp.exp2(st - m_new[:, cols]).astype(BF16)
            acc_ref[h, :, cols] = alpha[:, cols] * acc_ref[h, :, cols] + jnp.dot(vt, p, preferred_element_type=F32)

        if blk is None:
            off = c * tq
            accumulate(slice(0, half), st_ref[slot, 0:half, 0:half], vt_ref[vrows(h), off:off + half])
            accumulate(slice(half, tq), st_ref[slot, :, half:tq], vt_ref[vrows(h), off:off + tq])
        else:
            accumulate(slice(0, tq), st_ref[slot], vt_ref[vrows(h), blk * tq:(blk + 1) * tq])

    for c in range(nq):
        @pl.when(i == c)
        def _(c=c):
            items = [(h, None) for h in range(hg)] + [(h, blk) for blk in range(c) for h in range(hg)]
            store_scores(items[0], 0, issue_scores(items[0], c))
            for n, item in enumerate(items):
                nxt = items[n + 1] if n + 1 < len(items) else None
                if nxt is not None:
                    vals = issue_scores(nxt, c)
                consume(item, n % 2, c)
                if nxt is not None:
                    store_scores(nxt, (n + 1) % 2, vals)

    ot = jnp.concatenate([acc_ref[h, 0:HEAD_DIM, :] / acc_ref[h, HEAD_DIM:HEAD_DIM + 1, :] for h in range(hg)], axis=0)
    o_ref[...] = ot.T.astype(BF16)


def _attention(q, k, vt, batch, seq):
    n = q.shape[0]
    tq = min(ATTN_BLOCK, seq)
    nq = seq // tq
    hg = ATTN_HEADS_PER_STEP
    return pl.pallas_call(
        functools.partial(_attn_kernel, tq=tq, nq=nq, hg=hg),
        grid=(batch, N_HEADS // hg, nq),
        in_specs=[
            pl.BlockSpec((tq, hg * HEAD_PAD), lambda b, p, i: (b * nq + i, p)),
            pl.BlockSpec((seq, hg * HEAD_PAD), lambda b, p, i: (b, p)),
            pl.BlockSpec((None, hg * VT_ROWS, seq), lambda b, p, i: (b, p, 0)),
        ],
        out_specs=pl.BlockSpec((tq, hg * HEAD_DIM), lambda b, p, i: (b * nq + i, p)),
        out_shape=jax.ShapeDtypeStruct((n, GROUP_W), BF16),
        scratch_shapes=[pltpu.VMEM((hg, 1, tq), F32), pltpu.VMEM((hg, VT_ROWS, tq), F32),
                        pltpu.VMEM((2, tq, tq), F32), pltpu.VMEM((2, 1, tq), F32)],
        compiler_params=_cparams(("parallel", "parallel", "arbitrary")),
        name="causal_attention",
    )(q, k, vt)


def _fox_prep_kernel(q_ref, k_ref, v_ref, sm_ref, bf_ref, gq_ref, gk_ref, ones_ref, eye_ref, place_ref, augq_ref,
                     qo_ref, ko_ref, vt_ref, carry_ref, *, tt):
    @pl.when(pl.program_id(1) == 0)
    def _():
        carry_ref[...] = jnp.zeros_like(carry_ref)

    vt = lax.dot_general(eye_ref[...], v_ref[...], (((1,), (1,)), ((), ())), preferred_element_type=F32)
    vt_ref[...] = _with_ones_rows(vt).astype(BF16)

    def headnorm(ref, g_ref):
        z = ref[...].astype(F32)
        ssq = jnp.dot((z * z).astype(BF16), ones_ref[...], preferred_element_type=F32)
        return z * lax.rsqrt(ssq + HEAD_DIM * EPS) * g_ref[...]

    qn = headnorm(q_ref, gq_ref)
    kn = headnorm(k_ref, gk_ref)

    log_f = _log_sigmoid(sm_ref[...] + bf_ref[...])
    cum = _row_prefix_sum(log_f, tt) + carry_ref[0:1, :]
    carry_ref[0:1, :] = cum[tt - 1:tt, :]
    neg = cum * (-LOG2E)

    lane = lax.broadcasted_iota(jnp.int32, (tt, LANES), 1)
    hi = neg.astype(BF16).astype(F32)
    r1 = neg - hi
    mid = r1.astype(BF16).astype(F32)
    lo = r1 - mid
    pieces = jnp.where(lane < SM_BF + PIECE_STRIDE, hi,
                       jnp.where(lane < SM_BF + 2 * PIECE_STRIDE, pltpu.roll(mid, PIECE_STRIDE, 1),
                                 pltpu.roll(lo, 2 * PIECE_STRIDE, 1)))
    aug_k = jnp.dot(pieces.astype(BF16), place_ref[...], preferred_element_type=F32)
    aug_q = augq_ref[...]
    for h in range(N_HEADS):
        pair, e = h // 2, h % 2
        in_head = (lane >= e * HEAD_DIM) & (lane < (e + 1) * HEAD_DIM)
        blk = slice(h * HEAD_PAD, (h + 1) * HEAD_PAD)
        kpair = kn[:, pair * LANES:(pair + 1) * LANES]
        qpair = qn[:, pair * LANES:(pair + 1) * LANES]
        ko_ref[:, blk] = jnp.where(in_head, kpair, aug_k[:, blk]).astype(BF16)
        qo_ref[:, blk] = jnp.where(in_head, qpair, aug_q[:, blk]).astype(BF16)


def _fox_prep(proj, small, bf_row, gq, gk, ones64, eye, place, aug_q, batch, seq):
    n = proj.shape[0]
    tt = min(ROWS_PREP, seq)
    nt = seq // tt
    row = lambda shape: pl.BlockSpec(shape, lambda b, t: (0, 0))
    return pl.pallas_call(
        functools.partial(_fox_prep_kernel, tt=tt),
        grid=(batch, nt),
        in_specs=[
            pl.BlockSpec((tt, GROUP_W), lambda b, t: (b * nt + t, COL_BQ // GROUP_W)),
            pl.BlockSpec((tt, GROUP_W), lambda b, t: (b * nt + t, COL_BK // GROUP_W)),
            pl.BlockSpec((tt, GROUP_W), lambda b, t: (b * nt + t, COL_BV // GROUP_W)),
            pl.BlockSpec((tt, LANES), lambda b, t: (b * nt + t, 0)),
            row((1, LANES)), row((1, GROUP_W)), row((1, GROUP_W)), row((GROUP_W, GROUP_W)),
            row((N_HEADS * VT_ROWS, GROUP_W)), row((LANES, N_HEADS * HEAD_PAD)), row((1, N_HEADS * HEAD_PAD)),
        ],
        out_specs=[
            pl.BlockSpec((tt, N_HEADS * HEAD_PAD), lambda b, t: (b * nt + t, 0)),
            pl.BlockSpec((tt, N_HEADS * HEAD_PAD), lambda b, t: (b * nt + t, 0)),
            pl.BlockSpec((None, N_HEADS * VT_ROWS, tt), lambda b, t: (b, 0, t)),
        ],
        out_shape=[jax.ShapeDtypeStruct((n, N_HEADS * HEAD_PAD), BF16)] * 2
        + [jax.ShapeDtypeStruct((batch, N_HEADS * VT_ROWS, seq), BF16)],
        scratch_shapes=[pltpu.VMEM((SUBLANES, LANES), F32)],
        compiler_params=_cparams(("parallel", "arbitrary")),
        name="fox_prep",
    )(proj, proj, proj, small, bf_row, gq, gk, ones64, eye, place, aug_q)


def _rope_kernel(pos_ref, invf_ref, c_ref, s_ref):
    ang = pos_ref[...].astype(F32) * invf_ref[...]
    lane = lax.broadcasted_iota(jnp.int32, ang.shape, 1)
    half = MLA_ROPE // 2
    c, s = jnp.cos(ang), jnp.sin(ang)
    c_ref[...] = jnp.where((lane >= MLA_NOPE) & (lane < MLA_NOPE + MLA_ROPE), c, 1.0)
    s_ref[...] = jnp.where((lane >= MLA_NOPE) & (lane < MLA_NOPE + half), -s,
                           jnp.where((lane >= MLA_NOPE + half) & (lane < MLA_NOPE + MLA_ROPE), s, 0.0))


def _rope_tables(pos_col, invf_row):
    n = pos_col.shape[0]
    tm = min(ROWS_PREP, n)
    out = pl.BlockSpec((tm, LANES), lambda i: (i, 0))
    return pl.pallas_call(
        _rope_kernel,
        grid=(n // tm,),
        in_specs=[pl.BlockSpec((tm, 1), lambda i: (i, 0)), pl.BlockSpec((1, LANES), lambda i: (0, 0))],
        out_specs=[out, out],
        out_shape=[jax.ShapeDtypeStruct((n, LANES), F32)] * 2,
        compiler_params=_cparams(("parallel",)),
        name="rope_tables",
    )(pos_col, invf_row)


def _mla_prep_kernel(cq_ref, ckv_ref, ckr_ref, c_ref, s_ref, gcq_ref, wuq_ref, wuqs_ref, gckv_ref, wuk_ref, wuv_ref,
                     gq_ref, gqs_ref, gk_ref, gks_ref, ones_ref, qo_ref, ko_ref, vo_ref):
    d_qk = float(MLA_NOPE + MLA_ROPE)
    cos, sin = c_ref[...], s_ref[...]
    half = MLA_ROPE // 2

    def rms(z, g):
        return (z * lax.rsqrt(jnp.mean(z * z, axis=-1, keepdims=True) + EPS) * g).astype(BF16)

    def head_ssq(z):
        zz = (z * z).astype(BF16)
        parts = [jnp.dot(zz[:, c * MXU_WIDTH:(c + 1) * MXU_WIDTH], ones_ref[...], preferred_element_type=F32)
                 for c in range(z.shape[1] // MXU_WIDTH)]
        return jnp.concatenate(parts, axis=1)

    cn = rms(cq_ref[...].astype(F32), gcq_ref[...])
    q = jnp.dot(cn, wuq_ref[...], preferred_element_type=F32)
    q_sw = jnp.dot(cn, wuqs_ref[...], preferred_element_type=F32)
    q_inv = lax.rsqrt(head_ssq(q) + d_qk * EPS)

    kvn = rms(ckv_ref[...].astype(F32), gckv_ref[...])
    kn = jnp.dot(kvn, wuk_ref[...], preferred_element_type=F32)
    vt = lax.dot_general(wuv_ref[...], kvn, (((1,), (1,)), ((), ())), preferred_element_type=F32)
    vo_ref[...] = _with_ones_rows(vt).astype(BF16)
    lane = lax.broadcasted_iota(jnp.int32, ckr_ref.shape, 1)
    kr = jnp.where(lane < MLA_ROPE, ckr_ref[...].astype(F32), 0.0)
    ss_r = jnp.sum(kr * kr, axis=-1, keepdims=True)
    kr_placed = pltpu.roll(kr, MLA_NOPE, 1)
    kr_swapped = jnp.where(lane < MLA_NOPE + half, pltpu.roll(kr, MLA_NOPE - half, 1), pltpu.roll(kr, MLA_NOPE + half, 1))
    kr_swapped = jnp.where((lane >= MLA_NOPE) & (lane < MLA_NOPE + MLA_ROPE), kr_swapped, 0.0)
    k_inv = lax.rsqrt(head_ssq(kn) + ss_r + d_qk * EPS)

    gc_q, gs_q = gq_ref[...] * cos, gqs_ref[...] * sin
    gc_k = gk_ref[...] * cos
    kr_rot = kr_placed * gc_k + kr_swapped * (gks_ref[...] * sin)
    for h in range(N_HEADS):
        sl = slice(h * HEAD_PAD, (h + 1) * HEAD_PAD)
        qo_ref[:, sl] = (q_inv[:, sl] * (q[:, sl] * gc_q + q_sw[:, sl] * gs_q)).astype(BF16)
        ko_ref[:, sl] = (k_inv[:, sl] * (kn[:, sl] * gc_k + kr_rot)).astype(BF16)


def _mla_prep(proj, cos_t, sin_t, gcq, wuq, wuq_sw, gckv, wuk, wuv_t, gq, gq_sw, gk, gk_sw, ones128, batch, seq):
    n = proj.shape[0]
    tm = min(ROWS_PREP, seq)
    nt = seq // tm
    full = lambda shape: pl.BlockSpec(shape, lambda i: (0, 0))
    tab = pl.BlockSpec((tm, LANES), lambda i: (i, 0))
    wide = pl.BlockSpec((tm, N_HEADS * HEAD_PAD), lambda i: (i, 0))
    return pl.pallas_call(
        _mla_prep_kernel,
        grid=(n // tm,),
        in_specs=[
            pl.BlockSpec((tm, MLA_Q_RANK), lambda i: (i, COL_CQ // MLA_Q_RANK)),
            pl.BlockSpec((tm, MLA_KV_RANK), lambda i: (i, COL_CKV // LANES)),
            pl.BlockSpec((tm, LANES), lambda i: (i, COL_CKR // LANES)),
            tab, tab,
            full((1, MLA_Q_RANK)), full((MLA_Q_RANK, N_HEADS * HEAD_PAD)), full((MLA_Q_RANK, N_HEADS * HEAD_PAD)),
            full((1, MLA_KV_RANK)), full((MLA_KV_RANK, N_HEADS * HEAD_PAD)), full((N_HEADS * VT_ROWS, MLA_KV_RANK)),
            full((1, HEAD_PAD)), full((1, HEAD_PAD)), full((1, HEAD_PAD)), full((1, HEAD_PAD)), full((MXU_WIDTH, MXU_WIDTH)),
        ],
        out_specs=[wide, wide, pl.BlockSpec((None, N_HEADS * VT_ROWS, tm), lambda i: (i // nt, 0, i % nt))],
        out_shape=[jax.ShapeDtypeStruct((n, N_HEADS * HEAD_PAD), BF16)] * 2
        + [jax.ShapeDtypeStruct((batch, N_HEADS * VT_ROWS, seq), BF16)],
        compiler_params=_cparams(("parallel",)),
        name="mla_prep",
    )(proj, proj, proj, cos_t, sin_t, gcq, wuq, wuq_sw, gckv, wuk, wuv_t, gq, gq_sw, gk, gk_sw, ones128)


def _gla_kernel(q_ref, k_ref, v_ref, sm_ref, wh_ref, wl_ref, b_ref, go_ref, ones_ref, sel_ref, o_ref, st_ref,
                inter_ref, *, tc):
    hk = N_HEADS * GLA_DK
    ck = GLA_CHUNK

    @pl.when(pl.program_id(1) == 0)
    def _():
        st_ref[...] = jnp.zeros_like(st_ref)

    sm = sm_ref[...]
    sm_hi = sm.astype(BF16)
    sm_lo = (sm - sm_hi.astype(F32)).astype(BF16)
    z = (jnp.dot(sm_hi, wh_ref[...], preferred_element_type=F32)
         + jnp.dot(sm_lo, wh_ref[...], preferred_element_type=F32)
         + jnp.dot(sm_hi, wl_ref[...], preferred_element_type=F32)) + b_ref[...]
    g_all = _row_prefix_sum(_log_sigmoid(z) * (1.0 / GLA_NORMALIZER), ck)

    lane_k = lax.broadcasted_iota(jnp.int32, (ck, hk), 1) // GLA_DK
    lane_pair = lax.broadcasted_iota(jnp.int32, (ck, LANES), 1)
    a_row = lax.broadcasted_iota(jnp.int32, (N_HEADS * ck, ck), 0) % ck
    a_col = lax.broadcasted_iota(jnp.int32, (N_HEADS * ck, ck), 1)
    st_blockdiag = (lax.broadcasted_iota(jnp.int32, (GROUP_W, hk), 0) // HEAD_DIM
                    == lax.broadcasted_iota(jnp.int32, (GROUP_W, hk), 1) // GLA_DK)

    def stack_heads(x):
        return jnp.concatenate([jnp.where(lane_k == h, x, jnp.zeros_like(x)) for h in range(N_HEADS)], axis=0)

    def intra_output(a, v):
        parts = []
        for p in range(N_HEADS // 2):
            r = jnp.dot(a[2 * p * ck:2 * (p + 1) * ck, :], v[:, p * LANES:(p + 1) * LANES], preferred_element_type=F32)
            parts.append(jnp.where(lane_pair < HEAD_DIM, r[0:ck, :], r[ck:2 * ck, :]))
        return jnp.concatenate(parts, axis=1)

    def write_normed(o):
        ssq = jnp.dot((o * o).astype(BF16), ones_ref[...], preferred_element_type=F32)
        o_ref[...] = (o * lax.rsqrt(ssq * (1.0 / HEAD_DIM) + EPS) * go_ref[...]).astype(BF16)

    n_chunks = tc // ck
    qts, intras, incs, decays = [], [], [], []
    for c in range(n_chunks):
        rows = slice(c * ck, (c + 1) * ck)
        g = g_all[rows]
        q = q_ref[rows, :].astype(F32) * (GLA_DK ** -0.5)
        k = k_ref[rows, :].astype(F32)
        v = v_ref[rows, :]
        g_last = g[ck - 1:ck, :]
        qt = (q * jnp.exp(g)).astype(BF16)
        kt = (k * jnp.exp(-g)).astype(BF16)
        kh = (k * jnp.exp(g_last - g)).astype(BF16)

        u = lax.dot_general(v, kh, (((0,), (0,)), ((), ())), preferred_element_type=F32)
        incs.append(jnp.where(st_blockdiag, u, 0.0))
        decays.append(jnp.exp(g_last))

        a = lax.dot_general(stack_heads(qt), kt, (((1,), (1,)), ((), ())), preferred_element_type=F32)
        intras.append(intra_output(jnp.where(a_row >= a_col, a, 0.0).astype(BF16), v))
        qts.append(qt)

    state = st_ref[...]
    states = []
    for c in range(n_chunks):
        states.append(state.astype(BF16))
        state = state * decays[c] + incs[c]
    st_ref[...] = state

    outs = []
    for c in range(n_chunks):
        o = lax.dot_general(qts[c], states[c], (((1,), (1,)), ((), ())), preferred_element_type=F32)
        inter_ref[c * ck:(c + 1) * ck, :] = o
        outs.append(o + intras[c])
    write_normed(jnp.concatenate(outs, axis=0))

    @pl.when(jnp.min(g_all) < -GLA_SAFE_LOG_DECAY)
    def _():
        t_idx = lax.broadcasted_iota(jnp.int32, (ck, hk), 0)
        nt_dims = (((1,), (1,)), ((), ()))
        redo = []
        for c in range(n_chunks):
            rows = slice(c * ck, (c + 1) * ck)
            g = g_all[rows]
            q = q_ref[rows, :].astype(F32) * (GLA_DK ** -0.5)
            k = k_ref[rows, :].astype(F32)
            g_hi = g.astype(BF16)
            g_r1 = g - g_hi.astype(F32)
            g_mid = g_r1.astype(BF16)
            g_lo = (g_r1 - g_mid.astype(F32)).astype(BF16)
            g3 = jnp.concatenate([g_hi, g_mid, g_lo], axis=0)
            a = lax.dot_general(stack_heads(q.astype(BF16)), k.astype(BF16), nt_dims, preferred_element_type=F32)
            a = jnp.where(a_row == a_col, a, 0.0)
            for lvl in range(GLA_LEVELS):
                half = ck >> (lvl + 1)
                g_mid_row = jnp.dot(sel_ref[lvl], g3, preferred_element_type=F32)
                d = g - g_mid_row
                second = (t_idx // half) % 2 == 1
                qd = jnp.where(second, q * jnp.exp(jnp.minimum(d, 0.0)), 0.0).astype(BF16)
                kd = jnp.where(second, 0.0, k * jnp.exp(jnp.minimum(-d, 0.0))).astype(BF16)
                part = lax.dot_general(stack_heads(qd), kd, nt_dims, preferred_element_type=F32)
                a = a + jnp.where(a_row // (2 * half) == a_col // (2 * half), part, 0.0)
            redo.append(inter_ref[rows, :] + intra_output(a.astype(BF16), v_ref[rows, :]))
        write_normed(jnp.concatenate(redo, axis=0))


def _gla(proj, small, w_hi, w_lo, b_up, g_out, ones64, sel, batch, seq):
    n = proj.shape[0]
    tc = min(ROWS_GLA, seq)
    nt = seq // tc
    hk = N_HEADS * GLA_DK
    full = lambda shape: pl.BlockSpec(shape, lambda b, t: (0, 0))
    return pl.pallas_call(
        functools.partial(_gla_kernel, tc=tc),
        grid=(batch, nt),
        in_specs=[
            pl.BlockSpec((tc, hk), lambda b, t: (b * nt + t, COL_AQ // hk)),
            pl.BlockSpec((tc, hk), lambda b, t: (b * nt + t, COL_AK // hk)),
            pl.BlockSpec((tc, GROUP_W), lambda b, t: (b * nt + t, COL_AV // GROUP_W)),
            pl.BlockSpec((tc, LANES), lambda b, t: (b * nt + t, 0)),
            full((LANES, hk)), full((LANES, hk)), full((1, hk)), full((1, GROUP_W)), full((GROUP_W, GROUP_W)),
            pl.BlockSpec((GLA_LEVELS, GLA_CHUNK, 3 * GLA_CHUNK), lambda b, t: (0, 0, 0)),
        ],
        out_specs=pl.BlockSpec((tc, GROUP_W), lambda b, t: (b * nt + t, 0)),
        out_shape=jax.ShapeDtypeStruct((n, GROUP_W), BF16),
        scratch_shapes=[pltpu.VMEM((GROUP_W, hk), F32), pltpu.VMEM((tc, GROUP_W), F32)],
        compiler_params=_cparams(("parallel", "arbitrary")),
        name="gla",
    )(proj, proj, proj, small, w_hi, w_lo, b_up, g_out, ones64, sel)


def _lru_kernel(x_ref, cw_ref, cb_ref, wr_ref, br_ref, wi_ref, bi_ref, lam_ref, o_ref, xprev_ref, h_ref, *, tr):
    @pl.when(pl.program_id(1) == 0)
    def _():
        xprev_ref[...] = jnp.zeros_like(xprev_ref)
        h_ref[...] = jnp.zeros_like(h_ref)

    x3 = x_ref[...].astype(F32).reshape(tr // SUBLANES, SUBLANES, GROUP_W)
    xg = jnp.concatenate([xprev_ref[...][None], x3], axis=0)
    xprev_ref[...] = x3[tr // SUBLANES - 1]
    sub3 = lax.broadcasted_iota(jnp.int32, x3.shape, 1)
    xc3 = cb_ref[...][None] + cw_ref[CONV_W - 1:CONV_W, :][None] * x3
    for s in range(1, CONV_W):
        rot = pltpu.roll(xg, s, 1)
        shifted = jnp.where(sub3 >= s, rot[1:], rot[:-1])
        xc3 = xc3 + cw_ref[CONV_W - 1 - s:CONV_W - s, :][None] * shifted
    xc = xc3.reshape(tr, GROUP_W)

    xb = xc.astype(BF16)
    t_r = jnp.tanh(0.5 * (jnp.dot(xb, wr_ref[...], preferred_element_type=F32) + br_ref[...]))
    t_i = jnp.tanh(0.5 * (jnp.dot(xb, wi_ref[...], preferred_element_type=F32) + bi_ref[...]))
    lam = lam_ref[...]
    softplus = jnp.maximum(-lam, 0.0) + jnp.log1p(jnp.exp(-jnp.abs(lam)))
    half_rate = (-0.5 * LRU_C) * softplus
    log_a = half_rate * t_r + half_rate
    one_minus_a = _neg_expm1(log_a)
    a = 1.0 - one_minus_a
    half_x = 0.5 * xc
    bx = jnp.sqrt(one_minus_a * (2.0 - one_minus_a)) * (half_x * t_i + half_x)

    a3 = a.reshape(tr // SUBLANES, SUBLANES, GROUP_W)
    b3 = bx.reshape(tr // SUBLANES, SUBLANES, GROUP_W)
    sub = lax.broadcasted_iota(jnp.int32, a3.shape, 1)
    for s in (1, 2, 4):
        keep = sub >= s
        a_prev = jnp.where(keep, pltpu.roll(a3, s, 1), 1.0)
        b_prev = jnp.where(keep, pltpu.roll(b3, s, 1), 0.0)
        b3 = a3 * b_prev + b3
        a3 = a3 * a_prev
    h_prev = h_ref[0:1, :]
    for g in range(tr // SUBLANES):
        hg = a3[g] * h_prev + b3[g]
        o_ref[g * SUBLANES:(g + 1) * SUBLANES, :] = hg.astype(BF16)
        h_prev = hg[SUBLANES - 1:SUBLANES, :]
    h_ref[0:1, :] = h_prev


def _lru(proj, conv_w, conv_b, wr, br, wi, bi, lam, batch, seq):
    n = proj.shape[0]
    tr = min(ROWS_LRU, seq)
    nt = seq // tr
    full = lambda shape: pl.BlockSpec(shape, lambda b, t: (0, 0))
    return pl.pallas_call(
        functools.partial(_lru_kernel, tr=tr),
        grid=(batch, nt),
        in_specs=[
            pl.BlockSpec((tr, GROUP_W), lambda b, t: (b * nt + t, COL_DX // GROUP_W)),
            full((SUBLANES, GROUP_W)), full((1, GROUP_W)),
            full((GROUP_W, GROUP_W)), full((1, GROUP_W)),
            full((GROUP_W, GROUP_W)), full((1, GROUP_W)), full((1, GROUP_W)),
        ],
        out_specs=pl.BlockSpec((tr, GROUP_W), lambda b, t: (b * nt + t, 0)),
        out_shape=jax.ShapeDtypeStruct((n, GROUP_W), BF16),
        scratch_shapes=[pltpu.VMEM((SUBLANES, GROUP_W), F32), pltpu.VMEM((SUBLANES, GROUP_W), F32)],
        compiler_params=_cparams(("parallel", "arbitrary")),
        name="rglru",
    )(proj, conv_w, conv_b, wr, br, wi, bi, lam)


def _pad_cols(w, width):
    return jnp.pad(w, [(0, 0)] * (w.ndim - 1) + [(0, width - w.shape[-1])])


def _block_diag_ones(n_blocks, size):
    return jnp.asarray(np.kron(np.eye(n_blocks, dtype=np.float32), np.ones((size, size), np.float32)), BF16)


def _swap_rotary(w):
    lead = w.shape[:-1]
    w = w.reshape(lead + (N_HEADS, MLA_NOPE + MLA_ROPE))
    half = MLA_ROPE // 2
    z1, z2 = w[..., MLA_NOPE:MLA_NOPE + half], w[..., MLA_NOPE + half:]
    w = jnp.concatenate([jnp.zeros_like(w[..., :MLA_NOPE]), z2, z1], axis=-1)
    return w.reshape(lead + (N_HEADS * (MLA_NOPE + MLA_ROPE),))


def _pad_heads(w, d_real):
    lead = w.shape[:-1]
    w = w.reshape(lead + (N_HEADS, d_real))
    w = jnp.pad(w, [(0, 0)] * len(lead) + [(0, 0), (0, HEAD_PAD - d_real)])
    return w.reshape(lead + (N_HEADS * HEAD_PAD,))


def kernel(x, positions, norm_g, w_in, w_out, gla_w_up, gla_b_up, gla_g_out, fox_b_f, fox_g_q, fox_g_k, mla_g_cq, mla_w_uq, mla_g_ckv, mla_w_ukv, mla_g_q, mla_g_k, lru_conv_w, lru_conv_b, lru_w_r, lru_b_r, lru_w_i, lru_b_i, lru_lam):
    batch, seq, _ = x.shape
    depth = w_in.shape[0]
    n = batch * seq

    gla_qk = N_HEADS * GLA_DK
    sizes = (D_MIX, gla_qk, gla_qk, GROUP_W, GLA_RANK, GROUP_W, GROUP_W, GROUP_W, N_HEADS,
             MLA_Q_RANK, MLA_KV_RANK, MLA_ROPE, GROUP_W)
    offs = np.concatenate([[0], np.cumsum(sizes)])
    seg = [w_in[:, :, offs[i]:offs[i + 1]] for i in range(len(sizes))]
    (w_gate, w_aq, w_ak, w_av, w_ag, w_bq, w_bk, w_bv, w_bf, w_cq, w_ckv, w_ckr, w_dx) = seg
    assert (SM_AG, SM_BF) == (MLA_ROPE, MLA_ROPE + GLA_RANK)
    w_kr_gates = _pad_cols(jnp.concatenate([w_ckr, w_ag, w_bf], axis=-1), LANES)
    w_main = jnp.concatenate(
        [w_gate, w_aq, w_ak, w_av, w_bq, w_bk, w_bv, w_cq, w_ckv, w_kr_gates, w_dx], axis=-1).astype(BF16)
    w_out_b = w_out.astype(BF16)

    ones64 = _block_diag_ones(GROUP_W // HEAD_DIM, HEAD_DIM)
    ones128 = _block_diag_ones(MXU_WIDTH // HEAD_PAD, HEAD_PAD)

    wup = jnp.pad(gla_w_up, ((0, 0), (SM_AG, LANES - SM_AG - GLA_RANK), (0, 0)))
    wup_hi = wup.astype(BF16)
    wup_lo = (wup - wup_hi.astype(F32)).astype(BF16)

    sel = np.zeros((GLA_LEVELS, GLA_CHUNK, 3 * GLA_CHUNK), np.float32)
    for lvl in range(GLA_LEVELS):
        half = GLA_CHUNK >> (lvl + 1)
        for t in range(GLA_CHUNK):
            mid = (t // (2 * half)) * 2 * half + half - 1
            for piece in range(3):
                sel[lvl, t, piece * GLA_CHUNK + mid] = 1.0
    gla_sel = jnp.asarray(sel, BF16)

    fox_scale = LOG2E
    mla_scale = (MLA_NOPE + MLA_ROPE) ** -0.5 * LOG2E
    fox_gq = jnp.tile(fox_g_q, (1, N_HEADS)) * fox_scale
    fox_gk = jnp.tile(fox_g_k, (1, N_HEADS)) * HEAD_DIM ** 0.5
    place = np.zeros((LANES, N_HEADS * HEAD_PAD), np.float32)
    aug_q = np.zeros((1, N_HEADS * HEAD_PAD), np.float32)
    for h in range(N_HEADS):
        base = h * HEAD_PAD + (HEAD_DIM if h % 2 == 0 else 0)
        for piece in range(3):
            place[SM_BF + piece * PIECE_STRIDE + h, base + piece] = 1.0
            aug_q[0, base + piece] = 1.0
    place, aug_q = jnp.asarray(place, BF16), jnp.asarray(aug_q)
    fox_bf = jnp.pad(fox_b_f, ((0, 0), (SM_BF, LANES - SM_BF - N_HEADS)))

    wuq = _pad_heads(mla_w_uq, MLA_NOPE + MLA_ROPE).astype(BF16)
    wuq_sw = _pad_heads(_swap_rotary(mla_w_uq), MLA_NOPE + MLA_ROPE).astype(BF16)
    wukv = mla_w_ukv.reshape(depth, MLA_KV_RANK, N_HEADS, 2 * HEAD_DIM)
    wuk = _pad_heads(wukv[..., :MLA_NOPE].reshape(depth, MLA_KV_RANK, N_HEADS * MLA_NOPE), MLA_NOPE).astype(BF16)
    pad_rows = lambda m: jnp.pad(m.reshape(m.shape[:-2] + (N_HEADS, HEAD_DIM, m.shape[-1])),
                                 [(0, 0)] * (m.ndim - 1) + [(0, VT_ROWS - HEAD_DIM), (0, 0)]
                                 ).reshape(m.shape[:-2] + (N_HEADS * VT_ROWS, m.shape[-1]))
    wuv_t = pad_rows(jnp.swapaxes(wukv[..., MLA_NOPE:].reshape(depth, MLA_KV_RANK, GROUP_W), 1, 2)).astype(BF16)
    eye = pad_rows(jnp.eye(GROUP_W, dtype=F32)).astype(BF16)
    d_qk = MLA_NOPE + MLA_ROPE
    half = MLA_ROPE // 2
    swap_halves = lambda g: jnp.concatenate(
        [jnp.zeros_like(g[:, :MLA_NOPE]), g[:, MLA_NOPE + half:], g[:, MLA_NOPE:MLA_NOPE + half]], axis=-1)
    mla_gq_all = mla_g_q * (mla_scale * d_qk ** 0.5)
    mla_gk_all = mla_g_k * d_qk ** 0.5
    mla_gq, mla_gq_sw = _pad_cols(mla_gq_all, HEAD_PAD), _pad_cols(swap_halves(mla_gq_all), HEAD_PAD)
    mla_gk, mla_gk_sw = _pad_cols(mla_gk_all, HEAD_PAD), _pad_cols(swap_halves(mla_gk_all), HEAD_PAD)

    eye_h = jnp.eye(N_HEADS, dtype=F32)
    wr_bd = jnp.einsum("lncd,nm->lncmd", lru_w_r, eye_h).reshape(depth, GROUP_W, GROUP_W).astype(BF16)
    wi_bd = jnp.einsum("lncd,nm->lncmd", lru_w_i, eye_h).reshape(depth, GROUP_W, GROUP_W).astype(BF16)
    conv_w = jnp.pad(lru_conv_w, ((0, 0), (0, SUBLANES - CONV_W), (0, 0)))

    inv_freq = ROPE_THETA ** (-jnp.arange(half, dtype=F32) / half)
    invf_row = jnp.pad(jnp.concatenate([inv_freq, inv_freq]), (MLA_NOPE, LANES - MLA_NOPE - MLA_ROPE))[None, :]
    cos_t, sin_t = _rope_tables(positions.reshape(n, 1), invf_row)

    xf = x.reshape(n, D_MODEL)
    for l in range(depth):
        proj, small = _norm_inproj(xf, norm_g[l][None, :], w_main, l)
        ya = _gla(proj, small, wup_hi[l], wup_lo[l], gla_b_up[l][None, :],
                  jnp.tile(gla_g_out[l], N_HEADS)[None, :], ones64, gla_sel, batch, seq)
        fq, fk, fvt = _fox_prep(proj, small, fox_bf[l][None, :], fox_gq[l][None, :], fox_gk[l][None, :], ones64, eye,
                                place, aug_q, batch, seq)
        yb = _attention(fq, fk, fvt, batch, seq)
        mq, mk, mvt = _mla_prep(proj, cos_t, sin_t, mla_g_cq[l][None, :], wuq[l], wuq_sw[l], mla_g_ckv[l][None, :],
                                wuk[l], wuv_t[l], mla_gq[l][None, :], mla_gq_sw[l][None, :], mla_gk[l][None, :],
                                mla_gk_sw[l][None, :], ones128, batch, seq)
        yc = _attention(mq, mk, mvt, batch, seq)
        yd = _lru(proj, conv_w[l], lru_conv_b[l][None, :], wr_bd[l], lru_b_r[l][None, :], wi_bd[l],
                  lru_b_i[l][None, :], lru_lam[l][None, :], batch, seq)
        xf = _gate_outproj(ya, yb, yc, yd, proj, w_out_b, xf, l)
    return xf.reshape(batch, seq, D_MODEL)
```

```python
import functools

import numpy as np
import jax
import jax.numpy as jnp
from jax import lax
from jax.experimental import pallas as pl
from jax.experimental.pallas import tpu as pltpu

F32 = jnp.float32
BF16 = jnp.bfloat16

D_MODEL = 1024
D_MIX = 2048
GROUP_W = 512
N_HEADS = 8
HEAD_DIM = 64
EPS = 1e-6
GLA_DK = 32
GLA_RANK = 16
GLA_NORMALIZER = 16.0
GLA_CHUNK = 64
GLA_LEVELS = 6
GLA_SAFE_LOG_DECAY = 40.0
MLA_Q_RANK = 256
MLA_KV_RANK = 128
MLA_NOPE = 64
MLA_ROPE = 32
ROPE_THETA = 10000.0
CONV_W = 4
LRU_C = 8.0
LOG2E = 1.4426950408889634
NEG_BIG = -1e30
EXPM1_SERIES_RANGE = 1.0 / 64.0

LANES = 128
SUBLANES = 8
MXU_WIDTH = 256
HEAD_PAD = 128
VT_ROWS = 80

COL_AQ, COL_AK, COL_AV = 2048, 2304, 2560
COL_BQ, COL_BK, COL_BV = 3072, 3584, 4096
COL_CQ, COL_CKV, COL_CKR = 4608, 4864, 4992
COL_DX = 5120
D_PROJ = 5632
SM_AG, SM_BF = 32, 48
PIECE_STRIDE = 16


ROWS_INPROJ = 1024
ROWS_OUTPROJ = 1024
ROWS_PREP = 1024
ROWS_GLA = 512
ROWS_LRU = 1024
ATTN_BLOCK = 512
ATTN_HEADS_PER_STEP = 8
VMEM_LIMIT_MB = 48
VMEM_LIMIT_INPROJ_MB = 56


def _cparams(sem, vmem_mb=VMEM_LIMIT_MB):
    return pltpu.CompilerParams(dimension_semantics=sem, vmem_limit_bytes=vmem_mb * 1024 * 1024)


def _sigmoid(z):
    return 0.5 * jnp.tanh(0.5 * z) + 0.5


def _with_ones_rows(vt):
    row = lax.broadcasted_iota(jnp.int32, vt.shape, 0) % VT_ROWS
    return jnp.where(row == HEAD_DIM, 1.0, vt)


def _neg_expm1(z):
    series = -z * (1.0 + z * (0.5 + z * (1.0 / 6.0 + z * (1.0 / 24.0))))
    return jnp.where(z > -EXPM1_SERIES_RANGE, series, 1.0 - jnp.exp(z))


def _log_sigmoid(z):
    return jnp.minimum(z, 0.0) - jnp.log1p(jnp.exp(-jnp.abs(z)))


def _row_prefix_sum(v, seg):
    rows, width = v.shape
    g = v.reshape(rows // SUBLANES, SUBLANES, width)
    sub = lax.broadcasted_iota(jnp.int32, g.shape, 1)
    for s in (1, 2, 4):
        g = g + jnp.where(sub >= s, pltpu.roll(g, s, 1), 0.0)
    per_seg = seg // SUBLANES
    g = g.reshape(rows // seg, per_seg, SUBLANES, width)
    groups = [g[:, 0]]
    for i in range(1, per_seg):
        groups.append(g[:, i] + groups[-1][:, SUBLANES - 1:SUBLANES, :])
    return jnp.stack(groups, axis=1).reshape(rows, width)


def _inproj_kernel(x_ref, g_ref, w_ref, o_ref, os_ref, h_ref, *, small_col):
    @pl.when(pl.program_id(1) == 0)
    def _():
        x = x_ref[...]
        ms = jnp.mean(x * x, axis=-1, keepdims=True)
        h_ref[...] = (x * lax.rsqrt(ms + EPS) * g_ref[...]).astype(BF16)

    res = jnp.dot(h_ref[...], w_ref[...], preferred_element_type=F32)
    o_ref[...] = res.astype(BF16)

    @pl.when(pl.program_id(1) == pl.num_programs(1) - 1)
    def _():
        os_ref[...] = res[:, small_col:small_col + LANES]


def _norm_inproj(x, g, w_all, layer):
    n = x.shape[0]
    tm = min(ROWS_INPROJ, n)
    tn = D_PROJ // 2
    assert COL_CKR >= D_PROJ - tn
    return pl.pallas_call(
        functools.partial(_inproj_kernel, small_col=COL_CKR - (D_PROJ - tn)),
        grid=(n // tm, D_PROJ // tn),
        in_specs=[
            pl.BlockSpec((tm, D_MODEL), lambda i, j: (i, 0)),
            pl.BlockSpec((1, D_MODEL), lambda i, j: (0, 0)),
            pl.BlockSpec((None, D_MODEL, tn), lambda i, j: (layer, 0, j)),
        ],
        out_specs=[
            pl.BlockSpec((tm, tn), lambda i, j: (i, j)),
            pl.BlockSpec((tm, LANES), lambda i, j: (i, 0)),
        ],
        out_shape=[jax.ShapeDtypeStruct((n, D_PROJ), BF16), jax.ShapeDtypeStruct((n, LANES), F32)],
        scratch_shapes=[pltpu.VMEM((tm, D_MODEL), BF16)],
        compiler_params=_cparams(("parallel", "arbitrary"), vmem_mb=VMEM_LIMIT_INPROJ_MB),
        name="norm_inproj",
    )(x, g, w_all)


def _outproj_kernel(ya_ref, yb_ref, yc_ref, yd_ref, gate_ref, w_ref, x_ref, o_ref, z_ref):
    for gi, y_ref in enumerate((ya_ref, yb_ref, yc_ref, yd_ref)):
        lo, hi = gi * GROUP_W, (gi + 1) * GROUP_W
        g = gate_ref[:, lo:hi].astype(F32)
        z_ref[:, lo:hi] = (y_ref[...].astype(F32) * (g * _sigmoid(g))).astype(BF16)
    o_ref[...] = x_ref[...] + jnp.dot(z_ref[...], w_ref[...], preferred_element_type=F32)


def _gate_outproj(ya, yb, yc, yd, proj, w_all, x, layer):
    n = x.shape[0]
    tm = min(ROWS_OUTPROJ, n)
    yspec = pl.BlockSpec((tm, GROUP_W), lambda i: (i, 0))
    return pl.pallas_call(
        _outproj_kernel,
        grid=(n // tm,),
        in_specs=[
            yspec, yspec, yspec, yspec,
            pl.BlockSpec((tm, D_MIX), lambda i: (i, 0)),
            pl.BlockSpec((None, D_MIX, D_MODEL), lambda i: (layer, 0, 0)),
            pl.BlockSpec((tm, D_MODEL), lambda i: (i, 0)),
        ],
        out_specs=pl.BlockSpec((tm, D_MODEL), lambda i: (i, 0)),
        out_shape=jax.ShapeDtypeStruct((n, D_MODEL), F32),
        scratch_shapes=[pltpu.VMEM((tm, D_MIX), BF16)],
        compiler_params=_cparams(("parallel",)),
        name="gate_outproj",
    )(ya, yb, yc, yd, proj, w_all, x)


def _attn_kernel(q_ref, k_ref, vt_ref, o_ref, m_ref, acc_ref, st_ref, mx_ref, *, tq, nq, hg):
    i = pl.program_id(2)
    half = tq // 2
    nt_dims = (((1,), (1,)), ((), ()))

    def scores(h, krows, qrows):
        lanes = slice(h * HEAD_PAD, (h + 1) * HEAD_PAD)
        return lax.dot_general(k_ref[krows, lanes], q_ref[qrows, lanes], nt_dims, preferred_element_type=F32)

    def vrows(h):
        return slice(h * VT_ROWS, (h + 1) * VT_ROWS)

    tri = lax.broadcasted_iota(jnp.int32, (half, half), 0) <= lax.broadcasted_iota(jnp.int32, (half, half), 1)
    low = lax.broadcasted_iota(jnp.int32, (tq, half), 0) <= lax.broadcasted_iota(jnp.int32, (tq, half), 1) + half

    def issue_scores(item, c):
        h, blk = item
        if blk is None:
            off = c * tq
            s_l = jnp.where(tri, scores(h, slice(off, off + half), slice(0, half)), NEG_BIG)
            s_r = jnp.where(low, scores(h, slice(off, off + tq), slice(half, tq)), NEG_BIG)
            mx = jnp.concatenate([jnp.max(s_l, axis=0, keepdims=True), jnp.max(s_r, axis=0, keepdims=True)], axis=1)
            return (s_l, s_r, mx)
        st = scores(h, slice(blk * tq, (blk + 1) * tq), slice(0, tq))
        return (st, jnp.max(st, axis=0, keepdims=True))

    def store_scores(item, slot, vals):
        if item[1] is None:
            st_ref[slot, 0:half, 0:half] = vals[0]
            st_ref[slot, :, half:tq] = vals[1]
        else:
            st_ref[slot] = vals[0]
        mx_ref[slot] = vals[-1]

    def consume(item, slot, c):
        h, blk = item
        opens = blk is None
        if opens:
            m_new = mx_ref[slot]
        else:
            m = m_ref[h]
            m_new = jnp.maximum(m, mx_ref[slot])
            alpha = jnp.exp2(m - m_new)
        m_ref[h] = m_new

        def accumulate(cols, st, vt):
            p = jnp.exp2(st - m_new[:, cols]).astype(BF16)
            pv = jnp.dot(vt, p, preferred_element_type=F32)
            acc_ref[h, :, cols] = pv if opens else alpha[:, cols] * acc_ref[h, :, cols] + pv

        if blk is None:
            off = c * tq
            accumulate(slice(0, half), st_ref[slot, 0:half, 0:half], vt_ref[vrows(h), off:off + half])
            accumulate(slice(half, tq), st_ref[slot, :, half:tq], vt_ref[vrows(h), off:off + tq])
        else:
            accumulate(slice(0, tq), st_ref[slot], vt_ref[vrows(h), blk * tq:(blk + 1) * tq])

    for c in range(nq):
        @pl.when(i == c)
        def _(c=c):
            items = [(h, None) for h in range(hg)] + [(h, blk) for blk in range(c) for h in range(hg)]
            store_scores(items[0], 0, issue_scores(items[0], c))
            for n, item in enumerate(items):
                nxt = items[n + 1] if n + 1 < len(items) else None
                if nxt is not None:
                    vals = issue_scores(nxt, c)
                consume(item, n % 2, c)
                if nxt is not None:
                    store_scores(nxt, (n + 1) % 2, vals)

    ot = jnp.concatenate([acc_ref[h, 0:HEAD_DIM, :] * (1.0 / acc_ref[h, HEAD_DIM:HEAD_DIM + 1, :]) for h in range(hg)],
                         axis=0)
    o_ref[...] = ot.T.astype(BF16)


def _attention(q, k, vt, batch, seq):
    n = q.shape[0]
    tq = min(ATTN_BLOCK, seq)
    nq = seq // tq
    hg = ATTN_HEADS_PER_STEP
    return pl.pallas_call(
        functools.partial(_attn_kernel, tq=tq, nq=nq, hg=hg),
        grid=(batch, N_HEADS // hg, nq),
        in_specs=[
            pl.BlockSpec((tq, hg * HEAD_PAD), lambda b, p, i: (b * nq + i, p)),
            pl.BlockSpec((seq, hg * HEAD_PAD), lambda b, p, i: (b, p)),
            pl.BlockSpec((None, hg * VT_ROWS, seq), lambda b, p, i: (b, p, 0)),
        ],
        out_specs=pl.BlockSpec((tq, hg * HEAD_DIM), lambda b, p, i: (b * nq + i, p)),
        out_shape=jax.ShapeDtypeStruct((n, GROUP_W), BF16),
        scratch_shapes=[pltpu.VMEM((hg, 1, tq), F32), pltpu.VMEM((hg, VT_ROWS, tq), F32),
                        pltpu.VMEM((2, tq, tq), F32), pltpu.VMEM((2, 1, tq), F32)],
        compiler_params=_cparams(("parallel", "parallel", "arbitrary")),
        name="causal_attention",
    )(q, k, vt)


def _fox_prep_kernel(q_ref, k_ref, v_ref, sm_ref, bf_ref, gq_ref, gk_ref, ones_ref, eye_ref, place_ref, augq_ref,
                     qo_ref, ko_ref, vt_ref, carry_ref, *, tt):
    @pl.when(pl.program_id(1) == 0)
    def _():
        carry_ref[...] = jnp.zeros_like(carry_ref)

    vt = lax.dot_general(eye_ref[...], v_ref[...], (((1,), (1,)), ((), ())), preferred_element_type=F32)
    vt_ref[...] = _with_ones_rows(vt).astype(BF16)

    def headnorm(ref, g_ref):
        z = ref[...].astype(F32)
        ssq = jnp.dot((z * z).astype(BF16), ones_ref[...], preferred_element_type=F32)
        return z * lax.rsqrt(ssq + HEAD_DIM * EPS) * g_ref[...]

    qn = headnorm(q_ref, gq_ref)
    kn = headnorm(k_ref, gk_ref)

    log_f = _log_sigmoid(sm_ref[...] + bf_ref[...])
    cum = _row_prefix_sum(log_f, tt) + carry_ref[0:1, :]
    carry_ref[0:1, :] = cum[tt - 1:tt, :]
    neg = cum * (-LOG2E)

    lane = lax.broadcasted_iota(jnp.int32, (tt, LANES), 1)
    hi = neg.astype(BF16).astype(F32)
    r1 = neg - hi
    mid = r1.astype(BF16).astype(F32)
    lo = r1 - mid
    pieces = jnp.where(lane < SM_BF + PIECE_STRIDE, hi,
                       jnp.where(lane < SM_BF + 2 * PIECE_STRIDE, pltpu.roll(mid, PIECE_STRIDE, 1),
                                 pltpu.roll(lo, 2 * PIECE_STRIDE, 1)))
    aug_k = jnp.dot(pieces.astype(BF16), place_ref[...], preferred_element_type=F32)
    aug_q = augq_ref[...]
    for h in range(N_HEADS):
        pair, e = h // 2, h % 2
        in_head = (lane >= e * HEAD_DIM) & (lane < (e + 1) * HEAD_DIM)
        blk = slice(h * HEAD_PAD, (h + 1) * HEAD_PAD)
        kpair = kn[:, pair * LANES:(pair + 1) * LANES]
        qpair = qn[:, pair * LANES:(pair + 1) * LANES]
        ko_ref[:, blk] = jnp.where(in_head, kpair, aug_k[:, blk]).astype(BF16)
        qo_ref[:, blk] = jnp.where(in_head, qpair, aug_q[:, blk]).astype(BF16)


def _fox_prep(proj, small, bf_row, gq, gk, ones64, eye, place, aug_q, batch, seq):
    n = proj.shape[0]
    tt = min(ROWS_PREP, seq)
    nt = seq // tt
    row = lambda shape: pl.BlockSpec(shape, lambda b, t: (0, 0))
    return pl.pallas_call(
        functools.partial(_fox_prep_kernel, tt=tt),
        grid=(batch, nt),
        in_specs=[
            pl.BlockSpec((tt, GROUP_W), lambda b, t: (b * nt + t, COL_BQ // GROUP_W)),
            pl.BlockSpec((tt, GROUP_W), lambda b, t: (b * nt + t, COL_BK // GROUP_W)),
            pl.BlockSpec((tt, GROUP_W), lambda b, t: (b * nt + t, COL_BV // GROUP_W)),
            pl.BlockSpec((tt, LANES), lambda b, t: (b * nt + t, 0)),
            row((1, LANES)), row((1, GROUP_W)), row((1, GROUP_W)), row((GROUP_W, GROUP_W)),
            row((N_HEADS * VT_ROWS, GROUP_W)), row((LANES, N_HEADS * HEAD_PAD)), row((1, N_HEADS * HEAD_PAD)),
        ],
        out_specs=[
            pl.BlockSpec((tt, N_HEADS * HEAD_PAD), lambda b, t: (b * nt + t, 0)),
            pl.BlockSpec((tt, N_HEADS * HEAD_PAD), lambda b, t: (b * nt + t, 0)),
            pl.BlockSpec((None, N_HEADS * VT_ROWS, tt), lambda b, t: (b, 0, t)),
        ],
        out_shape=[jax.ShapeDtypeStruct((n, N_HEADS * HEAD_PAD), BF16)] * 2
        + [jax.ShapeDtypeStruct((batch, N_HEADS * VT_ROWS, seq), BF16)],
        scratch_shapes=[pltpu.VMEM((SUBLANES, LANES), F32)],
        compiler_params=_cparams(("parallel", "arbitrary")),
        name="fox_prep",
    )(proj, proj, proj, small, bf_row, gq, gk, ones64, eye, place, aug_q)


def _rope_kernel(pos_ref, invf_ref, c_ref, s_ref):
    ang = pos_ref[...].astype(F32) * invf_ref[...]
    lane = lax.broadcasted_iota(jnp.int32, ang.shape, 1)
    half = MLA_ROPE // 2
    c, s = jnp.cos(ang), jnp.sin(ang)
    c_ref[...] = jnp.where((lane >= MLA_NOPE) & (lane < MLA_NOPE + MLA_ROPE), c, 1.0)
    s_ref[...] = jnp.where((lane >= MLA_NOPE) & (lane < MLA_NOPE + half), -s,
                           jnp.where((lane >= MLA_NOPE + half) & (lane < MLA_NOPE + MLA_ROPE), s, 0.0))


def _rope_tables(pos_col, invf_row):
    n = pos_col.shape[0]
    tm = min(ROWS_PREP, n)
    out = pl.BlockSpec((tm, LANES), lambda i: (i, 0))
    return pl.pallas_call(
        _rope_kernel,
        grid=(n // tm,),
        in_specs=[pl.BlockSpec((tm, 1), lambda i: (i, 0)), pl.BlockSpec((1, LANES), lambda i: (0, 0))],
        out_specs=[out, out],
        out_shape=[jax.ShapeDtypeStruct((n, LANES), F32)] * 2,
        compiler_params=_cparams(("parallel",)),
        name="rope_tables",
    )(pos_col, invf_row)


def _mla_prep_kernel(cq_ref, ckv_ref, ckr_ref, c_ref, s_ref, gcq_ref, wuq_ref, wuqs_ref, gckv_ref, wuk_ref, wuv_ref,
                     gq_ref, gqs_ref, gk_ref, gks_ref, ones_ref, qo_ref, ko_ref, vo_ref):
    d_qk = float(MLA_NOPE + MLA_ROPE)
    cos, sin = c_ref[...], s_ref[...]
    half = MLA_ROPE // 2

    def rms(z, g):
        return (z * lax.rsqrt(jnp.mean(z * z, axis=-1, keepdims=True) + EPS) * g).astype(BF16)

    def head_ssq(z):
        zz = (z * z).astype(BF16)
        parts = [jnp.dot(zz[:, c * MXU_WIDTH:(c + 1) * MXU_WIDTH], ones_ref[...], preferred_element_type=F32)
                 for c in range(z.shape[1] // MXU_WIDTH)]
        return jnp.concatenate(parts, axis=1)

    cn = rms(cq_ref[...].astype(F32), gcq_ref[...])
    q = jnp.dot(cn, wuq_ref[...], preferred_element_type=F32)
    q_sw = jnp.dot(cn, wuqs_ref[...], preferred_element_type=F32)
    q_inv = lax.rsqrt(head_ssq(q) + d_qk * EPS)

    kvn = rms(ckv_ref[...].astype(F32), gckv_ref[...])
    kn = jnp.dot(kvn, wuk_ref[...], preferred_element_type=F32)
    vt = lax.dot_general(wuv_ref[...], kvn, (((1,), (1,)), ((), ())), preferred_element_type=F32)
    vo_ref[...] = _with_ones_rows(vt).astype(BF16)
    lane = lax.broadcasted_iota(jnp.int32, ckr_ref.shape, 1)
    kr = jnp.where(lane < MLA_ROPE, ckr_ref[...].astype(F32), 0.0)
    ss_r = jnp.sum(kr * kr, axis=-1, keepdims=True)
    kr_placed = pltpu.roll(kr, MLA_NOPE, 1)
    kr_swapped = jnp.where(lane < MLA_NOPE + half, pltpu.roll(kr, MLA_NOPE - half, 1), pltpu.roll(kr, MLA_NOPE + half, 1))
    kr_swapped = jnp.where((lane >= MLA_NOPE) & (lane < MLA_NOPE + MLA_ROPE), kr_swapped, 0.0)
    k_inv = lax.rsqrt(head_ssq(kn) + ss_r + d_qk * EPS)

    gc_q, gs_q = gq_ref[...] * cos, gqs_ref[...] * sin
    gc_k = gk_ref[...] * cos
    kr_rot = kr_placed * gc_k + kr_swapped * (gks_ref[...] * sin)
    for h in range(N_HEADS):
        sl = slice(h * HEAD_PAD, (h + 1) * HEAD_PAD)
        qo_ref[:, sl] = (q_inv[:, sl] * (q[:, sl] * gc_q + q_sw[:, sl] * gs_q)).astype(BF16)
        ko_ref[:, sl] = (k_inv[:, sl] * (kn[:, sl] * gc_k + kr_rot)).astype(BF16)


def _mla_prep(proj, cos_t, sin_t, gcq, wuq, wuq_sw, gckv, wuk, wuv_t, gq, gq_sw, gk, gk_sw, ones128, batch, seq):
    n = proj.shape[0]
    tm = min(ROWS_PREP, seq)
    nt = seq // tm
    full = lambda shape: pl.BlockSpec(shape, lambda i: (0, 0))
    tab = pl.BlockSpec((tm, LANES), lambda i: (i, 0))
    wide = pl.BlockSpec((tm, N_HEADS * HEAD_PAD), lambda i: (i, 0))
    return pl.pallas_call(
        _mla_prep_kernel,
        grid=(n // tm,),
        in_specs=[
            pl.BlockSpec((tm, MLA_Q_RANK), lambda i: (i, COL_CQ // MLA_Q_RANK)),
            pl.BlockSpec((tm, MLA_KV_RANK), lambda i: (i, COL_CKV // LANES)),
            pl.BlockSpec((tm, LANES), lambda i: (i, COL_CKR // LANES)),
            tab, tab,
            full((1, MLA_Q_RANK)), full((MLA_Q_RANK, N_HEADS * HEAD_PAD)), full((MLA_Q_RANK, N_HEADS * HEAD_PAD)),
            full((1, MLA_KV_RANK)), full((MLA_KV_RANK, N_HEADS * HEAD_PAD)), full((N_HEADS * VT_ROWS, MLA_KV_RANK)),
            full((1, HEAD_PAD)), full((1, HEAD_PAD)), full((1, HEAD_PAD)), full((1, HEAD_PAD)), full((MXU_WIDTH, MXU_WIDTH)),
        ],
        out_specs=[wide, wide, pl.BlockSpec((None, N_HEADS * VT_ROWS, tm), lambda i: (i // nt, 0, i % nt))],
        out_shape=[jax.ShapeDtypeStruct((n, N_HEADS * HEAD_PAD), BF16)] * 2
        + [jax.ShapeDtypeStruct((batch, N_HEADS * VT_ROWS, seq), BF16)],
        compiler_params=_cparams(("parallel",)),
        name="mla_prep",
    )(proj, proj, proj, cos_t, sin_t, gcq, wuq, wuq_sw, gckv, wuk, wuv_t, gq, gq_sw, gk, gk_sw, ones128)


def _gla_kernel(q_ref, k_ref, v_ref, sm_ref, wh_ref, wl_ref, b_ref, go_ref, ones_ref, sel_ref, o_ref, st_ref,
                inter_ref, *, tc):
    hk = N_HEADS * GLA_DK
    ck = GLA_CHUNK

    @pl.when(pl.program_id(1) == 0)
    def _():
        st_ref[...] = jnp.zeros_like(st_ref)

    sm = sm_ref[...]
    sm_hi = sm.astype(BF16)
    sm_lo = (sm - sm_hi.astype(F32)).astype(BF16)
    z = (jnp.dot(sm_hi, wh_ref[...], preferred_element_type=F32)
         + jnp.dot(sm_lo, wh_ref[...], preferred_element_type=F32)
         + jnp.dot(sm_hi, wl_ref[...], preferred_element_type=F32)) + b_ref[...]
    g_all = _row_prefix_sum(_log_sigmoid(z) * (1.0 / GLA_NORMALIZER), ck)

    lane_k = lax.broadcasted_iota(jnp.int32, (ck, hk), 1) // GLA_DK
    lane_pair = lax.broadcasted_iota(jnp.int32, (ck, LANES), 1)
    a_row = lax.broadcasted_iota(jnp.int32, (N_HEADS * ck, ck), 0) % ck
    a_col = lax.broadcasted_iota(jnp.int32, (N_HEADS * ck, ck), 1)
    st_blockdiag = (lax.broadcasted_iota(jnp.int32, (GROUP_W, hk), 0) // HEAD_DIM
                    == lax.broadcasted_iota(jnp.int32, (GROUP_W, hk), 1) // GLA_DK)

    def stack_heads(x):
        return jnp.concatenate([jnp.where(lane_k == h, x, jnp.zeros_like(x)) for h in range(N_HEADS)], axis=0)

    def intra_output(a, v):
        parts = []
        for p in range(N_HEADS // 2):
            r = jnp.dot(a[2 * p * ck:2 * (p + 1) * ck, :], v[:, p * LANES:(p + 1) * LANES], preferred_element_type=F32)
            parts.append(jnp.where(lane_pair < HEAD_DIM, r[0:ck, :], r[ck:2 * ck, :]))
        return jnp.concatenate(parts, axis=1)

    def write_normed(o):
        ssq = jnp.dot((o * o).astype(BF16), ones_ref[...], preferred_element_type=F32)
        o_ref[...] = (o * lax.rsqrt(ssq * (1.0 / HEAD_DIM) + EPS) * go_ref[...]).astype(BF16)

    n_chunks = tc // ck
    qts, intras, incs, decays = [], [], [], []
    for c in range(n_chunks):
        rows = slice(c * ck, (c + 1) * ck)
        g = g_all[rows]
        q = q_ref[rows, :].astype(F32) * (GLA_DK ** -0.5)
        k = k_ref[rows, :].astype(F32)
        v = v_ref[rows, :]
        g_last = g[ck - 1:ck, :]
        qt = (q * jnp.exp(g)).astype(BF16)
        kt = (k * jnp.exp(-g)).astype(BF16)
        kh = (k * jnp.exp(g_last - g)).astype(BF16)

        u = lax.dot_general(v, kh, (((0,), (0,)), ((), ())), preferred_element_type=F32)
        incs.append(jnp.where(st_blockdiag, u, 0.0))
        decays.append(jnp.exp(g_last))

        a = lax.dot_general(stack_heads(qt), kt, (((1,), (1,)), ((), ())), preferred_element_type=F32)
        intras.append(intra_output(jnp.where(a_row >= a_col, a, 0.0).astype(BF16), v))
        qts.append(qt)

    state = st_ref[...]
    states = []
    for c in range(n_chunks):
        states.append(state.astype(BF16))
        state = state * decays[c] + incs[c]
    st_ref[...] = state

    outs = []
    for c in range(n_chunks):
        o = lax.dot_general(qts[c], states[c], (((1,), (1,)), ((), ())), preferred_element_type=F32)
        inter_ref[c * ck:(c + 1) * ck, :] = o
        outs.append(o + intras[c])
    write_normed(jnp.concatenate(outs, axis=0))

    @pl.when(jnp.min(g_all) < -GLA_SAFE_LOG_DECAY)
    def _():
        t_idx = lax.broadcasted_iota(jnp.int32, (ck, hk), 0)
        nt_dims = (((1,), (1,)), ((), ()))
        redo = []
        for c in range(n_chunks):
            rows = slice(c * ck, (c + 1) * ck)
            g = g_all[rows]
            q = q_ref[rows, :].astype(F32) * (GLA_DK ** -0.5)
            k = k_ref[rows, :].astype(F32)
            g_hi = g.astype(BF16)
            g_r1 = g - g_hi.astype(F32)
            g_mid = g_r1.astype(BF16)
            g_lo = (g_r1 - g_mid.astype(F32)).astype(BF16)
            g3 = jnp.concatenate([g_hi, g_mid, g_lo], axis=0)
            a = lax.dot_general(stack_heads(q.astype(BF16)), k.astype(BF16), nt_dims, preferred_element_type=F32)
            a = jnp.where(a_row == a_col, a, 0.0)
            for lvl in range(GLA_LEVELS):
                half = ck >> (lvl + 1)
                g_mid_row = jnp.dot(sel_ref[lvl], g3, preferred_element_type=F32)
                d = g - g_mid_row
                second = (t_idx // half) % 2 == 1
                qd = jnp.where(second, q * jnp.exp(jnp.minimum(d, 0.0)), 0.0).astype(BF16)
                kd = jnp.where(second, 0.0, k * jnp.exp(jnp.minimum(-d, 0.0))).astype(BF16)
                part = lax.dot_general(stack_heads(qd), kd, nt_dims, preferred_element_type=F32)
                a = a + jnp.where(a_row // (2 * half) == a_col // (2 * half), part, 0.0)
            redo.append(inter_ref[rows, :] + intra_output(a.astype(BF16), v_ref[rows, :]))
        write_normed(jnp.concatenate(redo, axis=0))


def _gla(proj, small, w_hi, w_lo, b_up, g_out, ones64, sel, batch, seq):
    n = proj.shape[0]
    tc = min(ROWS_GLA, seq)
    nt = seq // tc
    hk = N_HEADS * GLA_DK
    full = lambda shape: pl.BlockSpec(shape, lambda b, t: (0, 0))
    return pl.pallas_call(
        functools.partial(_gla_kernel, tc=tc),
        grid=(batch, nt),
        in_specs=[
            pl.BlockSpec((tc, hk), lambda b, t: (b * nt + t, COL_AQ // hk)),
            pl.BlockSpec((tc, hk), lambda b, t: (b * nt + t, COL_AK // hk)),
            pl.BlockSpec((tc, GROUP_W), lambda b, t: (b * nt + t, COL_AV // GROUP_W)),
            pl.BlockSpec((tc, LANES), lambda b, t: (b * nt + t, 0)),
            full((LANES, hk)), full((LANES, hk)), full((1, hk)), full((1, GROUP_W)), full((GROUP_W, GROUP_W)),
            pl.BlockSpec((GLA_LEVELS, GLA_CHUNK, 3 * GLA_CHUNK), lambda b, t: (0, 0, 0)),
        ],
        out_specs=pl.BlockSpec((tc, GROUP_W), lambda b, t: (b * nt + t, 0)),
        out_shape=jax.ShapeDtypeStruct((n, GROUP_W), BF16),
        scratch_shapes=[pltpu.VMEM((GROUP_W, hk), F32), pltpu.VMEM((tc, GROUP_W), F32)],
        compiler_params=_cparams(("parallel", "arbitrary")),
        name="gla",
    )(proj, proj, proj, small, w_hi, w_lo, b_up, g_out, ones64, sel)


def _lru_kernel(x_ref, cw_ref, cb_ref, wr_ref, br_ref, wi_ref, bi_ref, lam_ref, o_ref, xprev_ref, h_ref, *, tr):
    @pl.when(pl.program_id(1) == 0)
    def _():
        xprev_ref[...] = jnp.zeros_like(xprev_ref)
        h_ref[...] = jnp.zeros_like(h_ref)

    x3 = x_ref[...].astype(F32).reshape(tr // SUBLANES, SUBLANES, GROUP_W)
    xg = jnp.concatenate([xprev_ref[...][None], x3], axis=0)
    xprev_ref[...] = x3[tr // SUBLANES - 1]
    sub3 = lax.broadcasted_iota(jnp.int32, x3.shape, 1)
    xc3 = cb_ref[...][None] + cw_ref[CONV_W - 1:CONV_W, :][None] * x3
    for s in range(1, CONV_W):
        rot = pltpu.roll(xg, s, 1)
        shifted = jnp.where(sub3 >= s, rot[1:], rot[:-1])
        xc3 = xc3 + cw_ref[CONV_W - 1 - s:CONV_W - s, :][None] * shifted
    xc = xc3.reshape(tr, GROUP_W)

    xb = xc.astype(BF16)
    t_r = jnp.tanh(0.5 * (jnp.dot(xb, wr_ref[...], preferred_element_type=F32) + br_ref[...]))
    t_i = jnp.tanh(0.5 * (jnp.dot(xb, wi_ref[...], preferred_element_type=F32) + bi_ref[...]))
    lam = lam_ref[...]
    softplus = jnp.maximum(-lam, 0.0) + jnp.log1p(jnp.exp(-jnp.abs(lam)))
    half_rate = (-0.5 * LRU_C) * softplus
    log_a = half_rate * t_r + half_rate
    one_minus_a = _neg_expm1(log_a)
    a = 1.0 - one_minus_a
    half_x = 0.5 * xc
    bx = jnp.sqrt(one_minus_a * (2.0 - one_minus_a)) * (half_x * t_i + half_x)

    a3 = a.reshape(tr // SUBLANES, SUBLANES, GROUP_W)
    b3 = bx.reshape(tr // SUBLANES, SUBLANES, GROUP_W)
    sub = lax.broadcasted_iota(jnp.int32, a3.shape, 1)
    for s in (1, 2, 4):
        keep = sub >= s
        a_prev = jnp.where(keep, pltpu.roll(a3, s, 1), 1.0)
        b_prev = jnp.where(keep, pltpu.roll(b3, s, 1), 0.0)
        b3 = a3 * b_prev + b3
        a3 = a3 * a_prev
    h_prev = h_ref[0:1, :]
    for g in range(tr // SUBLANES):
        hg = a3[g] * h_prev + b3[g]
        o_ref[g * SUBLANES:(g + 1) * SUBLANES, :] = hg.astype(BF16)
        h_prev = hg[SUBLANES - 1:SUBLANES, :]
    h_ref[0:1, :] = h_prev


def _lru(proj, conv_w, conv_b, wr, br, wi, bi, lam, batch, seq):
    n = proj.shape[0]
    tr = min(ROWS_LRU, seq)
    nt = seq // tr
    full = lambda shape: pl.BlockSpec(shape, lambda b, t: (0, 0))
    return pl.pallas_call(
        functools.partial(_lru_kernel, tr=tr),
        grid=(batch, nt),
        in_specs=[
            pl.BlockSpec((tr, GROUP_W), lambda b, t: (b * nt + t, COL_DX // GROUP_W)),
            full((SUBLANES, GROUP_W)), full((1, GROUP_W)),
            full((GROUP_W, GROUP_W)), full((1, GROUP_W)),
            full((GROUP_W, GROUP_W)), full((1, GROUP_W)), full((1, GROUP_W)),
        ],
        out_specs=pl.BlockSpec((tr, GROUP_W), lambda b, t: (b * nt + t, 0)),
        out_shape=jax.ShapeDtypeStruct((n, GROUP_W), BF16),
        scratch_shapes=[pltpu.VMEM((SUBLANES, GROUP_W), F32), pltpu.VMEM((SUBLANES, GROUP_W), F32)],
        compiler_params=_cparams(("parallel", "arbitrary")),
        name="rglru",
    )(proj, conv_w, conv_b, wr, br, wi, bi, lam)


def _pad_cols(w, width):
    return jnp.pad(w, [(0, 0)] * (w.ndim - 1) + [(0, width - w.shape[-1])])


def _block_diag_ones(n_blocks, size):
    return jnp.asarray(np.kron(np.eye(n_blocks, dtype=np.float32), np.ones((size, size), np.float32)), BF16)


def _swap_rotary(w):
    lead = w.shape[:-1]
    w = w.reshape(lead + (N_HEADS, MLA_NOPE + MLA_ROPE))
    half = MLA_ROPE // 2
    z1, z2 = w[..., MLA_NOPE:MLA_NOPE + half], w[..., MLA_NOPE + half:]
    w = jnp.concatenate([jnp.zeros_like(w[..., :MLA_NOPE]), z2, z1], axis=-1)
    return w.reshape(lead + (N_HEADS * (MLA_NOPE + MLA_ROPE),))


def _pad_heads(w, d_real):
    lead = w.shape[:-1]
    w = w.reshape(lead + (N_HEADS, d_real))
    w = jnp.pad(w, [(0, 0)] * len(lead) + [(0, 0), (0, HEAD_PAD - d_real)])
    return w.reshape(lead + (N_HEADS * HEAD_PAD,))


def kernel(x, positions, norm_g, w_in, w_out, gla_w_up, gla_b_up, gla_g_out, fox_b_f, fox_g_q, fox_g_k, mla_g_cq, mla_w_uq, mla_g_ckv, mla_w_ukv, mla_g_q, mla_g_k, lru_conv_w, lru_conv_b, lru_w_r, lru_b_r, lru_w_i, lru_b_i, lru_lam):
    batch, seq, _ = x.shape
    depth = w_in.shape[0]
    n = batch * seq

    gla_qk = N_HEADS * GLA_DK
    sizes = (D_MIX, gla_qk, gla_qk, GROUP_W, GLA_RANK, GROUP_W, GROUP_W, GROUP_W, N_HEADS,
             MLA_Q_RANK, MLA_KV_RANK, MLA_ROPE, GROUP_W)
    offs = np.concatenate([[0], np.cumsum(sizes)])
    seg = [w_in[:, :, offs[i]:offs[i + 1]] for i in range(len(sizes))]
    (w_gate, w_aq, w_ak, w_av, w_ag, w_bq, w_bk, w_bv, w_bf, w_cq, w_ckv, w_ckr, w_dx) = seg
    assert (SM_AG, SM_BF) == (MLA_ROPE, MLA_ROPE + GLA_RANK)
    w_kr_gates = _pad_cols(jnp.concatenate([w_ckr, w_ag, w_bf], axis=-1), LANES)
    w_main = jnp.concatenate(
        [w_gate, w_aq, w_ak, w_av, w_bq, w_bk, w_bv, w_cq, w_ckv, w_kr_gates, w_dx], axis=-1).astype(BF16)
    w_out_b = w_out.astype(BF16)

    ones64 = _block_diag_ones(GROUP_W // HEAD_DIM, HEAD_DIM)
    ones128 = _block_diag_ones(MXU_WIDTH // HEAD_PAD, HEAD_PAD)

    wup = jnp.pad(gla_w_up, ((0, 0), (SM_AG, LANES - SM_AG - GLA_RANK), (0, 0)))
    wup_hi = wup.astype(BF16)
    wup_lo = (wup - wup_hi.astype(F32)).astype(BF16)

    sel = np.zeros((GLA_LEVELS, GLA_CHUNK, 3 * GLA_CHUNK), np.float32)
    for lvl in range(GLA_LEVELS):
        half = GLA_CHUNK >> (lvl + 1)
        for t in range(GLA_CHUNK):
            mid = (t // (2 * half)) * 2 * half + half - 1
            for piece in range(3):
                sel[lvl, t, piece * GLA_CHUNK + mid] = 1.0
    gla_sel = jnp.asarray(sel, BF16)

    fox_scale = LOG2E
    mla_scale = (MLA_NOPE + MLA_ROPE) ** -0.5 * LOG2E
    fox_gq = jnp.tile(fox_g_q, (1, N_HEADS)) * fox_scale
    fox_gk = jnp.tile(fox_g_k, (1, N_HEADS)) * HEAD_DIM ** 0.5
    place = np.zeros((LANES, N_HEADS * HEAD_PAD), np.float32)
    aug_q = np.zeros((1, N_HEADS * HEAD_PAD), np.float32)
    for h in range(N_HEADS):
        base = h * HEAD_PAD + (HEAD_DIM if h % 2 == 0 else 0)
        for piece in range(3):
            place[SM_BF + piece * PIECE_STRIDE + h, base + piece] = 1.0
            aug_q[0, base + piece] = 1.0
    place, aug_q = jnp.asarray(place, BF16), jnp.asarray(aug_q)
    fox_bf = jnp.pad(fox_b_f, ((0, 0), (SM_BF, LANES - SM_BF - N_HEADS)))

    wuq = _pad_heads(mla_w_uq, MLA_NOPE + MLA_ROPE).astype(BF16)
    wuq_sw = _pad_heads(_swap_rotary(mla_w_uq), MLA_NOPE + MLA_ROPE).astype(BF16)
    wukv = mla_w_ukv.reshape(depth, MLA_KV_RANK, N_HEADS, 2 * HEAD_DIM)
    wuk = _pad_heads(wukv[..., :MLA_NOPE].reshape(depth, MLA_KV_RANK, N_HEADS * MLA_NOPE), MLA_NOPE).astype(BF16)
    pad_rows = lambda m: jnp.pad(m.reshape(m.shape[:-2] + (N_HEADS, HEAD_DIM, m.shape[-1])),
                                 [(0, 0)] * (m.ndim - 1) + [(0, VT_ROWS - HEAD_DIM), (0, 0)]
                                 ).reshape(m.shape[:-2] + (N_HEADS * VT_ROWS, m.shape[-1]))
    wuv_t = pad_rows(jnp.swapaxes(wukv[..., MLA_NOPE:].reshape(depth, MLA_KV_RANK, GROUP_W), 1, 2)).astype(BF16)
    eye = pad_rows(jnp.eye(GROUP_W, dtype=F32)).astype(BF16)
    d_qk = MLA_NOPE + MLA_ROPE
    half = MLA_ROPE // 2
    swap_halves = lambda g: jnp.concatenate(
        [jnp.zeros_like(g[:, :MLA_NOPE]), g[:, MLA_NOPE + half:], g[:, MLA_NOPE:MLA_NOPE + half]], axis=-1)
    mla_gq_all = mla_g_q * (mla_scale * d_qk ** 0.5)
    mla_gk_all = mla_g_k * d_qk ** 0.5
    mla_gq, mla_gq_sw = _pad_cols(mla_gq_all, HEAD_PAD), _pad_cols(swap_halves(mla_gq_all), HEAD_PAD)
    mla_gk, mla_gk_sw = _pad_cols(mla_gk_all, HEAD_PAD), _pad_cols(swap_halves(mla_gk_all), HEAD_PAD)

    eye_h = jnp.eye(N_HEADS, dtype=F32)
    wr_bd = jnp.einsum("lncd,nm->lncmd", lru_w_r, eye_h).reshape(depth, GROUP_W, GROUP_W).astype(BF16)
    wi_bd = jnp.einsum("lncd,nm->lncmd", lru_w_i, eye_h).reshape(depth, GROUP_W, GROUP_W).astype(BF16)
    conv_w = jnp.pad(lru_conv_w, ((0, 0), (0, SUBLANES - CONV_W), (0, 0)))

    inv_freq = ROPE_THETA ** (-jnp.arange(half, dtype=F32) / half)
    invf_row = jnp.pad(jnp.concatenate([inv_freq, inv_freq]), (MLA_NOPE, LANES - MLA_NOPE - MLA_ROPE))[None, :]
    cos_t, sin_t = _rope_tables(positions.reshape(n, 1), invf_row)

    xf = x.reshape(n, D_MODEL)
    for l in range(depth):
        proj, small = _norm_inproj(xf, norm_g[l][None, :], w_main, l)
        ya = _gla(proj, small, wup_hi[l], wup_lo[l], gla_b_up[l][None, :],
                  jnp.tile(gla_g_out[l], N_HEADS)[None, :], ones64, gla_sel, batch, seq)
        fq, fk, fvt = _fox_prep(proj, small, fox_bf[l][None, :], fox_gq[l][None, :], fox_gk[l][None, :], ones64, eye,
                                place, aug_q, batch, seq)
        yb = _attention(fq, fk, fvt, batch, seq)
        mq, mk, mvt = _mla_prep(proj, cos_t, sin_t, mla_g_cq[l][None, :], wuq[l], wuq_sw[l], mla_g_ckv[l][None, :],
                                wuk[l], wuv_t[l], mla_gq[l][None, :], mla_gq_sw[l][None, :], mla_gk[l][None, :],
                                mla_gk_sw[l][None, :], ones128, batch, seq)
        yc = _attention(mq, mk, mvt, batch, seq)
        yd = _lru(proj, conv_w[l], lru_conv_b[l][None, :], wr_bd[l], lru_b_r[l][None, :], wi_bd[l],
                  lru_b_i[l][None, :], lru_lam[l][None, :], batch, seq)
        xf = _gate_outproj(ya, yb, yc, yd, proj, w_out_b, xf, l)
    return xf.reshape(batch, seq, D_MODEL)
```

```python
import functools

import numpy as np
import jax
import jax.numpy as jnp
from jax import lax
from jax.experimental import pallas as pl
from jax.experimental.pallas import tpu as pltpu

F32 = jnp.float32
BF16 = jnp.bfloat16

D_MODEL = 1024
D_MIX = 2048
GROUP_W = 512
N_HEADS = 8
HEAD_DIM = 64
EPS = 1e-6
GLA_DK = 32
GLA_RANK = 16
GLA_NORMALIZER = 16.0
GLA_CHUNK = 64
GLA_LEVELS = 6
GLA_SAFE_LOG_DECAY = 40.0
MLA_Q_RANK = 256
MLA_KV_RANK = 128
MLA_NOPE = 64
MLA_ROPE = 32
ROPE_THETA = 10000.0
CONV_W = 4
LRU_C = 8.0
LOG2E = 1.4426950408889634
NEG_BIG = -1e30
EXPM1_SERIES_RANGE = 1.0 / 64.0

LANES = 128
SUBLANES = 8
MXU_WIDTH = 256
HEAD_PAD = 128
VT_ROWS = 80

COL_AQ, COL_AK, COL_AV = 2048, 2304, 2560
COL_BQ, COL_BK, COL_BV = 3072, 3584, 4096
COL_CQ, COL_CKV, COL_CKR = 4608, 4864, 4992
COL_DX = 5120
D_PROJ = 5632
SM_AG, SM_BF = 32, 48
PIECE_STRIDE = 16


ROWS_INPROJ = 1024
ROWS_OUTPROJ = 1024
ROWS_PREP = 1024
ROWS_GLA = 512
ROWS_LRU = 1024
ATTN_BLOCK = 512
ATTN_HEADS_PER_STEP = 8
VMEM_LIMIT_MB = 48
VMEM_LIMIT_INPROJ_MB = 56


def _cparams(sem, vmem_mb=VMEM_LIMIT_MB):
    return pltpu.CompilerParams(dimension_semantics=sem, vmem_limit_bytes=vmem_mb * 1024 * 1024)


def _sigmoid(z):
    return 0.5 * jnp.tanh(0.5 * z) + 0.5


def _with_ones_rows(vt):
    row = lax.broadcasted_iota(jnp.int32, vt.shape, 0) % VT_ROWS
    return jnp.where(row == HEAD_DIM, 1.0, vt)


def _neg_expm1(z):
    series = -z * (1.0 + z * (0.5 + z * (1.0 / 6.0 + z * (1.0 / 24.0))))
    return jnp.where(z > -EXPM1_SERIES_RANGE, series, 1.0 - jnp.exp(z))


def _log_sigmoid(z):
    return jnp.minimum(z, 0.0) - jnp.log1p(jnp.exp(-jnp.abs(z)))


def _row_prefix_sum(v, seg):
    rows, width = v.shape
    g = v.reshape(rows // SUBLANES, SUBLANES, width)
    sub = lax.broadcasted_iota(jnp.int32, g.shape, 1)
    for s in (1, 2, 4):
        g = g + jnp.where(sub >= s, pltpu.roll(g, s, 1), 0.0)
    per_seg = seg // SUBLANES
    g = g.reshape(rows // seg, per_seg, SUBLANES, width)
    groups = [g[:, 0]]
    for i in range(1, per_seg):
        groups.append(g[:, i] + groups[-1][:, SUBLANES - 1:SUBLANES, :])
    return jnp.stack(groups, axis=1).reshape(rows, width)


def _inproj_kernel(x_ref, g_ref, w_ref, o_ref, os_ref, h_ref, *, small_col):
    @pl.when(pl.program_id(1) == 0)
    def _():
        x = x_ref[...]
        ms = jnp.mean(x * x, axis=-1, keepdims=True)
        h_ref[...] = (x * lax.rsqrt(ms + EPS) * g_ref[...]).astype(BF16)

    res = jnp.dot(h_ref[...], w_ref[...], preferred_element_type=F32)
    o_ref[...] = res.astype(BF16)

    @pl.when(pl.program_id(1) == pl.num_programs(1) - 1)
    def _():
        os_ref[...] = res[:, small_col:small_col + LANES]


def _norm_inproj(x, g, w_all, layer):
    n = x.shape[0]
    tm = min(ROWS_INPROJ, n)
    tn = D_PROJ // 2
    assert COL_CKR >= D_PROJ - tn
    return pl.pallas_call(
        functools.partial(_inproj_kernel, small_col=COL_CKR - (D_PROJ - tn)),
        grid=(n // tm, D_PROJ // tn),
        in_specs=[
            pl.BlockSpec((tm, D_MODEL), lambda i, j: (i, 0)),
            pl.BlockSpec((1, D_MODEL), lambda i, j: (0, 0)),
            pl.BlockSpec((None, D_MODEL, tn), lambda i, j: (layer, 0, j)),
        ],
        out_specs=[
            pl.BlockSpec((tm, tn), lambda i, j: (i, j)),
            pl.BlockSpec((tm, LANES), lambda i, j: (i, 0)),
        ],
        out_shape=[jax.ShapeDtypeStruct((n, D_PROJ), BF16), jax.ShapeDtypeStruct((n, LANES), F32)],
        scratch_shapes=[pltpu.VMEM((tm, D_MODEL), BF16)],
        compiler_params=_cparams(("parallel", "arbitrary"), vmem_mb=VMEM_LIMIT_INPROJ_MB),
        name="norm_inproj",
    )(x, g, w_all)


def _outproj_kernel(ya_ref, yb_ref, yc_ref, yd_ref, gate_ref, w_ref, x_ref, o_ref, z_ref):
    for gi, y_ref in enumerate((ya_ref, yb_ref, yc_ref, yd_ref)):
        lo, hi = gi * GROUP_W, (gi + 1) * GROUP_W
        g = gate_ref[:, lo:hi].astype(F32)
        z_ref[:, lo:hi] = (y_ref[...].astype(F32) * (g * _sigmoid(g))).astype(BF16)
    o_ref[...] = x_ref[...] + jnp.dot(z_ref[...], w_ref[...], preferred_element_type=F32)


def _gate_outproj(ya, yb, yc, yd, proj, w_all, x, layer):
    n = x.shape[0]
    tm = min(ROWS_OUTPROJ, n)
    yspec = pl.BlockSpec((tm, GROUP_W), lambda i: (i, 0))
    return pl.pallas_call(
        _outproj_kernel,
        grid=(n // tm,),
        in_specs=[
            yspec, yspec, yspec, yspec,
            pl.BlockSpec((tm, D_MIX), lambda i: (i, 0)),
            pl.BlockSpec((None, D_MIX, D_MODEL), lambda i: (layer, 0, 0)),
            pl.BlockSpec((tm, D_MODEL), lambda i: (i, 0)),
        ],
        out_specs=pl.BlockSpec((tm, D_MODEL), lambda i: (i, 0)),
        out_shape=jax.ShapeDtypeStruct((n, D_MODEL), F32),
        scratch_shapes=[pltpu.VMEM((tm, D_MIX), BF16)],
        compiler_params=_cparams(("parallel",)),
        name="gate_outproj",
    )(ya, yb, yc, yd, proj, w_all, x)


def _attn_kernel(q_ref, k_ref, vt_ref, o_ref, m_ref, acc_ref, st_ref, mx_ref, *, tq, nq, hg):
    i = pl.program_id(2)
    half = tq // 2
    nt_dims = (((1,), (1,)), ((), ()))

    def scores(h, krows, qrows):
        lanes = slice(h * HEAD_PAD, (h + 1) * HEAD_PAD)
        return lax.dot_general(k_ref[krows, lanes], q_ref[qrows, lanes], nt_dims, preferred_element_type=F32)

    def vrows(h):
        return slice(h * VT_ROWS, (h + 1) * VT_ROWS)

    tri = lax.broadcasted_iota(jnp.int32, (half, half), 0) <= lax.broadcasted_iota(jnp.int32, (half, half), 1)
    low = lax.broadcasted_iota(jnp.int32, (tq, half), 0) <= lax.broadcasted_iota(jnp.int32, (tq, half), 1) + half

    def issue_scores(item, c):
        h, blk = item
        if blk is None:
            off = c * tq
            s_l = jnp.where(tri, scores(h, slice(off, off + half), slice(0, half)), NEG_BIG)
            s_r = jnp.where(low, scores(h, slice(off, off + tq), slice(half, tq)), NEG_BIG)
            mx = jnp.concatenate([jnp.max(s_l, axis=0, keepdims=True), jnp.max(s_r, axis=0, keepdims=True)], axis=1)
            return (s_l, s_r, mx)
        st = scores(h, slice(blk * tq, (blk + 1) * tq), slice(0, tq))
        return (st, jnp.max(st, axis=0, keepdims=True))

    def store_scores(item, slot, vals):
        if item[1] is None:
            st_ref[slot, 0:half, 0:half] = vals[0]
            st_ref[slot, :, half:tq] = vals[1]
        else:
            st_ref[slot] = vals[0]
        mx_ref[slot] = vals[-1]

    def consume(item, slot, c):
        h, blk = item
        opens = blk is None
        if opens:
            m_new = mx_ref[slot]
        else:
            m = m_ref[h]
            m_new = jnp.maximum(m, mx_ref[slot])
            alpha = jnp.exp2(m - m_new)
        m_ref[h] = m_new

        def accumulate(cols, st, vt):
            p = jnp.exp2(st - m_new[:, cols]).astype(BF16)
            pv = jnp.dot(vt, p, preferred_element_type=F32)
            acc_ref[h, :, cols] = pv if opens else alpha[:, cols] * acc_ref[h, :, cols] + pv

        if blk is None:
            off = c * tq
            accumulate(slice(0, half), st_ref[slot, 0:half, 0:half], vt_ref[vrows(h), off:off + half])
            accumulate(slice(half, tq), st_ref[slot, :, half:tq], vt_ref[vrows(h), off:off + tq])
        else:
            accumulate(slice(0, tq), st_ref[slot], vt_ref[vrows(h), blk * tq:(blk + 1) * tq])

    for c in range(nq):
        @pl.when(i == c)
        def _(c=c):
            items = [(h, None) for h in range(hg)] + [(h, blk) for blk in range(c) for h in range(hg)]
            store_scores(items[0], 0, issue_scores(items[0], c))
            for n, item in enumerate(items):
                nxt = items[n + 1] if n + 1 < len(items) else None
                if nxt is not None:
                    vals = issue_scores(nxt, c)
                consume(item, n % 2, c)
                if nxt is not None:
                    store_scores(nxt, (n + 1) % 2, vals)

    ot = jnp.concatenate([acc_ref[h, 0:HEAD_DIM, :] * (1.0 / acc_ref[h, HEAD_DIM:HEAD_DIM + 1, :]) for h in range(hg)],
                         axis=0)
    o_ref[...] = ot.T.astype(BF16)


def _attention(q, k, vt, batch, seq):
    n = q.shape[0]
    tq = min(ATTN_BLOCK, seq)
    nq = seq // tq
    hg = ATTN_HEADS_PER_STEP
    return pl.pallas_call(
        functools.partial(_attn_kernel, tq=tq, nq=nq, hg=hg),
        grid=(batch, N_HEADS // hg, nq),
        in_specs=[
            pl.BlockSpec((tq, hg * HEAD_PAD), lambda b, p, i: (b * nq + i, p)),
            pl.BlockSpec((seq, hg * HEAD_PAD), lambda b, p, i: (b, p)),
            pl.BlockSpec((None, hg * VT_ROWS, seq), lambda b, p, i: (b, p, 0)),
        ],
        out_specs=pl.BlockSpec((tq, hg * HEAD_DIM), lambda b, p, i: (b * nq + i, p)),
        out_shape=jax.ShapeDtypeStruct((n, GROUP_W), BF16),
        scratch_shapes=[pltpu.VMEM((hg, 1, tq), F32), pltpu.VMEM((hg, VT_ROWS, tq), F32),
                        pltpu.VMEM((2, tq, tq), F32), pltpu.VMEM((2, 1, tq), F32)],
        compiler_params=_cparams(("parallel", "parallel", "arbitrary")),
        name="causal_attention",
    )(q, k, vt)


def _fox_prep_kernel(q_ref, k_ref, v_ref, sm_ref, bf_ref, gq_ref, gk_ref, ones_ref, eye_ref, place_ref, augq_ref,
                     qo_ref, ko_ref, vt_ref, carry_ref, *, tt):
    @pl.when(pl.program_id(1) == 0)
    def _():
        carry_ref[...] = jnp.zeros_like(carry_ref)

    vt = lax.dot_general(eye_ref[...], v_ref[...], (((1,), (1,)), ((), ())), preferred_element_type=F32)
    vt_ref[...] = _with_ones_rows(vt).astype(BF16)

    def headnorm(ref, g_ref):
        z = ref[...].astype(F32)
        ssq = jnp.dot((z * z).astype(BF16), ones_ref[...], preferred_element_type=F32)
        return z * lax.rsqrt(ssq + HEAD_DIM * EPS) * g_ref[...]

    qn = headnorm(q_ref, gq_ref)
    kn = headnorm(k_ref, gk_ref)

    log_f = _log_sigmoid(sm_ref[...] + bf_ref[...])
    cum = _row_prefix_sum(log_f, tt) + carry_ref[0:1, :]
    carry_ref[0:1, :] = cum[tt - 1:tt, :]
    neg = cum * (-LOG2E)

    lane = lax.broadcasted_iota(jnp.int32, (tt, LANES), 1)
    hi = neg.astype(BF16).astype(F32)
    r1 = neg - hi
    mid = r1.astype(BF16).astype(F32)
    lo = r1 - mid
    pieces = jnp.where(lane < SM_BF + PIECE_STRIDE, hi,
                       jnp.where(lane < SM_BF + 2 * PIECE_STRIDE, pltpu.roll(mid, PIECE_STRIDE, 1),
                                 pltpu.roll(lo, 2 * PIECE_STRIDE, 1)))
    aug_k = jnp.dot(pieces.astype(BF16), place_ref[...], preferred_element_type=F32)
    aug_q = augq_ref[...]
    for h in range(N_HEADS):
        pair, e = h // 2, h % 2
        in_head = (lane >= e * HEAD_DIM) & (lane < (e + 1) * HEAD_DIM)
        blk = slice(h * HEAD_PAD, (h + 1) * HEAD_PAD)
        kpair = kn[:, pair * LANES:(pair + 1) * LANES]
        qpair = qn[:, pair * LANES:(pair + 1) * LANES]
        ko_ref[:, blk] = jnp.where(in_head, kpair, aug_k[:, blk]).astype(BF16)
        qo_ref[:, blk] = jnp.where(in_head, qpair, aug_q[:, blk]).astype(BF16)


def _fox_prep(proj, small, bf_row, gq, gk, ones64, eye, place, aug_q, batch, seq):
    n = proj.shape[0]
    tt = min(ROWS_PREP, seq)
    nt = seq // tt
    row = lambda shape: pl.BlockSpec(shape, lambda b, t: (0, 0))
    return pl.pallas_call(
        functools.partial(_fox_prep_kernel, tt=tt),
        grid=(batch, nt),
        in_specs=[
            pl.BlockSpec((tt, GROUP_W), lambda b, t: (b * nt + t, COL_BQ // GROUP_W)),
            pl.BlockSpec((tt, GROUP_W), lambda b, t: (b * nt + t, COL_BK // GROUP_W)),
            pl.BlockSpec((tt, GROUP_W), lambda b, t: (b * nt + t, COL_BV // GROUP_W)),
            pl.BlockSpec((tt, LANES), lambda b, t: (b * nt + t, 0)),
            row((1, LANES)), row((1, GROUP_W)), row((1, GROUP_W)), row((GROUP_W, GROUP_W)),
            row((N_HEADS * VT_ROWS, GROUP_W)), row((LANES, N_HEADS * HEAD_PAD)), row((1, N_HEADS * HEAD_PAD)),
        ],
        out_specs=[
            pl.BlockSpec((tt, N_HEADS * HEAD_PAD), lambda b, t: (b * nt + t, 0)),
            pl.BlockSpec((tt, N_HEADS * HEAD_PAD), lambda b, t: (b * nt + t, 0)),
            pl.BlockSpec((None, N_HEADS * VT_ROWS, tt), lambda b, t: (b, 0, t)),
        ],
        out_shape=[jax.ShapeDtypeStruct((n, N_HEADS * HEAD_PAD), BF16)] * 2
        + [jax.ShapeDtypeStruct((batch, N_HEADS * VT_ROWS, seq), BF16)],
        scratch_shapes=[pltpu.VMEM((SUBLANES, LANES), F32)],
        compiler_params=_cparams(("parallel", "arbitrary")),
        name="fox_prep",
    )(proj, proj, proj, small, bf_row, gq, gk, ones64, eye, place, aug_q)


def _rope_kernel(pos_ref, invf_ref, c_ref, s_ref):
    ang = pos_ref[...].astype(F32) * invf_ref[...]
    lane = lax.broadcasted_iota(jnp.int32, ang.shape, 1)
    half = MLA_ROPE // 2
    c, s = jnp.cos(ang), jnp.sin(ang)
    c_ref[...] = jnp.where((lane >= MLA_NOPE) & (lane < MLA_NOPE + MLA_ROPE), c, 1.0)
    s_ref[...] = jnp.where((lane >= MLA_NOPE) & (lane < MLA_NOPE + half), -s,
                           jnp.where((lane >= MLA_NOPE + half) & (lane < MLA_NOPE + MLA_ROPE), s, 0.0))


def _rope_tables(pos_col, invf_row):
    n = pos_col.shape[0]
    tm = min(ROWS_PREP, n)
    out = pl.BlockSpec((tm, LANES), lambda i: (i, 0))
    return pl.pallas_call(
        _rope_kernel,
        grid=(n // tm,),
        in_specs=[pl.BlockSpec((tm, 1), lambda i: (i, 0)), pl.BlockSpec((1, LANES), lambda i: (0, 0))],
        out_specs=[out, out],
        out_shape=[jax.ShapeDtypeStruct((n, LANES), F32)] * 2,
        compiler_params=_cparams(("parallel",)),
        name="rope_tables",
    )(pos_col, invf_row)


def _mla_prep_kernel(cq_ref, ckv_ref, ckr_ref, c_ref, s_ref, gcq_ref, wuq_ref, wuqs_ref, gckv_ref, wuk_ref, wuv_ref,
                     gq_ref, gqs_ref, gk_ref, gks_ref, ones_ref, qo_ref, ko_ref, vo_ref):
    d_qk = float(MLA_NOPE + MLA_ROPE)
    cos, sin = c_ref[...], s_ref[...]
    half = MLA_ROPE // 2

    def rms(z, g):
        return (z * lax.rsqrt(jnp.mean(z * z, axis=-1, keepdims=True) + EPS) * g).astype(BF16)

    def head_ssq(z):
        zz = (z * z).astype(BF16)
        parts = [jnp.dot(zz[:, c * MXU_WIDTH:(c + 1) * MXU_WIDTH], ones_ref[...], preferred_element_type=F32)
                 for c in range(z.shape[1] // MXU_WIDTH)]
        return jnp.concatenate(parts, axis=1)

    cn = rms(cq_ref[...].astype(F32), gcq_ref[...])
    q = jnp.dot(cn, wuq_ref[...], preferred_element_type=F32)
    q_sw = jnp.dot(cn, wuqs_ref[...], preferred_element_type=F32)
    q_inv = lax.rsqrt(head_ssq(q) + d_qk * EPS)

    kvn = rms(ckv_ref[...].astype(F32), gckv_ref[...])
    kn = jnp.dot(kvn, wuk_ref[...], preferred_element_type=F32)
    vt = lax.dot_general(wuv_ref[...], kvn, (((1,), (1,)), ((), ())), preferred_element_type=F32)
    vo_ref[...] = _with_ones_rows(vt).astype(BF16)
    lane = lax.broadcasted_iota(jnp.int32, ckr_ref.shape, 1)
    kr = jnp.where(lane < MLA_ROPE, ckr_ref[...].astype(F32), 0.0)
    ss_r = jnp.sum(kr * kr, axis=-1, keepdims=True)
    kr_placed = pltpu.roll(kr, MLA_NOPE, 1)
    kr_swapped = jnp.where(lane < MLA_NOPE + half, pltpu.roll(kr, MLA_NOPE - half, 1), pltpu.roll(kr, MLA_NOPE + half, 1))
    kr_swapped = jnp.where((lane >= MLA_NOPE) & (lane < MLA_NOPE + MLA_ROPE), kr_swapped, 0.0)
    k_inv = lax.rsqrt(head_ssq(kn) + ss_r + d_qk * EPS)

    gc_q, gs_q = gq_ref[...] * cos, gqs_ref[...] * sin
    gc_k = gk_ref[...] * cos
    kr_rot = kr_placed * gc_k + kr_swapped * (gks_ref[...] * sin)
    for h in range(N_HEADS):
        sl = slice(h * HEAD_PAD, (h + 1) * HEAD_PAD)
        qo_ref[:, sl] = (q_inv[:, sl] * (q[:, sl] * gc_q + q_sw[:, sl] * gs_q)).astype(BF16)
        ko_ref[:, sl] = (k_inv[:, sl] * (kn[:, sl] * gc_k + kr_rot)).astype(BF16)


def _mla_prep(proj, cos_t, sin_t, gcq, wuq, wuq_sw, gckv, wuk, wuv_t, gq, gq_sw, gk, gk_sw, ones128, batch, seq):
    n = proj.shape[0]
    tm = min(ROWS_PREP, seq)
    nt = seq // tm
    full = lambda shape: pl.BlockSpec(shape, lambda i: (0, 0))
    tab = pl.BlockSpec((tm, LANES), lambda i: (i, 0))
    wide = pl.BlockSpec((tm, N_HEADS * HEAD_PAD), lambda i: (i, 0))
    return pl.pallas_call(
        _mla_prep_kernel,
        grid=(n // tm,),
        in_specs=[
            pl.BlockSpec((tm, MLA_Q_RANK), lambda i: (i, COL_CQ // MLA_Q_RANK)),
            pl.BlockSpec((tm, MLA_KV_RANK), lambda i: (i, COL_CKV // LANES)),
            pl.BlockSpec((tm, LANES), lambda i: (i, COL_CKR // LANES)),
            tab, tab,
            full((1, MLA_Q_RANK)), full((MLA_Q_RANK, N_HEADS * HEAD_PAD)), full((MLA_Q_RANK, N_HEADS * HEAD_PAD)),
            full((1, MLA_KV_RANK)), full((MLA_KV_RANK, N_HEADS * HEAD_PAD)), full((N_HEADS * VT_ROWS, MLA_KV_RANK)),
            full((1, HEAD_PAD)), full((1, HEAD_PAD)), full((1, HEAD_PAD)), full((1, HEAD_PAD)), full((MXU_WIDTH, MXU_WIDTH)),
        ],
        out_specs=[wide, wide, pl.BlockSpec((None, N_HEADS * VT_ROWS, tm), lambda i: (i // nt, 0, i % nt))],
        out_shape=[jax.ShapeDtypeStruct((n, N_HEADS * HEAD_PAD), BF16)] * 2
        + [jax.ShapeDtypeStruct((batch, N_HEADS * VT_ROWS, seq), BF16)],
        compiler_params=_cparams(("parallel",)),
        name="mla_prep",
    )(proj, proj, proj, cos_t, sin_t, gcq, wuq, wuq_sw, gckv, wuk, wuv_t, gq, gq_sw, gk, gk_sw, ones128)


def _gla_kernel(q_ref, k_ref, v_ref, sm_ref, wh_ref, wl_ref, b_ref, go_ref, ones_ref, sel_ref, o_ref, st_ref,
                inter_ref, *, tc):
    hk = N_HEADS * GLA_DK
    ck = GLA_CHUNK

    @pl.when(pl.program_id(1) == 0)
    def _():
        st_ref[...] = jnp.zeros_like(st_ref)

    sm = sm_ref[...]
    sm_hi = sm.astype(BF16)
    sm_lo = (sm - sm_hi.astype(F32)).astype(BF16)
    z = (jnp.dot(sm_hi, wh_ref[...], preferred_element_type=F32)
         + jnp.dot(sm_lo, wh_ref[...], preferred_element_type=F32)
         + jnp.dot(sm_hi, wl_ref[...], preferred_element_type=F32)) + b_ref[...]
    g_all = _row_prefix_sum(_log_sigmoid(z) * (1.0 / GLA_NORMALIZER), ck)

    lane_k = lax.broadcasted_iota(jnp.int32, (ck, hk), 1) // GLA_DK
    lane_pair = lax.broadcasted_iota(jnp.int32, (ck, LANES), 1)
    a_row = lax.broadcasted_iota(jnp.int32, (N_HEADS * ck, ck), 0) % ck
    a_col = lax.broadcasted_iota(jnp.int32, (N_HEADS * ck, ck), 1)
    st_blockdiag = (lax.broadcasted_iota(jnp.int32, (GROUP_W, hk), 0) // HEAD_DIM
                    == lax.broadcasted_iota(jnp.int32, (GROUP_W, hk), 1) // GLA_DK)

    def stack_heads(x):
        return jnp.concatenate([jnp.where(lane_k == h, x, jnp.zeros_like(x)) for h in range(N_HEADS)], axis=0)

    def intra_output(a, v):
        parts = []
        for p in range(N_HEADS // 2):
            r = jnp.dot(a[2 * p * ck:2 * (p + 1) * ck, :], v[:, p * LANES:(p + 1) * LANES], preferred_element_type=F32)
            parts.append(jnp.where(lane_pair < HEAD_DIM, r[0:ck, :], r[ck:2 * ck, :]))
        return jnp.concatenate(parts, axis=1)

    def write_normed(o):
        ssq = jnp.dot((o * o).astype(BF16), ones_ref[...], preferred_element_type=F32)
        o_ref[...] = (o * lax.rsqrt(ssq * (1.0 / HEAD_DIM) + EPS) * go_ref[...]).astype(BF16)

    n_chunks = tc // ck
    qts, intras, incs, decays = [], [], [], []
    for c in range(n_chunks):
        rows = slice(c * ck, (c + 1) * ck)
        g = g_all[rows]
        q = q_ref[rows, :].astype(F32) * (GLA_DK ** -0.5)
        k = k_ref[rows, :].astype(F32)
        v = v_ref[rows, :]
        g_last = g[ck - 1:ck, :]
        qt = (q * jnp.exp(g)).astype(BF16)
        kt = (k * jnp.exp(-g)).astype(BF16)
        kh = (k * jnp.exp(g_last - g)).astype(BF16)

        u = lax.dot_general(v, kh, (((0,), (0,)), ((), ())), preferred_element_type=F32)
        incs.append(jnp.where(st_blockdiag, u, 0.0))
        decays.append(jnp.exp(g_last))

        a = lax.dot_general(stack_heads(qt), kt, (((1,), (1,)), ((), ())), preferred_element_type=F32)
        intras.append(intra_output(jnp.where(a_row >= a_col, a, 0.0).astype(BF16), v))
        qts.append(qt)

    state = st_ref[...]
    states = []
    for c in range(n_chunks):
        states.append(state.astype(BF16))
        state = state * decays[c] + incs[c]
    st_ref[...] = state

    outs = []
    for c in range(n_chunks):
        o = lax.dot_general(qts[c], states[c], (((1,), (1,)), ((), ())), preferred_element_type=F32)
        inter_ref[c * ck:(c + 1) * ck, :] = o
        outs.append(o + intras[c])
    write_normed(jnp.concatenate(outs, axis=0))

    @pl.when(jnp.min(g_all) < -GLA_SAFE_LOG_DECAY)
    def _():
        t_idx = lax.broadcasted_iota(jnp.int32, (ck, hk), 0)
        nt_dims = (((1,), (1,)), ((), ()))
        redo = []
        for c in range(n_chunks):
            rows = slice(c * ck, (c + 1) * ck)
            g = g_all[rows]
            q = q_ref[rows, :].astype(F32) * (GLA_DK ** -0.5)
            k = k_ref[rows, :].astype(F32)
            g_hi = g.astype(BF16)
            g_r1 = g - g_hi.astype(F32)
            g_mid = g_r1.astype(BF16)
            g_lo = (g_r1 - g_mid.astype(F32)).astype(BF16)
            g3 = jnp.concatenate([g_hi, g_mid, g_lo], axis=0)
            a = lax.dot_general(stack_heads(q.astype(BF16)), k.astype(BF16), nt_dims, preferred_element_type=F32)
            a = jnp.where(a_row == a_col, a, 0.0)
            for lvl in range(GLA_LEVELS):
                half = ck >> (lvl + 1)
                g_mid_row = jnp.dot(sel_ref[lvl], g3, preferred_element_type=F32)
                d = g - g_mid_row
                second = (t_idx // half) % 2 == 1
                qd = jnp.where(second, q * jnp.exp(jnp.minimum(d, 0.0)), 0.0).astype(BF16)
                kd = jnp.where(second, 0.0, k * jnp.exp(jnp.minimum(-d, 0.0))).astype(BF16)
                part = lax.dot_general(stack_heads(qd), kd, nt_dims, preferred_element_type=F32)
                a = a + jnp.where(a_row // (2 * half) == a_col // (2 * half), part, 0.0)
            redo.append(inter_ref[rows, :] + intra_output(a.astype(BF16), v_ref[rows, :]))
        write_normed(jnp.concatenate(redo, axis=0))


def _gla(proj, small, w_hi, w_lo, b_up, g_out, ones64, sel, batch, seq):
    n = proj.shape[0]
    tc = min(ROWS_GLA, seq)
    nt = seq // tc
    hk = N_HEADS * GLA_DK
    full = lambda shape: pl.BlockSpec(shape, lambda b, t: (0, 0))
    return pl.pallas_call(
        functools.partial(_gla_kernel, tc=tc),
        grid=(batch, nt),
        in_specs=[
            pl.BlockSpec((tc, hk), lambda b, t: (b * nt + t, COL_AQ // hk)),
            pl.BlockSpec((tc, hk), lambda b, t: (b * nt + t, COL_AK // hk)),
            pl.BlockSpec((tc, GROUP_W), lambda b, t: (b * nt + t, COL_AV // GROUP_W)),
            pl.BlockSpec((tc, LANES), lambda b, t: (b * nt + t, 0)),
            full((LANES, hk)), full((LANES, hk)), full((1, hk)), full((1, GROUP_W)), full((GROUP_W, GROUP_W)),
            pl.BlockSpec((GLA_LEVELS, GLA_CHUNK, 3 * GLA_CHUNK), lambda b, t: (0, 0, 0)),
        ],
        out_specs=pl.BlockSpec((tc, GROUP_W), lambda b, t: (b * nt + t, 0)),
        out_shape=jax.ShapeDtypeStruct((n, GROUP_W), BF16),
        scratch_shapes=[pltpu.VMEM((GROUP_W, hk), F32), pltpu.VMEM((tc, GROUP_W), F32)],
        compiler_params=_cparams(("parallel", "arbitrary")),
        name="gla",
    )(proj, proj, proj, small, w_hi, w_lo, b_up, g_out, ones64, sel)


def _lru_kernel(x_ref, cw_ref, cb_ref, wr_ref, br_ref, wi_ref, bi_ref, lam_ref, o_ref, xprev_ref, h_ref, *, tr):
    @pl.when(pl.program_id(1) == 0)
    def _():
        xprev_ref[...] = jnp.zeros_like(xprev_ref)
        h_ref[...] = jnp.zeros_like(h_ref)

    x3 = x_ref[...].astype(F32).reshape(tr // SUBLANES, SUBLANES, GROUP_W)
    xg = jnp.concatenate([xprev_ref[...][None], x3], axis=0)
    xprev_ref[...] = x3[tr // SUBLANES - 1]
    sub3 = lax.broadcasted_iota(jnp.int32, x3.shape, 1)
    xc3 = cb_ref[...][None] + cw_ref[CONV_W - 1:CONV_W, :][None] * x3
    for s in range(1, CONV_W):
        rot = pltpu.roll(xg, s, 1)
        shifted = jnp.where(sub3 >= s, rot[1:], rot[:-1])
        xc3 = xc3 + cw_ref[CONV_W - 1 - s:CONV_W - s, :][None] * shifted
    xc = xc3.reshape(tr, GROUP_W)

    xb = xc.astype(BF16)
    t_r = jnp.tanh(0.5 * (jnp.dot(xb, wr_ref[...], preferred_element_type=F32) + br_ref[...]))
    t_i = jnp.tanh(0.5 * (jnp.dot(xb, wi_ref[...], preferred_element_type=F32) + bi_ref[...]))
    lam = lam_ref[...]
    softplus = jnp.maximum(-lam, 0.0) + jnp.log1p(jnp.exp(-jnp.abs(lam)))
    half_rate = (-0.5 * LRU_C) * softplus
    log_a = half_rate * t_r + half_rate
    one_minus_a = _neg_expm1(log_a)
    a = 1.0 - one_minus_a
    half_x = 0.5 * xc
    bx = jnp.sqrt(one_minus_a * (2.0 - one_minus_a)) * (half_x * t_i + half_x)

    a3 = a.reshape(tr // SUBLANES, SUBLANES, GROUP_W)
    b3 = bx.reshape(tr // SUBLANES, SUBLANES, GROUP_W)
    sub = lax.broadcasted_iota(jnp.int32, a3.shape, 1)
    for s in (1, 2, 4):
        keep = sub >= s
        a_prev = jnp.where(keep, pltpu.roll(a3, s, 1), 1.0)
        b_prev = jnp.where(keep, pltpu.roll(b3, s, 1), 0.0)
        b3 = a3 * b_prev + b3
        a3 = a3 * a_prev
    h_prev = h_ref[0:1, :]
    for g in range(tr // SUBLANES):
        hg = a3[g] * h_prev + b3[g]
        o_ref[g * SUBLANES:(g + 1) * SUBLANES, :] = hg.astype(BF16)
        h_prev = hg[SUBLANES - 1:SUBLANES, :]
    h_ref[0:1, :] = h_prev


def _lru(proj, conv_w, conv_b, wr, br, wi, bi, lam, batch, seq):
    n = proj.shape[0]
    tr = min(ROWS_LRU, seq)
    nt = seq // tr
    full = lambda shape: pl.BlockSpec(shape, lambda b, t: (0, 0))
    return pl.pallas_call(
        functools.partial(_lru_kernel, tr=tr),
        grid=(batch, nt),
        in_specs=[
            pl.BlockSpec((tr, GROUP_W), lambda b, t: (b * nt + t, COL_DX // GROUP_W)),
            full((SUBLANES, GROUP_W)), full((1, GROUP_W)),
            full((GROUP_W, GROUP_W)), full((1, GROUP_W)),
            full((GROUP_W, GROUP_W)), full((1, GROUP_W)), full((1, GROUP_W)),
        ],
        out_specs=pl.BlockSpec((tr, GROUP_W), lambda b, t: (b * nt + t, 0)),
        out_shape=jax.ShapeDtypeStruct((n, GROUP_W), BF16),
        scratch_shapes=[pltpu.VMEM((SUBLANES, GROUP_W), F32), pltpu.VMEM((SUBLANES, GROUP_W), F32)],
        compiler_params=_cparams(("parallel", "arbitrary")),
        name="rglru",
    )(proj, conv_w, conv_b, wr, br, wi, bi, lam)


def _pad_cols(w, width):
    return jnp.pad(w, [(0, 0)] * (w.ndim - 1) + [(0, width - w.shape[-1])])


def _block_diag_ones(n_blocks, size):
    return jnp.asarray(np.kron(np.eye(n_blocks, dtype=np.float32), np.ones((size, size), np.float32)), BF16)


def _swap_rotary(w):
    lead = w.shape[:-1]
    w = w.reshape(lead + (N_HEADS, MLA_NOPE + MLA_ROPE))
    half = MLA_ROPE // 2
    z1, z2 = w[..., MLA_NOPE:MLA_NOPE + half], w[..., MLA_NOPE + half:]
    w = jnp.concatenate([jnp.zeros_like(w[..., :MLA_NOPE]), z2, z1], axis=-1)
    return w.reshape(lead + (N_HEADS * (MLA_NOPE + MLA_ROPE),))


def _pad_heads(w, d_real):
    lead = w.shape[:-1]
    w = w.reshape(lead + (N_HEADS, d_real))
    w = jnp.pad(w, [(0, 0)] * len(lead) + [(0, 0), (0, HEAD_PAD - d_real)])
    return w.reshape(lead + (N_HEADS * HEAD_PAD,))


def kernel(x, positions, norm_g, w_in, w_out, gla_w_up, gla_b_up, gla_g_out, fox_b_f, fox_g_q, fox_g_k, mla_g_cq, mla_w_uq, mla_g_ckv, mla_w_ukv, mla_g_q, mla_g_k, lru_conv_w, lru_conv_b, lru_w_r, lru_b_r, lru_w_i, lru_b_i, lru_lam):
    batch, seq, d_model = x.shape
    depth = w_in.shape[0]
    n = batch * seq
    assert d_model == D_MODEL and w_in.shape[1:] == (D_MODEL, sum(
        (D_MIX, 2 * N_HEADS * GLA_DK, GROUP_W, GLA_RANK, 3 * GROUP_W, N_HEADS, MLA_Q_RANK, MLA_KV_RANK, MLA_ROPE, GROUP_W)))
    for rows in (ROWS_PREP, ROWS_GLA, ROWS_LRU, ATTN_BLOCK):
        assert seq % min(rows, seq) == 0 and min(rows, seq) % GLA_CHUNK == 0, (seq, rows)
    for rows in (ROWS_INPROJ, ROWS_OUTPROJ, ROWS_PREP):
        assert n % min(rows, n) == 0, (n, rows)

    gla_qk = N_HEADS * GLA_DK
    sizes = (D_MIX, gla_qk, gla_qk, GROUP_W, GLA_RANK, GROUP_W, GROUP_W, GROUP_W, N_HEADS,
             MLA_Q_RANK, MLA_KV_RANK, MLA_ROPE, GROUP_W)
    offs = np.concatenate([[0], np.cumsum(sizes)])
    seg = [w_in[:, :, offs[i]:offs[i + 1]] for i in range(len(sizes))]
    (w_gate, w_aq, w_ak, w_av, w_ag, w_bq, w_bk, w_bv, w_bf, w_cq, w_ckv, w_ckr, w_dx) = seg
    assert (SM_AG, SM_BF) == (MLA_ROPE, MLA_ROPE + GLA_RANK)
    w_kr_gates = _pad_cols(jnp.concatenate([w_ckr, w_ag, w_bf], axis=-1), LANES)
    w_main = jnp.concatenate(
        [w_gate, w_aq, w_ak, w_av, w_bq, w_bk, w_bv, w_cq, w_ckv, w_kr_gates, w_dx], axis=-1).astype(BF16)
    w_out_b = w_out.astype(BF16)

    ones64 = _block_diag_ones(GROUP_W // HEAD_DIM, HEAD_DIM)
    ones128 = _block_diag_ones(MXU_WIDTH // HEAD_PAD, HEAD_PAD)

    wup = jnp.pad(gla_w_up, ((0, 0), (SM_AG, LANES - SM_AG - GLA_RANK), (0, 0)))
    wup_hi = wup.astype(BF16)
    wup_lo = (wup - wup_hi.astype(F32)).astype(BF16)

    sel = np.zeros((GLA_LEVELS, GLA_CHUNK, 3 * GLA_CHUNK), np.float32)
    for lvl in range(GLA_LEVELS):
        half = GLA_CHUNK >> (lvl + 1)
        for t in range(GLA_CHUNK):
            mid = (t // (2 * half)) * 2 * half + half - 1
            for piece in range(3):
                sel[lvl, t, piece * GLA_CHUNK + mid] = 1.0
    gla_sel = jnp.asarray(sel, BF16)

    fox_scale = LOG2E
    mla_scale = (MLA_NOPE + MLA_ROPE) ** -0.5 * LOG2E
    fox_gq = jnp.tile(fox_g_q, (1, N_HEADS)) * fox_scale
    fox_gk = jnp.tile(fox_g_k, (1, N_HEADS)) * HEAD_DIM ** 0.5
    place = np.zeros((LANES, N_HEADS * HEAD_PAD), np.float32)
    aug_q = np.zeros((1, N_HEADS * HEAD_PAD), np.float32)
    for h in range(N_HEADS):
        base = h * HEAD_PAD + (HEAD_DIM if h % 2 == 0 else 0)
        for piece in range(3):
            place[SM_BF + piece * PIECE_STRIDE + h, base + piece] = 1.0
            aug_q[0, base + piece] = 1.0
    place, aug_q = jnp.asarray(place, BF16), jnp.asarray(aug_q)
    fox_bf = jnp.pad(fox_b_f, ((0, 0), (SM_BF, LANES - SM_BF - N_HEADS)))

    wuq = _pad_heads(mla_w_uq, MLA_NOPE + MLA_ROPE).astype(BF16)
    wuq_sw = _pad_heads(_swap_rotary(mla_w_uq), MLA_NOPE + MLA_ROPE).astype(BF16)
    wukv = mla_w_ukv.reshape(depth, MLA_KV_RANK, N_HEADS, 2 * HEAD_DIM)
    wuk = _pad_heads(wukv[..., :MLA_NOPE].reshape(depth, MLA_KV_RANK, N_HEADS * MLA_NOPE), MLA_NOPE).astype(BF16)
    pad_rows = lambda m: jnp.pad(m.reshape(m.shape[:-2] + (N_HEADS, HEAD_DIM, m.shape[-1])),
                                 [(0, 0)] * (m.ndim - 1) + [(0, VT_ROWS - HEAD_DIM), (0, 0)]
                                 ).reshape(m.shape[:-2] + (N_HEADS * VT_ROWS, m.shape[-1]))
    wuv_t = pad_rows(jnp.swapaxes(wukv[..., MLA_NOPE:].reshape(depth, MLA_KV_RANK, GROUP_W), 1, 2)).astype(BF16)
    eye = pad_rows(jnp.eye(GROUP_W, dtype=F32)).astype(BF16)
    d_qk = MLA_NOPE + MLA_ROPE
    half = MLA_ROPE // 2
    swap_halves = lambda g: jnp.concatenate(
        [jnp.zeros_like(g[:, :MLA_NOPE]), g[:, MLA_NOPE + half:], g[:, MLA_NOPE:MLA_NOPE + half]], axis=-1)
    mla_gq_all = mla_g_q * (mla_scale * d_qk ** 0.5)
    mla_gk_all = mla_g_k * d_qk ** 0.5
    mla_gq, mla_gq_sw = _pad_cols(mla_gq_all, HEAD_PAD), _pad_cols(swap_halves(mla_gq_all), HEAD_PAD)
    mla_gk, mla_gk_sw = _pad_cols(mla_gk_all, HEAD_PAD), _pad_cols(swap_halves(mla_gk_all), HEAD_PAD)

    eye_h = jnp.eye(N_HEADS, dtype=F32)
    wr_bd = jnp.einsum("lncd,nm->lncmd", lru_w_r, eye_h).reshape(depth, GROUP_W, GROUP_W).astype(BF16)
    wi_bd = jnp.einsum("lncd,nm->lncmd", lru_w_i, eye_h).reshape(depth, GROUP_W, GROUP_W).astype(BF16)
    conv_w = jnp.pad(lru_conv_w, ((0, 0), (0, SUBLANES - CONV_W), (0, 0)))

    inv_freq = ROPE_THETA ** (-jnp.arange(half, dtype=F32) / half)
    invf_row = jnp.pad(jnp.concatenate([inv_freq, inv_freq]), (MLA_NOPE, LANES - MLA_NOPE - MLA_ROPE))[None, :]
    cos_t, sin_t = _rope_tables(positions.reshape(n, 1), invf_row)

    xf = x.reshape(n, D_MODEL)
    for l in range(depth):
        proj, small = _norm_inproj(xf, norm_g[l][None, :], w_main, l)
        ya = _gla(proj, small, wup_hi[l], wup_lo[l], gla_b_up[l][None, :],
                  jnp.tile(gla_g_out[l], N_HEADS)[None, :], ones64, gla_sel, batch, seq)
        fq, fk, fvt = _fox_prep(proj, small, fox_bf[l][None, :], fox_gq[l][None, :], fox_gk[l][None, :], ones64, eye,
                                place, aug_q, batch, seq)
        yb = _attention(fq, fk, fvt, batch, seq)
        mq, mk, mvt = _mla_prep(proj, cos_t, sin_t, mla_g_cq[l][None, :], wuq[l], wuq_sw[l], mla_g_ckv[l][None, :],
                                wuk[l], wuv_t[l], mla_gq[l][None, :], mla_gq_sw[l][None, :], mla_gk[l][None, :],
                                mla_gk_sw[l][None, :], ones128, batch, seq)
        yc = _attention(mq, mk, mvt, batch, seq)
        yd = _lru(proj, conv_w[l], lru_conv_b[l][None, :], wr_bd[l], lru_b_r[l][None, :], wi_bd[l],
                  lru_b_i[l][None, :], lru_lam[l][None, :], batch, seq)
        xf = _gate_outproj(ya, yb, yc, yd, proj, w_out_b, xf, l)
    return xf.reshape(batch, seq, D_MODEL)
```

```python
import functools

import numpy as np
import jax
import jax.numpy as jnp
from jax import lax
from jax.experimental import pallas as pl
from jax.experimental.pallas import tpu as pltpu

F32 = jnp.float32
BF16 = jnp.bfloat16

D_MODEL = 1024
D_MIX = 2048
GROUP_W = 512
N_HEADS = 8
HEAD_DIM = 64
EPS = 1e-6
GLA_DK = 32
GLA_RANK = 16
GLA_NORMALIZER = 16.0
GLA_CHUNK = 64
GLA_LEVELS = 6
GLA_SAFE_LOG_DECAY = 40.0
MLA_Q_RANK = 256
MLA_KV_RANK = 128
MLA_NOPE = 64
MLA_ROPE = 32
ROPE_THETA = 10000.0
CONV_W = 4
LRU_C = 8.0
LOG2E = 1.4426950408889634
NEG_BIG = -1e30
EXPM1_SERIES_RANGE = 1.0 / 64.0

LANES = 128
SUBLANES = 8
MXU_WIDTH = 256
HEAD_PAD = 128
VT_ROWS = 80

COL_AQ, COL_AK, COL_AV = 2048, 2304, 2560
COL_BQ, COL_BK, COL_BV = 3072, 3584, 4096
COL_CQ, COL_CKV, COL_CKR = 4608, 4864, 4992
COL_DX = 5120
D_PROJ = 5632
SM_AG, SM_BF = 32, 48
PIECE_STRIDE = 16


ROWS_INPROJ = 1024
ROWS_OUTPROJ = 1024
ROWS_PREP = 1024
ROWS_GLA = 512
ROWS_LRU = 1024
ATTN_BLOCK = 512
ATTN_HEADS_PER_STEP = 8
ATTN_BLOCKS_PER_ITEM = 2
VMEM_LIMIT_MB = 48
VMEM_LIMIT_INPROJ_MB = 56


def _cparams(sem, vmem_mb=VMEM_LIMIT_MB):
    return pltpu.CompilerParams(dimension_semantics=sem, vmem_limit_bytes=vmem_mb * 1024 * 1024)


def _sigmoid(z):
    return 0.5 * jnp.tanh(0.5 * z) + 0.5


def _with_ones_rows(vt):
    row = lax.broadcasted_iota(jnp.int32, vt.shape, 0) % VT_ROWS
    return jnp.where(row == HEAD_DIM, 1.0, vt)


def _neg_expm1(z):
    series = -z * (1.0 + z * (0.5 + z * (1.0 / 6.0 + z * (1.0 / 24.0))))
    return jnp.where(z > -EXPM1_SERIES_RANGE, series, 1.0 - jnp.exp(z))


def _log_sigmoid(z):
    return jnp.minimum(z, 0.0) - jnp.log1p(jnp.exp(-jnp.abs(z)))


def _row_prefix_sum(v, seg):
    rows, width = v.shape
    g = v.reshape(rows // SUBLANES, SUBLANES, width)
    sub = lax.broadcasted_iota(jnp.int32, g.shape, 1)
    for s in (1, 2, 4):
        g = g + jnp.where(sub >= s, pltpu.roll(g, s, 1), 0.0)
    per_seg = seg // SUBLANES
    g = g.reshape(rows // seg, per_seg, SUBLANES, width)
    groups = [g[:, 0]]
    for i in range(1, per_seg):
        groups.append(g[:, i] + groups[-1][:, SUBLANES - 1:SUBLANES, :])
    return jnp.stack(groups, axis=1).reshape(rows, width)


def _inproj_kernel(x_ref, g_ref, w_ref, o_ref, os_ref, h_ref, *, small_col):
    @pl.when(pl.program_id(1) == 0)
    def _():
        x = x_ref[...]
        ms = jnp.mean(x * x, axis=-1, keepdims=True)
        h_ref[...] = (x * lax.rsqrt(ms + EPS) * g_ref[...]).astype(BF16)

    res = jnp.dot(h_ref[...], w_ref[...], preferred_element_type=F32)
    o_ref[...] = res.astype(BF16)

    @pl.when(pl.program_id(1) == pl.num_programs(1) - 1)
    def _():
        os_ref[...] = res[:, small_col:small_col + LANES]


def _norm_inproj(x, g, w_all, layer):
    n = x.shape[0]
    tm = min(ROWS_INPROJ, n)
    tn = D_PROJ // 2
    assert COL_CKR >= D_PROJ - tn
    return pl.pallas_call(
        functools.partial(_inproj_kernel, small_col=COL_CKR - (D_PROJ - tn)),
        grid=(n // tm, D_PROJ // tn),
        in_specs=[
            pl.BlockSpec((tm, D_MODEL), lambda i, j: (i, 0)),
            pl.BlockSpec((1, D_MODEL), lambda i, j: (0, 0)),
            pl.BlockSpec((None, D_MODEL, tn), lambda i, j: (layer, 0, j)),
        ],
        out_specs=[
            pl.BlockSpec((tm, tn), lambda i, j: (i, j)),
            pl.BlockSpec((tm, LANES), lambda i, j: (i, 0)),
        ],
        out_shape=[jax.ShapeDtypeStruct((n, D_PROJ), BF16), jax.ShapeDtypeStruct((n, LANES), F32)],
        scratch_shapes=[pltpu.VMEM((tm, D_MODEL), BF16)],
        compiler_params=_cparams(("parallel", "arbitrary"), vmem_mb=VMEM_LIMIT_INPROJ_MB),
        name="norm_inproj",
    )(x, g, w_all)


def _outproj_kernel(ya_ref, yb_ref, yc_ref, yd_ref, gate_ref, w_ref, x_ref, o_ref, z_ref):
    for gi, y_ref in enumerate((ya_ref, yb_ref, yc_ref, yd_ref)):
        lo, hi = gi * GROUP_W, (gi + 1) * GROUP_W
        g = gate_ref[:, lo:hi].astype(F32)
        z_ref[:, lo:hi] = (y_ref[...].astype(F32) * (g * _sigmoid(g))).astype(BF16)
    o_ref[...] = x_ref[...] + jnp.dot(z_ref[...], w_ref[...], preferred_element_type=F32)


def _gate_outproj(ya, yb, yc, yd, proj, w_all, x, layer):
    n = x.shape[0]
    tm = min(ROWS_OUTPROJ, n)
    yspec = pl.BlockSpec((tm, GROUP_W), lambda i: (i, 0))
    return pl.pallas_call(
        _outproj_kernel,
        grid=(n // tm,),
        in_specs=[
            yspec, yspec, yspec, yspec,
            pl.BlockSpec((tm, D_MIX), lambda i: (i, 0)),
            pl.BlockSpec((None, D_MIX, D_MODEL), lambda i: (layer, 0, 0)),
            pl.BlockSpec((tm, D_MODEL), lambda i: (i, 0)),
        ],
        out_specs=pl.BlockSpec((tm, D_MODEL), lambda i: (i, 0)),
        out_shape=jax.ShapeDtypeStruct((n, D_MODEL), F32),
        scratch_shapes=[pltpu.VMEM((tm, D_MIX), BF16)],
        compiler_params=_cparams(("parallel",)),
        name="gate_outproj",
    )(ya, yb, yc, yd, proj, w_all, x)


def _attn_kernel(q_ref, k_ref, vt_ref, o_ref, m_ref, acc_ref, st_ref, mx_ref, *, tq, nq, hg):
    i = pl.program_id(2)
    half = tq // 2
    nt_dims = (((1,), (1,)), ((), ()))

    def scores(h, krows, qrows):
        lanes = slice(h * HEAD_PAD, (h + 1) * HEAD_PAD)
        return lax.dot_general(k_ref[krows, lanes], q_ref[qrows, lanes], nt_dims, preferred_element_type=F32)

    def vrows(h):
        return slice(h * VT_ROWS, (h + 1) * VT_ROWS)

    tri = lax.broadcasted_iota(jnp.int32, (half, half), 0) <= lax.broadcasted_iota(jnp.int32, (half, half), 1)
    low = lax.broadcasted_iota(jnp.int32, (tq, half), 0) <= lax.broadcasted_iota(jnp.int32, (tq, half), 1) + half

    def issue_scores(item, c):
        h, blk = item[0], item[1]
        if blk is None:
            off = c * tq
            s_l = jnp.where(tri, scores(h, slice(off, off + half), slice(0, half)), NEG_BIG)
            s_r = jnp.where(low, scores(h, slice(off, off + tq), slice(half, tq)), NEG_BIG)
            mx = jnp.concatenate([jnp.max(s_l, axis=0, keepdims=True), jnp.max(s_r, axis=0, keepdims=True)], axis=1)
            return (s_l, s_r, mx)
        st = scores(h, slice(blk * tq, (blk + item[2]) * tq), slice(0, tq))
        return (st, jnp.max(st, axis=0, keepdims=True))

    def store_scores(item, slot, vals):
        if item[1] is None:
            st_ref[slot, 0:half, 0:half] = vals[0]
            st_ref[slot, 0:tq, half:tq] = vals[1]
        else:
            st_ref[slot, 0:item[2] * tq, :] = vals[0]
        mx_ref[slot] = vals[-1]

    def consume(item, slot, c):
        h, blk = item[0], item[1]
        opens = blk is None
        if opens:
            m_new = mx_ref[slot]
        else:
            m = m_ref[h]
            m_new = jnp.maximum(m, mx_ref[slot])
            alpha = jnp.exp2(m - m_new)
        m_ref[h] = m_new

        def accumulate(cols, st, vt):
            p = jnp.exp2(st - m_new[:, cols]).astype(BF16)
            pv = jnp.dot(vt, p, preferred_element_type=F32)
            acc_ref[h, :, cols] = pv if opens else alpha[:, cols] * acc_ref[h, :, cols] + pv

        if blk is None:
            off = c * tq
            accumulate(slice(0, half), st_ref[slot, 0:half, 0:half], vt_ref[vrows(h), off:off + half])
            accumulate(slice(half, tq), st_ref[slot, 0:tq, half:tq], vt_ref[vrows(h), off:off + tq])
        else:
            accumulate(slice(0, tq), st_ref[slot, 0:item[2] * tq, :], vt_ref[vrows(h), blk * tq:(blk + item[2]) * tq])

    for c in range(nq):
        @pl.when(i == c)
        def _(c=c):
            groups = [(b0, min(ATTN_BLOCKS_PER_ITEM, c - b0)) for b0 in range(0, c, ATTN_BLOCKS_PER_ITEM)]
            items = [(h, None, 1) for h in range(hg)] + [(h, b0, nb) for b0, nb in groups for h in range(hg)]
            store_scores(items[0], 0, issue_scores(items[0], c))
            for n, item in enumerate(items):
                nxt = items[n + 1] if n + 1 < len(items) else None
                if nxt is not None:
                    vals = issue_scores(nxt, c)
                consume(item, n % 2, c)
                if nxt is not None:
                    store_scores(nxt, (n + 1) % 2, vals)

    ot = jnp.concatenate([acc_ref[h, 0:HEAD_DIM, :] * (1.0 / acc_ref[h, HEAD_DIM:HEAD_DIM + 1, :]) for h in range(hg)],
                         axis=0)
    o_ref[...] = ot.T.astype(BF16)


def _attention(q, k, vt, batch, seq):
    n = q.shape[0]
    tq = min(ATTN_BLOCK, seq)
    nq = seq // tq
    hg = ATTN_HEADS_PER_STEP
    return pl.pallas_call(
        functools.partial(_attn_kernel, tq=tq, nq=nq, hg=hg),
        grid=(batch, N_HEADS // hg, nq),
        in_specs=[
            pl.BlockSpec((tq, hg * HEAD_PAD), lambda b, p, i: (b * nq + i, p)),
            pl.BlockSpec((seq, hg * HEAD_PAD), lambda b, p, i: (b, p)),
            pl.BlockSpec((None, hg * VT_ROWS, seq), lambda b, p, i: (b, p, 0)),
        ],
        out_specs=pl.BlockSpec((tq, hg * HEAD_DIM), lambda b, p, i: (b * nq + i, p)),
        out_shape=jax.ShapeDtypeStruct((n, GROUP_W), BF16),
        scratch_shapes=[pltpu.VMEM((hg, 1, tq), F32), pltpu.VMEM((hg, VT_ROWS, tq), F32),
                        pltpu.VMEM((2, ATTN_BLOCKS_PER_ITEM * tq, tq), F32), pltpu.VMEM((2, 1, tq), F32)],
        compiler_params=_cparams(("parallel", "parallel", "arbitrary")),
        name="causal_attention",
    )(q, k, vt)


def _fox_prep_kernel(q_ref, k_ref, v_ref, sm_ref, bf_ref, gq_ref, gk_ref, ones_ref, eye_ref, place_ref, augq_ref,
                     qo_ref, ko_ref, vt_ref, carry_ref, *, tt):
    @pl.when(pl.program_id(1) == 0)
    def _():
        carry_ref[...] = jnp.zeros_like(carry_ref)

    vt = lax.dot_general(eye_ref[...], v_ref[...], (((1,), (1,)), ((), ())), preferred_element_type=F32)
    vt_ref[...] = _with_ones_rows(vt).astype(BF16)

    def headnorm(ref, g_ref):
        z = ref[...].astype(F32)
        ssq = jnp.dot((z * z).astype(BF16), ones_ref[...], preferred_element_type=F32)
        return z * lax.rsqrt(ssq + HEAD_DIM * EPS) * g_ref[...]

    qn = headnorm(q_ref, gq_ref)
    kn = headnorm(k_ref, gk_ref)

    log_f = _log_sigmoid(sm_ref[...] + bf_ref[...])
    cum = _row_prefix_sum(log_f, tt) + carry_ref[0:1, :]
    carry_ref[0:1, :] = cum[tt - 1:tt, :]
    neg = cum * (-LOG2E)

    lane = lax.broadcasted_iota(jnp.int32, (tt, LANES), 1)
    hi = neg.astype(BF16).astype(F32)
    r1 = neg - hi
    mid = r1.astype(BF16).astype(F32)
    lo = r1 - mid
    pieces = jnp.where(lane < SM_BF + PIECE_STRIDE, hi,
                       jnp.where(lane < SM_BF + 2 * PIECE_STRIDE, pltpu.roll(mid, PIECE_STRIDE, 1),
                                 pltpu.roll(lo, 2 * PIECE_STRIDE, 1)))
    aug_k = jnp.dot(pieces.astype(BF16), place_ref[...], preferred_element_type=F32)
    aug_q = augq_ref[...]
    for h in range(N_HEADS):
        pair, e = h // 2, h % 2
        in_head = (lane >= e * HEAD_DIM) & (lane < (e + 1) * HEAD_DIM)
        blk = slice(h * HEAD_PAD, (h + 1) * HEAD_PAD)
        kpair = kn[:, pair * LANES:(pair + 1) * LANES]
        qpair = qn[:, pair * LANES:(pair + 1) * LANES]
        ko_ref[:, blk] = jnp.where(in_head, kpair, aug_k[:, blk]).astype(BF16)
        qo_ref[:, blk] = jnp.where(in_head, qpair, aug_q[:, blk]).astype(BF16)


def _fox_prep(proj, small, bf_row, gq, gk, ones64, eye, place, aug_q, batch, seq):
    n = proj.shape[0]
    tt = min(ROWS_PREP, seq)
    nt = seq // tt
    row = lambda shape: pl.BlockSpec(shape, lambda b, t: (0, 0))
    return pl.pallas_call(
        functools.partial(_fox_prep_kernel, tt=tt),
        grid=(batch, nt),
        in_specs=[
            pl.BlockSpec((tt, GROUP_W), lambda b, t: (b * nt + t, COL_BQ // GROUP_W)),
            pl.BlockSpec((tt, GROUP_W), lambda b, t: (b * nt + t, COL_BK // GROUP_W)),
            pl.BlockSpec((tt, GROUP_W), lambda b, t: (b * nt + t, COL_BV // GROUP_W)),
            pl.BlockSpec((tt, LANES), lambda b, t: (b * nt + t, 0)),
            row((1, LANES)), row((1, GROUP_W)), row((1, GROUP_W)), row((GROUP_W, GROUP_W)),
            row((N_HEADS * VT_ROWS, GROUP_W)), row((LANES, N_HEADS * HEAD_PAD)), row((1, N_HEADS * HEAD_PAD)),
        ],
        out_specs=[
            pl.BlockSpec((tt, N_HEADS * HEAD_PAD), lambda b, t: (b * nt + t, 0)),
            pl.BlockSpec((tt, N_HEADS * HEAD_PAD), lambda b, t: (b * nt + t, 0)),
            pl.BlockSpec((None, N_HEADS * VT_ROWS, tt), lambda b, t: (b, 0, t)),
        ],
        out_shape=[jax.ShapeDtypeStruct((n, N_HEADS * HEAD_PAD), BF16)] * 2
        + [jax.ShapeDtypeStruct((batch, N_HEADS * VT_ROWS, seq), BF16)],
        scratch_shapes=[pltpu.VMEM((SUBLANES, LANES), F32)],
        compiler_params=_cparams(("parallel", "arbitrary")),
        name="fox_prep",
    )(proj, proj, proj, small, bf_row, gq, gk, ones64, eye, place, aug_q)


def _rope_kernel(pos_ref, invf_ref, c_ref, s_ref):
    ang = pos_ref[...].astype(F32) * invf_ref[...]
    lane = lax.broadcasted_iota(jnp.int32, ang.shape, 1)
    half = MLA_ROPE // 2
    c, s = jnp.cos(ang), jnp.sin(ang)
    c_ref[...] = jnp.where((lane >= MLA_NOPE) & (lane < MLA_NOPE + MLA_ROPE), c, 1.0)
    s_ref[...] = jnp.where((lane >= MLA_NOPE) & (lane < MLA_NOPE + half), -s,
                           jnp.where((lane >= MLA_NOPE + half) & (lane < MLA_NOPE + MLA_ROPE), s, 0.0))


def _rope_tables(pos_col, invf_row):
    n = pos_col.shape[0]
    tm = min(ROWS_PREP, n)
    out = pl.BlockSpec((tm, LANES), lambda i: (i, 0))
    return pl.pallas_call(
        _rope_kernel,
        grid=(n // tm,),
        in_specs=[pl.BlockSpec((tm, 1), lambda i: (i, 0)), pl.BlockSpec((1, LANES), lambda i: (0, 0))],
        out_specs=[out, out],
        out_shape=[jax.ShapeDtypeStruct((n, LANES), F32)] * 2,
        compiler_params=_cparams(("parallel",)),
        name="rope_tables",
    )(pos_col, invf_row)


def _mla_prep_kernel(cq_ref, ckv_ref, ckr_ref, c_ref, s_ref, gcq_ref, wuq_ref, wuqs_ref, gckv_ref, wuk_ref, wuv_ref,
                     gq_ref, gqs_ref, gk_ref, gks_ref, ones_ref, qo_ref, ko_ref, vo_ref):
    d_qk = float(MLA_NOPE + MLA_ROPE)
    cos, sin = c_ref[...], s_ref[...]
    half = MLA_ROPE // 2

    def rms(z, g):
        return (z * lax.rsqrt(jnp.mean(z * z, axis=-1, keepdims=True) + EPS) * g).astype(BF16)

    def head_ssq(z):
        zz = (z * z).astype(BF16)
        parts = [jnp.dot(zz[:, c * MXU_WIDTH:(c + 1) * MXU_WIDTH], ones_ref[...], preferred_element_type=F32)
                 for c in range(z.shape[1] // MXU_WIDTH)]
        return jnp.concatenate(parts, axis=1)

    cn = rms(cq_ref[...].astype(F32), gcq_ref[...])
    q = jnp.dot(cn, wuq_ref[...], preferred_element_type=F32)
    q_sw = jnp.dot(cn, wuqs_ref[...], preferred_element_type=F32)
    q_inv = lax.rsqrt(head_ssq(q) + d_qk * EPS)

    kvn = rms(ckv_ref[...].astype(F32), gckv_ref[...])
    kn = jnp.dot(kvn, wuk_ref[...], preferred_element_type=F32)
    vt = lax.dot_general(wuv_ref[...], kvn, (((1,), (1,)), ((), ())), preferred_element_type=F32)
    vo_ref[...] = _with_ones_rows(vt).astype(BF16)
    lane = lax.broadcasted_iota(jnp.int32, ckr_ref.shape, 1)
    kr = jnp.where(lane < MLA_ROPE, ckr_ref[...].astype(F32), 0.0)
    ss_r = jnp.sum(kr * kr, axis=-1, keepdims=True)
    kr_placed = pltpu.roll(kr, MLA_NOPE, 1)
    kr_swapped = jnp.where(lane < MLA_NOPE + half, pltpu.roll(kr, MLA_NOPE - half, 1), pltpu.roll(kr, MLA_NOPE + half, 1))
    kr_swapped = jnp.where((lane >= MLA_NOPE) & (lane < MLA_NOPE + MLA_ROPE), kr_swapped, 0.0)
    k_inv = lax.rsqrt(head_ssq(kn) + ss_r + d_qk * EPS)

    gc_q, gs_q = gq_ref[...] * cos, gqs_ref[...] * sin
    gc_k = gk_ref[...] * cos
    kr_rot = kr_placed * gc_k + kr_swapped * (gks_ref[...] * sin)
    for h in range(N_HEADS):
        sl = slice(h * HEAD_PAD, (h + 1) * HEAD_PAD)
        qo_ref[:, sl] = (q_inv[:, sl] * (q[:, sl] * gc_q + q_sw[:, sl] * gs_q)).astype(BF16)
        ko_ref[:, sl] = (k_inv[:, sl] * (kn[:, sl] * gc_k + kr_rot)).astype(BF16)


def _mla_prep(proj, cos_t, sin_t, gcq, wuq, wuq_sw, gckv, wuk, wuv_t, gq, gq_sw, gk, gk_sw, ones128, batch, seq):
    n = proj.shape[0]
    tm = min(ROWS_PREP, seq)
    nt = seq // tm
    full = lambda shape: pl.BlockSpec(shape, lambda i: (0, 0))
    tab = pl.BlockSpec((tm, LANES), lambda i: (i, 0))
    wide = pl.BlockSpec((tm, N_HEADS * HEAD_PAD), lambda i: (i, 0))
    return pl.pallas_call(
        _mla_prep_kernel,
        grid=(n // tm,),
        in_specs=[
            pl.BlockSpec((tm, MLA_Q_RANK), lambda i: (i, COL_CQ // MLA_Q_RANK)),
            pl.BlockSpec((tm, MLA_KV_RANK), lambda i: (i, COL_CKV // LANES)),
            pl.BlockSpec((tm, LANES), lambda i: (i, COL_CKR // LANES)),
            tab, tab,
            full((1, MLA_Q_RANK)), full((MLA_Q_RANK, N_HEADS * HEAD_PAD)), full((MLA_Q_RANK, N_HEADS * HEAD_PAD)),
            full((1, MLA_KV_RANK)), full((MLA_KV_RANK, N_HEADS * HEAD_PAD)), full((N_HEADS * VT_ROWS, MLA_KV_RANK)),
            full((1, HEAD_PAD)), full((1, HEAD_PAD)), full((1, HEAD_PAD)), full((1, HEAD_PAD)), full((MXU_WIDTH, MXU_WIDTH)),
        ],
        out_specs=[wide, wide, pl.BlockSpec((None, N_HEADS * VT_ROWS, tm), lambda i: (i // nt, 0, i % nt))],
        out_shape=[jax.ShapeDtypeStruct((n, N_HEADS * HEAD_PAD), BF16)] * 2
        + [jax.ShapeDtypeStruct((batch, N_HEADS * VT_ROWS, seq), BF16)],
        compiler_params=_cparams(("parallel",)),
        name="mla_prep",
    )(proj, proj, proj, cos_t, sin_t, gcq, wuq, wuq_sw, gckv, wuk, wuv_t, gq, gq_sw, gk, gk_sw, ones128)


def _gla_kernel(q_ref, k_ref, v_ref, sm_ref, wh_ref, wl_ref, b_ref, go_ref, ones_ref, sel_ref, o_ref, st_ref,
                inter_ref, *, tc):
    hk = N_HEADS * GLA_DK
    ck = GLA_CHUNK

    @pl.when(pl.program_id(1) == 0)
    def _():
        st_ref[...] = jnp.zeros_like(st_ref)

    sm = sm_ref[...]
    sm_hi = sm.astype(BF16)
    sm_lo = (sm - sm_hi.astype(F32)).astype(BF16)
    z = (jnp.dot(sm_hi, wh_ref[...], preferred_element_type=F32)
         + jnp.dot(sm_lo, wh_ref[...], preferred_element_type=F32)
         + jnp.dot(sm_hi, wl_ref[...], preferred_element_type=F32)) + b_ref[...]
    g_all = _row_prefix_sum(_log_sigmoid(z) * (1.0 / GLA_NORMALIZER), ck)

    lane_k = lax.broadcasted_iota(jnp.int32, (ck, hk), 1) // GLA_DK
    lane_pair = lax.broadcasted_iota(jnp.int32, (ck, LANES), 1)
    a_row = lax.broadcasted_iota(jnp.int32, (N_HEADS * ck, ck), 0) % ck
    a_col = lax.broadcasted_iota(jnp.int32, (N_HEADS * ck, ck), 1)
    st_blockdiag = (lax.broadcasted_iota(jnp.int32, (GROUP_W, hk), 0) // HEAD_DIM
                    == lax.broadcasted_iota(jnp.int32, (GROUP_W, hk), 1) // GLA_DK)

    def stack_heads(x):
        return jnp.concatenate([jnp.where(lane_k == h, x, jnp.zeros_like(x)) for h in range(N_HEADS)], axis=0)

    def intra_output(a, v):
        parts = []
        for p in range(N_HEADS // 2):
            r = jnp.dot(a[2 * p * ck:2 * (p + 1) * ck, :], v[:, p * LANES:(p + 1) * LANES], preferred_element_type=F32)
            parts.append(jnp.where(lane_pair < HEAD_DIM, r[0:ck, :], r[ck:2 * ck, :]))
        return jnp.concatenate(parts, axis=1)

    def write_normed(o):
        ssq = jnp.dot((o * o).astype(BF16), ones_ref[...], preferred_element_type=F32)
        o_ref[...] = (o * lax.rsqrt(ssq * (1.0 / HEAD_DIM) + EPS) * go_ref[...]).astype(BF16)

    n_chunks = tc // ck
    qts, intras, incs, decays = [], [], [], []
    for c in range(n_chunks):
        rows = slice(c * ck, (c + 1) * ck)
        g = g_all[rows]
        q = q_ref[rows, :].astype(F32) * (GLA_DK ** -0.5)
        k = k_ref[rows, :].astype(F32)
        v = v_ref[rows, :]
        g_last = g[ck - 1:ck, :]
        qt = (q * jnp.exp(g)).astype(BF16)
        kt = (k * jnp.exp(-g)).astype(BF16)
        kh = (k * jnp.exp(g_last - g)).astype(BF16)

        u = lax.dot_general(v, kh, (((0,), (0,)), ((), ())), preferred_element_type=F32)
        incs.append(jnp.where(st_blockdiag, u, 0.0))
        decays.append(jnp.exp(g_last))

        a = lax.dot_general(stack_heads(qt), kt, (((1,), (1,)), ((), ())), preferred_element_type=F32)
        intras.append(intra_output(jnp.where(a_row >= a_col, a, 0.0).astype(BF16), v))
        qts.append(qt)

    state = st_ref[...]
    states = []
    for c in range(n_chunks):
        states.append(state.astype(BF16))
        state = state * decays[c] + incs[c]
    st_ref[...] = state

    outs = []
    for c in range(n_chunks):
        o = lax.dot_general(qts[c], states[c], (((1,), (1,)), ((), ())), preferred_element_type=F32)
        inter_ref[c * ck:(c + 1) * ck, :] = o
        outs.append(o + intras[c])
    write_normed(jnp.concatenate(outs, axis=0))

    @pl.when(jnp.min(g_all) < -GLA_SAFE_LOG_DECAY)
    def _():
        t_idx = lax.broadcasted_iota(jnp.int32, (ck, hk), 0)
        nt_dims = (((1,), (1,)), ((), ()))
        redo = []
        for c in range(n_chunks):
            rows = slice(c * ck, (c + 1) * ck)
            g = g_all[rows]
            q = q_ref[rows, :].astype(F32) * (GLA_DK ** -0.5)
            k = k_ref[rows, :].astype(F32)
            g_hi = g.astype(BF16)
            g_r1 = g - g_hi.astype(F32)
            g_mid = g_r1.astype(BF16)
            g_lo = (g_r1 - g_mid.astype(F32)).astype(BF16)
            g3 = jnp.concatenate([g_hi, g_mid, g_lo], axis=0)
            a = lax.dot_general(stack_heads(q.astype(BF16)), k.astype(BF16), nt_dims, preferred_element_type=F32)
            a = jnp.where(a_row == a_col, a, 0.0)
            for lvl in range(GLA_LEVELS):
                half = ck >> (lvl + 1)
                g_mid_row = jnp.dot(sel_ref[lvl], g3, preferred_element_type=F32)
                d = g - g_mid_row
                second = (t_idx // half) % 2 == 1
                qd = jnp.where(second, q * jnp.exp(jnp.minimum(d, 0.0)), 0.0).astype(BF16)
                kd = jnp.where(second, 0.0, k * jnp.exp(jnp.minimum(-d, 0.0))).astype(BF16)
                part = lax.dot_general(stack_heads(qd), kd, nt_dims, preferred_element_type=F32)
                a = a + jnp.where(a_row // (2 * half) == a_col // (2 * half), part, 0.0)
            redo.append(inter_ref[rows, :] + intra_output(a.astype(BF16), v_ref[rows, :]))
        write_normed(jnp.concatenate(redo, axis=0))


def _gla(proj, small, w_hi, w_lo, b_up, g_out, ones64, sel, batch, seq):
    n = proj.shape[0]
    tc = min(ROWS_GLA, seq)
    nt = seq // tc
    hk = N_HEADS * GLA_DK
    full = lambda shape: pl.BlockSpec(shape, lambda b, t: (0, 0))
    return pl.pallas_call(
        functools.partial(_gla_kernel, tc=tc),
        grid=(batch, nt),
        in_specs=[
            pl.BlockSpec((tc, hk), lambda b, t: (b * nt + t, COL_AQ // hk)),
            pl.BlockSpec((tc, hk), lambda b, t: (b * nt + t, COL_AK // hk)),
            pl.BlockSpec((tc, GROUP_W), lambda b, t: (b * nt + t, COL_AV // GROUP_W)),
            pl.BlockSpec((tc, LANES), lambda b, t: (b * nt + t, 0)),
            full((LANES, hk)), full((LANES, hk)), full((1, hk)), full((1, GROUP_W)), full((GROUP_W, GROUP_W)),
            pl.BlockSpec((GLA_LEVELS, GLA_CHUNK, 3 * GLA_CHUNK), lambda b, t: (0, 0, 0)),
        ],
        out_specs=pl.BlockSpec((tc, GROUP_W), lambda b, t: (b * nt + t, 0)),
        out_shape=jax.ShapeDtypeStruct((n, GROUP_W), BF16),
        scratch_shapes=[pltpu.VMEM((GROUP_W, hk), F32), pltpu.VMEM((tc, GROUP_W), F32)],
        compiler_params=_cparams(("parallel", "arbitrary")),
        name="gla",
    )(proj, proj, proj, small, w_hi, w_lo, b_up, g_out, ones64, sel)


def _lru_kernel(x_ref, cw_ref, cb_ref, wr_ref, br_ref, wi_ref, bi_ref, lam_ref, o_ref, xprev_ref, h_ref, *, tr):
    @pl.when(pl.program_id(1) == 0)
    def _():
        xprev_ref[...] = jnp.zeros_like(xprev_ref)
        h_ref[...] = jnp.zeros_like(h_ref)

    x3 = x_ref[...].astype(F32).reshape(tr // SUBLANES, SUBLANES, GROUP_W)
    xg = jnp.concatenate([xprev_ref[...][None], x3], axis=0)
    xprev_ref[...] = x3[tr // SUBLANES - 1]
    sub3 = lax.broadcasted_iota(jnp.int32, x3.shape, 1)
    xc3 = cb_ref[...][None] + cw_ref[CONV_W - 1:CONV_W, :][None] * x3
    for s in range(1, CONV_W):
        rot = pltpu.roll(xg, s, 1)
        shifted = jnp.where(sub3 >= s, rot[1:], rot[:-1])
        xc3 = xc3 + cw_ref[CONV_W - 1 - s:CONV_W - s, :][None] * shifted
    xc = xc3.reshape(tr, GROUP_W)

    xb = xc.astype(BF16)
    t_r = jnp.tanh(0.5 * (jnp.dot(xb, wr_ref[...], preferred_element_type=F32) + br_ref[...]))
    t_i = jnp.tanh(0.5 * (jnp.dot(xb, wi_ref[...], preferred_element_type=F32) + bi_ref[...]))
    lam = lam_ref[...]
    softplus = jnp.maximum(-lam, 0.0) + jnp.log1p(jnp.exp(-jnp.abs(lam)))
    half_rate = (-0.5 * LRU_C) * softplus
    log_a = half_rate * t_r + half_rate
    one_minus_a = _neg_expm1(log_a)
    a = 1.0 - one_minus_a
    half_x = 0.5 * xc
    bx = jnp.sqrt(one_minus_a * (2.0 - one_minus_a)) * (half_x * t_i + half_x)

    a3 = a.reshape(tr // SUBLANES, SUBLANES, GROUP_W)
    b3 = bx.reshape(tr // SUBLANES, SUBLANES, GROUP_W)
    sub = lax.broadcasted_iota(jnp.int32, a3.shape, 1)
    for s in (1, 2, 4):
        keep = sub >= s
        a_prev = jnp.where(keep, pltpu.roll(a3, s, 1), 1.0)
        b_prev = jnp.where(keep, pltpu.roll(b3, s, 1), 0.0)
        b3 = a3 * b_prev + b3
        a3 = a3 * a_prev
    h_prev = h_ref[0:1, :]
    for g in range(tr // SUBLANES):
        hg = a3[g] * h_prev + b3[g]
        o_ref[g * SUBLANES:(g + 1) * SUBLANES, :] = hg.astype(BF16)
        h_prev = hg[SUBLANES - 1:SUBLANES, :]
    h_ref[0:1, :] = h_prev


def _lru(proj, conv_w, conv_b, wr, br, wi, bi, lam, batch, seq):
    n = proj.shape[0]
    tr = min(ROWS_LRU, seq)
    nt = seq // tr
    full = lambda shape: pl.BlockSpec(shape, lambda b, t: (0, 0))
    return pl.pallas_call(
        functools.partial(_lru_kernel, tr=tr),
        grid=(batch, nt),
        in_specs=[
            pl.BlockSpec((tr, GROUP_W), lambda b, t: (b * nt + t, COL_DX // GROUP_W)),
            full((SUBLANES, GROUP_W)), full((1, GROUP_W)),
            full((GROUP_W, GROUP_W)), full((1, GROUP_W)),
            full((GROUP_W, GROUP_W)), full((1, GROUP_W)), full((1, GROUP_W)),
        ],
        out_specs=pl.BlockSpec((tr, GROUP_W), lambda b, t: (b * nt + t, 0)),
        out_shape=jax.ShapeDtypeStruct((n, GROUP_W), BF16),
        scratch_shapes=[pltpu.VMEM((SUBLANES, GROUP_W), F32), pltpu.VMEM((SUBLANES, GROUP_W), F32)],
        compiler_params=_cparams(("parallel", "arbitrary")),
        name="rglru",
    )(proj, conv_w, conv_b, wr, br, wi, bi, lam)


def _pad_cols(w, width):
    return jnp.pad(w, [(0, 0)] * (w.ndim - 1) + [(0, width - w.shape[-1])])


def _block_diag_ones(n_blocks, size):
    return jnp.asarray(np.kron(np.eye(n_blocks, dtype=np.float32), np.ones((size, size), np.float32)), BF16)


def _swap_rotary(w):
    lead = w.shape[:-1]
    w = w.reshape(lead + (N_HEADS, MLA_NOPE + MLA_ROPE))
    half = MLA_ROPE // 2
    z1, z2 = w[..., MLA_NOPE:MLA_NOPE + half], w[..., MLA_NOPE + half:]
    w = jnp.concatenate([jnp.zeros_like(w[..., :MLA_NOPE]), z2, z1], axis=-1)
    return w.reshape(lead + (N_HEADS * (MLA_NOPE + MLA_ROPE),))


def _pad_heads(w, d_real):
    lead = w.shape[:-1]
    w = w.reshape(lead + (N_HEADS, d_real))
    w = jnp.pad(w, [(0, 0)] * len(lead) + [(0, 0), (0, HEAD_PAD - d_real)])
    return w.reshape(lead + (N_HEADS * HEAD_PAD,))


def kernel(x, positions, norm_g, w_in, w_out, gla_w_up, gla_b_up, gla_g_out, fox_b_f, fox_g_q, fox_g_k, mla_g_cq, mla_w_uq, mla_g_ckv, mla_w_ukv, mla_g_q, mla_g_k, lru_conv_w, lru_conv_b, lru_w_r, lru_b_r, lru_w_i, lru_b_i, lru_lam):
    batch, seq, d_model = x.shape
    depth = w_in.shape[0]
    n = batch * seq
    assert d_model == D_MODEL and w_in.shape[1:] == (D_MODEL, sum(
        (D_MIX, 2 * N_HEADS * GLA_DK, GROUP_W, GLA_RANK, 3 * GROUP_W, N_HEADS, MLA_Q_RANK, MLA_KV_RANK, MLA_ROPE, GROUP_W)))
    for rows in (ROWS_PREP, ROWS_GLA, ROWS_LRU, ATTN_BLOCK):
        assert seq % min(rows, seq) == 0 and min(rows, seq) % GLA_CHUNK == 0, (seq, rows)
    for rows in (ROWS_INPROJ, ROWS_OUTPROJ, ROWS_PREP):
        assert n % min(rows, n) == 0, (n, rows)

    gla_qk = N_HEADS * GLA_DK
    sizes = (D_MIX, gla_qk, gla_qk, GROUP_W, GLA_RANK, GROUP_W, GROUP_W, GROUP_W, N_HEADS,
             MLA_Q_RANK, MLA_KV_RANK, MLA_ROPE, GROUP_W)
    offs = np.concatenate([[0], np.cumsum(sizes)])
    seg = [w_in[:, :, offs[i]:offs[i + 1]] for i in range(len(sizes))]
    (w_gate, w_aq, w_ak, w_av, w_ag, w_bq, w_bk, w_bv, w_bf, w_cq, w_ckv, w_ckr, w_dx) = seg
    assert (SM_AG, SM_BF) == (MLA_ROPE, MLA_ROPE + GLA_RANK)
    w_kr_gates = _pad_cols(jnp.concatenate([w_ckr, w_ag, w_bf], axis=-1), LANES)
    w_main = jnp.concatenate(
        [w_gate, w_aq, w_ak, w_av, w_bq, w_bk, w_bv, w_cq, w_ckv, w_kr_gates, w_dx], axis=-1).astype(BF16)
    w_out_b = w_out.astype(BF16)

    ones64 = _block_diag_ones(GROUP_W // HEAD_DIM, HEAD_DIM)
    ones128 = _block_diag_ones(MXU_WIDTH // HEAD_PAD, HEAD_PAD)

    wup = jnp.pad(gla_w_up, ((0, 0), (SM_AG, LANES - SM_AG - GLA_RANK), (0, 0)))
    wup_hi = wup.astype(BF16)
    wup_lo = (wup - wup_hi.astype(F32)).astype(BF16)

    sel = np.zeros((GLA_LEVELS, GLA_CHUNK, 3 * GLA_CHUNK), np.float32)
    for lvl in range(GLA_LEVELS):
        half = GLA_CHUNK >> (lvl + 1)
        for t in range(GLA_CHUNK):
            mid = (t // (2 * half)) * 2 * half + half - 1
            for piece in range(3):
                sel[lvl, t, piece * GLA_CHUNK + mid] = 1.0
    gla_sel = jnp.asarray(sel, BF16)

    fox_scale = LOG2E
    mla_scale = (MLA_NOPE + MLA_ROPE) ** -0.5 * LOG2E
    fox_gq = jnp.tile(fox_g_q, (1, N_HEADS)) * fox_scale
    fox_gk = jnp.tile(fox_g_k, (1, N_HEADS)) * HEAD_DIM ** 0.5
    place = np.zeros((LANES, N_HEADS * HEAD_PAD), np.float32)
    aug_q = np.zeros((1, N_HEADS * HEAD_PAD), np.float32)
    for h in range(N_HEADS):
        base = h * HEAD_PAD + (HEAD_DIM if h % 2 == 0 else 0)
        for piece in range(3):
            place[SM_BF + piece * PIECE_STRIDE + h, base + piece] = 1.0
            aug_q[0, base + piece] = 1.0
    place, aug_q = jnp.asarray(place, BF16), jnp.asarray(aug_q)
    fox_bf = jnp.pad(fox_b_f, ((0, 0), (SM_BF, LANES - SM_BF - N_HEADS)))

    wuq = _pad_heads(mla_w_uq, MLA_NOPE + MLA_ROPE).astype(BF16)
    wuq_sw = _pad_heads(_swap_rotary(mla_w_uq), MLA_NOPE + MLA_ROPE).astype(BF16)
    wukv = mla_w_ukv.reshape(depth, MLA_KV_RANK, N_HEADS, 2 * HEAD_DIM)
    wuk = _pad_heads(wukv[..., :MLA_NOPE].reshape(depth, MLA_KV_RANK, N_HEADS * MLA_NOPE), MLA_NOPE).astype(BF16)
    pad_rows = lambda m: jnp.pad(m.reshape(m.shape[:-2] + (N_HEADS, HEAD_DIM, m.shape[-1])),
                                 [(0, 0)] * (m.ndim - 1) + [(0, VT_ROWS - HEAD_DIM), (0, 0)]
                                 ).reshape(m.shape[:-2] + (N_HEADS * VT_ROWS, m.shape[-1]))
    wuv_t = pad_rows(jnp.swapaxes(wukv[..., MLA_NOPE:].reshape(depth, MLA_KV_RANK, GROUP_W), 1, 2)).astype(BF16)
    eye = pad_rows(jnp.eye(GROUP_W, dtype=F32)).astype(BF16)
    d_qk = MLA_NOPE + MLA_ROPE
    half = MLA_ROPE // 2
    swap_halves = lambda g: jnp.concatenate(
        [jnp.zeros_like(g[:, :MLA_NOPE]), g[:, MLA_NOPE + half:], g[:, MLA_NOPE:MLA_NOPE + half]], axis=-1)
    mla_gq_all = mla_g_q * (mla_scale * d_qk ** 0.5)
    mla_gk_all = mla_g_k * d_qk ** 0.5
    mla_gq, mla_gq_sw = _pad_cols(mla_gq_all, HEAD_PAD), _pad_cols(swap_halves(mla_gq_all), HEAD_PAD)
    mla_gk, mla_gk_sw = _pad_cols(mla_gk_all, HEAD_PAD), _pad_cols(swap_halves(mla_gk_all), HEAD_PAD)

    eye_h = jnp.eye(N_HEADS, dtype=F32)
    wr_bd = jnp.einsum("lncd,nm->lncmd", lru_w_r, eye_h).reshape(depth, GROUP_W, GROUP_W).astype(BF16)
    wi_bd = jnp.einsum("lncd,nm->lncmd", lru_w_i, eye_h).reshape(depth, GROUP_W, GROUP_W).astype(BF16)
    conv_w = jnp.pad(lru_conv_w, ((0, 0), (0, SUBLANES - CONV_W), (0, 0)))

    inv_freq = ROPE_THETA ** (-jnp.arange(half, dtype=F32) / half)
    invf_row = jnp.pad(jnp.concatenate([inv_freq, inv_freq]), (MLA_NOPE, LANES - MLA_NOPE - MLA_ROPE))[None, :]
    cos_t, sin_t = _rope_tables(positions.reshape(n, 1), invf_row)

    xf = x.reshape(n, D_MODEL)
    for l in range(depth):
        proj, small = _norm_inproj(xf, norm_g[l][None, :], w_main, l)
        ya = _gla(proj, small, wup_hi[l], wup_lo[l], gla_b_up[l][None, :],
                  jnp.tile(gla_g_out[l], N_HEADS)[None, :], ones64, gla_sel, batch, seq)
        fq, fk, fvt = _fox_prep(proj, small, fox_bf[l][None, :], fox_gq[l][None, :], fox_gk[l][None, :], ones64, eye,
                                place, aug_q, batch, seq)
        yb = _attention(fq, fk, fvt, batch, seq)
        mq, mk, mvt = _mla_prep(proj, cos_t, sin_t, mla_g_cq[l][None, :], wuq[l], wuq_sw[l], mla_g_ckv[l][None, :],
                                wuk[l], wuv_t[l], mla_gq[l][None, :], mla_gq_sw[l][None, :], mla_gk[l][None, :],
                                mla_gk_sw[l][None, :], ones128, batch, seq)
        yc = _attention(mq, mk, mvt, batch, seq)
        yd = _lru(proj, conv_w[l], lru_conv_b[l][None, :], wr_bd[l], lru_b_r[l][None, :], wi_bd[l],
                  lru_b_i[l][None, :], lru_lam[l][None, :], batch, seq)
        xf = _gate_outproj(ya, yb, yc, yd, proj, w_out_b, xf, l)
    return xf.reshape(batch, seq, D_MODEL)
```

```python
import functools

import numpy as np
import jax
import jax.numpy as jnp
from jax import lax
from jax.experimental import pallas as pl
from jax.experimental.pallas import tpu as pltpu

F32 = jnp.float32
BF16 = jnp.bfloat16

D_MODEL = 1024
D_MIX = 2048
GROUP_W = 512
N_HEADS = 8
HEAD_DIM = 64
EPS = 1e-6
GLA_DK = 32
GLA_RANK = 16
GLA_NORMALIZER = 16.0
GLA_CHUNK = 64
GLA_LEVELS = 6
GLA_SAFE_LOG_DECAY = 40.0
MLA_Q_RANK = 256
MLA_KV_RANK = 128
MLA_NOPE = 64
MLA_ROPE = 32
ROPE_THETA = 10000.0
CONV_W = 4
LRU_C = 8.0
LOG2E = 1.4426950408889634
NEG_BIG = -1e30
EXPM1_SERIES_RANGE = 1.0 / 64.0

LANES = 128
SUBLANES = 8
MXU_WIDTH = 256
HEAD_PAD = 128
VT_ROWS = 80

COL_AQ, COL_AK, COL_AV = 2048, 2304, 2560
COL_BQ, COL_BK, COL_BV = 3072, 3584, 4096
COL_CQ, COL_CKV, COL_CKR = 4608, 4864, 4992
COL_DX = 5120
D_PROJ = 5632
SM_AG, SM_BF = 32, 48
PIECE_STRIDE = 16


ROWS_INPROJ = 1024
ROWS_OUTPROJ = 1024
ROWS_PREP = 1024
ROWS_GLA = 512
ROWS_LRU = 1024
ATTN_BLOCK = 512
ATTN_HEADS_PER_STEP = 8
ATTN_BLOCKS_PER_ITEM = 2
VMEM_LIMIT_MB = 48
VMEM_LIMIT_INPROJ_MB = 56


def _cparams(sem, vmem_mb=VMEM_LIMIT_MB):
    return pltpu.CompilerParams(dimension_semantics=sem, vmem_limit_bytes=vmem_mb * 1024 * 1024)


def _sigmoid(z):
    return 0.5 * jnp.tanh(0.5 * z) + 0.5


def _with_ones_rows(vt):
    row = lax.broadcasted_iota(jnp.int32, vt.shape, 0) % VT_ROWS
    return jnp.where(row == HEAD_DIM, 1.0, vt)


def _neg_expm1(z):
    series = -z * (1.0 + z * (0.5 + z * (1.0 / 6.0 + z * (1.0 / 24.0))))
    return jnp.where(z > -EXPM1_SERIES_RANGE, series, 1.0 - jnp.exp(z))


def _log_sigmoid(z):
    return jnp.minimum(z, 0.0) - jnp.log1p(jnp.exp(-jnp.abs(z)))


def _row_prefix_sum(v, seg):
    rows, width = v.shape
    g = v.reshape(rows // SUBLANES, SUBLANES, width)
    sub = lax.broadcasted_iota(jnp.int32, g.shape, 1)
    for s in (1, 2, 4):
        g = g + jnp.where(sub >= s, pltpu.roll(g, s, 1), 0.0)
    per_seg = seg // SUBLANES
    g = g.reshape(rows // seg, per_seg, SUBLANES, width)
    groups = [g[:, 0]]
    for i in range(1, per_seg):
        groups.append(g[:, i] + groups[-1][:, SUBLANES - 1:SUBLANES, :])
    return jnp.stack(groups, axis=1).reshape(rows, width)


def _inproj_kernel(x_ref, g_ref, w_ref, o_ref, os_ref, h_ref, *, small_col):
    @pl.when(pl.program_id(1) == 0)
    def _():
        x = x_ref[...]
        ms = jnp.mean(x * x, axis=-1, keepdims=True)
        h_ref[...] = (x * lax.rsqrt(ms + EPS) * g_ref[...]).astype(BF16)

    res = jnp.dot(h_ref[...], w_ref[...], preferred_element_type=F32)
    o_ref[...] = res.astype(BF16)

    @pl.when(pl.program_id(1) == pl.num_programs(1) - 1)
    def _():
        os_ref[...] = res[:, small_col:small_col + LANES]


def _norm_inproj(x, g, w_all, layer):
    n = x.shape[0]
    tm = min(ROWS_INPROJ, n)
    tn = D_PROJ // 2
    assert COL_CKR >= D_PROJ - tn
    return pl.pallas_call(
        functools.partial(_inproj_kernel, small_col=COL_CKR - (D_PROJ - tn)),
        grid=(n // tm, D_PROJ // tn),
        in_specs=[
            pl.BlockSpec((tm, D_MODEL), lambda i, j: (i, 0)),
            pl.BlockSpec((1, D_MODEL), lambda i, j: (0, 0)),
            pl.BlockSpec((None, D_MODEL, tn), lambda i, j: (layer, 0, j)),
        ],
        out_specs=[
            pl.BlockSpec((tm, tn), lambda i, j: (i, j)),
            pl.BlockSpec((tm, LANES), lambda i, j: (i, 0)),
        ],
        out_shape=[jax.ShapeDtypeStruct((n, D_PROJ), BF16), jax.ShapeDtypeStruct((n, LANES), F32)],
        scratch_shapes=[pltpu.VMEM((tm, D_MODEL), BF16)],
        compiler_params=_cparams(("parallel", "arbitrary"), vmem_mb=VMEM_LIMIT_INPROJ_MB),
        name="norm_inproj",
    )(x, g, w_all)


def _outproj_kernel(ya_ref, yb_ref, yc_ref, yd_ref, gate_ref, w_ref, x_ref, o_ref, z_ref):
    for gi, y_ref in enumerate((ya_ref, yb_ref, yc_ref, yd_ref)):
        lo, hi = gi * GROUP_W, (gi + 1) * GROUP_W
        g = gate_ref[:, lo:hi].astype(F32)
        z_ref[:, lo:hi] = (y_ref[...].astype(F32) * (g * _sigmoid(g))).astype(BF16)
    o_ref[...] = x_ref[...] + jnp.dot(z_ref[...], w_ref[...], preferred_element_type=F32)


def _gate_outproj(ya, yb, yc, yd, proj, w_all, x, layer):
    n = x.shape[0]
    tm = min(ROWS_OUTPROJ, n)
    yspec = pl.BlockSpec((tm, GROUP_W), lambda i: (i, 0))
    return pl.pallas_call(
        _outproj_kernel,
        grid=(n // tm,),
        in_specs=[
            yspec, yspec, yspec, yspec,
            pl.BlockSpec((tm, D_MIX), lambda i: (i, 0)),
            pl.BlockSpec((None, D_MIX, D_MODEL), lambda i: (layer, 0, 0)),
            pl.BlockSpec((tm, D_MODEL), lambda i: (i, 0)),
        ],
        out_specs=pl.BlockSpec((tm, D_MODEL), lambda i: (i, 0)),
        out_shape=jax.ShapeDtypeStruct((n, D_MODEL), F32),
        scratch_shapes=[pltpu.VMEM((tm, D_MIX), BF16)],
        compiler_params=_cparams(("parallel",)),
        name="gate_outproj",
    )(ya, yb, yc, yd, proj, w_all, x)


def _attn_kernel(q_ref, k_ref, vt_ref, o_ref, m_ref, acc_ref, st_ref, mx_ref, *, tq, nq, hg):
    i = pl.program_id(2)
    half = tq // 2
    nt_dims = (((1,), (1,)), ((), ()))

    def scores(h, krows, qrows):
        lanes = slice(h * HEAD_PAD, (h + 1) * HEAD_PAD)
        return lax.dot_general(k_ref[krows, lanes], q_ref[qrows, lanes], nt_dims, preferred_element_type=F32)

    def vrows(h):
        return slice(h * VT_ROWS, (h + 1) * VT_ROWS)

    tri = lax.broadcasted_iota(jnp.int32, (half, half), 0) <= lax.broadcasted_iota(jnp.int32, (half, half), 1)
    low = lax.broadcasted_iota(jnp.int32, (tq, half), 0) <= lax.broadcasted_iota(jnp.int32, (tq, half), 1) + half

    def issue_scores(item, c):
        h, blk = item[0], item[1]
        if blk is None:
            off = c * tq
            s_l = jnp.where(tri, scores(h, slice(off, off + half), slice(0, half)), NEG_BIG)
            s_r = jnp.where(low, scores(h, slice(off, off + tq), slice(half, tq)), NEG_BIG)
            mx = jnp.concatenate([jnp.max(s_l, axis=0, keepdims=True), jnp.max(s_r, axis=0, keepdims=True)], axis=1)
            return (s_l, s_r, mx)
        st = scores(h, slice(blk * tq, (blk + item[2]) * tq), slice(0, tq))
        return (st, jnp.max(st, axis=0, keepdims=True))

    def store_scores(item, slot, vals):
        if item[1] is None:
            st_ref[slot, 0:half, 0:half] = vals[0]
            st_ref[slot, 0:tq, half:tq] = vals[1]
        else:
            st_ref[slot, 0:item[2] * tq, :] = vals[0]
        mx_ref[slot] = vals[-1]

    def consume(item, slot, c):
        h, blk = item[0], item[1]
        opens = blk is None
        if opens:
            m_new = mx_ref[slot]
        else:
            m = m_ref[h]
            m_new = jnp.maximum(m, mx_ref[slot])
            alpha = jnp.exp2(m - m_new)
        m_ref[h] = m_new

        def accumulate(cols, st, vt):
            p = jnp.exp2(st - m_new[:, cols]).astype(BF16)
            pv = jnp.dot(vt, p, preferred_element_type=F32)
            acc_ref[h, :, cols] = pv if opens else alpha[:, cols] * acc_ref[h, :, cols] + pv

        if blk is None:
            off = c * tq
            accumulate(slice(0, half), st_ref[slot, 0:half, 0:half], vt_ref[vrows(h), off:off + half])
            accumulate(slice(half, tq), st_ref[slot, 0:tq, half:tq], vt_ref[vrows(h), off:off + tq])
        else:
            accumulate(slice(0, tq), st_ref[slot, 0:item[2] * tq, :], vt_ref[vrows(h), blk * tq:(blk + item[2]) * tq])

    for c in range(nq):
        @pl.when(i == c)
        def _(c=c):
            groups = [(b0, min(ATTN_BLOCKS_PER_ITEM, c - b0)) for b0 in range(0, c, ATTN_BLOCKS_PER_ITEM)]
            items = [(h, None, 1) for h in range(hg)] + [(h, b0, nb) for b0, nb in groups for h in range(hg)]
            store_scores(items[0], 0, issue_scores(items[0], c))
            for n, item in enumerate(items):
                nxt = items[n + 1] if n + 1 < len(items) else None
                if nxt is not None:
                    vals = issue_scores(nxt, c)
                consume(item, n % 2, c)
                if nxt is not None:
                    store_scores(nxt, (n + 1) % 2, vals)

    ot = jnp.concatenate([acc_ref[h, 0:HEAD_DIM, :] * (1.0 / acc_ref[h, HEAD_DIM:HEAD_DIM + 1, :]) for h in range(hg)],
                         axis=0)
    o_ref[...] = ot.T.astype(BF16)


def _attention(q, k, vt, batch, seq):
    n = q.shape[0]
    tq = min(ATTN_BLOCK, seq)
    nq = seq // tq
    hg = ATTN_HEADS_PER_STEP
    return pl.pallas_call(
        functools.partial(_attn_kernel, tq=tq, nq=nq, hg=hg),
        grid=(batch, N_HEADS // hg, nq),
        in_specs=[
            pl.BlockSpec((tq, hg * HEAD_PAD), lambda b, p, i: (b * nq + i, p)),
            pl.BlockSpec((seq, hg * HEAD_PAD), lambda b, p, i: (b, p)),
            pl.BlockSpec((None, hg * VT_ROWS, seq), lambda b, p, i: (b, p, 0)),
        ],
        out_specs=pl.BlockSpec((tq, hg * HEAD_DIM), lambda b, p, i: (b * nq + i, p)),
        out_shape=jax.ShapeDtypeStruct((n, GROUP_W), BF16),
        scratch_shapes=[pltpu.VMEM((hg, 1, tq), F32), pltpu.VMEM((hg, VT_ROWS, tq), F32),
                        pltpu.VMEM((2, ATTN_BLOCKS_PER_ITEM * tq, tq), F32), pltpu.VMEM((2, 1, tq), F32)],
        compiler_params=_cparams(("parallel", "parallel", "arbitrary")),
        name="causal_attention",
    )(q, k, vt)


def _fox_prep_kernel(q_ref, k_ref, v_ref, sm_ref, bf_ref, gq_ref, gk_ref, ones_ref, eye_ref, place_ref, augq_ref,
                     qo_ref, ko_ref, vt_ref, carry_ref, *, tt):
    @pl.when(pl.program_id(1) == 0)
    def _():
        carry_ref[...] = jnp.zeros_like(carry_ref)

    vt = lax.dot_general(eye_ref[...], v_ref[...], (((1,), (1,)), ((), ())), preferred_element_type=F32)
    vt_ref[...] = _with_ones_rows(vt).astype(BF16)

    def headnorm(ref, g_ref):
        z = ref[...].astype(F32)
        ssq = jnp.dot((z * z).astype(BF16), ones_ref[...], preferred_element_type=F32)
        return z * lax.rsqrt(ssq + HEAD_DIM * EPS) * g_ref[...]

    qn = headnorm(q_ref, gq_ref)
    kn = headnorm(k_ref, gk_ref)

    log_f = _log_sigmoid(sm_ref[...] + bf_ref[...])
    cum = _row_prefix_sum(log_f, tt) + carry_ref[0:1, :]
    carry_ref[0:1, :] = cum[tt - 1:tt, :]
    neg = cum * (-LOG2E)

    lane = lax.broadcasted_iota(jnp.int32, (tt, LANES), 1)
    hi = neg.astype(BF16).astype(F32)
    r1 = neg - hi
    mid = r1.astype(BF16).astype(F32)
    lo = r1 - mid
    pieces = jnp.where(lane < SM_BF + PIECE_STRIDE, hi,
                       jnp.where(lane < SM_BF + 2 * PIECE_STRIDE, pltpu.roll(mid, PIECE_STRIDE, 1),
                                 pltpu.roll(lo, 2 * PIECE_STRIDE, 1)))
    aug_k = jnp.dot(pieces.astype(BF16), place_ref[...], preferred_element_type=F32)
    aug_q = augq_ref[...]
    for h in range(N_HEADS):
        pair, e = h // 2, h % 2
        in_head = (lane >= e * HEAD_DIM) & (lane < (e + 1) * HEAD_DIM)
        blk = slice(h * HEAD_PAD, (h + 1) * HEAD_PAD)
        kpair = kn[:, pair * LANES:(pair + 1) * LANES]
        qpair = qn[:, pair * LANES:(pair + 1) * LANES]
        ko_ref[:, blk] = jnp.where(in_head, kpair, aug_k[:, blk]).astype(BF16)
        qo_ref[:, blk] = jnp.where(in_head, qpair, aug_q[:, blk]).astype(BF16)


def _fox_prep(proj, small, bf_row, gq, gk, ones64, eye, place, aug_q, batch, seq):
    n = proj.shape[0]
    tt = min(ROWS_PREP, seq)
    nt = seq // tt
    row = lambda shape: pl.BlockSpec(shape, lambda b, t: (0, 0))
    return pl.pallas_call(
        functools.partial(_fox_prep_kernel, tt=tt),
        grid=(batch, nt),
        in_specs=[
            pl.BlockSpec((tt, GROUP_W), lambda b, t: (b * nt + t, COL_BQ // GROUP_W)),
            pl.BlockSpec((tt, GROUP_W), lambda b, t: (b * nt + t, COL_BK // GROUP_W)),
            pl.BlockSpec((tt, GROUP_W), lambda b, t: (b * nt + t, COL_BV // GROUP_W)),
            pl.BlockSpec((tt, LANES), lambda b, t: (b * nt + t, 0)),
            row((1, LANES)), row((1, GROUP_W)), row((1, GROUP_W)), row((GROUP_W, GROUP_W)),
            row((N_HEADS * VT_ROWS, GROUP_W)), row((LANES, N_HEADS * HEAD_PAD)), row((1, N_HEADS * HEAD_PAD)),
        ],
        out_specs=[
            pl.BlockSpec((tt, N_HEADS * HEAD_PAD), lambda b, t: (b * nt + t, 0)),
            pl.BlockSpec((tt, N_HEADS * HEAD_PAD), lambda b, t: (b * nt + t, 0)),
            pl.BlockSpec((None, N_HEADS * VT_ROWS, tt), lambda b, t: (b, 0, t)),
        ],
        out_shape=[jax.ShapeDtypeStruct((n, N_HEADS * HEAD_PAD), BF16)] * 2
        + [jax.ShapeDtypeStruct((batch, N_HEADS * VT_ROWS, seq), BF16)],
        scratch_shapes=[pltpu.VMEM((SUBLANES, LANES), F32)],
        compiler_params=_cparams(("parallel", "arbitrary")),
        name="fox_prep",
    )(proj, proj, proj, small, bf_row, gq, gk, ones64, eye, place, aug_q)


ROPE_PACK = LANES // MLA_ROPE


def _rope_kernel(pos_ref, invf_ref, c_ref, s_ref):
    rows = pos_ref.shape[0]
    ang = pos_ref[...].astype(F32) * invf_ref[...]
    c, s = jnp.cos(ang), jnp.sin(ang)
    lane = lax.broadcasted_iota(jnp.int32, ang.shape, 1)
    half = MLA_ROPE // 2
    rotary = (lane >= MLA_NOPE) & (lane < MLA_NOPE + MLA_ROPE)
    first_half = (lane >= MLA_NOPE) & (lane < MLA_NOPE + half)
    for i in range(ROPE_PACK):
        shift = (MLA_NOPE - MLA_ROPE * i) % LANES
        ci, si = pltpu.roll(c, shift, 1), pltpu.roll(s, shift, 1)
        c_ref[i * rows:(i + 1) * rows, :] = jnp.where(rotary, ci, 1.0)
        s_ref[i * rows:(i + 1) * rows, :] = jnp.where(first_half, -si, jnp.where(rotary, si, 0.0))


def _rope_tables(positions, inv_freq):
    n = positions.size
    tm = min(ROWS_PREP, n)
    rows = tm // ROPE_PACK
    assert n % tm == 0 and rows % SUBLANES == 0
    pos = positions.reshape(n // tm, ROPE_PACK, rows).transpose(0, 2, 1)
    pos = jnp.repeat(pos, MLA_ROPE, axis=2).reshape(n // ROPE_PACK, LANES)
    invf = jnp.tile(jnp.concatenate([inv_freq, inv_freq]), ROPE_PACK)[None, :]
    out = pl.BlockSpec((tm, LANES), lambda i: (i, 0))
    return pl.pallas_call(
        _rope_kernel,
        grid=(n // tm,),
        in_specs=[pl.BlockSpec((rows, LANES), lambda i: (i, 0)), pl.BlockSpec((1, LANES), lambda i: (0, 0))],
        out_specs=[out, out],
        out_shape=[jax.ShapeDtypeStruct((n, LANES), F32)] * 2,
        compiler_params=_cparams(("parallel",)),
        name="rope_tables",
    )(pos, invf)


def _mla_prep_kernel(cq_ref, ckv_ref, ckr_ref, c_ref, s_ref, gcq_ref, wuq_ref, wuqs_ref, gckv_ref, wuk_ref, wuv_ref,
                     gq_ref, gqs_ref, gk_ref, gks_ref, ones_ref, qo_ref, ko_ref, vo_ref):
    d_qk = float(MLA_NOPE + MLA_ROPE)
    cos, sin = c_ref[...], s_ref[...]
    half = MLA_ROPE // 2

    def rms(z, g):
        return (z * lax.rsqrt(jnp.mean(z * z, axis=-1, keepdims=True) + EPS) * g).astype(BF16)

    def head_ssq(z):
        zz = (z * z).astype(BF16)
        parts = [jnp.dot(zz[:, c * MXU_WIDTH:(c + 1) * MXU_WIDTH], ones_ref[...], preferred_element_type=F32)
                 for c in range(z.shape[1] // MXU_WIDTH)]
        return jnp.concatenate(parts, axis=1)

    cn = rms(cq_ref[...].astype(F32), gcq_ref[...])
    q = jnp.dot(cn, wuq_ref[...], preferred_element_type=F32)
    q_sw = jnp.dot(cn, wuqs_ref[...], preferred_element_type=F32)
    q_inv = lax.rsqrt(head_ssq(q) + d_qk * EPS)

    kvn = rms(ckv_ref[...].astype(F32), gckv_ref[...])
    kn = jnp.dot(kvn, wuk_ref[...], preferred_element_type=F32)
    vt = lax.dot_general(wuv_ref[...], kvn, (((1,), (1,)), ((), ())), preferred_element_type=F32)
    vo_ref[...] = _with_ones_rows(vt).astype(BF16)
    lane = lax.broadcasted_iota(jnp.int32, ckr_ref.shape, 1)
    kr = jnp.where(lane < MLA_ROPE, ckr_ref[...].astype(F32), 0.0)
    ss_r = jnp.sum(kr * kr, axis=-1, keepdims=True)
    kr_placed = pltpu.roll(kr, MLA_NOPE, 1)
    kr_swapped = jnp.where(lane < MLA_NOPE + half, pltpu.roll(kr, MLA_NOPE - half, 1), pltpu.roll(kr, MLA_NOPE + half, 1))
    kr_swapped = jnp.where((lane >= MLA_NOPE) & (lane < MLA_NOPE + MLA_ROPE), kr_swapped, 0.0)
    k_inv = lax.rsqrt(head_ssq(kn) + ss_r + d_qk * EPS)

    gc_q, gs_q = gq_ref[...] * cos, gqs_ref[...] * sin
    gc_k = gk_ref[...] * cos
    kr_rot = kr_placed * gc_k + kr_swapped * (gks_ref[...] * sin)
    for h in range(N_HEADS):
        sl = slice(h * HEAD_PAD, (h + 1) * HEAD_PAD)
        qo_ref[:, sl] = (q_inv[:, sl] * (q[:, sl] * gc_q + q_sw[:, sl] * gs_q)).astype(BF16)
        ko_ref[:, sl] = (k_inv[:, sl] * (kn[:, sl] * gc_k + kr_rot)).astype(BF16)


def _mla_prep(proj, cos_t, sin_t, gcq, wuq, wuq_sw, gckv, wuk, wuv_t, gq, gq_sw, gk, gk_sw, ones128, batch, seq):
    n = proj.shape[0]
    tm = min(ROWS_PREP, seq)
    nt = seq // tm
    full = lambda shape: pl.BlockSpec(shape, lambda i: (0, 0))
    tab = pl.BlockSpec((tm, LANES), lambda i: (i, 0))
    wide = pl.BlockSpec((tm, N_HEADS * HEAD_PAD), lambda i: (i, 0))
    return pl.pallas_call(
        _mla_prep_kernel,
        grid=(n // tm,),
        in_specs=[
            pl.BlockSpec((tm, MLA_Q_RANK), lambda i: (i, COL_CQ // MLA_Q_RANK)),
            pl.BlockSpec((tm, MLA_KV_RANK), lambda i: (i, COL_CKV // LANES)),
            pl.BlockSpec((tm, LANES), lambda i: (i, COL_CKR // LANES)),
            tab, tab,
            full((1, MLA_Q_RANK)), full((MLA_Q_RANK, N_HEADS * HEAD_PAD)), full((MLA_Q_RANK, N_HEADS * HEAD_PAD)),
            full((1, MLA_KV_RANK)), full((MLA_KV_RANK, N_HEADS * HEAD_PAD)), full((N_HEADS * VT_ROWS, MLA_KV_RANK)),
            full((1, HEAD_PAD)), full((1, HEAD_PAD)), full((1, HEAD_PAD)), full((1, HEAD_PAD)), full((MXU_WIDTH, MXU_WIDTH)),
        ],
        out_specs=[wide, wide, pl.BlockSpec((None, N_HEADS * VT_ROWS, tm), lambda i: (i // nt, 0, i % nt))],
        out_shape=[jax.ShapeDtypeStruct((n, N_HEADS * HEAD_PAD), BF16)] * 2
        + [jax.ShapeDtypeStruct((batch, N_HEADS * VT_ROWS, seq), BF16)],
        compiler_params=_cparams(("parallel",)),
        name="mla_prep",
    )(proj, proj, proj, cos_t, sin_t, gcq, wuq, wuq_sw, gckv, wuk, wuv_t, gq, gq_sw, gk, gk_sw, ones128)


def _gla_kernel(q_ref, k_ref, v_ref, sm_ref, wh_ref, wl_ref, b_ref, go_ref, ones_ref, sel_ref, o_ref, st_ref,
                inter_ref, *, tc):
    hk = N_HEADS * GLA_DK
    ck = GLA_CHUNK

    @pl.when(pl.program_id(1) == 0)
    def _():
        st_ref[...] = jnp.zeros_like(st_ref)

    sm = sm_ref[...]
    sm_hi = sm.astype(BF16)
    sm_lo = (sm - sm_hi.astype(F32)).astype(BF16)
    z = (jnp.dot(sm_hi, wh_ref[...], preferred_element_type=F32)
         + jnp.dot(sm_lo, wh_ref[...], preferred_element_type=F32)
         + jnp.dot(sm_hi, wl_ref[...], preferred_element_type=F32)) + b_ref[...]
    g_all = _row_prefix_sum(_log_sigmoid(z) * (1.0 / GLA_NORMALIZER), ck)

    lane_k = lax.broadcasted_iota(jnp.int32, (ck, hk), 1) // GLA_DK
    lane_pair = lax.broadcasted_iota(jnp.int32, (ck, LANES), 1)
    a_row = lax.broadcasted_iota(jnp.int32, (N_HEADS * ck, ck), 0) % ck
    a_col = lax.broadcasted_iota(jnp.int32, (N_HEADS * ck, ck), 1)
    st_blockdiag = (lax.broadcasted_iota(jnp.int32, (GROUP_W, hk), 0) // HEAD_DIM
                    == lax.broadcasted_iota(jnp.int32, (GROUP_W, hk), 1) // GLA_DK)

    def stack_heads(x):
        return jnp.concatenate([jnp.where(lane_k == h, x, jnp.zeros_like(x)) for h in range(N_HEADS)], axis=0)

    def intra_output(a, v):
        parts = []
        for p in range(N_HEADS // 2):
            r = jnp.dot(a[2 * p * ck:2 * (p + 1) * ck, :], v[:, p * LANES:(p + 1) * LANES], preferred_element_type=F32)
            parts.append(jnp.where(lane_pair < HEAD_DIM, r[0:ck, :], r[ck:2 * ck, :]))
        return jnp.concatenate(parts, axis=1)

    def write_normed(o):
        ssq = jnp.dot((o * o).astype(BF16), ones_ref[...], preferred_element_type=F32)
        o_ref[...] = (o * lax.rsqrt(ssq * (1.0 / HEAD_DIM) + EPS) * go_ref[...]).astype(BF16)

    n_chunks = tc // ck
    qts, intras, incs, decays = [], [], [], []
    for c in range(n_chunks):
        rows = slice(c * ck, (c + 1) * ck)
        g = g_all[rows]
        q = q_ref[rows, :].astype(F32) * (GLA_DK ** -0.5)
        k = k_ref[rows, :].astype(F32)
        v = v_ref[rows, :]
        g_last = g[ck - 1:ck, :]
        qt = (q * jnp.exp(g)).astype(BF16)
        kt = (k * jnp.exp(-g)).astype(BF16)
        kh = (k * jnp.exp(g_last - g)).astype(BF16)

        u = lax.dot_general(v, kh, (((0,), (0,)), ((), ())), preferred_element_type=F32)
        incs.append(jnp.where(st_blockdiag, u, 0.0))
        decays.append(jnp.exp(g_last))

        a = lax.dot_general(stack_heads(qt), kt, (((1,), (1,)), ((), ())), preferred_element_type=F32)
        intras.append(intra_output(jnp.where(a_row >= a_col, a, 0.0).astype(BF16), v))
        qts.append(qt)

    state = st_ref[...]
    states = []
    for c in range(n_chunks):
        states.append(state.astype(BF16))
        state = state * decays[c] + incs[c]
    st_ref[...] = state

    outs = []
    for c in range(n_chunks):
        o = lax.dot_general(qts[c], states[c], (((1,), (1,)), ((), ())), preferred_element_type=F32)
        inter_ref[c * ck:(c + 1) * ck, :] = o
        outs.append(o + intras[c])
    write_normed(jnp.concatenate(outs, axis=0))

    @pl.when(jnp.min(g_all) < -GLA_SAFE_LOG_DECAY)
    def _():
        t_idx = lax.broadcasted_iota(jnp.int32, (ck, hk), 0)
        nt_dims = (((1,), (1,)), ((), ()))
        redo = []
        for c in range(n_chunks):
            rows = slice(c * ck, (c + 1) * ck)
            g = g_all[rows]
            q = q_ref[rows, :].astype(F32) * (GLA_DK ** -0.5)
            k = k_ref[rows, :].astype(F32)
            g_hi = g.astype(BF16)
            g_r1 = g - g_hi.astype(F32)
            g_mid = g_r1.astype(BF16)
            g_lo = (g_r1 - g_mid.astype(F32)).astype(BF16)
            g3 = jnp.concatenate([g_hi, g_mid, g_lo], axis=0)
            a = lax.dot_general(stack_heads(q.astype(BF16)), k.astype(BF16), nt_dims, preferred_element_type=F32)
            a = jnp.where(a_row == a_col, a, 0.0)
            for lvl in range(GLA_LEVELS):
                half = ck >> (lvl + 1)
                g_mid_row = jnp.dot(sel_ref[lvl], g3, preferred_element_type=F32)
                d = g - g_mid_row
                second = (t_idx // half) % 2 == 1
                qd = jnp.where(second, q * jnp.exp(jnp.minimum(d, 0.0)), 0.0).astype(BF16)
                kd = jnp.where(second, 0.0, k * jnp.exp(jnp.minimum(-d, 0.0))).astype(BF16)
                part = lax.dot_general(stack_heads(qd), kd, nt_dims, preferred_element_type=F32)
                a = a + jnp.where(a_row // (2 * half) == a_col // (2 * half), part, 0.0)
            redo.append(inter_ref[rows, :] + intra_output(a.astype(BF16), v_ref[rows, :]))
        write_normed(jnp.concatenate(redo, axis=0))


def _gla(proj, small, w_hi, w_lo, b_up, g_out, ones64, sel, batch, seq):
    n = proj.shape[0]
    tc = min(ROWS_GLA, seq)
    nt = seq // tc
    hk = N_HEADS * GLA_DK
    full = lambda shape: pl.BlockSpec(shape, lambda b, t: (0, 0))
    return pl.pallas_call(
        functools.partial(_gla_kernel, tc=tc),
        grid=(batch, nt),
        in_specs=[
            pl.BlockSpec((tc, hk), lambda b, t: (b * nt + t, COL_AQ // hk)),
            pl.BlockSpec((tc, hk), lambda b, t: (b * nt + t, COL_AK // hk)),
            pl.BlockSpec((tc, GROUP_W), lambda b, t: (b * nt + t, COL_AV // GROUP_W)),
            pl.BlockSpec((tc, LANES), lambda b, t: (b * nt + t, 0)),
            full((LANES, hk)), full((LANES, hk)), full((1, hk)), full((1, GROUP_W)), full((GROUP_W, GROUP_W)),
            pl.BlockSpec((GLA_LEVELS, GLA_CHUNK, 3 * GLA_CHUNK), lambda b, t: (0, 0, 0)),
        ],
        out_specs=pl.BlockSpec((tc, GROUP_W), lambda b, t: (b * nt + t, 0)),
        out_shape=jax.ShapeDtypeStruct((n, GROUP_W), BF16),
        scratch_shapes=[pltpu.VMEM((GROUP_W, hk), F32), pltpu.VMEM((tc, GROUP_W), F32)],
        compiler_params=_cparams(("parallel", "arbitrary")),
        name="gla",
    )(proj, proj, proj, small, w_hi, w_lo, b_up, g_out, ones64, sel)


def _lru_kernel(x_ref, cw_ref, cb_ref, wr_ref, br_ref, wi_ref, bi_ref, lam_ref, o_ref, xprev_ref, h_ref, *, tr):
    @pl.when(pl.program_id(1) == 0)
    def _():
        xprev_ref[...] = jnp.zeros_like(xprev_ref)
        h_ref[...] = jnp.zeros_like(h_ref)

    x3 = x_ref[...].astype(F32).reshape(tr // SUBLANES, SUBLANES, GROUP_W)
    xg = jnp.concatenate([xprev_ref[...][None], x3], axis=0)
    xprev_ref[...] = x3[tr // SUBLANES - 1]
    sub3 = lax.broadcasted_iota(jnp.int32, x3.shape, 1)
    xc3 = cb_ref[...][None] + cw_ref[CONV_W - 1:CONV_W, :][None] * x3
    for s in range(1, CONV_W):
        rot = pltpu.roll(xg, s, 1)
        shifted = jnp.where(sub3 >= s, rot[1:], rot[:-1])
        xc3 = xc3 + cw_ref[CONV_W - 1 - s:CONV_W - s, :][None] * shifted
    xc = xc3.reshape(tr, GROUP_W)

    xb = xc.astype(BF16)
    t_r = jnp.tanh(0.5 * (jnp.dot(xb, wr_ref[...], preferred_element_type=F32) + br_ref[...]))
    t_i = jnp.tanh(0.5 * (jnp.dot(xb, wi_ref[...], preferred_element_type=F32) + bi_ref[...]))
    lam = lam_ref[...]
    softplus = jnp.maximum(-lam, 0.0) + jnp.log1p(jnp.exp(-jnp.abs(lam)))
    half_rate = (-0.5 * LRU_C) * softplus
    log_a = half_rate * t_r + half_rate
    one_minus_a = _neg_expm1(log_a)
    a = 1.0 - one_minus_a
    half_x = 0.5 * xc
    bx = jnp.sqrt(one_minus_a * (2.0 - one_minus_a)) * (half_x * t_i + half_x)

    a3 = a.reshape(tr // SUBLANES, SUBLANES, GROUP_W)
    b3 = bx.reshape(tr // SUBLANES, SUBLANES, GROUP_W)
    sub = lax.broadcasted_iota(jnp.int32, a3.shape, 1)
    for s in (1, 2, 4):
        keep = sub >= s
        a_prev = jnp.where(keep, pltpu.roll(a3, s, 1), 1.0)
        b_prev = jnp.where(keep, pltpu.roll(b3, s, 1), 0.0)
        b3 = a3 * b_prev + b3
        a3 = a3 * a_prev
    h_prev = h_ref[0:1, :]
    for g in range(tr // SUBLANES):
        hg = a3[g] * h_prev + b3[g]
        o_ref[g * SUBLANES:(g + 1) * SUBLANES, :] = hg.astype(BF16)
        h_prev = hg[SUBLANES - 1:SUBLANES, :]
    h_ref[0:1, :] = h_prev


def _lru(proj, conv_w, conv_b, wr, br, wi, bi, lam, batch, seq):
    n = proj.shape[0]
    tr = min(ROWS_LRU, seq)
    nt = seq // tr
    full = lambda shape: pl.BlockSpec(shape, lambda b, t: (0, 0))
    return pl.pallas_call(
        functools.partial(_lru_kernel, tr=tr),
        grid=(batch, nt),
        in_specs=[
            pl.BlockSpec((tr, GROUP_W), lambda b, t: (b * nt + t, COL_DX // GROUP_W)),
            full((SUBLANES, GROUP_W)), full((1, GROUP_W)),
            full((GROUP_W, GROUP_W)), full((1, GROUP_W)),
            full((GROUP_W, GROUP_W)), full((1, GROUP_W)), full((1, GROUP_W)),
        ],
        out_specs=pl.BlockSpec((tr, GROUP_W), lambda b, t: (b * nt + t, 0)),
        out_shape=jax.ShapeDtypeStruct((n, GROUP_W), BF16),
        scratch_shapes=[pltpu.VMEM((SUBLANES, GROUP_W), F32), pltpu.VMEM((SUBLANES, GROUP_W), F32)],
        compiler_params=_cparams(("parallel", "arbitrary")),
        name="rglru",
    )(proj, conv_w, conv_b, wr, br, wi, bi, lam)


def _pad_cols(w, width):
    return jnp.pad(w, [(0, 0)] * (w.ndim - 1) + [(0, width - w.shape[-1])])


def _block_diag_ones(n_blocks, size):
    return jnp.asarray(np.kron(np.eye(n_blocks, dtype=np.float32), np.ones((size, size), np.float32)), BF16)


def _swap_rotary(w):
    lead = w.shape[:-1]
    w = w.reshape(lead + (N_HEADS, MLA_NOPE + MLA_ROPE))
    half = MLA_ROPE // 2
    z1, z2 = w[..., MLA_NOPE:MLA_NOPE + half], w[..., MLA_NOPE + half:]
    w = jnp.concatenate([jnp.zeros_like(w[..., :MLA_NOPE]), z2, z1], axis=-1)
    return w.reshape(lead + (N_HEADS * (MLA_NOPE + MLA_ROPE),))


def _pad_heads(w, d_real):
    lead = w.shape[:-1]
    w = w.reshape(lead + (N_HEADS, d_real))
    w = jnp.pad(w, [(0, 0)] * len(lead) + [(0, 0), (0, HEAD_PAD - d_real)])
    return w.reshape(lead + (N_HEADS * HEAD_PAD,))


def kernel(x, positions, norm_g, w_in, w_out, gla_w_up, gla_b_up, gla_g_out, fox_b_f, fox_g_q, fox_g_k, mla_g_cq, mla_w_uq, mla_g_ckv, mla_w_ukv, mla_g_q, mla_g_k, lru_conv_w, lru_conv_b, lru_w_r, lru_b_r, lru_w_i, lru_b_i, lru_lam):
    batch, seq, d_model = x.shape
    depth = w_in.shape[0]
    n = batch * seq
    assert d_model == D_MODEL and w_in.shape[1:] == (D_MODEL, sum(
        (D_MIX, 2 * N_HEADS * GLA_DK, GROUP_W, GLA_RANK, 3 * GROUP_W, N_HEADS, MLA_Q_RANK, MLA_KV_RANK, MLA_ROPE, GROUP_W)))
    for rows in (ROWS_PREP, ROWS_GLA, ROWS_LRU, ATTN_BLOCK):
        assert seq % min(rows, seq) == 0 and min(rows, seq) % GLA_CHUNK == 0, (seq, rows)
    for rows in (ROWS_INPROJ, ROWS_OUTPROJ, ROWS_PREP):
        assert n % min(rows, n) == 0, (n, rows)

    gla_qk = N_HEADS * GLA_DK
    sizes = (D_MIX, gla_qk, gla_qk, GROUP_W, GLA_RANK, GROUP_W, GROUP_W, GROUP_W, N_HEADS,
             MLA_Q_RANK, MLA_KV_RANK, MLA_ROPE, GROUP_W)
    offs = np.concatenate([[0], np.cumsum(sizes)])
    seg = [w_in[:, :, offs[i]:offs[i + 1]] for i in range(len(sizes))]
    (w_gate, w_aq, w_ak, w_av, w_ag, w_bq, w_bk, w_bv, w_bf, w_cq, w_ckv, w_ckr, w_dx) = seg
    assert (SM_AG, SM_BF) == (MLA_ROPE, MLA_ROPE + GLA_RANK)
    w_kr_gates = _pad_cols(jnp.concatenate([w_ckr, w_ag, w_bf], axis=-1), LANES)
    w_main = jnp.concatenate(
        [w_gate, w_aq, w_ak, w_av, w_bq, w_bk, w_bv, w_cq, w_ckv, w_kr_gates, w_dx], axis=-1).astype(BF16)
    w_out_b = w_out.astype(BF16)

    ones64 = _block_diag_ones(GROUP_W // HEAD_DIM, HEAD_DIM)
    ones128 = _block_diag_ones(MXU_WIDTH // HEAD_PAD, HEAD_PAD)

    wup = jnp.pad(gla_w_up, ((0, 0), (SM_AG, LANES - SM_AG - GLA_RANK), (0, 0)))
    wup_hi = wup.astype(BF16)
    wup_lo = (wup - wup_hi.astype(F32)).astype(BF16)

    sel = np.zeros((GLA_LEVELS, GLA_CHUNK, 3 * GLA_CHUNK), np.float32)
    for lvl in range(GLA_LEVELS):
        half = GLA_CHUNK >> (lvl + 1)
        for t in range(GLA_CHUNK):
            mid = (t // (2 * half)) * 2 * half + half - 1
            for piece in range(3):
                sel[lvl, t, piece * GLA_CHUNK + mid] = 1.0
    gla_sel = jnp.asarray(sel, BF16)

    fox_scale = LOG2E
    mla_scale = (MLA_NOPE + MLA_ROPE) ** -0.5 * LOG2E
    fox_gq = jnp.tile(fox_g_q, (1, N_HEADS)) * fox_scale
    fox_gk = jnp.tile(fox_g_k, (1, N_HEADS)) * HEAD_DIM ** 0.5
    place = np.zeros((LANES, N_HEADS * HEAD_PAD), np.float32)
    aug_q = np.zeros((1, N_HEADS * HEAD_PAD), np.float32)
    for h in range(N_HEADS):
        base = h * HEAD_PAD + (HEAD_DIM if h % 2 == 0 else 0)
        for piece in range(3):
            place[SM_BF + piece * PIECE_STRIDE + h, base + piece] = 1.0
            aug_q[0, base + piece] = 1.0
    place, aug_q = jnp.asarray(place, BF16), jnp.asarray(aug_q)
    fox_bf = jnp.pad(fox_b_f, ((0, 0), (SM_BF, LANES - SM_BF - N_HEADS)))

    wuq = _pad_heads(mla_w_uq, MLA_NOPE + MLA_ROPE).astype(BF16)
    wuq_sw = _pad_heads(_swap_rotary(mla_w_uq), MLA_NOPE + MLA_ROPE).astype(BF16)
    wukv = mla_w_ukv.reshape(depth, MLA_KV_RANK, N_HEADS, 2 * HEAD_DIM)
    wuk = _pad_heads(wukv[..., :MLA_NOPE].reshape(depth, MLA_KV_RANK, N_HEADS * MLA_NOPE), MLA_NOPE).astype(BF16)
    pad_rows = lambda m: jnp.pad(m.reshape(m.shape[:-2] + (N_HEADS, HEAD_DIM, m.shape[-1])),
                                 [(0, 0)] * (m.ndim - 1) + [(0, VT_ROWS - HEAD_DIM), (0, 0)]
                                 ).reshape(m.shape[:-2] + (N_HEADS * VT_ROWS, m.shape[-1]))
    wuv_t = pad_rows(jnp.swapaxes(wukv[..., MLA_NOPE:].reshape(depth, MLA_KV_RANK, GROUP_W), 1, 2)).astype(BF16)
    eye = pad_rows(jnp.eye(GROUP_W, dtype=F32)).astype(BF16)
    d_qk = MLA_NOPE + MLA_ROPE
    half = MLA_ROPE // 2
    swap_halves = lambda g: jnp.concatenate(
        [jnp.zeros_like(g[:, :MLA_NOPE]), g[:, MLA_NOPE + half:], g[:, MLA_NOPE:MLA_NOPE + half]], axis=-1)
    mla_gq_all = mla_g_q * (mla_scale * d_qk ** 0.5)
    mla_gk_all = mla_g_k * d_qk ** 0.5
    mla_gq, mla_gq_sw = _pad_cols(mla_gq_all, HEAD_PAD), _pad_cols(swap_halves(mla_gq_all), HEAD_PAD)
    mla_gk, mla_gk_sw = _pad_cols(mla_gk_all, HEAD_PAD), _pad_cols(swap_halves(mla_gk_all), HEAD_PAD)

    eye_h = jnp.eye(N_HEADS, dtype=F32)
    wr_bd = jnp.einsum("lncd,nm->lncmd", lru_w_r, eye_h).reshape(depth, GROUP_W, GROUP_W).astype(BF16)
    wi_bd = jnp.einsum("lncd,nm->lncmd", lru_w_i, eye_h).reshape(depth, GROUP_W, GROUP_W).astype(BF16)
    conv_w = jnp.pad(lru_conv_w, ((0, 0), (0, SUBLANES - CONV_W), (0, 0)))

    inv_freq = ROPE_THETA ** (-jnp.arange(half, dtype=F32) / half)
    cos_t, sin_t = _rope_tables(positions, inv_freq)

    xf = x.reshape(n, D_MODEL)
    for l in range(depth):
        proj, small = _norm_inproj(xf, norm_g[l][None, :], w_main, l)
        ya = _gla(proj, small, wup_hi[l], wup_lo[l], gla_b_up[l][None, :],
                  jnp.tile(gla_g_out[l], N_HEADS)[None, :], ones64, gla_sel, batch, seq)
        fq, fk, fvt = _fox_prep(proj, small, fox_bf[l][None, :], fox_gq[l][None, :], fox_gk[l][None, :], ones64, eye,
                                place, aug_q, batch, seq)
        yb = _attention(fq, fk, fvt, batch, seq)
        mq, mk, mvt = _mla_prep(proj, cos_t, sin_t, mla_g_cq[l][None, :], wuq[l], wuq_sw[l], mla_g_ckv[l][None, :],
                                wuk[l], wuv_t[l], mla_gq[l][None, :], mla_gq_sw[l][None, :], mla_gk[l][None, :],
                                mla_gk_sw[l][None, :], ones128, batch, seq)
        yc = _attention(mq, mk, mvt, batch, seq)
        yd = _lru(proj, conv_w[l], lru_conv_b[l][None, :], wr_bd[l], lru_b_r[l][None, :], wi_bd[l],
                  lru_b_i[l][None, :], lru_lam[l][None, :], batch, seq)
        xf = _gate_outproj(ya, yb, yc, yd, proj, w_out_b, xf, l)
    return xf.reshape(batch, seq, D_MODEL)
```

```python
import functools

import numpy as np
import jax
import jax.numpy as jnp
from jax import lax
from jax.experimental import pallas as pl
from jax.experimental.pallas import tpu as pltpu

F32 = jnp.float32
BF16 = jnp.bfloat16

D_MODEL = 1024
D_MIX = 2048
GROUP_W = 512
N_HEADS = 8
HEAD_DIM = 64
EPS = 1e-6
GLA_DK = 32
GLA_RANK = 16
GLA_NORMALIZER = 16.0
GLA_CHUNK = 64
GLA_LEVELS = 6
GLA_SAFE_LOG_DECAY = 40.0
MLA_Q_RANK = 256
MLA_KV_RANK = 128
MLA_NOPE = 64
MLA_ROPE = 32
ROPE_THETA = 10000.0
CONV_W = 4
LRU_C = 8.0
LOG2E = 1.4426950408889634
NEG_BIG = -1e30
EXPM1_SERIES_RANGE = 1.0 / 64.0

LANES = 128
SUBLANES = 8
MXU_WIDTH = 256
HEAD_PAD = 128
VT_ROWS = 80

COL_AQ, COL_AK, COL_AV = 2048, 2304, 2560
COL_BQ, COL_BK, COL_BV = 3072, 3584, 4096
COL_CQ, COL_CKV, COL_CKR = 4608, 4864, 4992
COL_DX = 5120
D_PROJ = 5632
SM_AG, SM_BF = 32, 48
PIECE_STRIDE = 16


ROWS_INPROJ = 1024
ROWS_OUTPROJ = 1024
ROWS_PREP = 1024
ROWS_GLA = 512
ROWS_LRU = 1024
ROWS_PACK = 256
ATTN_BLOCK = 512
ATTN_HEADS_PER_STEP = 8
ATTN_BLOCKS_PER_ITEM = 2
VMEM_LIMIT_MB = 48
VMEM_LIMIT_INPROJ_MB = 56


def _cparams(sem, vmem_mb=VMEM_LIMIT_MB):
    return pltpu.CompilerParams(dimension_semantics=sem, vmem_limit_bytes=vmem_mb * 1024 * 1024)


def _sigmoid(z):
    return 0.5 * jnp.tanh(0.5 * z) + 0.5


def _with_ones_rows(vt):
    row = lax.broadcasted_iota(jnp.int32, vt.shape, 0) % VT_ROWS
    return jnp.where(row == HEAD_DIM, 1.0, vt)


def _neg_expm1(z):
    series = -z * (1.0 + z * (0.5 + z * (1.0 / 6.0 + z * (1.0 / 24.0))))
    return jnp.where(z > -EXPM1_SERIES_RANGE, series, 1.0 - jnp.exp(z))


def _log_sigmoid(z):
    return jnp.minimum(z, 0.0) - jnp.log1p(jnp.exp(-jnp.abs(z)))


def _row_prefix_sum(v, seg):
    rows, width = v.shape
    g = v.reshape(rows // SUBLANES, SUBLANES, width)
    sub = lax.broadcasted_iota(jnp.int32, g.shape, 1)
    for s in (1, 2, 4):
        g = g + jnp.where(sub >= s, pltpu.roll(g, s, 1), 0.0)
    per_seg = seg // SUBLANES
    g = g.reshape(rows // seg, per_seg, SUBLANES, width)
    groups = [g[:, 0]]
    for i in range(1, per_seg):
        groups.append(g[:, i] + groups[-1][:, SUBLANES - 1:SUBLANES, :])
    return jnp.stack(groups, axis=1).reshape(rows, width)


def _inproj_kernel(x_ref, g_ref, w_ref, o_ref, os_ref, h_ref, *, small_col):
    @pl.when(pl.program_id(1) == 0)
    def _():
        x = x_ref[...]
        ms = jnp.mean(x * x, axis=-1, keepdims=True)
        h_ref[...] = (x * lax.rsqrt(ms + EPS) * g_ref[...]).astype(BF16)

    res = jnp.dot(h_ref[...], w_ref[...], preferred_element_type=F32)
    o_ref[...] = res.astype(BF16)

    @pl.when(pl.program_id(1) == pl.num_programs(1) - 1)
    def _():
        os_ref[...] = res[:, small_col:small_col + LANES]


def _norm_inproj(x, g, w_all, layer):
    n = x.shape[0]
    tm = min(ROWS_INPROJ, n)
    tn = D_PROJ // 2
    assert COL_CKR >= D_PROJ - tn
    return pl.pallas_call(
        functools.partial(_inproj_kernel, small_col=COL_CKR - (D_PROJ - tn)),
        grid=(n // tm, D_PROJ // tn),
        in_specs=[
            pl.BlockSpec((tm, D_MODEL), lambda i, j: (i, 0)),
            pl.BlockSpec((1, D_MODEL), lambda i, j: (0, 0)),
            pl.BlockSpec((None, D_MODEL, tn), lambda i, j: (layer, 0, j)),
        ],
        out_specs=[
            pl.BlockSpec((tm, tn), lambda i, j: (i, j)),
            pl.BlockSpec((tm, LANES), lambda i, j: (i, 0)),
        ],
        out_shape=[jax.ShapeDtypeStruct((n, D_PROJ), BF16), jax.ShapeDtypeStruct((n, LANES), F32)],
        scratch_shapes=[pltpu.VMEM((tm, D_MODEL), BF16)],
        compiler_params=_cparams(("parallel", "arbitrary"), vmem_mb=VMEM_LIMIT_INPROJ_MB),
        name="norm_inproj",
    )(x, g, w_all)


def _pack_kernel(w_ref, o_ref, *, moves, zero_cols):
    for src, dst, width in moves:
        o_ref[:, dst:dst + width] = w_ref[:, src:src + width].astype(BF16)
    lo, hi = zero_cols
    o_ref[:, lo:hi] = jnp.zeros((o_ref.shape[0], hi - lo), BF16)


def _pack_inproj_weights(w_in, moves, zero_cols):
    layers, rows, cols = w_in.shape
    tr = ROWS_PACK
    return pl.pallas_call(
        functools.partial(_pack_kernel, moves=moves, zero_cols=zero_cols),
        grid=(layers, rows // tr),
        in_specs=[pl.BlockSpec((None, tr, cols), lambda l, i: (l, i, 0))],
        out_specs=pl.BlockSpec((None, tr, D_PROJ), lambda l, i: (l, i, 0)),
        out_shape=jax.ShapeDtypeStruct((layers, rows, D_PROJ), BF16),
        compiler_params=_cparams(("parallel", "parallel")),
        name="pack_inproj_weights",
    )(w_in)


def _outproj_kernel(ya_ref, yb_ref, yc_ref, yd_ref, gate_ref, w_ref, x_ref, o_ref, z_ref):
    for gi, y_ref in enumerate((ya_ref, yb_ref, yc_ref, yd_ref)):
        lo, hi = gi * GROUP_W, (gi + 1) * GROUP_W
        g = gate_ref[:, lo:hi].astype(F32)
        z_ref[:, lo:hi] = (y_ref[...].astype(F32) * (g * _sigmoid(g))).astype(BF16)
    o_ref[...] = x_ref[...] + jnp.dot(z_ref[...], w_ref[...], preferred_element_type=F32)


def _gate_outproj(ya, yb, yc, yd, proj, w_all, x, layer):
    n = x.shape[0]
    tm = min(ROWS_OUTPROJ, n)
    yspec = pl.BlockSpec((tm, GROUP_W), lambda i: (i, 0))
    return pl.pallas_call(
        _outproj_kernel,
        grid=(n // tm,),
        in_specs=[
            yspec, yspec, yspec, yspec,
            pl.BlockSpec((tm, D_MIX), lambda i: (i, 0)),
            pl.BlockSpec((None, D_MIX, D_MODEL), lambda i: (layer, 0, 0)),
            pl.BlockSpec((tm, D_MODEL), lambda i: (i, 0)),
        ],
        out_specs=pl.BlockSpec((tm, D_MODEL), lambda i: (i, 0)),
        out_shape=jax.ShapeDtypeStruct((n, D_MODEL), F32),
        scratch_shapes=[pltpu.VMEM((tm, D_MIX), BF16)],
        compiler_params=_cparams(("parallel",)),
        name="gate_outproj",
    )(ya, yb, yc, yd, proj, w_all, x)


def _attn_kernel(q_ref, k_ref, vt_ref, o_ref, m_ref, acc_ref, st_ref, mx_ref, *, tq, nq, hg):
    i = pl.program_id(2)
    half = tq // 2
    nt_dims = (((1,), (1,)), ((), ()))

    def scores(h, krows, qrows):
        lanes = slice(h * HEAD_PAD, (h + 1) * HEAD_PAD)
        return lax.dot_general(k_ref[krows, lanes], q_ref[qrows, lanes], nt_dims, preferred_element_type=F32)

    def vrows(h):
        return slice(h * VT_ROWS, (h + 1) * VT_ROWS)

    tri = lax.broadcasted_iota(jnp.int32, (half, half), 0) <= lax.broadcasted_iota(jnp.int32, (half, half), 1)
    low = lax.broadcasted_iota(jnp.int32, (tq, half), 0) <= lax.broadcasted_iota(jnp.int32, (tq, half), 1) + half

    def issue_scores(item, c):
        h, blk = item[0], item[1]
        if blk is None:
            off = c * tq
            s_l = jnp.where(tri, scores(h, slice(off, off + half), slice(0, half)), NEG_BIG)
            s_r = jnp.where(low, scores(h, slice(off, off + tq), slice(half, tq)), NEG_BIG)
            mx = jnp.concatenate([jnp.max(s_l, axis=0, keepdims=True), jnp.max(s_r, axis=0, keepdims=True)], axis=1)
            return (s_l, s_r, mx)
        st = scores(h, slice(blk * tq, (blk + item[2]) * tq), slice(0, tq))
        return (st, jnp.max(st, axis=0, keepdims=True))

    def store_scores(item, slot, vals):
        if item[1] is None:
            st_ref[slot, 0:half, 0:half] = vals[0]
            st_ref[slot, 0:tq, half:tq] = vals[1]
        else:
            st_ref[slot, 0:item[2] * tq, :] = vals[0]
        mx_ref[slot] = vals[-1]

    def consume(item, slot, c):
        h, blk = item[0], item[1]
        opens = blk is None
        if opens:
            m_new = mx_ref[slot]
        else:
            m = m_ref[h]
            m_new = jnp.maximum(m, mx_ref[slot])
            alpha = jnp.exp2(m - m_new)
        m_ref[h] = m_new

        def accumulate(cols, st, vt):
            p = jnp.exp2(st - m_new[:, cols]).astype(BF16)
            pv = jnp.dot(vt, p, preferred_element_type=F32)
            acc_ref[h, :, cols] = pv if opens else alpha[:, cols] * acc_ref[h, :, cols] + pv

        if blk is None:
            off = c * tq
            accumulate(slice(0, half), st_ref[slot, 0:half, 0:half], vt_ref[vrows(h), off:off + half])
            accumulate(slice(half, tq), st_ref[slot, 0:tq, half:tq], vt_ref[vrows(h), off:off + tq])
        else:
            accumulate(slice(0, tq), st_ref[slot, 0:item[2] * tq, :], vt_ref[vrows(h), blk * tq:(blk + item[2]) * tq])

    for c in range(nq):
        @pl.when(i == c)
        def _(c=c):
            groups = [(b0, min(ATTN_BLOCKS_PER_ITEM, c - b0)) for b0 in range(0, c, ATTN_BLOCKS_PER_ITEM)]
            items = [(h, None, 1) for h in range(hg)] + [(h, b0, nb) for b0, nb in groups for h in range(hg)]
            store_scores(items[0], 0, issue_scores(items[0], c))
            for n, item in enumerate(items):
                nxt = items[n + 1] if n + 1 < len(items) else None
                if nxt is not None:
                    vals = issue_scores(nxt, c)
                consume(item, n % 2, c)
                if nxt is not None:
                    store_scores(nxt, (n + 1) % 2, vals)

    ot = jnp.concatenate([acc_ref[h, 0:HEAD_DIM, :] * (1.0 / acc_ref[h, HEAD_DIM:HEAD_DIM + 1, :]) for h in range(hg)],
                         axis=0)
    o_ref[...] = ot.T.astype(BF16)


def _attention(q, k, vt, batch, seq):
    n = q.shape[0]
    tq = min(ATTN_BLOCK, seq)
    nq = seq // tq
    hg = ATTN_HEADS_PER_STEP
    return pl.pallas_call(
        functools.partial(_attn_kernel, tq=tq, nq=nq, hg=hg),
        grid=(batch, N_HEADS // hg, nq),
        in_specs=[
            pl.BlockSpec((tq, hg * HEAD_PAD), lambda b, p, i: (b * nq + i, p)),
            pl.BlockSpec((seq, hg * HEAD_PAD), lambda b, p, i: (b, p)),
            pl.BlockSpec((None, hg * VT_ROWS, seq), lambda b, p, i: (b, p, 0)),
        ],
        out_specs=pl.BlockSpec((tq, hg * HEAD_DIM), lambda b, p, i: (b * nq + i, p)),
        out_shape=jax.ShapeDtypeStruct((n, GROUP_W), BF16),
        scratch_shapes=[pltpu.VMEM((hg, 1, tq), F32), pltpu.VMEM((hg, VT_ROWS, tq), F32),
                        pltpu.VMEM((2, ATTN_BLOCKS_PER_ITEM * tq, tq), F32), pltpu.VMEM((2, 1, tq), F32)],
        compiler_params=_cparams(("parallel", "parallel", "arbitrary")),
        name="causal_attention",
    )(q, k, vt)


def _fox_prep_kernel(q_ref, k_ref, v_ref, sm_ref, bf_ref, gq_ref, gk_ref, ones_ref, eye_ref, place_ref, augq_ref,
                     qo_ref, ko_ref, vt_ref, carry_ref, *, tt):
    @pl.when(pl.program_id(1) == 0)
    def _():
        carry_ref[...] = jnp.zeros_like(carry_ref)

    vt = lax.dot_general(eye_ref[...], v_ref[...], (((1,), (1,)), ((), ())), preferred_element_type=F32)
    vt_ref[...] = _with_ones_rows(vt).astype(BF16)

    def headnorm(ref, g_ref):
        z = ref[...].astype(F32)
        ssq = jnp.dot((z * z).astype(BF16), ones_ref[...], preferred_element_type=F32)
        return z * lax.rsqrt(ssq + HEAD_DIM * EPS) * g_ref[...]

    qn = headnorm(q_ref, gq_ref)
    kn = headnorm(k_ref, gk_ref)

    log_f = _log_sigmoid(sm_ref[...] + bf_ref[...])
    cum = _row_prefix_sum(log_f, tt) + carry_ref[0:1, :]
    carry_ref[0:1, :] = cum[tt - 1:tt, :]
    neg = cum * (-LOG2E)

    lane = lax.broadcasted_iota(jnp.int32, (tt, LANES), 1)
    hi = neg.astype(BF16).astype(F32)
    r1 = neg - hi
    mid = r1.astype(BF16).astype(F32)
    lo = r1 - mid
    pieces = jnp.where(lane < SM_BF + PIECE_STRIDE, hi,
                       jnp.where(lane < SM_BF + 2 * PIECE_STRIDE, pltpu.roll(mid, PIECE_STRIDE, 1),
                                 pltpu.roll(lo, 2 * PIECE_STRIDE, 1)))
    aug_k = jnp.dot(pieces.astype(BF16), place_ref[...], preferred_element_type=F32)
    aug_q = augq_ref[...]
    for h in range(N_HEADS):
        pair, e = h // 2, h % 2
        in_head = (lane >= e * HEAD_DIM) & (lane < (e + 1) * HEAD_DIM)
        blk = slice(h * HEAD_PAD, (h + 1) * HEAD_PAD)
        kpair = kn[:, pair * LANES:(pair + 1) * LANES]
        qpair = qn[:, pair * LANES:(pair + 1) * LANES]
        ko_ref[:, blk] = jnp.where(in_head, kpair, aug_k[:, blk]).astype(BF16)
        qo_ref[:, blk] = jnp.where(in_head, qpair, aug_q[:, blk]).astype(BF16)


def _fox_prep(proj, small, bf_row, gq, gk, ones64, eye, place, aug_q, batch, seq):
    n = proj.shape[0]
    tt = min(ROWS_PREP, seq)
    nt = seq // tt
    row = lambda shape: pl.BlockSpec(shape, lambda b, t: (0, 0))
    return pl.pallas_call(
        functools.partial(_fox_prep_kernel, tt=tt),
        grid=(batch, nt),
        in_specs=[
            pl.BlockSpec((tt, GROUP_W), lambda b, t: (b * nt + t, COL_BQ // GROUP_W)),
            pl.BlockSpec((tt, GROUP_W), lambda b, t: (b * nt + t, COL_BK // GROUP_W)),
            pl.BlockSpec((tt, GROUP_W), lambda b, t: (b * nt + t, COL_BV // GROUP_W)),
            pl.BlockSpec((tt, LANES), lambda b, t: (b * nt + t, 0)),
            row((1, LANES)), row((1, GROUP_W)), row((1, GROUP_W)), row((GROUP_W, GROUP_W)),
            row((N_HEADS * VT_ROWS, GROUP_W)), row((LANES, N_HEADS * HEAD_PAD)), row((1, N_HEADS * HEAD_PAD)),
        ],
        out_specs=[
            pl.BlockSpec((tt, N_HEADS * HEAD_PAD), lambda b, t: (b * nt + t, 0)),
            pl.BlockSpec((tt, N_HEADS * HEAD_PAD), lambda b, t: (b * nt + t, 0)),
            pl.BlockSpec((None, N_HEADS * VT_ROWS, tt), lambda b, t: (b, 0, t)),
        ],
        out_shape=[jax.ShapeDtypeStruct((n, N_HEADS * HEAD_PAD), BF16)] * 2
        + [jax.ShapeDtypeStruct((batch, N_HEADS * VT_ROWS, seq), BF16)],
        scratch_shapes=[pltpu.VMEM((SUBLANES, LANES), F32)],
        compiler_params=_cparams(("parallel", "arbitrary")),
        name="fox_prep",
    )(proj, proj, proj, small, bf_row, gq, gk, ones64, eye, place, aug_q)


ROPE_PACK = LANES // MLA_ROPE


def _rope_kernel(pos_ref, invf_ref, c_ref, s_ref):
    rows = pos_ref.shape[0]
    ang = pos_ref[...].astype(F32) * invf_ref[...]
    c, s = jnp.cos(ang), jnp.sin(ang)
    lane = lax.broadcasted_iota(jnp.int32, ang.shape, 1)
    half = MLA_ROPE // 2
    rotary = (lane >= MLA_NOPE) & (lane < MLA_NOPE + MLA_ROPE)
    first_half = (lane >= MLA_NOPE) & (lane < MLA_NOPE + half)
    for i in range(ROPE_PACK):
        shift = (MLA_NOPE - MLA_ROPE * i) % LANES
        ci, si = pltpu.roll(c, shift, 1), pltpu.roll(s, shift, 1)
        c_ref[i * rows:(i + 1) * rows, :] = jnp.where(rotary, ci, 1.0)
        s_ref[i * rows:(i + 1) * rows, :] = jnp.where(first_half, -si, jnp.where(rotary, si, 0.0))


def _rope_tables(positions, inv_freq):
    n = positions.size
    tm = min(ROWS_PREP, n)
    rows = tm // ROPE_PACK
    assert n % tm == 0 and rows % SUBLANES == 0
    pos = positions.reshape(n // tm, ROPE_PACK, rows).transpose(0, 2, 1)
    pos = jnp.repeat(pos, MLA_ROPE, axis=2).reshape(n // ROPE_PACK, LANES)
    invf = jnp.tile(jnp.concatenate([inv_freq, inv_freq]), ROPE_PACK)[None, :]
    out = pl.BlockSpec((tm, LANES), lambda i: (i, 0))
    return pl.pallas_call(
        _rope_kernel,
        grid=(n // tm,),
        in_specs=[pl.BlockSpec((rows, LANES), lambda i: (i, 0)), pl.BlockSpec((1, LANES), lambda i: (0, 0))],
        out_specs=[out, out],
        out_shape=[jax.ShapeDtypeStruct((n, LANES), F32)] * 2,
        compiler_params=_cparams(("parallel",)),
        name="rope_tables",
    )(pos, invf)


def _mla_prep_kernel(cq_ref, ckv_ref, ckr_ref, c_ref, s_ref, gcq_ref, wuq_ref, wuqs_ref, gckv_ref, wuk_ref, wuv_ref,
                     gq_ref, gqs_ref, gk_ref, gks_ref, ones_ref, qo_ref, ko_ref, vo_ref):
    d_qk = float(MLA_NOPE + MLA_ROPE)
    cos, sin = c_ref[...], s_ref[...]
    half = MLA_ROPE // 2

    def rms(z, g):
        return (z * lax.rsqrt(jnp.mean(z * z, axis=-1, keepdims=True) + EPS) * g).astype(BF16)

    def head_ssq(z):
        zz = (z * z).astype(BF16)
        parts = [jnp.dot(zz[:, c * MXU_WIDTH:(c + 1) * MXU_WIDTH], ones_ref[...], preferred_element_type=F32)
                 for c in range(z.shape[1] // MXU_WIDTH)]
        return jnp.concatenate(parts, axis=1)

    cn = rms(cq_ref[...].astype(F32), gcq_ref[...])
    q = jnp.dot(cn, wuq_ref[...], preferred_element_type=F32)
    q_sw = jnp.dot(cn, wuqs_ref[...], preferred_element_type=F32)
    q_inv = lax.rsqrt(head_ssq(q) + d_qk * EPS)

    kvn = rms(ckv_ref[...].astype(F32), gckv_ref[...])
    kn = jnp.dot(kvn, wuk_ref[...], preferred_element_type=F32)
    vt = lax.dot_general(wuv_ref[...], kvn, (((1,), (1,)), ((), ())), preferred_element_type=F32)
    vo_ref[...] = _with_ones_rows(vt).astype(BF16)
    lane = lax.broadcasted_iota(jnp.int32, ckr_ref.shape, 1)
    kr = jnp.where(lane < MLA_ROPE, ckr_ref[...].astype(F32), 0.0)
    ss_r = jnp.sum(kr * kr, axis=-1, keepdims=True)
    kr_placed = pltpu.roll(kr, MLA_NOPE, 1)
    kr_swapped = jnp.where(lane < MLA_NOPE + half, pltpu.roll(kr, MLA_NOPE - half, 1), pltpu.roll(kr, MLA_NOPE + half, 1))
    kr_swapped = jnp.where((lane >= MLA_NOPE) & (lane < MLA_NOPE + MLA_ROPE), kr_swapped, 0.0)
    k_inv = lax.rsqrt(head_ssq(kn) + ss_r + d_qk * EPS)

    gc_q, gs_q = gq_ref[...] * cos, gqs_ref[...] * sin
    gc_k = gk_ref[...] * cos
    kr_rot = kr_placed * gc_k + kr_swapped * (gks_ref[...] * sin)
    for h in range(N_HEADS):
        sl = slice(h * HEAD_PAD, (h + 1) * HEAD_PAD)
        qo_ref[:, sl] = (q_inv[:, sl] * (q[:, sl] * gc_q + q_sw[:, sl] * gs_q)).astype(BF16)
        ko_ref[:, sl] = (k_inv[:, sl] * (kn[:, sl] * gc_k + kr_rot)).astype(BF16)


def _mla_prep(proj, cos_t, sin_t, gcq, wuq, wuq_sw, gckv, wuk, wuv_t, gq, gq_sw, gk, gk_sw, ones128, batch, seq):
    n = proj.shape[0]
    tm = min(ROWS_PREP, seq)
    nt = seq // tm
    full = lambda shape: pl.BlockSpec(shape, lambda i: (0, 0))
    tab = pl.BlockSpec((tm, LANES), lambda i: (i, 0))
    wide = pl.BlockSpec((tm, N_HEADS * HEAD_PAD), lambda i: (i, 0))
    return pl.pallas_call(
        _mla_prep_kernel,
        grid=(n // tm,),
        in_specs=[
            pl.BlockSpec((tm, MLA_Q_RANK), lambda i: (i, COL_CQ // MLA_Q_RANK)),
            pl.BlockSpec((tm, MLA_KV_RANK), lambda i: (i, COL_CKV // LANES)),
            pl.BlockSpec((tm, LANES), lambda i: (i, COL_CKR // LANES)),
            tab, tab,
            full((1, MLA_Q_RANK)), full((MLA_Q_RANK, N_HEADS * HEAD_PAD)), full((MLA_Q_RANK, N_HEADS * HEAD_PAD)),
            full((1, MLA_KV_RANK)), full((MLA_KV_RANK, N_HEADS * HEAD_PAD)), full((N_HEADS * VT_ROWS, MLA_KV_RANK)),
            full((1, HEAD_PAD)), full((1, HEAD_PAD)), full((1, HEAD_PAD)), full((1, HEAD_PAD)), full((MXU_WIDTH, MXU_WIDTH)),
        ],
        out_specs=[wide, wide, pl.BlockSpec((None, N_HEADS * VT_ROWS, tm), lambda i: (i // nt, 0, i % nt))],
        out_shape=[jax.ShapeDtypeStruct((n, N_HEADS * HEAD_PAD), BF16)] * 2
        + [jax.ShapeDtypeStruct((batch, N_HEADS * VT_ROWS, seq), BF16)],
        compiler_params=_cparams(("parallel",)),
        name="mla_prep",
    )(proj, proj, proj, cos_t, sin_t, gcq, wuq, wuq_sw, gckv, wuk, wuv_t, gq, gq_sw, gk, gk_sw, ones128)


def _gla_kernel(q_ref, k_ref, v_ref, sm_ref, wh_ref, wl_ref, b_ref, go_ref, ones_ref, sel_ref, o_ref, st_ref,
                inter_ref, *, tc):
    hk = N_HEADS * GLA_DK
    ck = GLA_CHUNK

    @pl.when(pl.program_id(1) == 0)
    def _():
        st_ref[...] = jnp.zeros_like(st_ref)

    sm = sm_ref[...]
    sm_hi = sm.astype(BF16)
    sm_lo = (sm - sm_hi.astype(F32)).astype(BF16)
    z = (jnp.dot(sm_hi, wh_ref[...], preferred_element_type=F32)
         + jnp.dot(sm_lo, wh_ref[...], preferred_element_type=F32)
         + jnp.dot(sm_hi, wl_ref[...], preferred_element_type=F32)) + b_ref[...]
    g_all = _row_prefix_sum(_log_sigmoid(z) * (1.0 / GLA_NORMALIZER), ck)

    lane_k = lax.broadcasted_iota(jnp.int32, (ck, hk), 1) // GLA_DK
    lane_pair = lax.broadcasted_iota(jnp.int32, (ck, LANES), 1)
    a_row = lax.broadcasted_iota(jnp.int32, (N_HEADS * ck, ck), 0) % ck
    a_col = lax.broadcasted_iota(jnp.int32, (N_HEADS * ck, ck), 1)
    st_blockdiag = (lax.broadcasted_iota(jnp.int32, (GROUP_W, hk), 0) // HEAD_DIM
                    == lax.broadcasted_iota(jnp.int32, (GROUP_W, hk), 1) // GLA_DK)

    def stack_heads(x):
        return jnp.concatenate([jnp.where(lane_k == h, x, jnp.zeros_like(x)) for h in range(N_HEADS)], axis=0)

    def intra_output(a, v):
        parts = []
        for p in range(N_HEADS // 2):
            r = jnp.dot(a[2 * p * ck:2 * (p + 1) * ck, :], v[:, p * LANES:(p + 1) * LANES], preferred_element_type=F32)
            parts.append(jnp.where(lane_pair < HEAD_DIM, r[0:ck, :], r[ck:2 * ck, :]))
        return jnp.concatenate(parts, axis=1)

    def write_normed(o):
        ssq = jnp.dot((o * o).astype(BF16), ones_ref[...], preferred_element_type=F32)
        o_ref[...] = (o * lax.rsqrt(ssq * (1.0 / HEAD_DIM) + EPS) * go_ref[...]).astype(BF16)

    n_chunks = tc // ck
    qts, intras, incs, decays = [], [], [], []
    for c in range(n_chunks):
        rows = slice(c * ck, (c + 1) * ck)
        g = g_all[rows]
        q = q_ref[rows, :].astype(F32) * (GLA_DK ** -0.5)
        k = k_ref[rows, :].astype(F32)
        v = v_ref[rows, :]
        g_last = g[ck - 1:ck, :]
        qt = (q * jnp.exp(g)).astype(BF16)
        kt = (k * jnp.exp(-g)).astype(BF16)
        kh = (k * jnp.exp(g_last - g)).astype(BF16)

        u = lax.dot_general(v, kh, (((0,), (0,)), ((), ())), preferred_element_type=F32)
        incs.append(jnp.where(st_blockdiag, u, 0.0))
        decays.append(jnp.exp(g_last))

        a = lax.dot_general(stack_heads(qt), kt, (((1,), (1,)), ((), ())), preferred_element_type=F32)
        intras.append(intra_output(jnp.where(a_row >= a_col, a, 0.0).astype(BF16), v))
        qts.append(qt)

    state = st_ref[...]
    states = []
    for c in range(n_chunks):
        states.append(state.astype(BF16))
        state = state * decays[c] + incs[c]
    st_ref[...] = state

    outs = []
    for c in range(n_chunks):
        o = lax.dot_general(qts[c], states[c], (((1,), (1,)), ((), ())), preferred_element_type=F32)
        inter_ref[c * ck:(c + 1) * ck, :] = o
        outs.append(o + intras[c])
    write_normed(jnp.concatenate(outs, axis=0))

    @pl.when(jnp.min(g_all) < -GLA_SAFE_LOG_DECAY)
    def _():
        t_idx = lax.broadcasted_iota(jnp.int32, (ck, hk), 0)
        nt_dims = (((1,), (1,)), ((), ()))
        redo = []
        for c in range(n_chunks):
            rows = slice(c * ck, (c + 1) * ck)
            g = g_all[rows]
            q = q_ref[rows, :].astype(F32) * (GLA_DK ** -0.5)
            k = k_ref[rows, :].astype(F32)
            g_hi = g.astype(BF16)
            g_r1 = g - g_hi.astype(F32)
            g_mid = g_r1.astype(BF16)
            g_lo = (g_r1 - g_mid.astype(F32)).astype(BF16)
            g3 = jnp.concatenate([g_hi, g_mid, g_lo], axis=0)
            a = lax.dot_general(stack_heads(q.astype(BF16)), k.astype(BF16), nt_dims, preferred_element_type=F32)
            a = jnp.where(a_row == a_col, a, 0.0)
            for lvl in range(GLA_LEVELS):
                half = ck >> (lvl + 1)
                g_mid_row = jnp.dot(sel_ref[lvl], g3, preferred_element_type=F32)
                d = g - g_mid_row
                second = (t_idx // half) % 2 == 1
                qd = jnp.where(second, q * jnp.exp(jnp.minimum(d, 0.0)), 0.0).astype(BF16)
                kd = jnp.where(second, 0.0, k * jnp.exp(jnp.minimum(-d, 0.0))).astype(BF16)
                part = lax.dot_general(stack_heads(qd), kd, nt_dims, preferred_element_type=F32)
                a = a + jnp.where(a_row // (2 * half) == a_col // (2 * half), part, 0.0)
            redo.append(inter_ref[rows, :] + intra_output(a.astype(BF16), v_ref[rows, :]))
        write_normed(jnp.concatenate(redo, axis=0))


def _gla(proj, small, w_hi, w_lo, b_up, g_out, ones64, sel, batch, seq):
    n = proj.shape[0]
    tc = min(ROWS_GLA, seq)
    nt = seq // tc
    hk = N_HEADS * GLA_DK
    full = lambda shape: pl.BlockSpec(shape, lambda b, t: (0, 0))
    return pl.pallas_call(
        functools.partial(_gla_kernel, tc=tc),
        grid=(batch, nt),
        in_specs=[
            pl.BlockSpec((tc, hk), lambda b, t: (b * nt + t, COL_AQ // hk)),
            pl.BlockSpec((tc, hk), lambda b, t: (b * nt + t, COL_AK // hk)),
            pl.BlockSpec((tc, GROUP_W), lambda b, t: (b * nt + t, COL_AV // GROUP_W)),
            pl.BlockSpec((tc, LANES), lambda b, t: (b * nt + t, 0)),
            full((LANES, hk)), full((LANES, hk)), full((1, hk)), full((1, GROUP_W)), full((GROUP_W, GROUP_W)),
            pl.BlockSpec((GLA_LEVELS, GLA_CHUNK, 3 * GLA_CHUNK), lambda b, t: (0, 0, 0)),
        ],
        out_specs=pl.BlockSpec((tc, GROUP_W), lambda b, t: (b * nt + t, 0)),
        out_shape=jax.ShapeDtypeStruct((n, GROUP_W), BF16),
        scratch_shapes=[pltpu.VMEM((GROUP_W, hk), F32), pltpu.VMEM((tc, GROUP_W), F32)],
        compiler_params=_cparams(("parallel", "arbitrary")),
        name="gla",
    )(proj, proj, proj, small, w_hi, w_lo, b_up, g_out, ones64, sel)


def _lru_kernel(x_ref, cw_ref, cb_ref, wr_ref, br_ref, wi_ref, bi_ref, lam_ref, o_ref, xprev_ref, h_ref, *, tr):
    @pl.when(pl.program_id(1) == 0)
    def _():
        xprev_ref[...] = jnp.zeros_like(xprev_ref)
        h_ref[...] = jnp.zeros_like(h_ref)

    x3 = x_ref[...].astype(F32).reshape(tr // SUBLANES, SUBLANES, GROUP_W)
    xg = jnp.concatenate([xprev_ref[...][None], x3], axis=0)
    xprev_ref[...] = x3[tr // SUBLANES - 1]
    sub3 = lax.broadcasted_iota(jnp.int32, x3.shape, 1)
    xc3 = cb_ref[...][None] + cw_ref[CONV_W - 1:CONV_W, :][None] * x3
    for s in range(1, CONV_W):
        rot = pltpu.roll(xg, s, 1)
        shifted = jnp.where(sub3 >= s, rot[1:], rot[:-1])
        xc3 = xc3 + cw_ref[CONV_W - 1 - s:CONV_W - s, :][None] * shifted
    xc = xc3.reshape(tr, GROUP_W)

    xb = xc.astype(BF16)
    t_r = jnp.tanh(0.5 * (jnp.dot(xb, wr_ref[...], preferred_element_type=F32) + br_ref[...]))
    t_i = jnp.tanh(0.5 * (jnp.dot(xb, wi_ref[...], preferred_element_type=F32) + bi_ref[...]))
    lam = lam_ref[...]
    softplus = jnp.maximum(-lam, 0.0) + jnp.log1p(jnp.exp(-jnp.abs(lam)))
    half_rate = (-0.5 * LRU_C) * softplus
    log_a = half_rate * t_r + half_rate
    one_minus_a = _neg_expm1(log_a)
    a = 1.0 - one_minus_a
    half_x = 0.5 * xc
    bx = jnp.sqrt(one_minus_a * (2.0 - one_minus_a)) * (half_x * t_i + half_x)

    a3 = a.reshape(tr // SUBLANES, SUBLANES, GROUP_W)
    b3 = bx.reshape(tr // SUBLANES, SUBLANES, GROUP_W)
    sub = lax.broadcasted_iota(jnp.int32, a3.shape, 1)
    for s in (1, 2, 4):
        keep = sub >= s
        a_prev = jnp.where(keep, pltpu.roll(a3, s, 1), 1.0)
        b_prev = jnp.where(keep, pltpu.roll(b3, s, 1), 0.0)
        b3 = a3 * b_prev + b3
        a3 = a3 * a_prev
    h_prev = h_ref[0:1, :]
    for g in range(tr // SUBLANES):
        hg = a3[g] * h_prev + b3[g]
        o_ref[g * SUBLANES:(g + 1) * SUBLANES, :] = hg.astype(BF16)
        h_prev = hg[SUBLANES - 1:SUBLANES, :]
    h_ref[0:1, :] = h_prev


def _lru(proj, conv_w, conv_b, wr, br, wi, bi, lam, batch, seq):
    n = proj.shape[0]
    tr = min(ROWS_LRU, seq)
    nt = seq // tr
    full = lambda shape: pl.BlockSpec(shape, lambda b, t: (0, 0))
    return pl.pallas_call(
        functools.partial(_lru_kernel, tr=tr),
        grid=(batch, nt),
        in_specs=[
            pl.BlockSpec((tr, GROUP_W), lambda b, t: (b * nt + t, COL_DX // GROUP_W)),
            full((SUBLANES, GROUP_W)), full((1, GROUP_W)),
            full((GROUP_W, GROUP_W)), full((1, GROUP_W)),
            full((GROUP_W, GROUP_W)), full((1, GROUP_W)), full((1, GROUP_W)),
        ],
        out_specs=pl.BlockSpec((tr, GROUP_W), lambda b, t: (b * nt + t, 0)),
        out_shape=jax.ShapeDtypeStruct((n, GROUP_W), BF16),
        scratch_shapes=[pltpu.VMEM((SUBLANES, GROUP_W), F32), pltpu.VMEM((SUBLANES, GROUP_W), F32)],
        compiler_params=_cparams(("parallel", "arbitrary")),
        name="rglru",
    )(proj, conv_w, conv_b, wr, br, wi, bi, lam)


def _pad_cols(w, width):
    return jnp.pad(w, [(0, 0)] * (w.ndim - 1) + [(0, width - w.shape[-1])])


def _block_diag_ones(n_blocks, size):
    return jnp.asarray(np.kron(np.eye(n_blocks, dtype=np.float32), np.ones((size, size), np.float32)), BF16)


def _swap_rotary(w):
    lead = w.shape[:-1]
    w = w.reshape(lead + (N_HEADS, MLA_NOPE + MLA_ROPE))
    half = MLA_ROPE // 2
    z1, z2 = w[..., MLA_NOPE:MLA_NOPE + half], w[..., MLA_NOPE + half:]
    w = jnp.concatenate([jnp.zeros_like(w[..., :MLA_NOPE]), z2, z1], axis=-1)
    return w.reshape(lead + (N_HEADS * (MLA_NOPE + MLA_ROPE),))


def _pad_heads(w, d_real):
    lead = w.shape[:-1]
    w = w.reshape(lead + (N_HEADS, d_real))
    w = jnp.pad(w, [(0, 0)] * len(lead) + [(0, 0), (0, HEAD_PAD - d_real)])
    return w.reshape(lead + (N_HEADS * HEAD_PAD,))


def kernel(x, positions, norm_g, w_in, w_out, gla_w_up, gla_b_up, gla_g_out, fox_b_f, fox_g_q, fox_g_k, mla_g_cq, mla_w_uq, mla_g_ckv, mla_w_ukv, mla_g_q, mla_g_k, lru_conv_w, lru_conv_b, lru_w_r, lru_b_r, lru_w_i, lru_b_i, lru_lam):
    batch, seq, d_model = x.shape
    depth = w_in.shape[0]
    n = batch * seq
    assert d_model == D_MODEL and w_in.shape[1:] == (D_MODEL, sum(
        (D_MIX, 2 * N_HEADS * GLA_DK, GROUP_W, GLA_RANK, 3 * GROUP_W, N_HEADS, MLA_Q_RANK, MLA_KV_RANK, MLA_ROPE, GROUP_W)))
    for rows in (ROWS_PREP, ROWS_GLA, ROWS_LRU, ATTN_BLOCK):
        assert seq % min(rows, seq) == 0 and min(rows, seq) % GLA_CHUNK == 0, (seq, rows)
    for rows in (ROWS_INPROJ, ROWS_OUTPROJ, ROWS_PREP):
        assert n % min(rows, n) == 0, (n, rows)

    gla_qk = N_HEADS * GLA_DK
    sizes = (D_MIX, gla_qk, gla_qk, GROUP_W, GLA_RANK, GROUP_W, GROUP_W, GROUP_W, N_HEADS,
             MLA_Q_RANK, MLA_KV_RANK, MLA_ROPE, GROUP_W)
    offs = np.concatenate([[0], np.cumsum(sizes)])
    assert (SM_AG, SM_BF) == (MLA_ROPE, MLA_ROPE + GLA_RANK)
    dests = (0, COL_AQ, COL_AK, COL_AV, COL_CKR + SM_AG, COL_BQ, COL_BK, COL_BV, COL_CKR + SM_BF,
             COL_CQ, COL_CKV, COL_CKR, COL_DX)
    moves = []
    for src, dst, width in zip(offs[:-1].tolist(), dests, sizes):
        if moves and moves[-1][0] + moves[-1][2] == src and moves[-1][1] + moves[-1][2] == dst:
            moves[-1] = (moves[-1][0], moves[-1][1], moves[-1][2] + width)
        else:
            moves.append((src, dst, width))
    w_main = _pack_inproj_weights(w_in, tuple(moves), (COL_CKR + SM_BF + N_HEADS, COL_DX))
    w_out_b = w_out.astype(BF16)

    ones64 = _block_diag_ones(GROUP_W // HEAD_DIM, HEAD_DIM)
    ones128 = _block_diag_ones(MXU_WIDTH // HEAD_PAD, HEAD_PAD)

    wup = jnp.pad(gla_w_up, ((0, 0), (SM_AG, LANES - SM_AG - GLA_RANK), (0, 0)))
    wup_hi = wup.astype(BF16)
    wup_lo = (wup - wup_hi.astype(F32)).astype(BF16)

    sel = np.zeros((GLA_LEVELS, GLA_CHUNK, 3 * GLA_CHUNK), np.float32)
    for lvl in range(GLA_LEVELS):
        half = GLA_CHUNK >> (lvl + 1)
        for t in range(GLA_CHUNK):
            mid = (t // (2 * half)) * 2 * half + half - 1
            for piece in range(3):
                sel[lvl, t, piece * GLA_CHUNK + mid] = 1.0
    gla_sel = jnp.asarray(sel, BF16)

    fox_scale = LOG2E
    mla_scale = (MLA_NOPE + MLA_ROPE) ** -0.5 * LOG2E
    fox_gq = jnp.tile(fox_g_q, (1, N_HEADS)) * fox_scale
    fox_gk = jnp.tile(fox_g_k, (1, N_HEADS)) * HEAD_DIM ** 0.5
    place = np.zeros((LANES, N_HEADS * HEAD_PAD), np.float32)
    aug_q = np.zeros((1, N_HEADS * HEAD_PAD), np.float32)
    for h in range(N_HEADS):
        base = h * HEAD_PAD + (HEAD_DIM if h % 2 == 0 else 0)
        for piece in range(3):
            place[SM_BF + piece * PIECE_STRIDE + h, base + piece] = 1.0
            aug_q[0, base + piece] = 1.0
    place, aug_q = jnp.asarray(place, BF16), jnp.asarray(aug_q)
    fox_bf = jnp.pad(fox_b_f, ((0, 0), (SM_BF, LANES - SM_BF - N_HEADS)))

    wuq = _pad_heads(mla_w_uq, MLA_NOPE + MLA_ROPE).astype(BF16)
    wuq_sw = _pad_heads(_swap_rotary(mla_w_uq), MLA_NOPE + MLA_ROPE).astype(BF16)
    wukv = mla_w_ukv.reshape(depth, MLA_KV_RANK, N_HEADS, 2 * HEAD_DIM)
    wuk = _pad_heads(wukv[..., :MLA_NOPE].reshape(depth, MLA_KV_RANK, N_HEADS * MLA_NOPE), MLA_NOPE).astype(BF16)
    pad_rows = lambda m: jnp.pad(m.reshape(m.shape[:-2] + (N_HEADS, HEAD_DIM, m.shape[-1])),
                                 [(0, 0)] * (m.ndim - 1) + [(0, VT_ROWS - HEAD_DIM), (0, 0)]
                                 ).reshape(m.shape[:-2] + (N_HEADS * VT_ROWS, m.shape[-1]))
    wuv_t = pad_rows(jnp.swapaxes(wukv[..., MLA_NOPE:].reshape(depth, MLA_KV_RANK, GROUP_W), 1, 2)).astype(BF16)
    eye = pad_rows(jnp.eye(GROUP_W, dtype=F32)).astype(BF16)
    d_qk = MLA_NOPE + MLA_ROPE
    half = MLA_ROPE // 2
    swap_halves = lambda g: jnp.concatenate(
        [jnp.zeros_like(g[:, :MLA_NOPE]), g[:, MLA_NOPE + half:], g[:, MLA_NOPE:MLA_NOPE + half]], axis=-1)
    mla_gq_all = mla_g_q * (mla_scale * d_qk ** 0.5)
    mla_gk_all = mla_g_k * d_qk ** 0.5
    mla_gq, mla_gq_sw = _pad_cols(mla_gq_all, HEAD_PAD), _pad_cols(swap_halves(mla_gq_all), HEAD_PAD)
    mla_gk, mla_gk_sw = _pad_cols(mla_gk_all, HEAD_PAD), _pad_cols(swap_halves(mla_gk_all), HEAD_PAD)

    eye_h = jnp.eye(N_HEADS, dtype=F32)
    wr_bd = jnp.einsum("lncd,nm->lncmd", lru_w_r, eye_h).reshape(depth, GROUP_W, GROUP_W).astype(BF16)
    wi_bd = jnp.einsum("lncd,nm->lncmd", lru_w_i, eye_h).reshape(depth, GROUP_W, GROUP_W).astype(BF16)
    conv_w = jnp.pad(lru_conv_w, ((0, 0), (0, SUBLANES - CONV_W), (0, 0)))

    inv_freq = ROPE_THETA ** (-jnp.arange(half, dtype=F32) / half)
    cos_t, sin_t = _rope_tables(positions, inv_freq)

    xf = x.reshape(n, D_MODEL)
    for l in range(depth):
        proj, small = _norm_inproj(xf, norm_g[l][None, :], w_main, l)
        ya = _gla(proj, small, wup_hi[l], wup_lo[l], gla_b_up[l][None, :],
                  jnp.tile(gla_g_out[l], N_HEADS)[None, :], ones64, gla_sel, batch, seq)
        fq, fk, fvt = _fox_prep(proj, small, fox_bf[l][None, :], fox_gq[l][None, :], fox_gk[l][None, :], ones64, eye,
                                place, aug_q, batch, seq)
        yb = _attention(fq, fk, fvt, batch, seq)
        mq, mk, mvt = _mla_prep(proj, cos_t, sin_t, mla_g_cq[l][None, :], wuq[l], wuq_sw[l], mla_g_ckv[l][None, :],
                                wuk[l], wuv_t[l], mla_gq[l][None, :], mla_gq_sw[l][None, :], mla_gk[l][None, :],
                                mla_gk_sw[l][None, :], ones128, batch, seq)
        yc = _attention(mq, mk, mvt, batch, seq)
        yd = _lru(proj, conv_w[l], lru_conv_b[l][None, :], wr_bd[l], lru_b_r[l][None, :], wi_bd[l],
                  lru_b_i[l][None, :], lru_lam[l][None, :], batch, seq)
        xf = _gate_outproj(ya, yb, yc, yd, proj, w_out_b, xf, l)
    return xf.reshape(batch, seq, D_MODEL)
```

```python
import functools

import numpy as np
import jax
import jax.numpy as jnp
from jax import lax
from jax.experimental import pallas as pl
from jax.experimental.pallas import tpu as pltpu

F32 = jnp.float32
BF16 = jnp.bfloat16

D_MODEL = 1024
D_MIX = 2048
GROUP_W = 512
N_HEADS = 8
HEAD_DIM = 64
EPS = 1e-6
GLA_DK = 32
GLA_RANK = 16
GLA_NORMALIZER = 16.0
GLA_CHUNK = 64
GLA_LEVELS = 6
GLA_SAFE_LOG_DECAY = 40.0
MLA_Q_RANK = 256
MLA_KV_RANK = 128
MLA_NOPE = 64
MLA_ROPE = 32
ROPE_THETA = 10000.0
CONV_W = 4
LRU_C = 8.0
LOG2E = 1.4426950408889634
NEG_BIG = -1e30
EXPM1_SERIES_RANGE = 1.0 / 64.0

LANES = 128
SUBLANES = 8
MXU_WIDTH = 256
HEAD_PAD = 128
VT_ROWS = 80

COL_AQ, COL_AK, COL_AV = 2048, 2304, 2560
COL_BQ, COL_BK, COL_BV = 3072, 3584, 4096
COL_CQ, COL_CKV, COL_CKR = 4608, 4864, 4992
COL_DX = 5120
D_PROJ = 5632
SM_AG, SM_BF = 32, 48
PIECE_STRIDE = 16


ROWS_INPROJ = 1024
ROWS_OUTPROJ = 1024
ROWS_PREP = 1024
ROWS_GLA = 512
ROWS_LRU = 1024
ATTN_BLOCK = 512
ATTN_HEADS_PER_STEP = 8
ATTN_BLOCKS_PER_ITEM = 2
VMEM_LIMIT_MB = 48
VMEM_LIMIT_INPROJ_MB = 56
RESIDUAL_SLOTS = 3


def _cparams(sem, vmem_mb=VMEM_LIMIT_MB):
    return pltpu.CompilerParams(dimension_semantics=sem, vmem_limit_bytes=vmem_mb * 1024 * 1024)


def _sigmoid(z):
    return 0.5 * jnp.tanh(0.5 * z) + 0.5


def _with_ones_rows(vt):
    row = lax.broadcasted_iota(jnp.int32, vt.shape, 0) % VT_ROWS
    return jnp.where(row == HEAD_DIM, 1.0, vt)


def _neg_expm1(z):
    series = -z * (1.0 + z * (0.5 + z * (1.0 / 6.0 + z * (1.0 / 24.0))))
    return jnp.where(z > -EXPM1_SERIES_RANGE, series, 1.0 - jnp.exp(z))


def _log_sigmoid(z):
    return jnp.minimum(z, 0.0) - jnp.log1p(jnp.exp(-jnp.abs(z)))


def _row_prefix_sum(v, seg):
    rows, width = v.shape
    g = v.reshape(rows // SUBLANES, SUBLANES, width)
    sub = lax.broadcasted_iota(jnp.int32, g.shape, 1)
    for s in (1, 2, 4):
        g = g + jnp.where(sub >= s, pltpu.roll(g, s, 1), 0.0)
    per_seg = seg // SUBLANES
    g = g.reshape(rows // seg, per_seg, SUBLANES, width)
    groups = [g[:, 0]]
    for i in range(1, per_seg):
        groups.append(g[:, i] + groups[-1][:, SUBLANES - 1:SUBLANES, :])
    return jnp.stack(groups, axis=1).reshape(rows, width)


def _inproj_kernel(x_ref, g_ref, w_ref, o_ref, os_ref, h_ref, *, small_col):
    @pl.when(pl.program_id(1) == 0)
    def _():
        x = x_ref[...]
        ms = jnp.mean(x * x, axis=-1, keepdims=True)
        h_ref[...] = (x * lax.rsqrt(ms + EPS) * g_ref[...]).astype(BF16)

    res = jnp.dot(h_ref[...], w_ref[...], preferred_element_type=F32)
    o_ref[...] = res.astype(BF16)

    @pl.when(pl.program_id(1) == pl.num_programs(1) - 1)
    def _():
        os_ref[...] = res[:, small_col:small_col + LANES]


def _norm_inproj(x, g, w_all, layer):
    n = x.shape[0]
    tm = min(ROWS_INPROJ, n)
    tn = D_PROJ // 2
    assert COL_CKR >= D_PROJ - tn
    return pl.pallas_call(
        functools.partial(_inproj_kernel, small_col=COL_CKR - (D_PROJ - tn)),
        grid=(n // tm, D_PROJ // tn),
        in_specs=[
            pl.BlockSpec((tm, D_MODEL), lambda i, j: (i, 0)),
            pl.BlockSpec((1, D_MODEL), lambda i, j: (0, 0)),
            pl.BlockSpec((None, D_MODEL, tn), lambda i, j: (layer, 0, j)),
        ],
        out_specs=[
            pl.BlockSpec((tm, tn), lambda i, j: (i, j)),
            pl.BlockSpec((tm, LANES), lambda i, j: (i, 0)),
        ],
        out_shape=[jax.ShapeDtypeStruct((n, D_PROJ), BF16), jax.ShapeDtypeStruct((n, LANES), F32)],
        scratch_shapes=[pltpu.VMEM((tm, D_MODEL), BF16)],
        compiler_params=_cparams(("parallel", "arbitrary"), vmem_mb=VMEM_LIMIT_INPROJ_MB),
        name="norm_inproj",
    )(x, g, w_all)


def _outproj_kernel(ya_ref, yb_ref, yc_ref, yd_ref, gate_ref, w_ref, x_hbm, o_ref, z_ref, xbuf_ref, sem_ref,
                    *, tm, steps):
    s = pl.program_id(0)

    def residual_copy(step, slot):
        return pltpu.make_async_copy(x_hbm.at[pl.ds(step * tm, tm), :], xbuf_ref.at[slot], sem_ref.at[slot])

    @pl.when(s == 0)
    def _():
        for step in range(min(RESIDUAL_SLOTS - 1, steps)):
            residual_copy(step, step).start()

    ahead = s + (RESIDUAL_SLOTS - 1)

    @pl.when(ahead < steps)
    def _():
        residual_copy(ahead, ahead % RESIDUAL_SLOTS).start()

    for gi, y_ref in enumerate((ya_ref, yb_ref, yc_ref, yd_ref)):
        lo, hi = gi * GROUP_W, (gi + 1) * GROUP_W
        g = gate_ref[:, lo:hi].astype(F32)
        z_ref[:, lo:hi] = (y_ref[...].astype(F32) * (g * _sigmoid(g))).astype(BF16)
    slot = s % RESIDUAL_SLOTS
    residual_copy(s, slot).wait()
    o_ref[...] = xbuf_ref[slot] + jnp.dot(z_ref[...], w_ref[...], preferred_element_type=F32)


def _gate_outproj(ya, yb, yc, yd, proj, w_all, x, layer):
    n = x.shape[0]
    tm = min(ROWS_OUTPROJ, n)
    yspec = pl.BlockSpec((tm, GROUP_W), lambda i: (i, 0))
    return pl.pallas_call(
        functools.partial(_outproj_kernel, tm=tm, steps=n // tm),
        grid=(n // tm,),
        in_specs=[
            yspec, yspec, yspec, yspec,
            pl.BlockSpec((tm, D_MIX), lambda i: (i, 0)),
            pl.BlockSpec((None, D_MIX, D_MODEL), lambda i: (layer, 0, 0)),
            pl.BlockSpec(memory_space=pl.ANY),
        ],
        out_specs=pl.BlockSpec((tm, D_MODEL), lambda i: (i, 0)),
        out_shape=jax.ShapeDtypeStruct((n, D_MODEL), F32),
        scratch_shapes=[pltpu.VMEM((tm, D_MIX), BF16), pltpu.VMEM((RESIDUAL_SLOTS, tm, D_MODEL), F32),
                        pltpu.SemaphoreType.DMA((RESIDUAL_SLOTS,))],
        compiler_params=_cparams(("arbitrary",), vmem_mb=VMEM_LIMIT_INPROJ_MB),
        name="gate_outproj",
    )(ya, yb, yc, yd, proj, w_all, x)


def _attn_kernel(q_ref, k_ref, vt_ref, o_ref, m_ref, acc_ref, st_ref, mx_ref, *, tq, nq, hg):
    i = pl.program_id(2)
    half = tq // 2
    nt_dims = (((1,), (1,)), ((), ()))

    def scores(h, krows, qrows):
        lanes = slice(h * HEAD_PAD, (h + 1) * HEAD_PAD)
        return lax.dot_general(k_ref[krows, lanes], q_ref[qrows, lanes], nt_dims, preferred_element_type=F32)

    def vrows(h):
        return slice(h * VT_ROWS, (h + 1) * VT_ROWS)

    tri = lax.broadcasted_iota(jnp.int32, (half, half), 0) <= lax.broadcasted_iota(jnp.int32, (half, half), 1)
    low = lax.broadcasted_iota(jnp.int32, (tq, half), 0) <= lax.broadcasted_iota(jnp.int32, (tq, half), 1) + half

    def issue_scores(item, c):
        h, blk = item[0], item[1]
        if blk is None:
            off = c * tq
            s_l = jnp.where(tri, scores(h, slice(off, off + half), slice(0, half)), NEG_BIG)
            s_r = jnp.where(low, scores(h, slice(off, off + tq), slice(half, tq)), NEG_BIG)
            mx = jnp.concatenate([jnp.max(s_l, axis=0, keepdims=True), jnp.max(s_r, axis=0, keepdims=True)], axis=1)
            return (s_l, s_r, mx)
        st = scores(h, slice(blk * tq, (blk + item[2]) * tq), slice(0, tq))
        return (st, jnp.max(st, axis=0, keepdims=True))

    def store_scores(item, slot, vals):
        if item[1] is None:
            st_ref[slot, 0:half, 0:half] = vals[0]
            st_ref[slot, 0:tq, half:tq] = vals[1]
        else:
            st_ref[slot, 0:item[2] * tq, :] = vals[0]
        mx_ref[slot] = vals[-1]

    def consume(item, slot, c):
        h, blk = item[0], item[1]
        opens = blk is None
        if opens:
            m_new = mx_ref[slot]
        else:
            m = m_ref[h]
            m_new = jnp.maximum(m, mx_ref[slot])
            alpha = jnp.exp2(m - m_new)
        m_ref[h] = m_new

        def accumulate(cols, st, vt):
            p = jnp.exp2(st - m_new[:, cols]).astype(BF16)
            pv = jnp.dot(vt, p, preferred_element_type=F32)
            acc_ref[h, :, cols] = pv if opens else alpha[:, cols] * acc_ref[h, :, cols] + pv

        if blk is None:
            off = c * tq
            accumulate(slice(0, half), st_ref[slot, 0:half, 0:half], vt_ref[vrows(h), off:off + half])
            accumulate(slice(half, tq), st_ref[slot, 0:tq, half:tq], vt_ref[vrows(h), off:off + tq])
        else:
            accumulate(slice(0, tq), st_ref[slot, 0:item[2] * tq, :], vt_ref[vrows(h), blk * tq:(blk + item[2]) * tq])

    for c in range(nq):
        @pl.when(i == c)
        def _(c=c):
            groups = [(b0, min(ATTN_BLOCKS_PER_ITEM, c - b0)) for b0 in range(0, c, ATTN_BLOCKS_PER_ITEM)]
            items = [(h, None, 1) for h in range(hg)] + [(h, b0, nb) for b0, nb in groups for h in range(hg)]
            store_scores(items[0], 0, issue_scores(items[0], c))
            for n, item in enumerate(items):
                nxt = items[n + 1] if n + 1 < len(items) else None
                if nxt is not None:
                    vals = issue_scores(nxt, c)
                consume(item, n % 2, c)
                if nxt is not None:
                    store_scores(nxt, (n + 1) % 2, vals)

    ot = jnp.concatenate([acc_ref[h, 0:HEAD_DIM, :] * (1.0 / acc_ref[h, HEAD_DIM:HEAD_DIM + 1, :]) for h in range(hg)],
                         axis=0)
    o_ref[...] = ot.T.astype(BF16)


def _attention(q, k, vt, batch, seq):
    n = q.shape[0]
    tq = min(ATTN_BLOCK, seq)
    nq = seq // tq
    hg = ATTN_HEADS_PER_STEP
    return pl.pallas_call(
        functools.partial(_attn_kernel, tq=tq, nq=nq, hg=hg),
        grid=(batch, N_HEADS // hg, nq),
        in_specs=[
            pl.BlockSpec((tq, hg * HEAD_PAD), lambda b, p, i: (b * nq + i, p)),
            pl.BlockSpec((seq, hg * HEAD_PAD), lambda b, p, i: (b, p)),
            pl.BlockSpec((None, hg * VT_ROWS, seq), lambda b, p, i: (b, p, 0)),
        ],
        out_specs=pl.BlockSpec((tq, hg * HEAD_DIM), lambda b, p, i: (b * nq + i, p)),
        out_shape=jax.ShapeDtypeStruct((n, GROUP_W), BF16),
        scratch_shapes=[pltpu.VMEM((hg, 1, tq), F32), pltpu.VMEM((hg, VT_ROWS, tq), F32),
                        pltpu.VMEM((2, ATTN_BLOCKS_PER_ITEM * tq, tq), F32), pltpu.VMEM((2, 1, tq), F32)],
        compiler_params=_cparams(("parallel", "parallel", "arbitrary")),
        name="causal_attention",
    )(q, k, vt)


def _fox_prep_kernel(q_ref, k_ref, v_ref, sm_ref, bf_ref, gq_ref, gk_ref, ones_ref, eye_ref, place_ref, augq_ref,
                     qo_ref, ko_ref, vt_ref, carry_ref, *, tt):
    @pl.when(pl.program_id(1) == 0)
    def _():
        carry_ref[...] = jnp.zeros_like(carry_ref)

    vt = lax.dot_general(eye_ref[...], v_ref[...], (((1,), (1,)), ((), ())), preferred_element_type=F32)
    vt_ref[...] = _with_ones_rows(vt).astype(BF16)

    def headnorm(ref, g_ref):
        z = ref[...].astype(F32)
        ssq = jnp.dot((z * z).astype(BF16), ones_ref[...], preferred_element_type=F32)
        return z * lax.rsqrt(ssq + HEAD_DIM * EPS) * g_ref[...]

    qn = headnorm(q_ref, gq_ref)
    kn = headnorm(k_ref, gk_ref)

    log_f = _log_sigmoid(sm_ref[...] + bf_ref[...])
    cum = _row_prefix_sum(log_f, tt) + carry_ref[0:1, :]
    carry_ref[0:1, :] = cum[tt - 1:tt, :]
    neg = cum * (-LOG2E)

    lane = lax.broadcasted_iota(jnp.int32, (tt, LANES), 1)
    hi = neg.astype(BF16).astype(F32)
    r1 = neg - hi
    mid = r1.astype(BF16).astype(F32)
    lo = r1 - mid
    pieces = jnp.where(lane < SM_BF + PIECE_STRIDE, hi,
                       jnp.where(lane < SM_BF + 2 * PIECE_STRIDE, pltpu.roll(mid, PIECE_STRIDE, 1),
                                 pltpu.roll(lo, 2 * PIECE_STRIDE, 1)))
    aug_k = jnp.dot(pieces.astype(BF16), place_ref[...], preferred_element_type=F32)
    aug_q = augq_ref[...]
    for h in range(N_HEADS):
        pair, e = h // 2, h % 2
        in_head = (lane >= e * HEAD_DIM) & (lane < (e + 1) * HEAD_DIM)
        blk = slice(h * HEAD_PAD, (h + 1) * HEAD_PAD)
        kpair = kn[:, pair * LANES:(pair + 1) * LANES]
        qpair = qn[:, pair * LANES:(pair + 1) * LANES]
        ko_ref[:, blk] = jnp.where(in_head, kpair, aug_k[:, blk]).astype(BF16)
        qo_ref[:, blk] = jnp.where(in_head, qpair, aug_q[:, blk]).astype(BF16)


def _fox_prep(proj, small, bf_row, gq, gk, ones64, eye, place, aug_q, batch, seq):
    n = proj.shape[0]
    tt = min(ROWS_PREP, seq)
    nt = seq // tt
    row = lambda shape: pl.BlockSpec(shape, lambda b, t: (0, 0))
    return pl.pallas_call(
        functools.partial(_fox_prep_kernel, tt=tt),
        grid=(batch, nt),
        in_specs=[
            pl.BlockSpec((tt, GROUP_W), lambda b, t: (b * nt + t, COL_BQ // GROUP_W)),
            pl.BlockSpec((tt, GROUP_W), lambda b, t: (b * nt + t, COL_BK // GROUP_W)),
            pl.BlockSpec((tt, GROUP_W), lambda b, t: (b * nt + t, COL_BV // GROUP_W)),
            pl.BlockSpec((tt, LANES), lambda b, t: (b * nt + t, 0)),
            row((1, LANES)), row((1, GROUP_W)), row((1, GROUP_W)), row((GROUP_W, GROUP_W)),
            row((N_HEADS * VT_ROWS, GROUP_W)), row((LANES, N_HEADS * HEAD_PAD)), row((1, N_HEADS * HEAD_PAD)),
        ],
        out_specs=[
            pl.BlockSpec((tt, N_HEADS * HEAD_PAD), lambda b, t: (b * nt + t, 0)),
            pl.BlockSpec((tt, N_HEADS * HEAD_PAD), lambda b, t: (b * nt + t, 0)),
            pl.BlockSpec((None, N_HEADS * VT_ROWS, tt), lambda b, t: (b, 0, t)),
        ],
        out_shape=[jax.ShapeDtypeStruct((n, N_HEADS * HEAD_PAD), BF16)] * 2
        + [jax.ShapeDtypeStruct((batch, N_HEADS * VT_ROWS, seq), BF16)],
        scratch_shapes=[pltpu.VMEM((SUBLANES, LANES), F32)],
        compiler_params=_cparams(("parallel", "arbitrary")),
        name="fox_prep",
    )(proj, proj, proj, small, bf_row, gq, gk, ones64, eye, place, aug_q)


ROPE_PACK = LANES // MLA_ROPE


def _rope_kernel(pos_ref, invf_ref, c_ref, s_ref):
    rows = pos_ref.shape[0]
    ang = pos_ref[...].astype(F32) * invf_ref[...]
    c, s = jnp.cos(ang), jnp.sin(ang)
    lane = lax.broadcasted_iota(jnp.int32, ang.shape, 1)
    half = MLA_ROPE // 2
    rotary = (lane >= MLA_NOPE) & (lane < MLA_NOPE + MLA_ROPE)
    first_half = (lane >= MLA_NOPE) & (lane < MLA_NOPE + half)
    for i in range(ROPE_PACK):
        shift = (MLA_NOPE - MLA_ROPE * i) % LANES
        ci, si = pltpu.roll(c, shift, 1), pltpu.roll(s, shift, 1)
        c_ref[i * rows:(i + 1) * rows, :] = jnp.where(rotary, ci, 1.0)
        s_ref[i * rows:(i + 1) * rows, :] = jnp.where(first_half, -si, jnp.where(rotary, si, 0.0))


def _rope_tables(positions, inv_freq):
    n = positions.size
    tm = min(ROWS_PREP, n)
    rows = tm // ROPE_PACK
    assert n % tm == 0 and rows % SUBLANES == 0
    pos = positions.reshape(n // tm, ROPE_PACK, rows).transpose(0, 2, 1)
    pos = jnp.repeat(pos, MLA_ROPE, axis=2).reshape(n // ROPE_PACK, LANES)
    invf = jnp.tile(jnp.concatenate([inv_freq, inv_freq]), ROPE_PACK)[None, :]
    out = pl.BlockSpec((tm, LANES), lambda i: (i, 0))
    return pl.pallas_call(
        _rope_kernel,
        grid=(n // tm,),
        in_specs=[pl.BlockSpec((rows, LANES), lambda i: (i, 0)), pl.BlockSpec((1, LANES), lambda i: (0, 0))],
        out_specs=[out, out],
        out_shape=[jax.ShapeDtypeStruct((n, LANES), F32)] * 2,
        compiler_params=_cparams(("parallel",)),
        name="rope_tables",
    )(pos, invf)


def _mla_prep_kernel(cq_ref, ckv_ref, ckr_ref, c_ref, s_ref, gcq_ref, wuq_ref, wuqs_ref, gckv_ref, wuk_ref, wuv_ref,
                     gq_ref, gqs_ref, gk_ref, gks_ref, ones_ref, qo_ref, ko_ref, vo_ref):
    d_qk = float(MLA_NOPE + MLA_ROPE)
    cos, sin = c_ref[...], s_ref[...]
    half = MLA_ROPE // 2

    def rms(z, g):
        return (z * lax.rsqrt(jnp.mean(z * z, axis=-1, keepdims=True) + EPS) * g).astype(BF16)

    def head_ssq(z):
        zz = (z * z).astype(BF16)
        parts = [jnp.dot(zz[:, c * MXU_WIDTH:(c + 1) * MXU_WIDTH], ones_ref[...], preferred_element_type=F32)
                 for c in range(z.shape[1] // MXU_WIDTH)]
        return jnp.concatenate(parts, axis=1)

    cn = rms(cq_ref[...].astype(F32), gcq_ref[...])
    q = jnp.dot(cn, wuq_ref[...], preferred_element_type=F32)
    q_sw = jnp.dot(cn, wuqs_ref[...], preferred_element_type=F32)
    q_inv = lax.rsqrt(head_ssq(q) + d_qk * EPS)

    kvn = rms(ckv_ref[...].astype(F32), gckv_ref[...])
    kn = jnp.dot(kvn, wuk_ref[...], preferred_element_type=F32)
    vt = lax.dot_general(wuv_ref[...], kvn, (((1,), (1,)), ((), ())), preferred_element_type=F32)
    vo_ref[...] = _with_ones_rows(vt).astype(BF16)
    lane = lax.broadcasted_iota(jnp.int32, ckr_ref.shape, 1)
    kr = jnp.where(lane < MLA_ROPE, ckr_ref[...].astype(F32), 0.0)
    ss_r = jnp.sum(kr * kr, axis=-1, keepdims=True)
    kr_placed = pltpu.roll(kr, MLA_NOPE, 1)
    kr_swapped = jnp.where(lane < MLA_NOPE + half, pltpu.roll(kr, MLA_NOPE - half, 1), pltpu.roll(kr, MLA_NOPE + half, 1))
    kr_swapped = jnp.where((lane >= MLA_NOPE) & (lane < MLA_NOPE + MLA_ROPE), kr_swapped, 0.0)
    k_inv = lax.rsqrt(head_ssq(kn) + ss_r + d_qk * EPS)

    gc_q, gs_q = gq_ref[...] * cos, gqs_ref[...] * sin
    gc_k = gk_ref[...] * cos
    kr_rot = kr_placed * gc_k + kr_swapped * (gks_ref[...] * sin)
    for h in range(N_HEADS):
        sl = slice(h * HEAD_PAD, (h + 1) * HEAD_PAD)
        qo_ref[:, sl] = (q_inv[:, sl] * (q[:, sl] * gc_q + q_sw[:, sl] * gs_q)).astype(BF16)
        ko_ref[:, sl] = (k_inv[:, sl] * (kn[:, sl] * gc_k + kr_rot)).astype(BF16)


def _mla_prep(proj, cos_t, sin_t, gcq, wuq, wuq_sw, gckv, wuk, wuv_t, gq, gq_sw, gk, gk_sw, ones128, batch, seq):
    n = proj.shape[0]
    tm = min(ROWS_PREP, seq)
    nt = seq // tm
    full = lambda shape: pl.BlockSpec(shape, lambda i: (0, 0))
    tab = pl.BlockSpec((tm, LANES), lambda i: (i, 0))
    wide = pl.BlockSpec((tm, N_HEADS * HEAD_PAD), lambda i: (i, 0))
    return pl.pallas_call(
        _mla_prep_kernel,
        grid=(n // tm,),
        in_specs=[
            pl.BlockSpec((tm, MLA_Q_RANK), lambda i: (i, COL_CQ // MLA_Q_RANK)),
            pl.BlockSpec((tm, MLA_KV_RANK), lambda i: (i, COL_CKV // LANES)),
            pl.BlockSpec((tm, LANES), lambda i: (i, COL_CKR // LANES)),
            tab, tab,
            full((1, MLA_Q_RANK)), full((MLA_Q_RANK, N_HEADS * HEAD_PAD)), full((MLA_Q_RANK, N_HEADS * HEAD_PAD)),
            full((1, MLA_KV_RANK)), full((MLA_KV_RANK, N_HEADS * HEAD_PAD)), full((N_HEADS * VT_ROWS, MLA_KV_RANK)),
            full((1, HEAD_PAD)), full((1, HEAD_PAD)), full((1, HEAD_PAD)), full((1, HEAD_PAD)), full((MXU_WIDTH, MXU_WIDTH)),
        ],
        out_specs=[wide, wide, pl.BlockSpec((None, N_HEADS * VT_ROWS, tm), lambda i: (i // nt, 0, i % nt))],
        out_shape=[jax.ShapeDtypeStruct((n, N_HEADS * HEAD_PAD), BF16)] * 2
        + [jax.ShapeDtypeStruct((batch, N_HEADS * VT_ROWS, seq), BF16)],
        compiler_params=_cparams(("parallel",)),
        name="mla_prep",
    )(proj, proj, proj, cos_t, sin_t, gcq, wuq, wuq_sw, gckv, wuk, wuv_t, gq, gq_sw, gk, gk_sw, ones128)


def _gla_kernel(q_ref, k_ref, v_ref, sm_ref, wh_ref, wl_ref, b_ref, go_ref, ones_ref, sel_ref, o_ref, st_ref,
                inter_ref, *, tc):
    hk = N_HEADS * GLA_DK
    ck = GLA_CHUNK

    @pl.when(pl.program_id(1) == 0)
    def _():
        st_ref[...] = jnp.zeros_like(st_ref)

    sm = sm_ref[...]
    sm_hi = sm.astype(BF16)
    sm_lo = (sm - sm_hi.astype(F32)).astype(BF16)
    z = (jnp.dot(sm_hi, wh_ref[...], preferred_element_type=F32)
         + jnp.dot(sm_lo, wh_ref[...], preferred_element_type=F32)
         + jnp.dot(sm_hi, wl_ref[...], preferred_element_type=F32)) + b_ref[...]
    g_all = _row_prefix_sum(_log_sigmoid(z) * (1.0 / GLA_NORMALIZER), ck)

    lane_k = lax.broadcasted_iota(jnp.int32, (ck, hk), 1) // GLA_DK
    lane_pair = lax.broadcasted_iota(jnp.int32, (ck, LANES), 1)
    a_row = lax.broadcasted_iota(jnp.int32, (N_HEADS * ck, ck), 0) % ck
    a_col = lax.broadcasted_iota(jnp.int32, (N_HEADS * ck, ck), 1)
    st_blockdiag = (lax.broadcasted_iota(jnp.int32, (GROUP_W, hk), 0) // HEAD_DIM
                    == lax.broadcasted_iota(jnp.int32, (GROUP_W, hk), 1) // GLA_DK)

    def stack_heads(x):
        return jnp.concatenate([jnp.where(lane_k == h, x, jnp.zeros_like(x)) for h in range(N_HEADS)], axis=0)

    def intra_output(a, v):
        parts = []
        for p in range(N_HEADS // 2):
            r = jnp.dot(a[2 * p * ck:2 * (p + 1) * ck, :], v[:, p * LANES:(p + 1) * LANES], preferred_element_type=F32)
            parts.append(jnp.where(lane_pair < HEAD_DIM, r[0:ck, :], r[ck:2 * ck, :]))
        return jnp.concatenate(parts, axis=1)

    def write_normed(o):
        ssq = jnp.dot((o * o).astype(BF16), ones_ref[...], preferred_element_type=F32)
        o_ref[...] = (o * lax.rsqrt(ssq * (1.0 / HEAD_DIM) + EPS) * go_ref[...]).astype(BF16)

    n_chunks = tc // ck
    qts, intras, incs, decays = [], [], [], []
    for c in range(n_chunks):
        rows = slice(c * ck, (c + 1) * ck)
        g = g_all[rows]
        q = q_ref[rows, :].astype(F32) * (GLA_DK ** -0.5)
        k = k_ref[rows, :].astype(F32)
        v = v_ref[rows, :]
        g_last = g[ck - 1:ck, :]
        qt = (q * jnp.exp(g)).astype(BF16)
        kt = (k * jnp.exp(-g)).astype(BF16)
        kh = (k * jnp.exp(g_last - g)).astype(BF16)

        u = lax.dot_general(v, kh, (((0,), (0,)), ((), ())), preferred_element_type=F32)
        incs.append(jnp.where(st_blockdiag, u, 0.0))
        decays.append(jnp.exp(g_last))

        a = lax.dot_general(stack_heads(qt), kt, (((1,), (1,)), ((), ())), preferred_element_type=F32)
        intras.append(intra_output(jnp.where(a_row >= a_col, a, 0.0).astype(BF16), v))
        qts.append(qt)

    state = st_ref[...]
    states = []
    for c in range(n_chunks):
        states.append(state.astype(BF16))
        state = state * decays[c] + incs[c]
    st_ref[...] = state

    outs = []
    for c in range(n_chunks):
        o = lax.dot_general(qts[c], states[c], (((1,), (1,)), ((), ())), preferred_element_type=F32)
        inter_ref[c * ck:(c + 1) * ck, :] = o
        outs.append(o + intras[c])
    write_normed(jnp.concatenate(outs, axis=0))

    @pl.when(jnp.min(g_all) < -GLA_SAFE_LOG_DECAY)
    def _():
        t_idx = lax.broadcasted_iota(jnp.int32, (ck, hk), 0)
        nt_dims = (((1,), (1,)), ((), ()))
        redo = []
        for c in range(n_chunks):
            rows = slice(c * ck, (c + 1) * ck)
            g = g_all[rows]
            q = q_ref[rows, :].astype(F32) * (GLA_DK ** -0.5)
            k = k_ref[rows, :].astype(F32)
            g_hi = g.astype(BF16)
            g_r1 = g - g_hi.astype(F32)
            g_mid = g_r1.astype(BF16)
            g_lo = (g_r1 - g_mid.astype(F32)).astype(BF16)
            g3 = jnp.concatenate([g_hi, g_mid, g_lo], axis=0)
            a = lax.dot_general(stack_heads(q.astype(BF16)), k.astype(BF16), nt_dims, preferred_element_type=F32)
            a = jnp.where(a_row == a_col, a, 0.0)
            for lvl in range(GLA_LEVELS):
                half = ck >> (lvl + 1)
                g_mid_row = jnp.dot(sel_ref[lvl], g3, preferred_element_type=F32)
                d = g - g_mid_row
                second = (t_idx // half) % 2 == 1
                qd = jnp.where(second, q * jnp.exp(jnp.minimum(d, 0.0)), 0.0).astype(BF16)
                kd = jnp.where(second, 0.0, k * jnp.exp(jnp.minimum(-d, 0.0))).astype(BF16)
                part = lax.dot_general(stack_heads(qd), kd, nt_dims, preferred_element_type=F32)
                a = a + jnp.where(a_row // (2 * half) == a_col // (2 * half), part, 0.0)
            redo.append(inter_ref[rows, :] + intra_output(a.astype(BF16), v_ref[rows, :]))
        write_normed(jnp.concatenate(redo, axis=0))


def _gla(proj, small, w_hi, w_lo, b_up, g_out, ones64, sel, batch, seq):
    n = proj.shape[0]
    tc = min(ROWS_GLA, seq)
    nt = seq // tc
    hk = N_HEADS * GLA_DK
    full = lambda shape: pl.BlockSpec(shape, lambda b, t: (0, 0))
    return pl.pallas_call(
        functools.partial(_gla_kernel, tc=tc),
        grid=(batch, nt),
        in_specs=[
            pl.BlockSpec((tc, hk), lambda b, t: (b * nt + t, COL_AQ // hk)),
            pl.BlockSpec((tc, hk), lambda b, t: (b * nt + t, COL_AK // hk)),
            pl.BlockSpec((tc, GROUP_W), lambda b, t: (b * nt + t, COL_AV // GROUP_W)),
            pl.BlockSpec((tc, LANES), lambda b, t: (b * nt + t, 0)),
            full((LANES, hk)), full((LANES, hk)), full((1, hk)), full((1, GROUP_W)), full((GROUP_W, GROUP_W)),
            pl.BlockSpec((GLA_LEVELS, GLA_CHUNK, 3 * GLA_CHUNK), lambda b, t: (0, 0, 0)),
        ],
        out_specs=pl.BlockSpec((tc, GROUP_W), lambda b, t: (b * nt + t, 0)),
        out_shape=jax.ShapeDtypeStruct((n, GROUP_W), BF16),
        scratch_shapes=[pltpu.VMEM((GROUP_W, hk), F32), pltpu.VMEM((tc, GROUP_W), F32)],
        compiler_params=_cparams(("parallel", "arbitrary")),
        name="gla",
    )(proj, proj, proj, small, w_hi, w_lo, b_up, g_out, ones64, sel)


def _lru_kernel(x_ref, cw_ref, cb_ref, wr_ref, br_ref, wi_ref, bi_ref, lam_ref, o_ref, xprev_ref, h_ref, *, tr):
    @pl.when(pl.program_id(1) == 0)
    def _():
        xprev_ref[...] = jnp.zeros_like(xprev_ref)
        h_ref[...] = jnp.zeros_like(h_ref)

    x3 = x_ref[...].astype(F32).reshape(tr // SUBLANES, SUBLANES, GROUP_W)
    xg = jnp.concatenate([xprev_ref[...][None], x3], axis=0)
    xprev_ref[...] = x3[tr // SUBLANES - 1]
    sub3 = lax.broadcasted_iota(jnp.int32, x3.shape, 1)
    xc3 = cb_ref[...][None] + cw_ref[CONV_W - 1:CONV_W, :][None] * x3
    for s in range(1, CONV_W):
        rot = pltpu.roll(xg, s, 1)
        shifted = jnp.where(sub3 >= s, rot[1:], rot[:-1])
        xc3 = xc3 + cw_ref[CONV_W - 1 - s:CONV_W - s, :][None] * shifted
    xc = xc3.reshape(tr, GROUP_W)

    xb = xc.astype(BF16)
    t_r = jnp.tanh(0.5 * (jnp.dot(xb, wr_ref[...], preferred_element_type=F32) + br_ref[...]))
    t_i = jnp.tanh(0.5 * (jnp.dot(xb, wi_ref[...], preferred_element_type=F32) + bi_ref[...]))
    lam = lam_ref[...]
    softplus = jnp.maximum(-lam, 0.0) + jnp.log1p(jnp.exp(-jnp.abs(lam)))
    half_rate = (-0.5 * LRU_C) * softplus
    log_a = half_rate * t_r + half_rate
    one_minus_a = _neg_expm1(log_a)
    a = 1.0 - one_minus_a
    half_x = 0.5 * xc
    bx = jnp.sqrt(one_minus_a * (2.0 - one_minus_a)) * (half_x * t_i + half_x)

    a3 = a.reshape(tr // SUBLANES, SUBLANES, GROUP_W)
    b3 = bx.reshape(tr // SUBLANES, SUBLANES, GROUP_W)
    sub = lax.broadcasted_iota(jnp.int32, a3.shape, 1)
    for s in (1, 2, 4):
        keep = sub >= s
        a_prev = jnp.where(keep, pltpu.roll(a3, s, 1), 1.0)
        b_prev = jnp.where(keep, pltpu.roll(b3, s, 1), 0.0)
        b3 = a3 * b_prev + b3
        a3 = a3 * a_prev
    h_prev = h_ref[0:1, :]
    for g in range(tr // SUBLANES):
        hg = a3[g] * h_prev + b3[g]
        o_ref[g * SUBLANES:(g + 1) * SUBLANES, :] = hg.astype(BF16)
        h_prev = hg[SUBLANES - 1:SUBLANES, :]
    h_ref[0:1, :] = h_prev


def _lru(proj, conv_w, conv_b, wr, br, wi, bi, lam, batch, seq):
    n = proj.shape[0]
    tr = min(ROWS_LRU, seq)
    nt = seq // tr
    full = lambda shape: pl.BlockSpec(shape, lambda b, t: (0, 0))
    return pl.pallas_call(
        functools.partial(_lru_kernel, tr=tr),
        grid=(batch, nt),
        in_specs=[
            pl.BlockSpec((tr, GROUP_W), lambda b, t: (b * nt + t, COL_DX // GROUP_W)),
            full((SUBLANES, GROUP_W)), full((1, GROUP_W)),
            full((GROUP_W, GROUP_W)), full((1, GROUP_W)),
            full((GROUP_W, GROUP_W)), full((1, GROUP_W)), full((1, GROUP_W)),
        ],
        out_specs=pl.BlockSpec((tr, GROUP_W), lambda b, t: (b * nt + t, 0)),
        out_shape=jax.ShapeDtypeStruct((n, GROUP_W), BF16),
        scratch_shapes=[pltpu.VMEM((SUBLANES, GROUP_W), F32), pltpu.VMEM((SUBLANES, GROUP_W), F32)],
        compiler_params=_cparams(("parallel", "arbitrary")),
        name="rglru",
    )(proj, conv_w, conv_b, wr, br, wi, bi, lam)


def _pad_cols(w, width):
    return jnp.pad(w, [(0, 0)] * (w.ndim - 1) + [(0, width - w.shape[-1])])


def _block_diag_ones(n_blocks, size):
    return jnp.asarray(np.kron(np.eye(n_blocks, dtype=np.float32), np.ones((size, size), np.float32)), BF16)


def _swap_rotary(w):
    lead = w.shape[:-1]
    w = w.reshape(lead + (N_HEADS, MLA_NOPE + MLA_ROPE))
    half = MLA_ROPE // 2
    z1, z2 = w[..., MLA_NOPE:MLA_NOPE + half], w[..., MLA_NOPE + half:]
    w = jnp.concatenate([jnp.zeros_like(w[..., :MLA_NOPE]), z2, z1], axis=-1)
    return w.reshape(lead + (N_HEADS * (MLA_NOPE + MLA_ROPE),))


def _pad_heads(w, d_real):
    lead = w.shape[:-1]
    w = w.reshape(lead + (N_HEADS, d_real))
    w = jnp.pad(w, [(0, 0)] * len(lead) + [(0, 0), (0, HEAD_PAD - d_real)])
    return w.reshape(lead + (N_HEADS * HEAD_PAD,))


def kernel(x, positions, norm_g, w_in, w_out, gla_w_up, gla_b_up, gla_g_out, fox_b_f, fox_g_q, fox_g_k, mla_g_cq, mla_w_uq, mla_g_ckv, mla_w_ukv, mla_g_q, mla_g_k, lru_conv_w, lru_conv_b, lru_w_r, lru_b_r, lru_w_i, lru_b_i, lru_lam):
    batch, seq, d_model = x.shape
    depth = w_in.shape[0]
    n = batch * seq
    assert d_model == D_MODEL and w_in.shape[1:] == (D_MODEL, sum(
        (D_MIX, 2 * N_HEADS * GLA_DK, GROUP_W, GLA_RANK, 3 * GROUP_W, N_HEADS, MLA_Q_RANK, MLA_KV_RANK, MLA_ROPE, GROUP_W)))
    for rows in (ROWS_PREP, ROWS_GLA, ROWS_LRU, ATTN_BLOCK):
        assert seq % min(rows, seq) == 0 and min(rows, seq) % GLA_CHUNK == 0, (seq, rows)
    for rows in (ROWS_INPROJ, ROWS_OUTPROJ, ROWS_PREP):
        assert n % min(rows, n) == 0, (n, rows)

    gla_qk = N_HEADS * GLA_DK
    sizes = (D_MIX, gla_qk, gla_qk, GROUP_W, GLA_RANK, GROUP_W, GROUP_W, GROUP_W, N_HEADS,
             MLA_Q_RANK, MLA_KV_RANK, MLA_ROPE, GROUP_W)
    offs = np.concatenate([[0], np.cumsum(sizes)])
    seg = [w_in[:, :, offs[i]:offs[i + 1]] for i in range(len(sizes))]
    (w_gate, w_aq, w_ak, w_av, w_ag, w_bq, w_bk, w_bv, w_bf, w_cq, w_ckv, w_ckr, w_dx) = seg
    assert (SM_AG, SM_BF) == (MLA_ROPE, MLA_ROPE + GLA_RANK)
    w_kr_gates = _pad_cols(jnp.concatenate([w_ckr, w_ag, w_bf], axis=-1), LANES)
    w_main = jnp.concatenate(
        [w_gate, w_aq, w_ak, w_av, w_bq, w_bk, w_bv, w_cq, w_ckv, w_kr_gates, w_dx], axis=-1).astype(BF16)
    w_out_b = w_out.astype(BF16)

    ones64 = _block_diag_ones(GROUP_W // HEAD_DIM, HEAD_DIM)
    ones128 = _block_diag_ones(MXU_WIDTH // HEAD_PAD, HEAD_PAD)

    wup = jnp.pad(gla_w_up, ((0, 0), (SM_AG, LANES - SM_AG - GLA_RANK), (0, 0)))
    wup_hi = wup.astype(BF16)
    wup_lo = (wup - wup_hi.astype(F32)).astype(BF16)

    sel = np.zeros((GLA_LEVELS, GLA_CHUNK, 3 * GLA_CHUNK), np.float32)
    for lvl in range(GLA_LEVELS):
        half = GLA_CHUNK >> (lvl + 1)
        for t in range(GLA_CHUNK):
            mid = (t // (2 * half)) * 2 * half + half - 1
            for piece in range(3):
                sel[lvl, t, piece * GLA_CHUNK + mid] = 1.0
    gla_sel = jnp.asarray(sel, BF16)

    fox_scale = LOG2E
    mla_scale = (MLA_NOPE + MLA_ROPE) ** -0.5 * LOG2E
    fox_gq = jnp.tile(fox_g_q, (1, N_HEADS)) * fox_scale
    fox_gk = jnp.tile(fox_g_k, (1, N_HEADS)) * HEAD_DIM ** 0.5
    place = np.zeros((LANES, N_HEADS * HEAD_PAD), np.float32)
    aug_q = np.zeros((1, N_HEADS * HEAD_PAD), np.float32)
    for h in range(N_HEADS):
        base = h * HEAD_PAD + (HEAD_DIM if h % 2 == 0 else 0)
        for piece in range(3):
            place[SM_BF + piece * PIECE_STRIDE + h, base + piece] = 1.0
            aug_q[0, base + piece] = 1.0
    place, aug_q = jnp.asarray(place, BF16), jnp.asarray(aug_q)
    fox_bf = jnp.pad(fox_b_f, ((0, 0), (SM_BF, LANES - SM_BF - N_HEADS)))

    wuq = _pad_heads(mla_w_uq, MLA_NOPE + MLA_ROPE).astype(BF16)
    wuq_sw = _pad_heads(_swap_rotary(mla_w_uq), MLA_NOPE + MLA_ROPE).astype(BF16)
    wukv = mla_w_ukv.reshape(depth, MLA_KV_RANK, N_HEADS, 2 * HEAD_DIM)
    wuk = _pad_heads(wukv[..., :MLA_NOPE].reshape(depth, MLA_KV_RANK, N_HEADS * MLA_NOPE), MLA_NOPE).astype(BF16)
    pad_rows = lambda m: jnp.pad(m.reshape(m.shape[:-2] + (N_HEADS, HEAD_DIM, m.shape[-1])),
                                 [(0, 0)] * (m.ndim - 1) + [(0, VT_ROWS - HEAD_DIM), (0, 0)]
                                 ).reshape(m.shape[:-2] + (N_HEADS * VT_ROWS, m.shape[-1]))
    wuv_t = pad_rows(jnp.swapaxes(wukv[..., MLA_NOPE:].reshape(depth, MLA_KV_RANK, GROUP_W), 1, 2)).astype(BF16)
    eye = pad_rows(jnp.eye(GROUP_W, dtype=F32)).astype(BF16)
    d_qk = MLA_NOPE + MLA_ROPE
    half = MLA_ROPE // 2
    swap_halves = lambda g: jnp.concatenate(
        [jnp.zeros_like(g[:, :MLA_NOPE]), g[:, MLA_NOPE + half:], g[:, MLA_NOPE:MLA_NOPE + half]], axis=-1)
    mla_gq_all = mla_g_q * (mla_scale * d_qk ** 0.5)
    mla_gk_all = mla_g_k * d_qk ** 0.5
    mla_gq, mla_gq_sw = _pad_cols(mla_gq_all, HEAD_PAD), _pad_cols(swap_halves(mla_gq_all), HEAD_PAD)
    mla_gk, mla_gk_sw = _pad_cols(mla_gk_all, HEAD_PAD), _pad_cols(swap_halves(mla_gk_all), HEAD_PAD)

    eye_h = jnp.eye(N_HEADS, dtype=F32)
    wr_bd = jnp.einsum("lncd,nm->lncmd", lru_w_r, eye_h).reshape(depth, GROUP_W, GROUP_W).astype(BF16)
    wi_bd = jnp.einsum("lncd,nm->lncmd", lru_w_i, eye_h).reshape(depth, GROUP_W, GROUP_W).astype(BF16)
    conv_w = jnp.pad(lru_conv_w, ((0, 0), (0, SUBLANES - CONV_W), (0, 0)))

    inv_freq = ROPE_THETA ** (-jnp.arange(half, dtype=F32) / half)
    cos_t, sin_t = _rope_tables(positions, inv_freq)

    xf = x.reshape(n, D_MODEL)
    for l in range(depth):
        proj, small = _norm_inproj(xf, norm_g[l][None, :], w_main, l)
        ya = _gla(proj, small, wup_hi[l], wup_lo[l], gla_b_up[l][None, :],
                  jnp.tile(gla_g_out[l], N_HEADS)[None, :], ones64, gla_sel, batch, seq)
        fq, fk, fvt = _fox_prep(proj, small, fox_bf[l][None, :], fox_gq[l][None, :], fox_gk[l][None, :], ones64, eye,
                                place, aug_q, batch, seq)
        yb = _attention(fq, fk, fvt, batch, seq)
        mq, mk, mvt = _mla_prep(proj, cos_t, sin_t, mla_g_cq[l][None, :], wuq[l], wuq_sw[l], mla_g_ckv[l][None, :],
                                wuk[l], wuv_t[l], mla_gq[l][None, :], mla_gq_sw[l][None, :], mla_gk[l][None, :],
                                mla_gk_sw[l][None, :], ones128, batch, seq)
        yc = _attention(mq, mk, mvt, batch, seq)
        yd = _lru(proj, conv_w[l], lru_conv_b[l][None, :], wr_bd[l], lru_b_r[l][None, :], wi_bd[l],
                  lru_b_i[l][None, :], lru_lam[l][None, :], batch, seq)
        xf = _gate_outproj(ya, yb, yc, yd, proj, w_out_b, xf, l)
    return xf.reshape(batch, seq, D_MODEL)
```
